```python
import math
import jax
import jax.numpy as jnp
from jax import lax
import numpy as np

D_MODEL = 1024
BATCH = 4
SEQ = 8192
DEPTH = 1
DEC_BATCH = 8
DEC_SEQ = 32
PAST_LEN = 2048

CHUNK = 64
D_MIX = D_MODEL
RWKV_HEAD_DIM = 64
RWKV_HEADS = D_MIX // 2 // RWKV_HEAD_DIM
RWKV_WIDTH = RWKV_HEADS * RWKV_HEAD_DIM
DECAY_RANK = 64
ICLR_RANK = 64
GATE_RANK = 128
RWKV_PROJ = 3 * RWKV_WIDTH + DECAY_RANK + ICLR_RANK + GATE_RANK
RWKV_SPLITS = (RWKV_WIDTH, 2 * RWKV_WIDTH, 3 * RWKV_WIDTH,
               3 * RWKV_WIDTH + DECAY_RANK, 3 * RWKV_WIDTH + DECAY_RANK + ICLR_RANK)
GN_EPS = 64e-5
HEAD_DIM = 64
ATTN_HEADS = (D_MIX - RWKV_WIDTH) // HEAD_DIM
KV_HEADS = ATTN_HEADS // 4
Q_GROUP = ATTN_HEADS // KV_HEADS
ATTN_WIDTH = ATTN_HEADS * HEAD_DIM
KV_WIDTH = KV_HEADS * HEAD_DIM
ATTN_PROJ = ATTN_WIDTH + 2 * KV_WIDTH
ATTN_SCALE = HEAD_DIM ** -0.5
WINDOW = 128
WIN_CHUNKS = WINDOW // CHUNK
BAND = (WIN_CHUNKS + 1) * CHUNK
REL_BUCKETS = 32
REL_MAX_DIST = 128
PROJ = RWKV_PROJ + ATTN_PROJ
N_EXPERTS = 32
TOP_K = 4
D_FF = D_MODEL
SWIGLU_ALPHA = 1.702
SWIGLU_LIMIT = 7.0
MOE_BLOCK = 128
RMS_EPS = 1e-6
NEG_INF = -1e30

kernel_name = "hymba_rwkv7_swa_sink_moe_stream_step"


def rms_norm(x, gain):
    xf = x.astype(jnp.float32)
    y = xf * lax.rsqrt(jnp.mean(xf * xf, axis=-1, keepdims=True) + RMS_EPS)
    return (y * gain.astype(jnp.float32)).astype(x.dtype)


def ada_modulate(x, gain, shift, scale):
    return rms_norm(x, gain) * (1 + scale[:, None, :]) + shift[:, None, :]


def t5_bucket(rel):
    half = REL_BUCKETS // 2
    max_exact = half // 2
    n = jnp.abs(rel)
    log_ratio = jnp.log(jnp.maximum(n, 1).astype(jnp.float32) / max_exact) / math.log(REL_MAX_DIST / max_exact)
    large = jnp.minimum(max_exact + (log_ratio * (half - max_exact)).astype(jnp.int32), half - 1)
    return jnp.where(rel > 0, half, 0) + jnp.where(n < max_exact, n, large)


def rel_bias_for(rel, rel_table):
    b = rel_table.astype(jnp.float32)[t5_bucket(rel)]
    return jnp.transpose(b, (2, 0, 1)).reshape(KV_HEADS, Q_GROUP, rel.shape[0], rel.shape[1])


def sink_softmax(s, sinks):
    sink_col = jnp.broadcast_to(sinks.astype(jnp.float32).reshape(KV_HEADS, Q_GROUP, 1, 1), s.shape[:-1] + (1,))
    p = jax.nn.softmax(jnp.concatenate([s, sink_col], axis=-1), axis=-1)
    return p[..., :-1]


def swa_prompt(q, k, v, rel_table, sinks):
    b, t = q.shape[:2]
    nc = t // CHUNK
    qc = q.reshape(b, nc, CHUNK, KV_HEADS, Q_GROUP, HEAD_DIM)

    def band(z):
        zc = z.reshape(b, nc, CHUNK, KV_HEADS, HEAD_DIM)
        zp = jnp.concatenate([jnp.zeros((b, WIN_CHUNKS, CHUNK, KV_HEADS, HEAD_DIM), z.dtype), zc], axis=1)
        return jnp.concatenate([zp[:, i:i + nc] for i in range(WIN_CHUNKS + 1)], axis=2)

    kb, vb = band(k), band(v)
    rel = (jnp.arange(BAND) - WIN_CHUNKS * CHUNK)[None, :] - jnp.arange(CHUNK)[:, None]
    bias = rel_bias_for(rel, rel_table)
    key_chunk = jnp.arange(nc)[:, None] + jnp.arange(BAND)[None, :] // CHUNK - WIN_CHUNKS
    valid = (key_chunk >= 0)[None, :, None, None, None, :]
    s = jnp.einsum("bcqhgd,bckhd->bchgqk", qc, kb, preferred_element_type=jnp.float32) * ATTN_SCALE + bias
    s = jnp.where(valid, s, NEG_INF)
    p = sink_softmax(s, sinks)
    o = jnp.einsum("bchgqk,bckhd->bcqhgd", p.astype(v.dtype), vb)
    return o.reshape(b, t, ATTN_WIDTH)


def swa_sample(q, k, v, cache_k, cache_v, rel_table, sinks):
    b, t = q.shape[:2]
    n_cache = cache_k.shape[1]
    kk = jnp.concatenate([cache_k.astype(k.dtype), k], axis=1)
    vv = jnp.concatenate([cache_v.astype(v.dtype), v], axis=1)
    k_pos = jnp.concatenate([jnp.arange(n_cache) - n_cache, jnp.arange(t)])
    rel = k_pos[None, :] - jnp.arange(t)[:, None]
    bias = rel_bias_for(rel, rel_table)
    qh = q.reshape(b, t, KV_HEADS, Q_GROUP, HEAD_DIM)
    s = jnp.einsum("bqhgd,bkhd->bhgqk", qh, kk, preferred_element_type=jnp.float32) * ATTN_SCALE + bias
    p = sink_softmax(s, sinks)
    o = jnp.einsum("bhgqk,bkhd->bqhgd", p.astype(vv.dtype), vv)
    return o.reshape(b, t, ATTN_WIDTH)


def wkv7_scan(r, w, k, v, a, bvec, s0):
    xs = tuple(jnp.moveaxis(z.astype(jnp.float32), 1, 0) for z in (r, w, k, v, a, bvec))

    def step(S, inp):
        r_t, w_t, k_t, v_t, a_t, b_t = inp
        sa = jnp.einsum("bhvk,bhk->bhv", S, a_t)
        S = S * w_t[:, :, None, :] + sa[..., None] * b_t[:, :, None, :] + v_t[..., None] * k_t[:, :, None, :]
        return S, jnp.einsum("bhvk,bhk->bhv", S, r_t)

    s_fin, ys = lax.scan(step, s0.astype(jnp.float32), xs)
    return jnp.moveaxis(ys, 0, 1), s_fin


def rwkv7_mix(p, prev_row, s0, lp):
    b, t, _ = p.shape
    p_prev = jnp.concatenate([prev_row[:, None, :].astype(p.dtype), p[:, :-1]], axis=1)
    xp = p + (p_prev - p) * lp["rwkv_mu"]
    r, k, v, wd, ad, gd = jnp.split(xp, RWKV_SPLITS, axis=-1)
    heads = lambda z: z.reshape(b, t, RWKV_HEADS, RWKV_HEAD_DIM)
    w_log = -jax.nn.softplus(-(lp["rwkv_w0"] + jnp.tanh(wd) @ lp["rwkv_w_up"])) - 0.5
    decay = jnp.exp(-jnp.exp(w_log.astype(jnp.float32)))
    a = jax.nn.sigmoid(lp["rwkv_a0"] + ad @ lp["rwkv_a_up"])
    g = jax.nn.sigmoid(gd) @ lp["rwkv_g_up"]
    kk = heads(k * lp["rwkv_k_k"]).astype(jnp.float32)
    kk = kk * lax.rsqrt(jnp.maximum(jnp.sum(kk * kk, axis=-1, keepdims=True), 1e-24))
    k = k * (1 + (a - 1) * lp["rwkv_k_a"])
    a_h = heads(a).astype(jnp.float32)
    y, s_fin = wkv7_scan(heads(r), heads(decay), heads(k), heads(v), -kk, kk * a_h, s0)
    yc = y - jnp.mean(y, axis=-1, keepdims=True)
    y = yc * lax.rsqrt(jnp.mean(yc * yc, axis=-1, keepdims=True) + GN_EPS)
    y = y.reshape(b, t, RWKV_WIDTH) * lp["rwkv_ln_g"] + lp["rwkv_ln_b"]
    rf, kf, vf = (heads(z).astype(jnp.float32) for z in (r, k, v))
    bonus = jnp.sum(rf * kf * lp["rwkv_r_k"], axis=-1, keepdims=True) * vf
    y = (y + bonus.reshape(b, t, RWKV_WIDTH)) * g
    return y.astype(p.dtype), s_fin, p[:, -1]


def clamped_swiglu(u):
    u_glu = jnp.minimum(u[..., ::2], SWIGLU_LIMIT)
    u_lin = jnp.clip(u[..., 1::2], -SWIGLU_LIMIT, SWIGLU_LIMIT)
    return u_glu * jax.nn.sigmoid(SWIGLU_ALPHA * u_glu) * (u_lin + 1)


def moe_ffn(h, lp):
    n_tok, d = h.shape
    logits = (h @ lp["router_w"] + lp["router_b"]).astype(jnp.float32)
    top_logit, top_e = lax.top_k(logits, TOP_K)
    gate = jax.nn.softmax(top_logit, axis=-1)
    n_assign = n_tok * TOP_K
    flat_e = top_e.reshape(-1)
    order = jnp.argsort(flat_e)
    sorted_e = flat_e[order]
    sorted_tok = (order // TOP_K).astype(jnp.int32)
    sorted_gate = gate.reshape(-1)[order]
    counts = jnp.bincount(flat_e, length=N_EXPERTS)
    padded = (counts + MOE_BLOCK - 1) // MOE_BLOCK * MOE_BLOCK
    pad_end = jnp.cumsum(padded)
    pad_start = pad_end - padded
    start = jnp.cumsum(counts) - counts
    dest = pad_start[sorted_e] + jnp.arange(n_assign) - start[sorted_e]
    n_blocks = -(-(n_assign + N_EXPERTS * (MOE_BLOCK - 1)) // MOE_BLOCK)
    n_rows = n_blocks * MOE_BLOCK
    row_tok = jnp.full((n_rows,), n_tok, jnp.int32).at[dest].set(sorted_tok)
    row_gate = jnp.zeros((n_rows,), jnp.float32).at[dest].set(sorted_gate)
    block_e = jnp.minimum(jnp.searchsorted(pad_end, jnp.arange(n_blocks) * MOE_BLOCK, side="right"), N_EXPERTS - 1)
    h_pad = jnp.concatenate([h, jnp.zeros((1, d), h.dtype)], axis=0)
    w_up, b_up, w_dn, b_dn = lp["exp_w_up"], lp["exp_b_up"], lp["exp_w_down"], lp["exp_b_down"]

    def expert_block(args):
        tok, e = args
        u = h_pad[tok] @ w_up[e] + b_up[e]
        return clamped_swiglu(u) @ w_dn[e] + b_dn[e]

    out = lax.map(expert_block, (row_tok.reshape(n_blocks, MOE_BLOCK), block_e))
    out = out.reshape(n_rows, d) * row_gate[:, None].astype(h.dtype)
    y = jnp.zeros((n_tok + 1, d), h.dtype).at[row_tok].add(out)
    return y[:n_tok]


def trunk_layer(x, c, shift_row, wkv_state, win_k, win_v, lp, rel_bias):
    b, t, _ = x.shape
    mod = jax.nn.silu(c) @ lp["w_ada"] + lp["b_ada"]
    sh1, sc1, gt1, sh2, sc2, gt2 = jnp.split(mod, 6, axis=-1)
    h = ada_modulate(x, lp["norm1_g"], sh1, sc1)
    proj = h @ lp["w_in"]
    y_r, new_wkv, new_shift = rwkv7_mix(proj[..., :RWKV_PROJ], shift_row, wkv_state, lp)
    pa = proj[..., RWKV_PROJ:]
    q = rms_norm(pa[..., :ATTN_WIDTH].reshape(b, t, ATTN_HEADS, HEAD_DIM), lp["q_norm"])
    k = rms_norm(pa[..., ATTN_WIDTH:ATTN_WIDTH + KV_WIDTH].reshape(b, t, KV_HEADS, HEAD_DIM), lp["k_norm"])
    v = pa[..., ATTN_WIDTH + KV_WIDTH:].reshape(b, t, KV_HEADS, HEAD_DIM)
    if win_k is None:
        y_a = swa_prompt(q, k, v, rel_bias, lp["sinks"])
        new_k, new_v = k[:, -WINDOW:], v[:, -WINDOW:]
    else:
        y_a = swa_sample(q, k, v, win_k, win_v, rel_bias, lp["sinks"])
        new_k, new_v = k, v
    x = x + gt1[:, None, :] * (jnp.concatenate([y_r, y_a], axis=-1) @ lp["w_out"])
    h2 = ada_modulate(x, lp["norm2_g"], sh2, sc2)
    x = x + gt2[:, None, :] * moe_ffn(h2.reshape(b * t, D_MODEL), lp).reshape(b, t, D_MODEL)
    return x, (new_shift, new_wkv.astype(x.dtype), new_k, new_v)


def setup_inputs(seed: int = 0) -> dict:
    key = jax.random.key(seed)
    keys = jax.random.split(key, 40)
    counter = [0]

    def nrm(shape, scale):
        counter[0] += 1
        return jax.random.normal(keys[counter[0] - 1], shape, jnp.float32) * scale

    def unif(shape, lo, hi):
        counter[0] += 1
        return jax.random.uniform(keys[counter[0] - 1], shape, jnp.float32, lo, hi)

    win_rows = min(WINDOW, PAST_LEN)
    L = DEPTH
    return {
        "x_prompt": nrm((BATCH, SEQ, D_MODEL), 1.0),
        "x_sample": nrm((DEC_BATCH, DEC_SEQ, D_MODEL), 1.0),
        "state_shift": nrm((L, DEC_BATCH, RWKV_PROJ), 1.0),
        "state_wkv": nrm((L, DEC_BATCH, RWKV_HEADS, RWKV_HEAD_DIM, RWKV_HEAD_DIM), 0.3),
        "cache_win_k": nrm((L, DEC_BATCH, win_rows, KV_HEADS, HEAD_DIM), 1.0),
        "cache_win_v": nrm((L, DEC_BATCH, win_rows, KV_HEADS, HEAD_DIM), 1.0),
        "c_prompt": nrm((BATCH, D_MODEL), 1.0),
        "c_sample": nrm((DEC_BATCH, D_MODEL), 1.0),
        "rel_bias": nrm((REL_BUCKETS, ATTN_HEADS), 0.5),
        "norm1_g": 1.0 + nrm((L, D_MODEL), 0.05),
        "norm2_g": 1.0 + nrm((L, D_MODEL), 0.05),
        "w_ada": nrm((L, D_MODEL, 6 * D_MODEL), 0.5 * D_MODEL ** -0.5),
        "b_ada": nrm((L, 6 * D_MODEL), 0.02),
        "w_in": nrm((L, D_MODEL, PROJ), D_MODEL ** -0.5),
        "w_out": nrm((L, D_MIX, D_MODEL), D_MIX ** -0.5),
        "rwkv_mu": unif((L, RWKV_PROJ), 0.0, 1.0),
        "rwkv_w0": unif((L, RWKV_WIDTH), -3.0, 0.5),
        "rwkv_w_up": nrm((L, DECAY_RANK, RWKV_WIDTH), 0.5 * DECAY_RANK ** -0.5),
        "rwkv_a0": nrm((L, RWKV_WIDTH), 0.5),
        "rwkv_a_up": nrm((L, ICLR_RANK, RWKV_WIDTH), ICLR_RANK ** -0.5),
        "rwkv_g_up": nrm((L, GATE_RANK, RWKV_WIDTH), GATE_RANK ** -0.5),
        "rwkv_k_k": 0.85 + nrm((L, RWKV_WIDTH), 0.05),
        "rwkv_k_a": 1.0 + nrm((L, RWKV_WIDTH), 0.05),
        "rwkv_r_k": nrm((L, RWKV_HEADS, RWKV_HEAD_DIM), 0.1),
        "rwkv_ln_g": 1.0 + nrm((L, RWKV_WIDTH), 0.05),
        "rwkv_ln_b": nrm((L, RWKV_WIDTH), 0.02),
        "q_norm": 1.0 + nrm((L, HEAD_DIM), 0.05),
        "k_norm": 1.0 + nrm((L, HEAD_DIM), 0.05),
        "sinks": nrm((L, ATTN_HEADS), 0.5),
        "router_w": nrm((L, D_MODEL, N_EXPERTS), D_MODEL ** -0.5),
        "router_b": nrm((L, N_EXPERTS), 0.01),
        "exp_w_up": nrm((L, N_EXPERTS, D_MODEL, 2 * D_FF), D_MODEL ** -0.5),
        "exp_b_up": nrm((L, N_EXPERTS, 2 * D_FF), 0.01),
        "exp_w_down": nrm((L, N_EXPERTS, D_FF, D_MODEL), D_FF ** -0.5),
        "exp_b_down": nrm((L, N_EXPERTS, D_MODEL), 0.01),
    }


def reference(x_prompt, x_sample, state_shift, state_wkv, cache_win_k, cache_win_v, c_prompt, c_sample,
              rel_bias, norm1_g, norm2_g, w_ada, b_ada, w_in, w_out,
              rwkv_mu, rwkv_w0, rwkv_w_up, rwkv_a0, rwkv_a_up, rwkv_g_up, rwkv_k_k, rwkv_k_a, rwkv_r_k,
              rwkv_ln_g, rwkv_ln_b, q_norm, k_norm, sinks,
              router_w, router_b, exp_w_up, exp_b_up, exp_w_down, exp_b_down):
    xp, xs = x_prompt, x_sample
    bp = xp.shape[0]
    st_p, st_s = [], []
    for layer in range(DEPTH):
        lp = dict(norm1_g=norm1_g[layer], norm2_g=norm2_g[layer], w_ada=w_ada[layer], b_ada=b_ada[layer],
                  w_in=w_in[layer], w_out=w_out[layer],
                  rwkv_mu=rwkv_mu[layer], rwkv_w0=rwkv_w0[layer], rwkv_w_up=rwkv_w_up[layer],
                  rwkv_a0=rwkv_a0[layer], rwkv_a_up=rwkv_a_up[layer], rwkv_g_up=rwkv_g_up[layer],
                  rwkv_k_k=rwkv_k_k[layer], rwkv_k_a=rwkv_k_a[layer], rwkv_r_k=rwkv_r_k[layer],
                  rwkv_ln_g=rwkv_ln_g[layer], rwkv_ln_b=rwkv_ln_b[layer],
                  q_norm=q_norm[layer], k_norm=k_norm[layer], sinks=sinks[layer],
                  router_w=router_w[layer], router_b=router_b[layer],
                  exp_w_up=exp_w_up[layer], exp_b_up=exp_b_up[layer],
                  exp_w_down=exp_w_down[layer], exp_b_down=exp_b_down[layer])
        zero_shift = jnp.zeros((bp, RWKV_PROJ), xp.dtype)
        zero_wkv = jnp.zeros((bp, RWKV_HEADS, RWKV_HEAD_DIM, RWKV_HEAD_DIM), jnp.float32)
        xp, sp = trunk_layer(xp, c_prompt, zero_shift, zero_wkv, None, None, lp, rel_bias)
        xs, ss = trunk_layer(xs, c_sample, state_shift[layer], state_wkv[layer],
                             cache_win_k[layer], cache_win_v[layer], lp, rel_bias)
        st_p.append(sp)
        st_s.append(ss)
    new_shift_p, new_wkv_p, new_k_p, new_v_p = (jnp.stack([s[i] for s in st_p]) for i in range(4))
    new_shift_s, new_wkv_s, new_k_s, new_v_s = (jnp.stack([s[i] for s in st_s]) for i in range(4))
    return (xp, xs, new_shift_p, new_wkv_p, new_k_p, new_v_p, new_shift_s, new_wkv_s, new_k_s, new_v_s)
```

```python
import functools
import math

import jax
import jax.numpy as jnp
from jax import lax
from jax.experimental import pallas as pl
from jax.experimental.pallas import tpu as pltpu

F32 = jnp.float32
BF16 = jnp.bfloat16
I32 = jnp.int32

HEAD = 64
RWKV_HEADS = 8
RWKV_WIDTH = RWKV_HEADS * HEAD
DECAY_RANK = 64
ICLR_RANK = 64
GATE_RANK = 128
RWKV_PROJ = 3 * RWKV_WIDTH + DECAY_RANK + ICLR_RANK + GATE_RANK
ATTN_HEADS = 8
KV_HEADS = 2
ATTN_WIDTH = ATTN_HEADS * HEAD
KV_WIDTH = KV_HEADS * HEAD
ATTN_PROJ = ATTN_WIDTH + 2 * KV_WIDTH
WINDOW = 128
CHUNK = 64
REL_BUCKETS = 32
REL_MAX_DIST = 128
N_EXPERTS = 32
TOP_K = 4
GN_EPS = 64e-5
RMS_EPS = 1e-6
NEG_INF = -1e30
ATTN_SCALE = HEAD ** -0.5
SWIGLU_ALPHA = 1.702
SWIGLU_LIMIT = 7.0

LANES = 128
GROUP = 256
VMEM_LIMIT = 56 * 1024 * 1024

TM_PROJ = 512
RWKV_SUB = 4
SWA_QC = 4
TM_FFN = 512
TK_DISPATCH = 1024
TK_COMBINE = 128


def _cparams(*sem):
    return pltpu.CompilerParams(dimension_semantics=sem, vmem_limit_bytes=VMEM_LIMIT)


def _dot(a, b, nt=False):
    dims = (((1,), (1,)), ((), ())) if nt else (((1,), (0,)), ((), ()))
    return lax.dot_general(a, b, dims, preferred_element_type=F32)


def _split(x):
    hi = x.astype(BF16)
    lo = (x - hi.astype(F32)).astype(BF16)
    return hi, lo


def _mm(a, b, passes=1, nt=False):
    if passes == 1:
        return _dot(a.astype(BF16), b.astype(BF16), nt)
    ah, al = _split(a)
    bh, bl = _split(b)
    return _dot(ah, bh, nt) + _dot(al, bh, nt) + _dot(ah, bl, nt)


def _mm_exact_rhs(a, b16, nt=False):
    ah, al = _split(a)
    return _dot(ah, b16, nt) + _dot(al, b16, nt)


def _iota(shape, dim):
    return lax.broadcasted_iota(I32, shape, dim)


def _head_ones(n):
    r = lax.shift_right_logical(_iota((n, n), 0), 6)
    c = lax.shift_right_logical(_iota((n, n), 1), 6)
    return (r == c).astype(BF16)


def _head_sum(x):
    width = x.shape[1]
    if width <= GROUP:
        return _mm_exact_rhs(x, _head_ones(width))
    ones = _head_ones(GROUP)
    parts = [_mm_exact_rhs(x[:, g:g + GROUP], ones) for g in range(0, width, GROUP)]
    return jnp.concatenate(parts, axis=1)


def _rms(x, eps):
    return x * lax.rsqrt(jnp.mean(x * x, axis=-1, keepdims=True) + eps)


def _mod_kernel(c_ref, w_ref, b_ref, o_ref):
    c = c_ref[...]
    s = c * jax.nn.sigmoid(c)
    o_ref[...] = _mm(s, w_ref[...], passes=3) + b_ref[...]


def _modulation(c_all, w_ada, b_ada):
    rows, d = c_all.shape
    n = w_ada.shape[1]
    tn = 1536
    return pl.pallas_call(
        _mod_kernel,
        grid=(n // tn,),
        in_specs=[pl.BlockSpec((rows, d), lambda j: (0, 0)),
                  pl.BlockSpec((d, tn), lambda j: (0, j)),
                  pl.BlockSpec((1, tn), lambda j: (0, j))],
        out_specs=pl.BlockSpec((rows, tn), lambda j: (0, j)),
        out_shape=jax.ShapeDtypeStruct((rows, n), F32),
        compiler_params=_cparams("arbitrary"),
        name="modulation",
    )(c_all, w_ada, b_ada.reshape(1, n))


def _inproj_kernel(x_ref, g_ref, sh_ref, sc_ref, w_ref, pr_ref, pa_ref):
    x = x_ref[...]
    h = _rms(x, RMS_EPS) * g_ref[...] * (1.0 + sc_ref[...]) + sh_ref[...]
    p = _dot(h.astype(BF16), w_ref[...])
    pr_ref[...] = p[:, :RWKV_PROJ]
    pa_ref[...] = p[:, RWKV_PROJ:]


def _inproj(x3, gain, sh3, sc3, w_in16, tm):
    g, r, d = x3.shape
    mr = sh3.shape[1]
    mrow = (lambda b, i: (b, 0, 0)) if mr == 1 else (lambda b, i: (b, i, 0))
    mblk = 1 if mr == 1 else tm
    return pl.pallas_call(
        _inproj_kernel,
        grid=(g, r // tm),
        in_specs=[pl.BlockSpec((None, tm, d), lambda b, i: (b, i, 0)),
                  pl.BlockSpec((1, d), lambda b, i: (0, 0)),
                  pl.BlockSpec((None, mblk, d), mrow),
                  pl.BlockSpec((None, mblk, d), mrow),
                  pl.BlockSpec(w_in16.shape, lambda b, i: (0, 0))],
        out_specs=[pl.BlockSpec((None, tm, RWKV_PROJ), lambda b, i: (b, i, 0)),
                   pl.BlockSpec((None, tm, ATTN_PROJ), lambda b, i: (b, i, 0))],
        out_shape=[jax.ShapeDtypeStruct((g, r, RWKV_PROJ), F32),
                   jax.ShapeDtypeStruct((g, r, ATTN_PROJ), F32)],
        compiler_params=_cparams("arbitrary", "arbitrary"),
        name="inproj",
    )(x3, gain.reshape(1, d), sh3, sc3, w_in16)


def _softplus(x):
    return jnp.maximum(x, 0.0) + jnp.log1p(jnp.exp(-jnp.abs(x)))


def _rwkv_kernel(p_ref, prev0_ref, s0_ref, mu_ref, w0_ref, wwa_ref, a0_ref, gup_ref, kk_ref, ka_ref,
                 rk_ref, lng_ref, lnb_ref, y_ref, sout_ref, prev_sc, s_sc, *, nsub, t_valid):
    tc = nsub * CHUNK
    c = pl.program_id(1)

    @pl.when(c == 0)
    def _():
        prev_sc[...] = prev0_ref[...]
        s_sc[...] = s0_ref[...]

    p = p_ref[...]
    row = _iota((tc, 1), 0)
    p_prev = jnp.where(row == 0, prev_sc[...], pltpu.roll(p, 1, axis=0))
    prev_sc[...] = p_ref[tc - 1:tc, :]
    xp = p + (p_prev - p) * mu_ref[...]

    w3 = 3 * RWKV_WIDTH
    r = xp[:, 0:RWKV_WIDTH]
    k = xp[:, RWKV_WIDTH:2 * RWKV_WIDTH]
    v = xp[:, 2 * RWKV_WIDTH:w3]
    z = xp[:, w3:w3 + DECAY_RANK + ICLR_RANK]
    gd = xp[:, w3 + DECAY_RANK + ICLR_RANK:]
    zl = _iota((1, DECAY_RANK + ICLR_RANK), 1)
    zz = jnp.where(zl < DECAY_RANK, jnp.tanh(z), z)
    wa = _mm(zz, wwa_ref[...], passes=3)
    w_log = -_softplus(-(w0_ref[...] + wa[:, :RWKV_WIDTH])) - 0.5
    lw = -jnp.exp(w_log)
    a = jax.nn.sigmoid(a0_ref[...] + wa[:, RWKV_WIDTH:])
    g = _mm(jax.nn.sigmoid(gd), gup_ref[...])
    kk = k * kk_ref[...]
    kk = kk * lax.rsqrt(jnp.maximum(_head_sum(kk * kk), 1e-24))
    k2 = k * (1.0 + (a - 1.0) * ka_ref[...])
    bvec = kk * a
    if t_valid is not None:
        ok = (c * tc + row) < t_valid
        lw = jnp.where(ok, lw, 0.0)
        kk = jnp.where(ok, kk, 0.0)
        bvec = jnp.where(ok, bvec, 0.0)
        k2u = jnp.where(ok, k2, 0.0)
        vu = jnp.where(ok, v, 0.0)
    else:
        k2u, vu = k2, v

    tril16 = (_iota((CHUNK, CHUNK), 0) >= _iota((CHUNK, CHUNK), 1)).astype(BF16)
    bd_mask = (lax.shift_right_logical(_iota((GROUP, GROUP), 0), 6)
               == lax.shift_right_logical(_iota((GROUP, GROUP), 1), 6))
    t_idx = _iota((CHUNK, GROUP), 0)
    s_idx = jnp.bitwise_and(_iota((CHUNK, GROUP), 1), CHUNK - 1)
    strict = t_idx > s_idx
    incl = t_idx >= s_idx
    eye = (t_idx == s_idx).astype(F32)

    def bd(x):
        return jnp.where(bd_mask, jnp.concatenate([x, x, x, x], axis=0), 0.0)

    ys = []
    for j in range(nsub):
        sl = slice(j * CHUNK, (j + 1) * CHUNK)
        lwj = lw[sl]
        h1 = lwj.astype(BF16)
        r1 = lwj - h1.astype(F32)
        h2 = r1.astype(BF16)
        h3 = (r1 - h2.astype(F32)).astype(BF16)
        cum = _dot(tril16, h1) + _dot(tril16, h2) + _dot(tril16, h3)
        e_cum = jnp.exp(cum)
        e_inv = jnp.exp(-cum)
        e_prev = jnp.exp(cum - lwj)
        w_last = e_cum[CHUNK - 1:CHUNK, :]
        at = -kk[sl] * e_prev
        bt = bvec[sl] * e_inv
        kt = k2u[sl] * e_inv
        rt = r[sl] * e_cum
        bw = bt * w_last
        kw = kt * w_last
        vj = vu[sl]
        yg = []
        for gi in range(RWKV_WIDTH // GROUP):
            gs = slice(gi * GROUP, (gi + 1) * GROUP)
            q2 = jnp.concatenate([at[:, gs], rt[:, gs]], axis=0)
            sb = _mm(q2, bd(bt[:, gs]), nt=True)
            sk = _mm(q2, bd(kt[:, gs]), nt=True)
            a_ab = jnp.where(strict, sb[:CHUNK], 0.0)
            a_ak = jnp.where(strict, sk[:CHUNK], 0.0)
            a_rb = jnp.where(incl, sb[CHUNK:], 0.0)
            a_rk = jnp.where(incl, sk[CHUNK:], 0.0)
            x = a_ab
            tinv = eye + a_ab
            for _ in range(5):
                x = _mm(x, bd(x), passes=3)
                tinv = tinv + _mm(tinv, bd(x), passes=3)
            sg = s_sc[gi]
            q2s = _mm(q2, sg, passes=3, nt=True)
            vbd = bd(vj[:, gs])
            rhs = q2s[:CHUNK] + _mm(a_ak, vbd)
            u = _mm(tinv, bd(rhs), passes=3)
            yg.append(q2s[CHUNK:] + _mm(a_rb, bd(u)) + _mm(a_rk, vbd))
            uv = jnp.concatenate([u, vj[:, gs]], axis=0)
            bk = jnp.concatenate([bw[:, gs], kw[:, gs]], axis=0)
            upd = _mm(uv.T, bk)
            s_sc[gi] = sg * w_last[:, gs] + jnp.where(bd_mask, upd, 0.0)
        ys.append(jnp.concatenate(yg, axis=1))
    y = jnp.concatenate(ys, axis=0) if nsub > 1 else ys[0]

    inv_n = 1.0 / HEAD
    yc = y - _head_sum(y) * inv_n
    yn = yc * lax.rsqrt(_head_sum(yc * yc) * inv_n + GN_EPS)
    yn = yn * lng_ref[...] + lnb_ref[...]
    bonus = _head_sum(r * k2 * rk_ref[...]) * v
    y_ref[...] = (yn + bonus) * g
    sout_ref[...] = s_sc[...]


def _rwkv(p_r, prev0, s0_bd, lp, nsub, t_valid):
    b, t, _ = p_r.shape
    tc = nsub * CHUNK
    ng = RWKV_WIDTH // GROUP
    row = lambda a: a.reshape(1, -1)
    zeros = jnp.zeros((DECAY_RANK, RWKV_WIDTH), F32)
    wwa = jnp.concatenate([jnp.concatenate([lp["rwkv_w_up"], zeros], axis=1),
                           jnp.concatenate([zeros, lp["rwkv_a_up"]], axis=1)], axis=0)
    const = lambda shape: pl.BlockSpec(shape, lambda bi, ci: (0,) * len(shape))
    kern = functools.partial(_rwkv_kernel, nsub=nsub, t_valid=t_valid)
    return pl.pallas_call(
        kern,
        grid=(b, t // tc),
        in_specs=[pl.BlockSpec((None, tc, RWKV_PROJ), lambda bi, ci: (bi, ci, 0)),
                  pl.BlockSpec((None, 1, RWKV_PROJ), lambda bi, ci: (bi, 0, 0)),
                  pl.BlockSpec((None, ng, GROUP, GROUP), lambda bi, ci: (bi, 0, 0, 0)),
                  const((1, RWKV_PROJ)), const((1, RWKV_WIDTH)),
                  const((DECAY_RANK + ICLR_RANK, 2 * RWKV_WIDTH)), const((1, RWKV_WIDTH)),
                  const((GATE_RANK, RWKV_WIDTH)), const((1, RWKV_WIDTH)), const((1, RWKV_WIDTH)),
                  const((1, RWKV_WIDTH)), const((1, RWKV_WIDTH)), const((1, RWKV_WIDTH))],
        out_specs=[pl.BlockSpec((None, tc, RWKV_WIDTH), lambda bi, ci: (bi, ci, 0)),
                   pl.BlockSpec((None, ng, GROUP, GROUP), lambda bi, ci: (bi, 0, 0, 0))],
        out_shape=[jax.ShapeDtypeStruct((b, t, RWKV_WIDTH), F32),
                   jax.ShapeDtypeStruct((b, ng, GROUP, GROUP), F32)],
        scratch_shapes=[pltpu.VMEM((1, RWKV_PROJ), F32), pltpu.VMEM((ng, GROUP, GROUP), F32)],
        compiler_params=_cparams("arbitrary", "arbitrary"),
        name="rwkv7",
    )(p_r, prev0, s0_bd, row(lp["rwkv_mu"]), row(lp["rwkv_w0"]), wwa, row(lp["rwkv_a0"]),
      lp["rwkv_g_up"], row(lp["rwkv_k_k"]), row(lp["rwkv_k_a"]), row(lp["rwkv_r_k"]),
      row(lp["rwkv_ln_g"]), row(lp["rwkv_ln_b"]))


def _state_to_bd(s):
    b = s.shape[0]
    hg = GROUP // HEAD
    s5 = s.reshape(b, RWKV_HEADS // hg, hg, HEAD, HEAD)
    out = jnp.einsum("bghvk,hj->bghvjk", s5, jnp.eye(hg, dtype=s.dtype))
    return out.reshape(b, RWKV_HEADS // hg, GROUP, GROUP)


def _state_from_bd(s_bd):
    b = s_bd.shape[0]
    hg = GROUP // HEAD
    s6 = s_bd.reshape(b, RWKV_HEADS // hg, hg, HEAD, hg, HEAD)
    out = jnp.einsum("bghvjk,hj->bghvk", s6, jnp.eye(hg, dtype=s_bd.dtype))
    return out.reshape(b, RWKV_HEADS, HEAD, HEAD)


def _swa_kernel(tab_ref, sink_ref, pa_ref, ck_ref, cv_ref, bkt_ref, qg_ref, kg_ref,
                ya_ref, kn_ref, kall, vall, bias_sc, *, cq, nqc, has_cache):
    tq = cq * nqc
    band = WINDOW + cq
    bi = pl.program_id(0)
    i = pl.program_id(1)
    lane = _iota((1, LANES), 1)
    low = lane < HEAD

    @pl.when(jnp.logical_and(bi == 0, i == 0))
    def _():
        bkt = bkt_ref[...]
        for h in range(ATTN_HEADS):
            def body(j, acc, h=h):
                return acc + jnp.where(bkt == j, tab_ref[j, h], 0.0)
            bh = lax.fori_loop(0, REL_BUCKETS, body, jnp.zeros((cq, band), F32))
            bias_sc[h // 2, (h % 2) * cq:(h % 2 + 1) * cq, :] = bh

    @pl.when(i == 0)
    def _():
        kall[0:WINDOW, :] = ck_ref[...]
        vall[0:WINDOW, :] = cv_ref[...]

    pa = pa_ref[...]
    q = pa[:, :ATTN_WIDTH]
    k = pa[:, ATTN_WIDTH:ATTN_WIDTH + KV_WIDTH]
    v = pa[:, ATTN_WIDTH + KV_WIDTH:]
    inv_n = 1.0 / HEAD
    qn = q * lax.rsqrt(_head_sum(q * q) * inv_n + RMS_EPS) * qg_ref[...]
    kn = k * lax.rsqrt(_head_sum(k * k) * inv_n + RMS_EPS) * kg_ref[...]
    kn_ref[...] = kn
    kall[WINDOW:WINDOW + tq, :] = kn
    vall[WINDOW:WINDOW + tq, :] = v

    row2 = _iota((2 * cq, 1), 0)
    col = _iota((1, band), 1)
    thr = jnp.where(i == 0, WINDOW, 0)
    for qc in range(nqc):
        kb = kall[qc * cq:qc * cq + band, :]
        vb = vall[qc * cq:qc * cq + band, :]
        kr = pltpu.roll(kb, HEAD, axis=1)
        vr = pltpu.roll(vb, HEAD, axis=1)
        kd = [jnp.where(low, kb, kr).astype(BF16), jnp.where(low, kr, kb).astype(BF16)]
        vd = [jnp.where(low, vb, vr).astype(BF16), jnp.where(low, vr, vb).astype(BF16)]
        for pi in range(ATTN_HEADS // 2):
            gk = (2 * pi) // (ATTN_HEADS // KV_HEADS)
            qp = qn[qc * cq:(qc + 1) * cq, pi * LANES:(pi + 1) * LANES]
            qs = jnp.concatenate([jnp.where(low, qp, 0.0), jnp.where(low, 0.0, qp)], axis=0)
            s = _dot(qs.astype(BF16), kd[gk], nt=True) * ATTN_SCALE + bias_sc[pi]
            if (not has_cache) and qc * cq < WINDOW:
                s = jnp.where((col + qc * cq) < thr, NEG_INF, s)
            sink = jnp.where(row2 < cq, sink_ref[0, 2 * pi], sink_ref[0, 2 * pi + 1])
            m = jnp.maximum(jnp.max(s, axis=-1, keepdims=True), sink)
            e = jnp.exp(s - m)
            den = jnp.sum(e, axis=-1, keepdims=True) + jnp.exp(sink - m)
            o = _dot((e / den).astype(BF16), vd[gk])
            ya_ref[qc * cq:(qc + 1) * cq, pi * LANES:(pi + 1) * LANES] = jnp.where(low, o[:cq], o[cq:])

    if tq >= WINDOW:
        knext = kall[tq:tq + WINDOW, :]
        vnext = vall[tq:tq + WINDOW, :]
        kall[0:WINDOW, :] = knext
        vall[0:WINDOW, :] = vnext


def _t5_bucket(rel):
    half = REL_BUCKETS // 2
    max_exact = half // 2
    n = jnp.abs(rel)
    log_ratio = jnp.log(jnp.maximum(n, 1).astype(F32) / max_exact) / math.log(REL_MAX_DIST / max_exact)
    large = jnp.minimum(max_exact + (log_ratio * (half - max_exact)).astype(I32), half - 1)
    return jnp.where(rel > 0, half, 0) + jnp.where(n < max_exact, n, large)


def _swa(p_a, cache_k, cache_v, rel_table, sinks, q_norm, k_norm, cq, nqc, has_cache):
    b, t, _ = p_a.shape
    tq = cq * nqc
    band = WINDOW + cq
    rel = (jnp.arange(band) - WINDOW)[None, :] - jnp.arange(cq)[:, None]
    bkt = _t5_bucket(rel).astype(I32)
    qg = jnp.tile(q_norm, ATTN_HEADS).reshape(1, ATTN_WIDTH)
    kg = jnp.tile(k_norm, KV_HEADS).reshape(1, KV_WIDTH)
    kern = functools.partial(_swa_kernel, cq=cq, nqc=nqc, has_cache=has_cache)
    smem = pl.BlockSpec(memory_space=pltpu.SMEM)
    return pl.pallas_call(
        kern,
        grid=(b, t // tq),
        in_specs=[smem, smem,
                  pl.BlockSpec((None, tq, ATTN_PROJ), lambda bi, i: (bi, i, 0)),
                  pl.BlockSpec((None, WINDOW, KV_WIDTH), lambda bi, i: (bi, 0, 0)),
                  pl.BlockSpec((None, WINDOW, KV_WIDTH), lambda bi, i: (bi, 0, 0)),
                  pl.BlockSpec((cq, band), lambda bi, i: (0, 0)),
                  pl.BlockSpec((1, ATTN_WIDTH), lambda bi, i: (0, 0)),
                  pl.BlockSpec((1, KV_WIDTH), lambda bi, i: (0, 0))],
        out_specs=[pl.BlockSpec((None, tq, ATTN_WIDTH), lambda bi, i: (bi, i, 0)),
                   pl.BlockSpec((None, tq, KV_WIDTH), lambda bi, i: (bi, i, 0))],
        out_shape=[jax.ShapeDtypeStruct((b, t, ATTN_WIDTH), F32),
                   jax.ShapeDtypeStruct((b, t, KV_WIDTH), F32)],
        scratch_shapes=[pltpu.VMEM((WINDOW + tq, KV_WIDTH), F32),
                        pltpu.VMEM((WINDOW + tq, KV_WIDTH), F32),
                        pltpu.VMEM((ATTN_HEADS // 2, 2 * cq, band), F32)],
        compiler_params=_cparams("arbitrary", "arbitrary"),
        name="swa",
    )(rel_table, sinks.reshape(1, ATTN_HEADS), p_a, cache_k, cache_v, bkt, qg, kg)


def _mid_kernel(yr_ref, ya_ref, x_ref, gt1_ref, sh2_ref, sc2_ref, g2_ref, wor_ref, woa_ref, wrt_ref,
                rb_ref, base0_ref, x1_ref, h2_ref, te_ref, gate_ref, pos_ref, cnt_ref, base_sc, *, tm):
    @pl.when(jnp.logical_and(pl.program_id(0) == 0, pl.program_id(1) == 0))
    def _():
        base_sc[...] = base0_ref[...]

    mix = _dot(yr_ref[...].astype(BF16), wor_ref[...]) + _dot(ya_ref[...].astype(BF16), woa_ref[...])
    x1 = x_ref[...] + gt1_ref[...] * mix
    x1_ref[...] = x1
    h2 = _rms(x1, RMS_EPS) * g2_ref[...] * (1.0 + sc2_ref[...]) + sh2_ref[...]
    h2_ref[...] = h2

    logits = _mm(wrt_ref[...], h2, passes=3, nt=True) + rb_ref[...]
    eidx = _iota((N_EXPERTS, tm), 0)
    work = logits
    tops, hots = [], []
    for j in range(TOP_K):
        m = jnp.max(work, axis=0, keepdims=True)
        idx = jnp.min(jnp.where(work == m, eidx, N_EXPERTS), axis=0, keepdims=True)
        hot = eidx == idx
        work = jnp.where(hot, -jnp.inf, work)
        tops.append(m)
        hots.append(hot)
        te_ref[j:j + 1, :] = idx
    exps = [jnp.exp(tj - tops[0]) for tj in tops]
    den = exps[0] + exps[1] + exps[2] + exps[3]
    for j in range(TOP_K):
        gate_ref[j:j + 1, :] = exps[j] / den

    member = (hots[0] | hots[1] | hots[2] | hots[3]).astype(BF16)
    upper = (_iota((tm, tm), 0) < _iota((tm, tm), 1)).astype(BF16)
    before = _dot(member, upper) + base_sc[...]
    for j in range(TOP_K):
        pos_ref[j:j + 1, :] = jnp.sum(jnp.where(hots[j], before, 0.0), axis=0, keepdims=True).astype(I32)
    base_sc[...] = base_sc[...] + jnp.sum(member.astype(F32), axis=1, keepdims=True)
    cnt_ref[...] = base_sc[...]


def _mid(y_r, y_a, x3, gt1, sh2, sc2, gain2, wo_r16, wo_a16, wr_t, rb, base0, tm):
    g, r, d = x3.shape
    n = g * r
    per = r // tm
    mr = gt1.shape[1]
    mrow = (lambda b, i: (b, 0, 0)) if mr == 1 else (lambda b, i: (b, i, 0))
    mblk = 1 if mr == 1 else tm
    tok = lambda w: pl.BlockSpec((None, tm, w), lambda b, i: (b, i, 0))
    mod = pl.BlockSpec((None, mblk, d), mrow)
    const = lambda shape: pl.BlockSpec(shape, lambda b, i: (0,) * len(shape))
    lane_out = pl.BlockSpec((TOP_K, tm), lambda b, i: (0, b * per + i))
    kern = functools.partial(_mid_kernel, tm=tm)
    return pl.pallas_call(
        kern,
        grid=(g, per),
        in_specs=[tok(RWKV_WIDTH), tok(ATTN_WIDTH), tok(d), mod, mod, mod, const((1, d)),
                  const((RWKV_WIDTH, d)), const((ATTN_WIDTH, d)), const((N_EXPERTS, d)),
                  const((N_EXPERTS, 1)), const((N_EXPERTS, 1))],
        out_specs=[tok(d), tok(d), lane_out, lane_out, lane_out, const((N_EXPERTS, 1))],
        out_shape=[jax.ShapeDtypeStruct((g, r, d), F32), jax.ShapeDtypeStruct((g, r, d), F32),
                   jax.ShapeDtypeStruct((TOP_K, n), I32), jax.ShapeDtypeStruct((TOP_K, n), F32),
                   jax.ShapeDtypeStruct((TOP_K, n), I32), jax.ShapeDtypeStruct((N_EXPERTS, 1), F32)],
        scratch_shapes=[pltpu.VMEM((N_EXPERTS, 1), F32)],
        compiler_params=_cparams("arbitrary", "arbitrary"),
        name="mid",
    )(y_r, y_a, x3, gt1, sh2, sc2, gain2.reshape(1, d), wo_r16, wo_a16, wr_t, rb, base0)


def _dispatch_kernel(te_ref, pos_ref, pstart_ref, h2_ref, xs_in_ref, xs_ref, sem, *, tk):
    del xs_in_ref
    base = pl.program_id(0) * tk

    def row_copy(src_row, dst_row):
        return pltpu.make_async_copy(h2_ref.at[pl.ds(src_row, 1)], xs_ref.at[pl.ds(dst_row, 1)], sem)

    def body(t, carry):
        for j in range(TOP_K):
            dst = pstart_ref[te_ref[j, t]] + pos_ref[j, t]
            row_copy(base + t, dst).start()
        return carry

    lax.fori_loop(0, tk, body, 0)
    for _ in range(TOP_K):
        pltpu.make_async_copy(h2_ref.at[pl.ds(0, tk)], xs_ref.at[pl.ds(0, tk)], sem).wait()


def _dispatch(te, pos, pstart, h2, xs, tk):
    n, d = h2.shape
    smem_blk = pl.BlockSpec((TOP_K, tk), lambda i: (0, i), memory_space=pltpu.SMEM)
    kern = functools.partial(_dispatch_kernel, tk=tk)
    return pl.pallas_call(
        kern,
        grid=(n // tk,),
        in_specs=[smem_blk, smem_blk, pl.BlockSpec(memory_space=pltpu.SMEM),
                  pl.BlockSpec(memory_space=pl.ANY), pl.BlockSpec(memory_space=pl.ANY)],
        out_specs=pl.BlockSpec(memory_space=pl.ANY),
        out_shape=jax.ShapeDtypeStruct(xs.shape, xs.dtype),
        scratch_shapes=[pltpu.SemaphoreType.DMA(())],
        input_output_aliases={4: 0},
        compiler_params=_cparams("arbitrary"),
        name="dispatch",
    )(te, pos, pstart, h2, xs)


def _ffn_kernel(te_ref, nu_ref, xs_ref, wu_ref, bu_ref, wd_ref, bd_ref, ys_ref):
    del te_ref

    @pl.when(pl.program_id(0) < nu_ref[0])
    def _():
        dff = wd_ref.shape[0]
        u = _dot(xs_ref[...].astype(BF16), wu_ref[...]) + bu_ref[...]
        glu = jnp.minimum(u[:, :dff], SWIGLU_LIMIT)
        lin = jnp.clip(u[:, dff:], -SWIGLU_LIMIT, SWIGLU_LIMIT)
        act = glu * jax.nn.sigmoid(SWIGLU_ALPHA * glu) * (lin + 1.0)
        ys_ref[...] = _dot(act.astype(BF16), wd_ref[...]) + bd_ref[...]

    @pl.when(pl.program_id(0) >= nu_ref[0])
    def _():
        ys_ref[...] = jnp.zeros_like(ys_ref)


def _ffn(tile_e, n_used, xs, wu16, bu, wd16, bd_, tm):
    rows, d = xs.shape
    n_exp, _, n_up = wu16.shape
    dff = wd16.shape[1]
    row_blk = lambda i, te, nu: (jnp.minimum(i, nu[0] - 1), 0)
    exp_blk = lambda i, te, nu: (te[i], 0, 0)
    grid_spec = pltpu.PrefetchScalarGridSpec(
        num_scalar_prefetch=2,
        grid=(rows // tm,),
        in_specs=[pl.BlockSpec((tm, d), row_blk),
                  pl.BlockSpec((None, d, n_up), exp_blk),
                  pl.BlockSpec((None, 1, n_up), exp_blk),
                  pl.BlockSpec((None, dff, d), exp_blk),
                  pl.BlockSpec((None, 1, d), exp_blk)],
        out_specs=pl.BlockSpec((tm, d), lambda i, te, nu: (i, 0)),
    )
    return pl.pallas_call(
        _ffn_kernel,
        grid_spec=grid_spec,
        out_shape=jax.ShapeDtypeStruct((rows, d), F32),
        compiler_params=_cparams("arbitrary"),
        name="expert_ffn",
    )(tile_e, n_used, xs, wu16, bu.reshape(n_exp, 1, n_up), wd16, bd_.reshape(n_exp, 1, d))


def _combine_kernel(te_ref, pos_ref, ten_ref, posn_ref, pstart_ref, gate_ref, x1_ref, gt2_ref, ys_ref,
                    o_ref, buf, sem, *, tk):
    i = pl.program_id(1)
    n = pl.num_programs(1)
    flat = pl.program_id(0) * n + i
    total = pl.num_programs(0) * n
    slot = lax.rem(flat, 2)

    def issue(te_r, pos_r, s):
        def body(t, carry):
            for j in range(TOP_K):
                src = pstart_ref[te_r[j, t]] + pos_r[j, t]
                pltpu.make_async_copy(ys_ref.at[pl.ds(src, 1)], buf.at[s, j, pl.ds(t, 1)], sem.at[s]).start()
            return carry
        lax.fori_loop(0, tk, body, 0)

    @pl.when(flat == 0)
    def _():
        issue(te_ref, pos_ref, 0)

    @pl.when(flat + 1 < total)
    def _():
        issue(ten_ref, posn_ref, 1 - slot)

    for j in range(TOP_K):
        pltpu.make_async_copy(ys_ref.at[pl.ds(0, tk)], buf.at[slot, j], sem.at[slot]).wait()

    gate = gate_ref[...]
    acc = gate[:, 0:1] * buf[slot, 0]
    for j in range(1, TOP_K):
        acc = acc + gate[:, j:j + 1] * buf[slot, j]
    o_ref[...] = x1_ref[...] + gt2_ref[...] * acc


def _combine(te, pos, pstart, gates_t, x1, gt2, ys, tk):
    g, r, d = x1.shape
    per = r // tk
    last = g * per - 1
    mr = gt2.shape[1]
    mrow = (lambda b, i: (b, 0, 0)) if mr == 1 else (lambda b, i: (b, i, 0))
    mblk = 1 if mr == 1 else tk
    cur = pl.BlockSpec((TOP_K, tk), lambda b, i: (0, b * per + i), memory_space=pltpu.SMEM)
    nxt = pl.BlockSpec((TOP_K, tk), lambda b, i: (0, jnp.minimum(b * per + i + 1, last)),
                       memory_space=pltpu.SMEM)
    kern = functools.partial(_combine_kernel, tk=tk)
    return pl.pallas_call(
        kern,
        grid=(g, per),
        in_specs=[cur, cur, nxt, nxt, pl.BlockSpec(memory_space=pltpu.SMEM),
                  pl.BlockSpec((tk, TOP_K), lambda b, i: (b * per + i, 0)),
                  pl.BlockSpec((None, tk, d), lambda b, i: (b, i, 0)),
                  pl.BlockSpec((None, mblk, d), mrow),
                  pl.BlockSpec(memory_space=pl.ANY)],
        out_specs=pl.BlockSpec((None, tk, d), lambda b, i: (b, i, 0)),
        out_shape=jax.ShapeDtypeStruct((g, r, d), F32),
        scratch_shapes=[pltpu.VMEM((2, TOP_K, tk, d), F32), pltpu.SemaphoreType.DMA((2,))],
        compiler_params=_cparams("arbitrary", "arbitrary"),
        name="combine",
    )(te, pos, te, pos, pstart, gates_t, x1, gt2, ys)


def _tile_plan(counts, tm, n_tiles):
    cnt = counts.reshape(-1).astype(I32)
    tiles = (cnt + tm - 1) // tm
    tile_end = jnp.cumsum(tiles)
    pstart = ((tile_end - tiles) * tm).astype(I32)
    n_used = tile_end[-1:].astype(I32)
    tile_e = jnp.searchsorted(tile_end, jnp.arange(n_tiles, dtype=I32), side="right").astype(I32)
    last_e = jnp.searchsorted(tile_end, n_used[0] - 1, side="right").astype(I32)
    tile_e = jnp.where(jnp.arange(n_tiles) < n_used[0], tile_e, last_e)
    return pstart, jnp.minimum(tile_e, N_EXPERTS - 1), n_used


def kernel(x_prompt, x_sample, state_shift, state_wkv, cache_win_k, cache_win_v, c_prompt, c_sample, rel_bias, norm1_g, norm2_g, w_ada, b_ada, w_in, w_out, rwkv_mu, rwkv_w0, rwkv_w_up, rwkv_a0, rwkv_a_up, rwkv_g_up, rwkv_k_k, rwkv_k_a, rwkv_r_k, rwkv_ln_g, rwkv_ln_b, q_norm, k_norm, sinks, router_w, router_b, exp_w_up, exp_b_up, exp_w_down, exp_b_down):
    bp, tp, d = x_prompt.shape
    bs, ts, _ = x_sample.shape
    depth = w_in.shape[0]
    assert depth == 1, "single-layer trunk"
    layer = 0
    lp = dict(rwkv_mu=rwkv_mu[layer], rwkv_w0=rwkv_w0[layer], rwkv_w_up=rwkv_w_up[layer],
              rwkv_a0=rwkv_a0[layer], rwkv_a_up=rwkv_a_up[layer], rwkv_g_up=rwkv_g_up[layer],
              rwkv_k_k=rwkv_k_k[layer], rwkv_k_a=rwkv_k_a[layer], rwkv_r_k=rwkv_r_k[layer].reshape(-1),
              rwkv_ln_g=rwkv_ln_g[layer], rwkv_ln_b=rwkv_ln_b[layer])

    rows = bp + bs
    pad = (-rows) % 8
    c_all = jnp.concatenate([c_prompt, c_sample, jnp.zeros((pad, d), F32)], axis=0)
    mod = _modulation(c_all, w_ada[layer], b_ada[layer])
    sh1, sc1, gt1, sh2, sc2, gt2 = [mod[:, i * d:(i + 1) * d] for i in range(6)]
    pm = lambda m: m[:bp].reshape(bp, 1, d)
    sm = lambda m: jnp.repeat(m[bp:bp + bs], ts, axis=0).reshape(1, bs * ts, d)

    w_in16 = w_in[layer].astype(BF16)
    wo16 = w_out[layer].astype(BF16)
    wo_r16, wo_a16 = wo16[:RWKV_WIDTH], wo16[RWKV_WIDTH:]
    wr_t = router_w[layer].T
    rb = router_b[layer].reshape(N_EXPERTS, 1)
    dff = exp_w_down.shape[2]
    wu16 = jnp.concatenate([exp_w_up[layer][:, :, 0::2], exp_w_up[layer][:, :, 1::2]], axis=-1).astype(BF16)
    bu = jnp.concatenate([exp_b_up[layer][:, 0::2], exp_b_up[layer][:, 1::2]], axis=-1)
    wd16 = exp_w_down[layer].astype(BF16)

    pr_p, pa_p = _inproj(x_prompt, norm1_g[layer], pm(sh1), pm(sc1), w_in16, TM_PROJ)
    zero_prev = jnp.zeros((bp, 1, RWKV_PROJ), F32)
    zero_state = jnp.zeros((bp, RWKV_WIDTH // GROUP, GROUP, GROUP), F32)
    yr_p, s_p = _rwkv(pr_p, zero_prev, zero_state, lp, RWKV_SUB, None)
    zero_kv = jnp.zeros((bp, WINDOW, KV_WIDTH), F32)
    ya_p, kn_p = _swa(pa_p, zero_kv, zero_kv, rel_bias, sinks[layer], q_norm[layer], k_norm[layer],
                      CHUNK, SWA_QC, False)

    ns = bs * ts
    xs_flat = x_sample.reshape(1, ns, d)
    pr_s, pa_s = _inproj(xs_flat, norm1_g[layer], sm(sh1), sm(sc1), w_in16, ns)
    pr_s = pr_s.reshape(bs, ts, RWKV_PROJ)
    pa_s = pa_s.reshape(bs, ts, ATTN_PROJ)
    t_pad = -(-ts // CHUNK) * CHUNK
    pr_s_pad = jnp.pad(pr_s, ((0, 0), (0, t_pad - ts), (0, 0)))
    yr_s, s_s = _rwkv(pr_s_pad, state_shift[layer].reshape(bs, 1, RWKV_PROJ), _state_to_bd(state_wkv[layer]),
                      lp, t_pad // CHUNK, ts)
    yr_s = yr_s[:, :ts]
    ya_s, kn_s = _swa(pa_s, cache_win_k[layer].reshape(bs, WINDOW, KV_WIDTH),
                      cache_win_v[layer].reshape(bs, WINDOW, KV_WIDTH), rel_bias, sinks[layer],
                      q_norm[layer], k_norm[layer], ts, 1, True)

    base0 = jnp.zeros((N_EXPERTS, 1), F32)
    x1_p, h2_p, te_p, gate_p, pos_p, cnt_p = _mid(yr_p, ya_p, x_prompt, pm(gt1), pm(sh2), pm(sc2), norm2_g[layer],
                                                  wo_r16, wo_a16, wr_t, rb, base0, TM_PROJ)
    x1_s, h2_s, te_s, gate_s, pos_s, cnt = _mid(yr_s.reshape(1, ns, RWKV_WIDTH), ya_s.reshape(1, ns, ATTN_WIDTH),
                                                xs_flat, sm(gt1), sm(sh2), sm(sc2), norm2_g[layer],
                                                wo_r16, wo_a16, wr_t, rb, cnt_p, ns)

    n_p = bp * tp
    n_rows = (n_p + ns) * TOP_K + N_EXPERTS * (TM_FFN - 1)
    n_tiles = -(-n_rows // TM_FFN)
    pstart, tile_e, n_used = _tile_plan(cnt, TM_FFN, n_tiles)
    xs_buf = jnp.zeros((n_tiles * TM_FFN, d), F32)
    xs_buf = _dispatch(te_p, pos_p, pstart, h2_p.reshape(n_p, d), xs_buf, TK_DISPATCH)
    xs_buf = _dispatch(te_s, pos_s, pstart, h2_s.reshape(ns, d), xs_buf, ns)
    ys_buf = _ffn(tile_e, n_used, xs_buf, wu16, bu, wd16, exp_b_down[layer], TM_FFN)
    y_p = _combine(te_p, pos_p, pstart, gate_p.T, x1_p, pm(gt2), ys_buf, TK_COMBINE)
    y_s = _combine(te_s, pos_s, pstart, gate_s.T, x1_s, sm(gt2), ys_buf, TK_COMBINE)

    kv4 = lambda z, rows_: z.reshape(z.shape[0], rows_, KV_HEADS, HEAD)[None]
    v_p = pa_p[:, tp - WINDOW:, ATTN_WIDTH + KV_WIDTH:]
    v_s = pa_s[:, :, ATTN_WIDTH + KV_WIDTH:]
    return (y_p, y_s.reshape(bs, ts, d),
            pr_p[:, tp - 1][None], _state_from_bd(s_p)[None], kv4(kn_p[:, tp - WINDOW:], WINDOW), kv4(v_p, WINDOW),
            pr_s[:, ts - 1][None], _state_from_bd(s_s)[None], kv4(kn_s, ts), kv4(v_s, ts))
```

```python
import functools
import math

import jax
import jax.numpy as jnp
from jax import lax
from jax.experimental import pallas as pl
from jax.experimental.pallas import tpu as pltpu

F32 = jnp.float32
BF16 = jnp.bfloat16
I32 = jnp.int32

HEAD = 64
RWKV_HEADS = 8
RWKV_WIDTH = RWKV_HEADS * HEAD
DECAY_RANK = 64
ICLR_RANK = 64
GATE_RANK = 128
RWKV_PROJ = 3 * RWKV_WIDTH + DECAY_RANK + ICLR_RANK + GATE_RANK
ATTN_HEADS = 8
KV_HEADS = 2
ATTN_WIDTH = ATTN_HEADS * HEAD
KV_WIDTH = KV_HEADS * HEAD
ATTN_PROJ = ATTN_WIDTH + 2 * KV_WIDTH
WINDOW = 128
CHUNK = 64
REL_BUCKETS = 32
REL_MAX_DIST = 128
N_EXPERTS = 32
TOP_K = 4
GN_EPS = 64e-5
RMS_EPS = 1e-6
NEG_INF = -1e30
ATTN_SCALE = HEAD ** -0.5
SWIGLU_ALPHA = 1.702
SWIGLU_LIMIT = 7.0

LANES = 128
GROUP = 256
VMEM_LIMIT = 56 * 1024 * 1024

TM_PROJ = 512
RWKV_SUB = 4
SWA_QC = 4
TM_FFN = 512
TK_DISPATCH = 1024
TK_COMBINE = 128

NEUMANN_PASSES = 1
STATE_PASSES = 1
SOLVE_PASSES = 1


def _cparams(*sem):
    return pltpu.CompilerParams(dimension_semantics=sem, vmem_limit_bytes=VMEM_LIMIT)


def _dot(a, b, nt=False):
    dims = (((1,), (1,)), ((), ())) if nt else (((1,), (0,)), ((), ()))
    return lax.dot_general(a, b, dims, preferred_element_type=F32)


def _split(x):
    hi = x.astype(BF16)
    lo = (x - hi.astype(F32)).astype(BF16)
    return hi, lo


def _mm(a, b, passes=1, nt=False):
    if passes == 1:
        return _dot(a.astype(BF16), b.astype(BF16), nt)
    ah, al = _split(a)
    bh, bl = _split(b)
    return _dot(ah, bh, nt) + _dot(al, bh, nt) + _dot(ah, bl, nt)


def _mm_exact_rhs(a, b16, nt=False):
    ah, al = _split(a)
    return _dot(ah, b16, nt) + _dot(al, b16, nt)


def _iota(shape, dim):
    return lax.broadcasted_iota(I32, shape, dim)


def _head_ones(n):
    r = lax.shift_right_logical(_iota((n, n), 0), 6)
    c = lax.shift_right_logical(_iota((n, n), 1), 6)
    return (r == c).astype(BF16)


def _head_sum(x):
    width = x.shape[1]
    if width <= GROUP:
        return _mm_exact_rhs(x, _head_ones(width))
    ones = _head_ones(GROUP)
    parts = [_mm_exact_rhs(x[:, g:g + GROUP], ones) for g in range(0, width, GROUP)]
    return jnp.concatenate(parts, axis=1)


def _rms(x, eps):
    return x * lax.rsqrt(jnp.mean(x * x, axis=-1, keepdims=True) + eps)


def _mod_kernel(c_ref, w_ref, b_ref, o_ref):
    c = c_ref[...]
    s = c * jax.nn.sigmoid(c)
    o_ref[...] = _mm(s, w_ref[...], passes=3) + b_ref[...]


def _modulation(c_all, w_ada, b_ada):
    rows, d = c_all.shape
    n = w_ada.shape[1]
    tn = 1536
    return pl.pallas_call(
        _mod_kernel,
        grid=(n // tn,),
        in_specs=[pl.BlockSpec((rows, d), lambda j: (0, 0)),
                  pl.BlockSpec((d, tn), lambda j: (0, j)),
                  pl.BlockSpec((1, tn), lambda j: (0, j))],
        out_specs=pl.BlockSpec((rows, tn), lambda j: (0, j)),
        out_shape=jax.ShapeDtypeStruct((rows, n), F32),
        compiler_params=_cparams("arbitrary"),
        name="modulation",
    )(c_all, w_ada, b_ada.reshape(1, n))


def _inproj_kernel(x_ref, g_ref, sh_ref, sc_ref, w_ref, pr_ref, pa_ref):
    x = x_ref[...]
    h = _rms(x, RMS_EPS) * g_ref[...] * (1.0 + sc_ref[...]) + sh_ref[...]
    p = _dot(h.astype(BF16), w_ref[...])
    pr_ref[...] = p[:, :RWKV_PROJ]
    pa_ref[...] = p[:, RWKV_PROJ:]


def _inproj(x3, gain, sh3, sc3, w_in16, tm):
    g, r, d = x3.shape
    mr = sh3.shape[1]
    mrow = (lambda b, i: (b, 0, 0)) if mr == 1 else (lambda b, i: (b, i, 0))
    mblk = 1 if mr == 1 else tm
    return pl.pallas_call(
        _inproj_kernel,
        grid=(g, r // tm),
        in_specs=[pl.BlockSpec((None, tm, d), lambda b, i: (b, i, 0)),
                  pl.BlockSpec((1, d), lambda b, i: (0, 0)),
                  pl.BlockSpec((None, mblk, d), mrow),
                  pl.BlockSpec((None, mblk, d), mrow),
                  pl.BlockSpec(w_in16.shape, lambda b, i: (0, 0))],
        out_specs=[pl.BlockSpec((None, tm, RWKV_PROJ), lambda b, i: (b, i, 0)),
                   pl.BlockSpec((None, tm, ATTN_PROJ), lambda b, i: (b, i, 0))],
        out_shape=[jax.ShapeDtypeStruct((g, r, RWKV_PROJ), F32),
                   jax.ShapeDtypeStruct((g, r, ATTN_PROJ), F32)],
        compiler_params=_cparams("arbitrary", "arbitrary"),
        name="inproj",
    )(x3, gain.reshape(1, d), sh3, sc3, w_in16)


def _softplus(x):
    return jnp.maximum(x, 0.0) + jnp.log1p(jnp.exp(-jnp.abs(x)))


def _rwkv_kernel(p_ref, prev0_ref, s0_ref, mu_ref, w0_ref, wwa_ref, a0_ref, gup_ref, kk_ref, ka_ref,
                 rk_ref, lng_ref, lnb_ref, y_ref, sout_ref, prev_sc, s_sc, *, nsub, t_valid):
    tc = nsub * CHUNK
    c = pl.program_id(1)

    @pl.when(c == 0)
    def _():
        prev_sc[...] = prev0_ref[...]
        s_sc[...] = s0_ref[...]

    p = p_ref[...]
    row = _iota((tc, 1), 0)
    p_prev = jnp.where(row == 0, prev_sc[...], pltpu.roll(p, 1, axis=0))
    prev_sc[...] = p_ref[tc - 1:tc, :]
    xp = p + (p_prev - p) * mu_ref[...]

    w3 = 3 * RWKV_WIDTH
    r = xp[:, 0:RWKV_WIDTH]
    k = xp[:, RWKV_WIDTH:2 * RWKV_WIDTH]
    v = xp[:, 2 * RWKV_WIDTH:w3]
    z = xp[:, w3:w3 + DECAY_RANK + ICLR_RANK]
    gd = xp[:, w3 + DECAY_RANK + ICLR_RANK:]
    zl = _iota((1, DECAY_RANK + ICLR_RANK), 1)
    zz = jnp.where(zl < DECAY_RANK, jnp.tanh(z), z)
    wa = _mm(zz, wwa_ref[...], passes=3)
    w_log = -_softplus(-(w0_ref[...] + wa[:, :RWKV_WIDTH])) - 0.5
    lw = -jnp.exp(w_log)
    a = jax.nn.sigmoid(a0_ref[...] + wa[:, RWKV_WIDTH:])
    g = _mm(jax.nn.sigmoid(gd), gup_ref[...])
    kk = k * kk_ref[...]
    kk = kk * lax.rsqrt(jnp.maximum(_head_sum(kk * kk), 1e-24))
    k2 = k * (1.0 + (a - 1.0) * ka_ref[...])
    bvec = kk * a
    if t_valid is not None:
        ok = (c * tc + row) < t_valid
        lw = jnp.where(ok, lw, 0.0)
        kk = jnp.where(ok, kk, 0.0)
        bvec = jnp.where(ok, bvec, 0.0)
        k2u = jnp.where(ok, k2, 0.0)
        vu = jnp.where(ok, v, 0.0)
    else:
        k2u, vu = k2, v

    tril16 = (_iota((CHUNK, CHUNK), 0) >= _iota((CHUNK, CHUNK), 1)).astype(BF16)
    bd_mask = (lax.shift_right_logical(_iota((GROUP, GROUP), 0), 6)
               == lax.shift_right_logical(_iota((GROUP, GROUP), 1), 6))
    t_idx = _iota((CHUNK, GROUP), 0)
    s_idx = jnp.bitwise_and(_iota((CHUNK, GROUP), 1), CHUNK - 1)
    strict = t_idx > s_idx
    incl = t_idx >= s_idx
    eye = (t_idx == s_idx).astype(F32)

    def bd(x):
        return jnp.where(bd_mask, jnp.concatenate([x, x, x, x], axis=0), 0.0)

    ys = []
    for j in range(nsub):
        sl = slice(j * CHUNK, (j + 1) * CHUNK)
        lwj = lw[sl]
        h1 = lwj.astype(BF16)
        r1 = lwj - h1.astype(F32)
        h2 = r1.astype(BF16)
        h3 = (r1 - h2.astype(F32)).astype(BF16)
        cum = _dot(tril16, h1) + _dot(tril16, h2) + _dot(tril16, h3)
        e_cum = jnp.exp(cum)
        e_inv = jnp.exp(-cum)
        e_prev = jnp.exp(cum - lwj)
        w_last = e_cum[CHUNK - 1:CHUNK, :]
        at = -kk[sl] * e_prev
        bt = bvec[sl] * e_inv
        kt = k2u[sl] * e_inv
        rt = r[sl] * e_cum
        bw = bt * w_last
        kw = kt * w_last
        vj = vu[sl]
        yg = []
        for gi in range(RWKV_WIDTH // GROUP):
            gs = slice(gi * GROUP, (gi + 1) * GROUP)
            q2 = jnp.concatenate([at[:, gs], rt[:, gs]], axis=0)
            sb = _mm(q2, bd(bt[:, gs]), nt=True)
            sk = _mm(q2, bd(kt[:, gs]), nt=True)
            a_ab = jnp.where(strict, sb[:CHUNK], 0.0)
            a_ak = jnp.where(strict, sk[:CHUNK], 0.0)
            a_rb = jnp.where(incl, sb[CHUNK:], 0.0)
            a_rk = jnp.where(incl, sk[CHUNK:], 0.0)
            tinv = eye + a_ab
            x = _mm(a_ab, bd(a_ab), passes=NEUMANN_PASSES)
            for step in range(1, 6):
                if step < 5:
                    both = _mm(x, jnp.concatenate([bd(tinv), bd(x)], axis=1), passes=NEUMANN_PASSES)
                    tinv = tinv + both[:, :GROUP]
                    x = both[:, GROUP:]
                else:
                    tinv = tinv + _mm(x, bd(tinv), passes=NEUMANN_PASSES)
            sg = s_sc[gi]
            q2s = _mm(q2, sg, passes=STATE_PASSES, nt=True)
            vbd = bd(vj[:, gs])
            rhs = q2s[:CHUNK] + _mm(a_ak, vbd)
            u = _mm(tinv, bd(rhs), passes=SOLVE_PASSES)
            yg.append(q2s[CHUNK:] + _mm(a_rb, bd(u)) + _mm(a_rk, vbd))
            uv = jnp.concatenate([u, vj[:, gs]], axis=0)
            bk = jnp.concatenate([bw[:, gs], kw[:, gs]], axis=0)
            upd = _mm(uv.T, bk)
            s_sc[gi] = sg * w_last[:, gs] + jnp.where(bd_mask, upd, 0.0)
        ys.append(jnp.concatenate(yg, axis=1))
    y = jnp.concatenate(ys, axis=0) if nsub > 1 else ys[0]

    inv_n = 1.0 / HEAD
    yc = y - _head_sum(y) * inv_n
    yn = yc * lax.rsqrt(_head_sum(yc * yc) * inv_n + GN_EPS)
    yn = yn * lng_ref[...] + lnb_ref[...]
    bonus = _head_sum(r * k2 * rk_ref[...]) * v
    y_ref[...] = (yn + bonus) * g
    sout_ref[...] = s_sc[...]


def _rwkv(p_r, prev0, s0_bd, lp, nsub, t_valid):
    b, t, _ = p_r.shape
    tc = nsub * CHUNK
    ng = RWKV_WIDTH // GROUP
    row = lambda a: a.reshape(1, -1)
    zeros = jnp.zeros((DECAY_RANK, RWKV_WIDTH), F32)
    wwa = jnp.concatenate([jnp.concatenate([lp["rwkv_w_up"], zeros], axis=1),
                           jnp.concatenate([zeros, lp["rwkv_a_up"]], axis=1)], axis=0)
    const = lambda shape: pl.BlockSpec(shape, lambda bi, ci: (0,) * len(shape))
    kern = functools.partial(_rwkv_kernel, nsub=nsub, t_valid=t_valid)
    return pl.pallas_call(
        kern,
        grid=(b, t // tc),
        in_specs=[pl.BlockSpec((None, tc, RWKV_PROJ), lambda bi, ci: (bi, ci, 0)),
                  pl.BlockSpec((None, 1, RWKV_PROJ), lambda bi, ci: (bi, 0, 0)),
                  pl.BlockSpec((None, ng, GROUP, GROUP), lambda bi, ci: (bi, 0, 0, 0)),
                  const((1, RWKV_PROJ)), const((1, RWKV_WIDTH)),
                  const((DECAY_RANK + ICLR_RANK, 2 * RWKV_WIDTH)), const((1, RWKV_WIDTH)),
                  const((GATE_RANK, RWKV_WIDTH)), const((1, RWKV_WIDTH)), const((1, RWKV_WIDTH)),
                  const((1, RWKV_WIDTH)), const((1, RWKV_WIDTH)), const((1, RWKV_WIDTH))],
        out_specs=[pl.BlockSpec((None, tc, RWKV_WIDTH), lambda bi, ci: (bi, ci, 0)),
                   pl.BlockSpec((None, ng, GROUP, GROUP), lambda bi, ci: (bi, 0, 0, 0))],
        out_shape=[jax.ShapeDtypeStruct((b, t, RWKV_WIDTH), F32),
                   jax.ShapeDtypeStruct((b, ng, GROUP, GROUP), F32)],
        scratch_shapes=[pltpu.VMEM((1, RWKV_PROJ), F32), pltpu.VMEM((ng, GROUP, GROUP), F32)],
        compiler_params=_cparams("arbitrary", "arbitrary"),
        name="rwkv7",
    )(p_r, prev0, s0_bd, row(lp["rwkv_mu"]), row(lp["rwkv_w0"]), wwa, row(lp["rwkv_a0"]),
      lp["rwkv_g_up"], row(lp["rwkv_k_k"]), row(lp["rwkv_k_a"]), row(lp["rwkv_r_k"]),
      row(lp["rwkv_ln_g"]), row(lp["rwkv_ln_b"]))


def _state_to_bd(s):
    b = s.shape[0]
    hg = GROUP // HEAD
    s5 = s.reshape(b, RWKV_HEADS // hg, hg, HEAD, HEAD)
    out = jnp.einsum("bghvk,hj->bghvjk", s5, jnp.eye(hg, dtype=s.dtype))
    return out.reshape(b, RWKV_HEADS // hg, GROUP, GROUP)


def _state_from_bd(s_bd):
    b = s_bd.shape[0]
    hg = GROUP // HEAD
    s6 = s_bd.reshape(b, RWKV_HEADS // hg, hg, HEAD, hg, HEAD)
    out = jnp.einsum("bghvjk,hj->bghvk", s6, jnp.eye(hg, dtype=s_bd.dtype))
    return out.reshape(b, RWKV_HEADS, HEAD, HEAD)


def _swa_kernel(tab_ref, sink_ref, pa_ref, ck_ref, cv_ref, bkt_ref, qg_ref, kg_ref,
                ya_ref, kn_ref, kall, vall, bias_sc, *, cq, nqc, has_cache):
    tq = cq * nqc
    band = WINDOW + cq
    bi = pl.program_id(0)
    i = pl.program_id(1)
    lane = _iota((1, LANES), 1)
    low = lane < HEAD

    @pl.when(jnp.logical_and(bi == 0, i == 0))
    def _():
        bkt = bkt_ref[...]
        for h in range(ATTN_HEADS):
            def body(j, acc, h=h):
                return acc + jnp.where(bkt == j, tab_ref[j, h], 0.0)
            bh = lax.fori_loop(0, REL_BUCKETS, body, jnp.zeros((cq, band), F32))
            bias_sc[h // 2, (h % 2) * cq:(h % 2 + 1) * cq, :] = bh

    @pl.when(i == 0)
    def _():
        kall[0:WINDOW, :] = ck_ref[...]
        vall[0:WINDOW, :] = cv_ref[...]

    pa = pa_ref[...]
    q = pa[:, :ATTN_WIDTH]
    k = pa[:, ATTN_WIDTH:ATTN_WIDTH + KV_WIDTH]
    v = pa[:, ATTN_WIDTH + KV_WIDTH:]
    inv_n = 1.0 / HEAD
    qn = q * lax.rsqrt(_head_sum(q * q) * inv_n + RMS_EPS) * qg_ref[...]
    kn = k * lax.rsqrt(_head_sum(k * k) * inv_n + RMS_EPS) * kg_ref[...]
    kn_ref[...] = kn
    kall[WINDOW:WINDOW + tq, :] = kn
    vall[WINDOW:WINDOW + tq, :] = v

    row2 = _iota((2 * cq, 1), 0)
    col = _iota((1, band), 1)
    thr = jnp.where(i == 0, WINDOW, 0)
    for qc in range(nqc):
        kb = kall[qc * cq:qc * cq + band, :]
        vb = vall[qc * cq:qc * cq + band, :]
        kr = pltpu.roll(kb, HEAD, axis=1)
        vr = pltpu.roll(vb, HEAD, axis=1)
        kd = [jnp.where(low, kb, kr).astype(BF16), jnp.where(low, kr, kb).astype(BF16)]
        vd = [jnp.where(low, vb, vr).astype(BF16), jnp.where(low, vr, vb).astype(BF16)]
        for pi in range(ATTN_HEADS // 2):
            gk = (2 * pi) // (ATTN_HEADS // KV_HEADS)
            qp = qn[qc * cq:(qc + 1) * cq, pi * LANES:(pi + 1) * LANES]
            qs = jnp.concatenate([jnp.where(low, qp, 0.0), jnp.where(low, 0.0, qp)], axis=0)
            s = _dot(qs.astype(BF16), kd[gk], nt=True) * ATTN_SCALE + bias_sc[pi]
            if (not has_cache) and qc * cq < WINDOW:
                s = jnp.where((col + qc * cq) < thr, NEG_INF, s)
            sink = jnp.where(row2 < cq, sink_ref[0, 2 * pi], sink_ref[0, 2 * pi + 1])
            m = jnp.maximum(jnp.max(s, axis=-1, keepdims=True), sink)
            e = jnp.exp(s - m)
            den = jnp.sum(e, axis=-1, keepdims=True) + jnp.exp(sink - m)
            o = _dot((e / den).astype(BF16), vd[gk])
            ya_ref[qc * cq:(qc + 1) * cq, pi * LANES:(pi + 1) * LANES] = jnp.where(low, o[:cq], o[cq:])

    if tq >= WINDOW:
        knext = kall[tq:tq + WINDOW, :]
        vnext = vall[tq:tq + WINDOW, :]
        kall[0:WINDOW, :] = knext
        vall[0:WINDOW, :] = vnext


def _t5_bucket(rel):
    half = REL_BUCKETS // 2
    max_exact = half // 2
    n = jnp.abs(rel)
    log_ratio = jnp.log(jnp.maximum(n, 1).astype(F32) / max_exact) / math.log(REL_MAX_DIST / max_exact)
    large = jnp.minimum(max_exact + (log_ratio * (half - max_exact)).astype(I32), half - 1)
    return jnp.where(rel > 0, half, 0) + jnp.where(n < max_exact, n, large)


def _swa(p_a, cache_k, cache_v, rel_table, sinks, q_norm, k_norm, cq, nqc, has_cache):
    b, t, _ = p_a.shape
    tq = cq * nqc
    band = WINDOW + cq
    rel = (jnp.arange(band) - WINDOW)[None, :] - jnp.arange(cq)[:, None]
    bkt = _t5_bucket(rel).astype(I32)
    qg = jnp.tile(q_norm, ATTN_HEADS).reshape(1, ATTN_WIDTH)
    kg = jnp.tile(k_norm, KV_HEADS).reshape(1, KV_WIDTH)
    kern = functools.partial(_swa_kernel, cq=cq, nqc=nqc, has_cache=has_cache)
    smem = pl.BlockSpec(memory_space=pltpu.SMEM)
    return pl.pallas_call(
        kern,
        grid=(b, t // tq),
        in_specs=[smem, smem,
                  pl.BlockSpec((None, tq, ATTN_PROJ), lambda bi, i: (bi, i, 0)),
                  pl.BlockSpec((None, WINDOW, KV_WIDTH), lambda bi, i: (bi, 0, 0)),
                  pl.BlockSpec((None, WINDOW, KV_WIDTH), lambda bi, i: (bi, 0, 0)),
                  pl.BlockSpec((cq, band), lambda bi, i: (0, 0)),
                  pl.BlockSpec((1, ATTN_WIDTH), lambda bi, i: (0, 0)),
                  pl.BlockSpec((1, KV_WIDTH), lambda bi, i: (0, 0))],
        out_specs=[pl.BlockSpec((None, tq, ATTN_WIDTH), lambda bi, i: (bi, i, 0)),
                   pl.BlockSpec((None, tq, KV_WIDTH), lambda bi, i: (bi, i, 0))],
        out_shape=[jax.ShapeDtypeStruct((b, t, ATTN_WIDTH), F32),
                   jax.ShapeDtypeStruct((b, t, KV_WIDTH), F32)],
        scratch_shapes=[pltpu.VMEM((WINDOW + tq, KV_WIDTH), F32),
                        pltpu.VMEM((WINDOW + tq, KV_WIDTH), F32),
                        pltpu.VMEM((ATTN_HEADS // 2, 2 * cq, band), F32)],
        compiler_params=_cparams("arbitrary", "arbitrary"),
        name="swa",
    )(rel_table, sinks.reshape(1, ATTN_HEADS), p_a, cache_k, cache_v, bkt, qg, kg)


def _mid_kernel(yr_ref, ya_ref, x_ref, gt1_ref, sh2_ref, sc2_ref, g2_ref, wor_ref, woa_ref, wrt_ref,
                rb_ref, base0_ref, x1_ref, h2_ref, te_ref, gate_ref, pos_ref, cnt_ref, base_sc, *, tm):
    @pl.when(jnp.logical_and(pl.program_id(0) == 0, pl.program_id(1) == 0))
    def _():
        base_sc[...] = base0_ref[...]

    mix = _dot(yr_ref[...].astype(BF16), wor_ref[...]) + _dot(ya_ref[...].astype(BF16), woa_ref[...])
    x1 = x_ref[...] + gt1_ref[...] * mix
    x1_ref[...] = x1
    h2 = _rms(x1, RMS_EPS) * g2_ref[...] * (1.0 + sc2_ref[...]) + sh2_ref[...]
    h2_ref[...] = h2

    logits = _mm(wrt_ref[...], h2, passes=3, nt=True) + rb_ref[...]
    eidx = _iota((N_EXPERTS, tm), 0)
    work = logits
    tops, hots = [], []
    for j in range(TOP_K):
        m = jnp.max(work, axis=0, keepdims=True)
        idx = jnp.min(jnp.where(work == m, eidx, N_EXPERTS), axis=0, keepdims=True)
        hot = eidx == idx
        work = jnp.where(hot, -jnp.inf, work)
        tops.append(m)
        hots.append(hot)
        te_ref[j:j + 1, :] = idx
    exps = [jnp.exp(tj - tops[0]) for tj in tops]
    den = exps[0] + exps[1] + exps[2] + exps[3]
    for j in range(TOP_K):
        gate_ref[j:j + 1, :] = exps[j] / den

    member = (hots[0] | hots[1] | hots[2] | hots[3]).astype(BF16)
    upper = (_iota((tm, tm), 0) < _iota((tm, tm), 1)).astype(BF16)
    before = _dot(member, upper) + base_sc[...]
    for j in range(TOP_K):
        pos_ref[j:j + 1, :] = jnp.sum(jnp.where(hots[j], before, 0.0), axis=0, keepdims=True).astype(I32)
    base_sc[...] = base_sc[...] + jnp.sum(member.astype(F32), axis=1, keepdims=True)
    cnt_ref[...] = base_sc[...]


def _mid(y_r, y_a, x3, gt1, sh2, sc2, gain2, wo_r16, wo_a16, wr_t, rb, base0, tm):
    g, r, d = x3.shape
    n = g * r
    per = r // tm
    mr = gt1.shape[1]
    mrow = (lambda b, i: (b, 0, 0)) if mr == 1 else (lambda b, i: (b, i, 0))
    mblk = 1 if mr == 1 else tm
    tok = lambda w: pl.BlockSpec((None, tm, w), lambda b, i: (b, i, 0))
    mod = pl.BlockSpec((None, mblk, d), mrow)
    const = lambda shape: pl.BlockSpec(shape, lambda b, i: (0,) * len(shape))
    lane_out = pl.BlockSpec((TOP_K, tm), lambda b, i: (0, b * per + i))
    kern = functools.partial(_mid_kernel, tm=tm)
    return pl.pallas_call(
        kern,
        grid=(g, per),
        in_specs=[tok(RWKV_WIDTH), tok(ATTN_WIDTH), tok(d), mod, mod, mod, const((1, d)),
                  const((RWKV_WIDTH, d)), const((ATTN_WIDTH, d)), const((N_EXPERTS, d)),
                  const((N_EXPERTS, 1)), const((N_EXPERTS, 1))],
        out_specs=[tok(d), tok(d), lane_out, lane_out, lane_out, const((N_EXPERTS, 1))],
        out_shape=[jax.ShapeDtypeStruct((g, r, d), F32), jax.ShapeDtypeStruct((g, r, d), F32),
                   jax.ShapeDtypeStruct((TOP_K, n), I32), jax.ShapeDtypeStruct((TOP_K, n), F32),
                   jax.ShapeDtypeStruct((TOP_K, n), I32), jax.ShapeDtypeStruct((N_EXPERTS, 1), F32)],
        scratch_shapes=[pltpu.VMEM((N_EXPERTS, 1), F32)],
        compiler_params=_cparams("arbitrary", "arbitrary"),
        name="mid",
    )(y_r, y_a, x3, gt1, sh2, sc2, gain2.reshape(1, d), wo_r16, wo_a16, wr_t, rb, base0)


def _dispatch_kernel(te_ref, pos_ref, pstart_ref, h2_ref, xs_in_ref, xs_ref, sem, *, tk):
    del xs_in_ref

    def body(t, carry):
        for j in range(TOP_K):
            dst = pstart_ref[te_ref[j, t]] + pos_ref[j, t]
            pltpu.make_async_copy(h2_ref.at[pl.ds(t, 1)], xs_ref.at[pl.ds(dst, 1)], sem).start()
        return carry

    lax.fori_loop(0, tk, body, 0)
    for _ in range(TOP_K):
        pltpu.make_async_copy(h2_ref, xs_ref.at[pl.ds(0, tk)], sem).wait()


def _dispatch(te, pos, pstart, h2, xs, tk):
    n, d = h2.shape
    smem_blk = pl.BlockSpec((TOP_K, tk), lambda i: (0, i), memory_space=pltpu.SMEM)
    kern = functools.partial(_dispatch_kernel, tk=tk)
    return pl.pallas_call(
        kern,
        grid=(n // tk,),
        in_specs=[smem_blk, smem_blk, pl.BlockSpec(memory_space=pltpu.SMEM),
                  pl.BlockSpec((tk, d), lambda i: (i, 0)), pl.BlockSpec(memory_space=pl.ANY)],
        out_specs=pl.BlockSpec(memory_space=pl.ANY),
        out_shape=jax.ShapeDtypeStruct(xs.shape, xs.dtype),
        scratch_shapes=[pltpu.SemaphoreType.DMA(())],
        input_output_aliases={4: 0},
        compiler_params=_cparams("arbitrary"),
        name="dispatch",
    )(te, pos, pstart, h2, xs)


def _ffn_kernel(te_ref, nu_ref, xs_ref, wu_ref, bu_ref, wd_ref, bd_ref, ys_ref, wu16, wd16):
    i = pl.program_id(0)
    n_up = wu_ref.shape[1]
    active = i < nu_ref[0]
    fresh = jnp.logical_or(i == 0, te_ref[i] != te_ref[jnp.maximum(i - 1, 0)])

    @pl.when(jnp.logical_and(active, fresh))
    def _():
        src = _iota((GROUP, GROUP), 0)
        dst = _iota((GROUP, GROUP), 1)
        want = jnp.where(dst < LANES, 2 * dst, 2 * (dst - LANES) + 1)
        perm = (src == want).astype(BF16)
        for b in range(n_up // GROUP):
            cs = slice(b * GROUP, (b + 1) * GROUP)
            wu16[:, cs] = _dot(wu_ref[:, cs].astype(BF16), perm).astype(BF16)
        wd16[...] = wd_ref[...].astype(BF16)

    @pl.when(active)
    def _():
        u = _dot(xs_ref[...].astype(BF16), wu16[...]) + bu_ref[...]
        acts = []
        for b in range(n_up // GROUP):
            glu = jnp.minimum(u[:, b * GROUP:b * GROUP + LANES], SWIGLU_LIMIT)
            lin = jnp.clip(u[:, b * GROUP + LANES:(b + 1) * GROUP], -SWIGLU_LIMIT, SWIGLU_LIMIT)
            acts.append((glu * jax.nn.sigmoid(SWIGLU_ALPHA * glu) * (lin + 1.0)).astype(BF16))
        act = jnp.concatenate(acts, axis=1)
        ys_ref[...] = _dot(act, wd16[...]) + bd_ref[...]

    @pl.when(jnp.logical_not(active))
    def _():
        ys_ref[...] = jnp.zeros_like(ys_ref)


def _ffn(tile_e, n_used, xs, wu, bu, wd, bd_, tm):
    rows, d = xs.shape
    n_exp, _, n_up = wu.shape
    dff = wd.shape[1]
    row_blk = lambda i, te, nu: (jnp.minimum(i, nu[0] - 1), 0)
    exp_blk = lambda i, te, nu: (te[i], 0, 0)
    grid_spec = pltpu.PrefetchScalarGridSpec(
        num_scalar_prefetch=2,
        grid=(rows // tm,),
        in_specs=[pl.BlockSpec((tm, d), row_blk),
                  pl.BlockSpec((None, d, n_up), exp_blk),
                  pl.BlockSpec((None, 1, n_up), exp_blk),
                  pl.BlockSpec((None, dff, d), exp_blk),
                  pl.BlockSpec((None, 1, d), exp_blk)],
        out_specs=pl.BlockSpec((tm, d), lambda i, te, nu: (i, 0)),
        scratch_shapes=[pltpu.VMEM((d, n_up), BF16), pltpu.VMEM((dff, d), BF16)],
    )
    return pl.pallas_call(
        _ffn_kernel,
        grid_spec=grid_spec,
        out_shape=jax.ShapeDtypeStruct((rows, d), F32),
        compiler_params=_cparams("arbitrary"),
        name="expert_ffn",
    )(tile_e, n_used, xs, wu, bu.reshape(n_exp, 1, n_up), wd, bd_.reshape(n_exp, 1, d))


def _combine_kernel(te_ref, pos_ref, ten_ref, posn_ref, pstart_ref, gate_ref, x1_ref, gt2_ref, ys_ref,
                    o_ref, buf, sem, *, tk):
    i = pl.program_id(1)
    n = pl.num_programs(1)
    flat = pl.program_id(0) * n + i
    total = pl.num_programs(0) * n
    slot = lax.rem(flat, 2)

    def issue(te_r, pos_r, s):
        def body(t, carry):
            for j in range(TOP_K):
                src = pstart_ref[te_r[j, t]] + pos_r[j, t]
                pltpu.make_async_copy(ys_ref.at[pl.ds(src, 1)], buf.at[s, j, pl.ds(t, 1)], sem.at[s]).start()
            return carry
        lax.fori_loop(0, tk, body, 0)

    @pl.when(flat == 0)
    def _():
        issue(te_ref, pos_ref, 0)

    @pl.when(flat + 1 < total)
    def _():
        issue(ten_ref, posn_ref, 1 - slot)

    for j in range(TOP_K):
        pltpu.make_async_copy(ys_ref.at[pl.ds(0, tk)], buf.at[slot, j], sem.at[slot]).wait()

    gate = gate_ref[...]
    acc = gate[:, 0:1] * buf[slot, 0]
    for j in range(1, TOP_K):
        acc = acc + gate[:, j:j + 1] * buf[slot, j]
    o_ref[...] = x1_ref[...] + gt2_ref[...] * acc


def _combine(te, pos, pstart, gates_t, x1, gt2, ys, tk):
    g, r, d = x1.shape
    per = r // tk
    last = g * per - 1
    mr = gt2.shape[1]
    mrow = (lambda b, i: (b, 0, 0)) if mr == 1 else (lambda b, i: (b, i, 0))
    mblk = 1 if mr == 1 else tk
    cur = pl.BlockSpec((TOP_K, tk), lambda b, i: (0, b * per + i), memory_space=pltpu.SMEM)
    nxt = pl.BlockSpec((TOP_K, tk), lambda b, i: (0, jnp.minimum(b * per + i + 1, last)),
                       memory_space=pltpu.SMEM)
    kern = functools.partial(_combine_kernel, tk=tk)
    return pl.pallas_call(
        kern,
        grid=(g, per),
        in_specs=[cur, cur, nxt, nxt, pl.BlockSpec(memory_space=pltpu.SMEM),
                  pl.BlockSpec((tk, TOP_K), lambda b, i: (b * per + i, 0)),
                  pl.BlockSpec((None, tk, d), lambda b, i: (b, i, 0)),
                  pl.BlockSpec((None, mblk, d), mrow),
                  pl.BlockSpec(memory_space=pl.ANY)],
        out_specs=pl.BlockSpec((None, tk, d), lambda b, i: (b, i, 0)),
        out_shape=jax.ShapeDtypeStruct((g, r, d), F32),
        scratch_shapes=[pltpu.VMEM((2, TOP_K, tk, d), F32), pltpu.SemaphoreType.DMA((2,))],
        compiler_params=_cparams("arbitrary", "arbitrary"),
        name="combine",
    )(te, pos, te, pos, pstart, gates_t, x1, gt2, ys)


def _tile_plan(counts, tm, n_tiles):
    cnt = counts.reshape(-1).astype(I32)
    tiles = (cnt + tm - 1) // tm
    tile_end = jnp.cumsum(tiles)
    pstart = ((tile_end - tiles) * tm).astype(I32)
    n_used = tile_end[-1:].astype(I32)
    idx = jnp.minimum(jnp.arange(n_tiles, dtype=I32), n_used[0] - 1)
    tile_e = jnp.sum((tile_end[None, :] <= idx[:, None]).astype(I32), axis=1)
    return pstart, jnp.minimum(tile_e, N_EXPERTS - 1), n_used


def kernel(x_prompt, x_sample, state_shift, state_wkv, cache_win_k, cache_win_v, c_prompt, c_sample, rel_bias, norm1_g, norm2_g, w_ada, b_ada, w_in, w_out, rwkv_mu, rwkv_w0, rwkv_w_up, rwkv_a0, rwkv_a_up, rwkv_g_up, rwkv_k_k, rwkv_k_a, rwkv_r_k, rwkv_ln_g, rwkv_ln_b, q_norm, k_norm, sinks, router_w, router_b, exp_w_up, exp_b_up, exp_w_down, exp_b_down):
    bp, tp, d = x_prompt.shape
    bs, ts, _ = x_sample.shape
    depth = w_in.shape[0]
    assert depth == 1, "single-layer trunk"
    layer = 0
    lp = dict(rwkv_mu=rwkv_mu[layer], rwkv_w0=rwkv_w0[layer], rwkv_w_up=rwkv_w_up[layer],
              rwkv_a0=rwkv_a0[layer], rwkv_a_up=rwkv_a_up[layer], rwkv_g_up=rwkv_g_up[layer],
              rwkv_k_k=rwkv_k_k[layer], rwkv_k_a=rwkv_k_a[layer], rwkv_r_k=rwkv_r_k[layer].reshape(-1),
              rwkv_ln_g=rwkv_ln_g[layer], rwkv_ln_b=rwkv_ln_b[layer])

    rows = bp + bs
    pad = (-rows) % 8
    c_all = jnp.concatenate([c_prompt, c_sample, jnp.zeros((pad, d), F32)], axis=0)
    mod = _modulation(c_all, w_ada[layer], b_ada[layer])
    sh1, sc1, gt1, sh2, sc2, gt2 = [mod[:, i * d:(i + 1) * d] for i in range(6)]
    pm = lambda m: m[:bp].reshape(bp, 1, d)
    sm = lambda m: jnp.repeat(m[bp:bp + bs], ts, axis=0).reshape(1, bs * ts, d)

    w_in16 = w_in[layer].astype(BF16)
    wo16 = w_out[layer].astype(BF16)
    wo_r16, wo_a16 = wo16[:RWKV_WIDTH], wo16[RWKV_WIDTH:]
    wr_t = router_w[layer].T
    rb = router_b[layer].reshape(N_EXPERTS, 1)
    wu = exp_w_up.reshape(exp_w_up.shape[1:])
    wd = exp_w_down.reshape(exp_w_down.shape[1:])
    n_up = wu.shape[-1]
    bu = exp_b_up[layer].reshape(N_EXPERTS, n_up // GROUP, LANES, 2).transpose(0, 1, 3, 2).reshape(N_EXPERTS, n_up)

    pr_p, pa_p = _inproj(x_prompt, norm1_g[layer], pm(sh1), pm(sc1), w_in16, TM_PROJ)
    zero_prev = jnp.zeros((bp, 1, RWKV_PROJ), F32)
    zero_state = jnp.zeros((bp, RWKV_WIDTH // GROUP, GROUP, GROUP), F32)
    yr_p, s_p = _rwkv(pr_p, zero_prev, zero_state, lp, RWKV_SUB, None)
    zero_kv = jnp.zeros((bp, WINDOW, KV_WIDTH), F32)
    ya_p, kn_p = _swa(pa_p, zero_kv, zero_kv, rel_bias, sinks[layer], q_norm[layer], k_norm[layer],
                      CHUNK, SWA_QC, False)

    ns = bs * ts
    xs_flat = x_sample.reshape(1, ns, d)
    pr_s, pa_s = _inproj(xs_flat, norm1_g[layer], sm(sh1), sm(sc1), w_in16, ns)
    pr_s = pr_s.reshape(bs, ts, RWKV_PROJ)
    pa_s = pa_s.reshape(bs, ts, ATTN_PROJ)
    t_pad = -(-ts // CHUNK) * CHUNK
    pr_s_pad = jnp.pad(pr_s, ((0, 0), (0, t_pad - ts), (0, 0)))
    yr_s, s_s = _rwkv(pr_s_pad, state_shift[layer].reshape(bs, 1, RWKV_PROJ), _state_to_bd(state_wkv[layer]),
                      lp, t_pad // CHUNK, ts)
    yr_s = yr_s[:, :ts]
    ya_s, kn_s = _swa(pa_s, cache_win_k[layer].reshape(bs, WINDOW, KV_WIDTH),
                      cache_win_v[layer].reshape(bs, WINDOW, KV_WIDTH), rel_bias, sinks[layer],
                      q_norm[layer], k_norm[layer], ts, 1, True)

    base0 = jnp.zeros((N_EXPERTS, 1), F32)
    x1_p, h2_p, te_p, gate_p, pos_p, cnt_p = _mid(yr_p, ya_p, x_prompt, pm(gt1), pm(sh2), pm(sc2), norm2_g[layer],
                                                  wo_r16, wo_a16, wr_t, rb, base0, TM_PROJ)
    x1_s, h2_s, te_s, gate_s, pos_s, cnt = _mid(yr_s.reshape(1, ns, RWKV_WIDTH), ya_s.reshape(1, ns, ATTN_WIDTH),
                                                xs_flat, sm(gt1), sm(sh2), sm(sc2), norm2_g[layer],
                                                wo_r16, wo_a16, wr_t, rb, cnt_p, ns)

    n_p = bp * tp
    n_rows = (n_p + ns) * TOP_K + N_EXPERTS * (TM_FFN - 1)
    n_tiles = -(-n_rows // TM_FFN)
    pstart, tile_e, n_used = _tile_plan(cnt, TM_FFN, n_tiles)
    xs_buf = jnp.zeros((n_tiles * TM_FFN, d), F32)
    xs_buf = _dispatch(te_p, pos_p, pstart, h2_p.reshape(n_p, d), xs_buf, TK_DISPATCH)
    xs_buf = _dispatch(te_s, pos_s, pstart, h2_s.reshape(ns, d), xs_buf, ns)
    ys_buf = _ffn(tile_e, n_used, xs_buf, wu, bu, wd, exp_b_down[layer], TM_FFN)
    y_p = _combine(te_p, pos_p, pstart, gate_p.T, x1_p, pm(gt2), ys_buf, TK_COMBINE)
    y_s = _combine(te_s, pos_s, pstart, gate_s.T, x1_s, sm(gt2), ys_buf, TK_COMBINE)

    kv4 = lambda z, rows_: z.reshape(z.shape[0], rows_, KV_HEADS, HEAD)[None]
    v_p = pa_p[:, tp - WINDOW:, ATTN_WIDTH + KV_WIDTH:]
    v_s = pa_s[:, :, ATTN_WIDTH + KV_WIDTH:]
    return (y_p, y_s.reshape(bs, ts, d),
            pr_p[:, tp - 1][None], _state_from_bd(s_p)[None], kv4(kn_p[:, tp - WINDOW:], WINDOW), kv4(v_p, WINDOW),
            pr_s[:, ts - 1][None], _state_from_bd(s_s)[None], kv4(kn_s, ts), kv4(v_s, ts))
```

```python
import functools
import math

import jax
import jax.numpy as jnp
from jax import lax
from jax.experimental import pallas as pl
from jax.experimental.pallas import tpu as pltpu

F32 = jnp.float32
BF16 = jnp.bfloat16
I32 = jnp.int32

HEAD = 64
RWKV_HEADS = 8
RWKV_WIDTH = RWKV_HEADS * HEAD
DECAY_RANK = 64
ICLR_RANK = 64
GATE_RANK = 128
RWKV_PROJ = 3 * RWKV_WIDTH + DECAY_RANK + ICLR_RANK + GATE_RANK
ATTN_HEADS = 8
KV_HEADS = 2
ATTN_WIDTH = ATTN_HEADS * HEAD
KV_WIDTH = KV_HEADS * HEAD
ATTN_PROJ = ATTN_WIDTH + 2 * KV_WIDTH
WINDOW = 128
CHUNK = 64
REL_BUCKETS = 32
REL_MAX_DIST = 128
N_EXPERTS = 32
TOP_K = 4
GN_EPS = 64e-5
RMS_EPS = 1e-6
NEG_INF = -1e30
ATTN_SCALE = HEAD ** -0.5
SWIGLU_ALPHA = 1.702
SWIGLU_LIMIT = 7.0

LANES = 128
GROUP = 256
VMEM_LIMIT = 56 * 1024 * 1024

TM_PROJ = 512
RWKV_SUB = 4
SWA_QC = 4
TM_FFN = 512
TK_DISPATCH = 1024
TK_COMBINE = 128
ROW_UNROLL = 8


def _cparams(*sem):
    return pltpu.CompilerParams(dimension_semantics=sem, vmem_limit_bytes=VMEM_LIMIT)


def _dot(a, b, nt=False):
    dims = (((1,), (1,)), ((), ())) if nt else (((1,), (0,)), ((), ()))
    return lax.dot_general(a, b, dims, preferred_element_type=F32)


def _split(x):
    hi = x.astype(BF16)
    lo = (x - hi.astype(F32)).astype(BF16)
    return hi, lo


def _mm(a, b, passes=1, nt=False):
    if passes == 1:
        return _dot(a.astype(BF16), b.astype(BF16), nt)
    ah, al = _split(a)
    bh, bl = _split(b)
    return _dot(ah, bh, nt) + _dot(al, bh, nt) + _dot(ah, bl, nt)


def _mm_exact_rhs(a, b16, nt=False):
    ah, al = _split(a)
    return _dot(ah, b16, nt) + _dot(al, b16, nt)


def _iota(shape, dim):
    return lax.broadcasted_iota(I32, shape, dim)


def _head_ones(n):
    r = lax.shift_right_logical(_iota((n, n), 0), 6)
    c = lax.shift_right_logical(_iota((n, n), 1), 6)
    return (r == c).astype(BF16)


def _head_sum(x):
    width = x.shape[1]
    if width <= GROUP:
        return _mm_exact_rhs(x, _head_ones(width))
    ones = _head_ones(GROUP)
    parts = [_mm_exact_rhs(x[:, g:g + GROUP], ones) for g in range(0, width, GROUP)]
    return jnp.concatenate(parts, axis=1)


def _rms(x, eps):
    return x * lax.rsqrt(jnp.mean(x * x, axis=-1, keepdims=True) + eps)


def _mod_kernel(c_ref, w_ref, b_ref, o_ref):
    c = c_ref[...]
    s = c * jax.nn.sigmoid(c)
    o_ref[...] = _mm(s, w_ref[...], passes=3) + b_ref[...]


def _modulation(c_all, w_ada, b_ada):
    rows, d = c_all.shape
    n = w_ada.shape[1]
    tn = 1536
    return pl.pallas_call(
        _mod_kernel,
        grid=(n // tn,),
        in_specs=[pl.BlockSpec((rows, d), lambda j: (0, 0)),
                  pl.BlockSpec((d, tn), lambda j: (0, j)),
                  pl.BlockSpec((1, tn), lambda j: (0, j))],
        out_specs=pl.BlockSpec((rows, tn), lambda j: (0, j)),
        out_shape=jax.ShapeDtypeStruct((rows, n), F32),
        compiler_params=_cparams("arbitrary"),
        name="modulation",
    )(c_all, w_ada, b_ada.reshape(1, n))


def _inproj_kernel(x_ref, g_ref, sh_ref, sc_ref, w_ref, pr_ref, pa_ref):
    x = x_ref[...]
    h = _rms(x, RMS_EPS) * g_ref[...] * (1.0 + sc_ref[...]) + sh_ref[...]
    p = _dot(h.astype(BF16), w_ref[...])
    pr_ref[...] = p[:, :RWKV_PROJ]
    pa_ref[...] = p[:, RWKV_PROJ:]


def _inproj(x3, gain, sh3, sc3, w_in16, tm):
    g, r, d = x3.shape
    mr = sh3.shape[1]
    mrow = (lambda b, i: (b, 0, 0)) if mr == 1 else (lambda b, i: (b, i, 0))
    mblk = 1 if mr == 1 else tm
    return pl.pallas_call(
        _inproj_kernel,
        grid=(g, r // tm),
        in_specs=[pl.BlockSpec((None, tm, d), lambda b, i: (b, i, 0)),
                  pl.BlockSpec((1, d), lambda b, i: (0, 0)),
                  pl.BlockSpec((None, mblk, d), mrow),
                  pl.BlockSpec((None, mblk, d), mrow),
                  pl.BlockSpec(w_in16.shape, lambda b, i: (0, 0))],
        out_specs=[pl.BlockSpec((None, tm, RWKV_PROJ), lambda b, i: (b, i, 0)),
                   pl.BlockSpec((None, tm, ATTN_PROJ), lambda b, i: (b, i, 0))],
        out_shape=[jax.ShapeDtypeStruct((g, r, RWKV_PROJ), F32),
                   jax.ShapeDtypeStruct((g, r, ATTN_PROJ), F32)],
        compiler_params=_cparams("arbitrary", "arbitrary"),
        name="inproj",
    )(x3, gain.reshape(1, d), sh3, sc3, w_in16)


def _softplus(x):
    return jnp.maximum(x, 0.0) + jnp.log1p(jnp.exp(-jnp.abs(x)))


def _rwkv_kernel(p_ref, prev0_ref, s0_ref, mu_ref, w0_ref, wwa_ref, a0_ref, gup_ref, kk_ref, ka_ref,
                 rk_ref, lng_ref, lnb_ref, y_ref, sout_ref, prev_sc, s_sc, *, nsub, t_valid):
    tc = nsub * CHUNK
    c = pl.program_id(1)

    @pl.when(c == 0)
    def _():
        prev_sc[...] = prev0_ref[...]
        s_sc[...] = s0_ref[...]

    p = p_ref[...]
    row = _iota((tc, 1), 0)
    p_prev = jnp.where(row == 0, prev_sc[...], pltpu.roll(p, 1, axis=0))
    prev_sc[...] = p_ref[tc - 1:tc, :]
    xp = p + (p_prev - p) * mu_ref[...]

    w3 = 3 * RWKV_WIDTH
    r = xp[:, 0:RWKV_WIDTH]
    k = xp[:, RWKV_WIDTH:2 * RWKV_WIDTH]
    v = xp[:, 2 * RWKV_WIDTH:w3]
    z = xp[:, w3:w3 + DECAY_RANK + ICLR_RANK]
    gd = xp[:, w3 + DECAY_RANK + ICLR_RANK:]
    zl = _iota((1, DECAY_RANK + ICLR_RANK), 1)
    zz = jnp.where(zl < DECAY_RANK, jnp.tanh(z), z)
    wa = _mm(zz, wwa_ref[...], passes=3)
    w_log = -_softplus(-(w0_ref[...] + wa[:, :RWKV_WIDTH])) - 0.5
    lw = -jnp.exp(w_log)
    a = jax.nn.sigmoid(a0_ref[...] + wa[:, RWKV_WIDTH:])
    g = _mm(jax.nn.sigmoid(gd), gup_ref[...])
    kk = k * kk_ref[...]
    kk = kk * lax.rsqrt(jnp.maximum(_head_sum(kk * kk), 1e-24))
    k2 = k * (1.0 + (a - 1.0) * ka_ref[...])
    bvec = kk * a
    if t_valid is not None:
        ok = (c * tc + row) < t_valid
        lw = jnp.where(ok, lw, 0.0)
        kk = jnp.where(ok, kk, 0.0)
        bvec = jnp.where(ok, bvec, 0.0)
        k2u = jnp.where(ok, k2, 0.0)
        vu = jnp.where(ok, v, 0.0)
    else:
        k2u, vu = k2, v

    tril16 = (_iota((CHUNK, CHUNK), 0) >= _iota((CHUNK, CHUNK), 1)).astype(BF16)
    bd_mask = (lax.shift_right_logical(_iota((GROUP, GROUP), 0), 6)
               == lax.shift_right_logical(_iota((GROUP, GROUP), 1), 6))
    t_idx = _iota((CHUNK, GROUP), 0)
    s_idx = jnp.bitwise_and(_iota((CHUNK, GROUP), 1), CHUNK - 1)
    strict = t_idx > s_idx
    incl = t_idx >= s_idx
    eye = (t_idx == s_idx).astype(F32)

    bd16 = bd_mask.astype(BF16)

    def bd(x):
        x16 = x.astype(BF16)
        return jnp.concatenate([x16, x16, x16, x16], axis=0) * bd16

    def b16(x):
        return x.astype(BF16)

    ng = RWKV_WIDTH // GROUP
    pairs = [(j, gi) for j in range(nsub) for gi in range(ng)]
    gsl = lambda gi: slice(gi * GROUP, (gi + 1) * GROUP)

    at, bt, kt, rt, bw, kw, vj, w_last = [], [], [], [], [], [], [], []
    for j in range(nsub):
        sl = slice(j * CHUNK, (j + 1) * CHUNK)
        lwj = lw[sl]
        h1 = lwj.astype(BF16)
        r1 = lwj - h1.astype(F32)
        h2 = r1.astype(BF16)
        h3 = (r1 - h2.astype(F32)).astype(BF16)
        cum = _dot(tril16, h1) + _dot(tril16, h2) + _dot(tril16, h3)
        e_cum = jnp.exp(cum)
        e_inv = jnp.exp(-cum)
        wl = e_cum[CHUNK - 1:CHUNK, :]
        at.append(-kk[sl] * jnp.exp(cum - lwj))
        bt.append(bvec[sl] * e_inv)
        kt.append(k2u[sl] * e_inv)
        rt.append(r[sl] * e_cum)
        bw.append(bt[j] * wl)
        kw.append(kt[j] * wl)
        vj.append(vu[sl])
        w_last.append(wl)

    a_ab, a_ak, a_rb, a_rk, vbd = {}, {}, {}, {}, {}
    for (j, gi) in pairs:
        gs = gsl(gi)
        q2 = b16(jnp.concatenate([at[j][:, gs], rt[j][:, gs]], axis=0))
        sc = _dot(q2, jnp.concatenate([bd(bt[j][:, gs]), bd(kt[j][:, gs])], axis=0), nt=True)
        a_ab[j, gi] = jnp.where(strict, sc[:CHUNK, :GROUP], 0.0)
        a_ak[j, gi] = jnp.where(strict, sc[:CHUNK, GROUP:], 0.0)
        a_rb[j, gi] = jnp.where(incl, sc[CHUNK:, :GROUP], 0.0)
        a_rk[j, gi] = jnp.where(incl, sc[CHUNK:, GROUP:], 0.0)
        vbd[j, gi] = bd(vj[j][:, gs])

    tinv = {p: eye + a_ab[p] for p in pairs}
    xpow = {p: _dot(b16(a_ab[p]), bd(a_ab[p])) for p in pairs}
    for step in range(1, 6):
        for p in pairs:
            if step < 5:
                both = _dot(b16(xpow[p]), jnp.concatenate([bd(tinv[p]), bd(xpow[p])], axis=1))
                tinv[p] = tinv[p] + both[:, :GROUP]
                xpow[p] = both[:, GROUP:]
            else:
                tinv[p] = tinv[p] + _dot(b16(xpow[p]), bd(tinv[p]))

    akv = {p: _dot(b16(a_ak[p]), vbd[p]) for p in pairs}
    pq = {(j, gi): _dot(b16(tinv[j, gi]), jnp.concatenate([bd(at[j][:, gsl(gi)]), bd(akv[j, gi])], axis=1))
          for (j, gi) in pairs}
    r2, y0, m_lr, n_add = {}, {}, {}, {}
    for (j, gi) in pairs:
        gs = gsl(gi)
        pm, qm = pq[j, gi][:, :GROUP], pq[j, gi][:, GROUP:]
        rr = _dot(b16(a_rb[j, gi]), jnp.concatenate([bd(pm), bd(qm)], axis=1))
        r2[j, gi] = b16(rt[j][:, gs] + rr[:, :GROUP])
        y0[j, gi] = rr[:, GROUP:] + _dot(b16(a_rk[j, gi]), vbd[j, gi])
        m_lr[j, gi] = b16(jnp.where(bd_mask, _dot(b16(pm.T), b16(bw[j][:, gs])), 0.0))
        qv = jnp.concatenate([qm, vj[j][:, gs]], axis=0)
        bk = jnp.concatenate([bw[j][:, gs], kw[j][:, gs]], axis=0)
        n_add[j, gi] = jnp.where(bd_mask, _dot(b16(qv.T), b16(bk)), 0.0)

    s_in = {}
    for gi in range(ng):
        s = s_sc[gi]
        for j in range(nsub):
            s16 = b16(s)
            s_in[j, gi] = s16
            s = s * w_last[j][:, gsl(gi)] + _dot(s16, m_lr[j, gi]) + n_add[j, gi]
        s_sc[gi] = s

    ys = []
    for j in range(nsub):
        yg = [_dot(r2[j, gi], s_in[j, gi], nt=True) + y0[j, gi] for gi in range(ng)]
        ys.append(jnp.concatenate(yg, axis=1))
    y = jnp.concatenate(ys, axis=0) if nsub > 1 else ys[0]

    inv_n = 1.0 / HEAD
    yc = y - _head_sum(y) * inv_n
    yn = yc * lax.rsqrt(_head_sum(yc * yc) * inv_n + GN_EPS)
    yn = yn * lng_ref[...] + lnb_ref[...]
    bonus = _head_sum(r * k2 * rk_ref[...]) * v
    y_ref[...] = (yn + bonus) * g
    sout_ref[...] = s_sc[...]


def _rwkv(p_r, prev0, s0_bd, lp, nsub, t_valid):
    b, t, _ = p_r.shape
    tc = nsub * CHUNK
    ng = RWKV_WIDTH // GROUP
    row = lambda a: a.reshape(1, -1)
    zeros = jnp.zeros((DECAY_RANK, RWKV_WIDTH), F32)
    wwa = jnp.concatenate([jnp.concatenate([lp["rwkv_w_up"], zeros], axis=1),
                           jnp.concatenate([zeros, lp["rwkv_a_up"]], axis=1)], axis=0)
    const = lambda shape: pl.BlockSpec(shape, lambda bi, ci: (0,) * len(shape))
    kern = functools.partial(_rwkv_kernel, nsub=nsub, t_valid=t_valid)
    return pl.pallas_call(
        kern,
        grid=(b, t // tc),
        in_specs=[pl.BlockSpec((None, tc, RWKV_PROJ), lambda bi, ci: (bi, ci, 0)),
                  pl.BlockSpec((None, 1, RWKV_PROJ), lambda bi, ci: (bi, 0, 0)),
                  pl.BlockSpec((None, ng, GROUP, GROUP), lambda bi, ci: (bi, 0, 0, 0)),
                  const((1, RWKV_PROJ)), const((1, RWKV_WIDTH)),
                  const((DECAY_RANK + ICLR_RANK, 2 * RWKV_WIDTH)), const((1, RWKV_WIDTH)),
                  const((GATE_RANK, RWKV_WIDTH)), const((1, RWKV_WIDTH)), const((1, RWKV_WIDTH)),
                  const((1, RWKV_WIDTH)), const((1, RWKV_WIDTH)), const((1, RWKV_WIDTH))],
        out_specs=[pl.BlockSpec((None, tc, RWKV_WIDTH), lambda bi, ci: (bi, ci, 0)),
                   pl.BlockSpec((None, ng, GROUP, GROUP), lambda bi, ci: (bi, 0, 0, 0))],
        out_shape=[jax.ShapeDtypeStruct((b, t, RWKV_WIDTH), F32),
                   jax.ShapeDtypeStruct((b, ng, GROUP, GROUP), F32)],
        scratch_shapes=[pltpu.VMEM((1, RWKV_PROJ), F32), pltpu.VMEM((ng, GROUP, GROUP), F32)],
        compiler_params=_cparams("arbitrary", "arbitrary"),
        name="rwkv7",
    )(p_r, prev0, s0_bd, row(lp["rwkv_mu"]), row(lp["rwkv_w0"]), wwa, row(lp["rwkv_a0"]),
      lp["rwkv_g_up"], row(lp["rwkv_k_k"]), row(lp["rwkv_k_a"]), row(lp["rwkv_r_k"]),
      row(lp["rwkv_ln_g"]), row(lp["rwkv_ln_b"]))


def _state_to_bd(s):
    b = s.shape[0]
    hg = GROUP // HEAD
    s5 = s.reshape(b, RWKV_HEADS // hg, hg, HEAD, HEAD)
    out = jnp.einsum("bghvk,hj->bghvjk", s5, jnp.eye(hg, dtype=s.dtype))
    return out.reshape(b, RWKV_HEADS // hg, GROUP, GROUP)


def _state_from_bd(s_bd):
    b = s_bd.shape[0]
    hg = GROUP // HEAD
    s6 = s_bd.reshape(b, RWKV_HEADS // hg, hg, HEAD, hg, HEAD)
    out = jnp.einsum("bghvjk,hj->bghvk", s6, jnp.eye(hg, dtype=s_bd.dtype))
    return out.reshape(b, RWKV_HEADS, HEAD, HEAD)


def _swa_kernel(tab_ref, sink_ref, pa_ref, ck_ref, cv_ref, bkt_ref, qg_ref, kg_ref,
                ya_ref, kn_ref, kall, vall, bias_sc, *, cq, nqc, has_cache):
    tq = cq * nqc
    band = WINDOW + cq
    bi = pl.program_id(0)
    i = pl.program_id(1)
    lane = _iota((1, LANES), 1)
    low = lane < HEAD

    @pl.when(jnp.logical_and(bi == 0, i == 0))
    def _():
        bkt = bkt_ref[...]
        for h in range(ATTN_HEADS):
            def body(j, acc, h=h):
                return acc + jnp.where(bkt == j, tab_ref[j, h], 0.0)
            bh = lax.fori_loop(0, REL_BUCKETS, body, jnp.zeros((cq, band), F32))
            bias_sc[h // 2, (h % 2) * cq:(h % 2 + 1) * cq, :] = bh

    @pl.when(i == 0)
    def _():
        kall[0:WINDOW, :] = ck_ref[...]
        vall[0:WINDOW, :] = cv_ref[...]

    pa = pa_ref[...]
    q = pa[:, :ATTN_WIDTH]
    k = pa[:, ATTN_WIDTH:ATTN_WIDTH + KV_WIDTH]
    v = pa[:, ATTN_WIDTH + KV_WIDTH:]
    inv_n = 1.0 / HEAD
    qn = q * lax.rsqrt(_head_sum(q * q) * inv_n + RMS_EPS) * qg_ref[...]
    kn = k * lax.rsqrt(_head_sum(k * k) * inv_n + RMS_EPS) * kg_ref[...]
    kn_ref[...] = kn
    kall[WINDOW:WINDOW + tq, :] = kn
    vall[WINDOW:WINDOW + tq, :] = v

    row2 = _iota((2 * cq, 1), 0)
    col = _iota((1, band), 1)
    thr = jnp.where(i == 0, WINDOW, 0)
    k_all = kall[...]
    v_all = vall[...]
    k_rot = pltpu.roll(k_all, HEAD, axis=1)
    v_rot = pltpu.roll(v_all, HEAD, axis=1)
    kd = [jnp.where(low, k_all, k_rot).astype(BF16), jnp.where(low, k_rot, k_all).astype(BF16)]
    vd = [jnp.where(low, v_all, v_rot).astype(BF16), jnp.where(low, v_rot, v_all).astype(BF16)]

    work = [(qc, pi) for qc in range(nqc) for pi in range(ATTN_HEADS // 2)]
    kv_of = lambda pi: (2 * pi) // (ATTN_HEADS // KV_HEADS)
    scores = {}
    for (qc, pi) in work:
        qp = qn[qc * cq:(qc + 1) * cq, pi * LANES:(pi + 1) * LANES]
        qs = jnp.concatenate([jnp.where(low, qp, 0.0), jnp.where(low, 0.0, qp)], axis=0)
        s = _dot(qs.astype(BF16), kd[kv_of(pi)][qc * cq:qc * cq + band], nt=True) * ATTN_SCALE + bias_sc[pi]
        if (not has_cache) and qc * cq < WINDOW:
            s = jnp.where((col + qc * cq) < thr, NEG_INF, s)
        scores[qc, pi] = s
    probs = {}
    for (qc, pi) in work:
        s = scores[qc, pi]
        sink = jnp.where(row2 < cq, sink_ref[0, 2 * pi], sink_ref[0, 2 * pi + 1])
        m = jnp.maximum(jnp.max(s, axis=-1, keepdims=True), sink)
        e = jnp.exp(s - m)
        den = jnp.sum(e, axis=-1, keepdims=True) + jnp.exp(sink - m)
        probs[qc, pi] = (e / den).astype(BF16)
    for (qc, pi) in work:
        o = _dot(probs[qc, pi], vd[kv_of(pi)][qc * cq:qc * cq + band])
        ya_ref[qc * cq:(qc + 1) * cq, pi * LANES:(pi + 1) * LANES] = jnp.where(low, o[:cq], o[cq:])

    if tq >= WINDOW:
        knext = kall[tq:tq + WINDOW, :]
        vnext = vall[tq:tq + WINDOW, :]
        kall[0:WINDOW, :] = knext
        vall[0:WINDOW, :] = vnext


def _t5_bucket(rel):
    half = REL_BUCKETS // 2
    max_exact = half // 2
    n = jnp.abs(rel)
    log_ratio = jnp.log(jnp.maximum(n, 1).astype(F32) / max_exact) / math.log(REL_MAX_DIST / max_exact)
    large = jnp.minimum(max_exact + (log_ratio * (half - max_exact)).astype(I32), half - 1)
    return jnp.where(rel > 0, half, 0) + jnp.where(n < max_exact, n, large)


def _swa(p_a, cache_k, cache_v, rel_table, sinks, q_norm, k_norm, cq, nqc, has_cache):
    b, t, _ = p_a.shape
    tq = cq * nqc
    band = WINDOW + cq
    rel = (jnp.arange(band) - WINDOW)[None, :] - jnp.arange(cq)[:, None]
    bkt = _t5_bucket(rel).astype(I32)
    qg = jnp.tile(q_norm, ATTN_HEADS).reshape(1, ATTN_WIDTH)
    kg = jnp.tile(k_norm, KV_HEADS).reshape(1, KV_WIDTH)
    kern = functools.partial(_swa_kernel, cq=cq, nqc=nqc, has_cache=has_cache)
    smem = pl.BlockSpec(memory_space=pltpu.SMEM)
    return pl.pallas_call(
        kern,
        grid=(b, t // tq),
        in_specs=[smem, smem,
                  pl.BlockSpec((None, tq, ATTN_PROJ), lambda bi, i: (bi, i, 0)),
                  pl.BlockSpec((None, WINDOW, KV_WIDTH), lambda bi, i: (bi, 0, 0)),
                  pl.BlockSpec((None, WINDOW, KV_WIDTH), lambda bi, i: (bi, 0, 0)),
                  pl.BlockSpec((cq, band), lambda bi, i: (0, 0)),
                  pl.BlockSpec((1, ATTN_WIDTH), lambda bi, i: (0, 0)),
                  pl.BlockSpec((1, KV_WIDTH), lambda bi, i: (0, 0))],
        out_specs=[pl.BlockSpec((None, tq, ATTN_WIDTH), lambda bi, i: (bi, i, 0)),
                   pl.BlockSpec((None, tq, KV_WIDTH), lambda bi, i: (bi, i, 0))],
        out_shape=[jax.ShapeDtypeStruct((b, t, ATTN_WIDTH), F32),
                   jax.ShapeDtypeStruct((b, t, KV_WIDTH), F32)],
        scratch_shapes=[pltpu.VMEM((WINDOW + tq, KV_WIDTH), F32),
                        pltpu.VMEM((WINDOW + tq, KV_WIDTH), F32),
                        pltpu.VMEM((ATTN_HEADS // 2, 2 * cq, band), F32)],
        compiler_params=_cparams("arbitrary", "arbitrary"),
        name="swa",
    )(rel_table, sinks.reshape(1, ATTN_HEADS), p_a, cache_k, cache_v, bkt, qg, kg)


def _mid_kernel(yr_ref, ya_ref, x_ref, gt1_ref, sh2_ref, sc2_ref, g2_ref, wor_ref, woa_ref, wrt_ref,
                rb_ref, base0_ref, x1_ref, h2_ref, te_ref, gate_ref, pos_ref, cnt_ref, base_sc, *, tm):
    @pl.when(jnp.logical_and(pl.program_id(0) == 0, pl.program_id(1) == 0))
    def _():
        base_sc[...] = base0_ref[...]

    mix = _dot(yr_ref[...].astype(BF16), wor_ref[...]) + _dot(ya_ref[...].astype(BF16), woa_ref[...])
    x1 = x_ref[...] + gt1_ref[...] * mix
    x1_ref[...] = x1
    h2 = _rms(x1, RMS_EPS) * g2_ref[...] * (1.0 + sc2_ref[...]) + sh2_ref[...]
    h2_ref[...] = h2

    logits = _mm(wrt_ref[...], h2, passes=3, nt=True) + rb_ref[...]
    eidx = _iota((N_EXPERTS, tm), 0)
    work = logits
    tops, hots = [], []
    for j in range(TOP_K):
        m = jnp.max(work, axis=0, keepdims=True)
        idx = jnp.min(jnp.where(work == m, eidx, N_EXPERTS), axis=0, keepdims=True)
        hot = eidx == idx
        work = jnp.where(hot, -jnp.inf, work)
        tops.append(m)
        hots.append(hot)
        te_ref[j:j + 1, :] = idx
    exps = [jnp.exp(tj - tops[0]) for tj in tops]
    den = exps[0] + exps[1] + exps[2] + exps[3]
    for j in range(TOP_K):
        gate_ref[j:j + 1, :] = exps[j] / den

    member = (hots[0] | hots[1] | hots[2] | hots[3]).astype(BF16)
    upper = (_iota((tm, tm), 0) < _iota((tm, tm), 1)).astype(BF16)
    before = _dot(member, upper) + base_sc[...]
    for j in range(TOP_K):
        pos_ref[j:j + 1, :] = jnp.sum(jnp.where(hots[j], before, 0.0), axis=0, keepdims=True).astype(I32)
    base_sc[...] = base_sc[...] + jnp.sum(member.astype(F32), axis=1, keepdims=True)
    cnt_ref[...] = base_sc[...]


def _mid(y_r, y_a, x3, gt1, sh2, sc2, gain2, wo_r16, wo_a16, wr_t, rb, base0, tm):
    g, r, d = x3.shape
    n = g * r
    per = r // tm
    mr = gt1.shape[1]
    mrow = (lambda b, i: (b, 0, 0)) if mr == 1 else (lambda b, i: (b, i, 0))
    mblk = 1 if mr == 1 else tm
    tok = lambda w: pl.BlockSpec((None, tm, w), lambda b, i: (b, i, 0))
    mod = pl.BlockSpec((None, mblk, d), mrow)
    const = lambda shape: pl.BlockSpec(shape, lambda b, i: (0,) * len(shape))
    lane_out = pl.BlockSpec((TOP_K, tm), lambda b, i: (0, b * per + i))
    kern = functools.partial(_mid_kernel, tm=tm)
    return pl.pallas_call(
        kern,
        grid=(g, per),
        in_specs=[tok(RWKV_WIDTH), tok(ATTN_WIDTH), tok(d), mod, mod, mod, const((1, d)),
                  const((RWKV_WIDTH, d)), const((ATTN_WIDTH, d)), const((N_EXPERTS, d)),
                  const((N_EXPERTS, 1)), const((N_EXPERTS, 1))],
        out_specs=[tok(d), tok(d), lane_out, lane_out, lane_out, const((N_EXPERTS, 1))],
        out_shape=[jax.ShapeDtypeStruct((g, r, d), F32), jax.ShapeDtypeStruct((g, r, d), F32),
                   jax.ShapeDtypeStruct((TOP_K, n), I32), jax.ShapeDtypeStruct((TOP_K, n), F32),
                   jax.ShapeDtypeStruct((TOP_K, n), I32), jax.ShapeDtypeStruct((N_EXPERTS, 1), F32)],
        scratch_shapes=[pltpu.VMEM((N_EXPERTS, 1), F32)],
        compiler_params=_cparams("arbitrary", "arbitrary"),
        name="mid",
    )(y_r, y_a, x3, gt1, sh2, sc2, gain2.reshape(1, d), wo_r16, wo_a16, wr_t, rb, base0)


def _slot_kernel(pstart_ref, te_ref, pos_ref, o_ref):
    te = te_ref[...]
    acc = pos_ref[...]
    for e in range(N_EXPERTS):
        acc = acc + jnp.where(te == e, pstart_ref[e], 0)
    o_ref[...] = acc


def _slot_rows(te, pos, pstart):
    k, n = te.shape
    return pl.pallas_call(
        _slot_kernel,
        grid=(1,),
        in_specs=[pl.BlockSpec(memory_space=pltpu.SMEM),
                  pl.BlockSpec((k, n), lambda i: (0, 0)), pl.BlockSpec((k, n), lambda i: (0, 0))],
        out_specs=pl.BlockSpec((k, n), lambda i: (0, 0)),
        out_shape=jax.ShapeDtypeStruct((k, n), I32),
        compiler_params=_cparams("arbitrary"),
        name="slot_rows",
    )(pstart, te, pos)


def _dispatch_kernel(dst_ref, h2_ref, xs_in_ref, xs_ref, sem, *, tk):
    del xs_in_ref

    def body(t8, carry):
        base = pl.multiple_of(t8 * ROW_UNROLL, ROW_UNROLL)
        for r in range(ROW_UNROLL):
            for j in range(TOP_K):
                dst = dst_ref[j, base + r]
                pltpu.make_async_copy(h2_ref.at[pl.ds(base + r, 1)], xs_ref.at[pl.ds(dst, 1)], sem).start()
        return carry

    lax.fori_loop(0, tk // ROW_UNROLL, body, 0)
    for _ in range(TOP_K):
        pltpu.make_async_copy(h2_ref, xs_ref.at[pl.ds(0, tk)], sem).wait()


def _dispatch(dest, h2, xs, tk):
    n, d = h2.shape
    smem_blk = pl.BlockSpec((TOP_K, tk), lambda i: (0, i), memory_space=pltpu.SMEM)
    kern = functools.partial(_dispatch_kernel, tk=tk)
    return pl.pallas_call(
        kern,
        grid=(n // tk,),
        in_specs=[smem_blk, pl.BlockSpec((tk, d), lambda i: (i, 0)), pl.BlockSpec(memory_space=pl.ANY)],
        out_specs=pl.BlockSpec(memory_space=pl.ANY),
        out_shape=jax.ShapeDtypeStruct(xs.shape, xs.dtype),
        scratch_shapes=[pltpu.SemaphoreType.DMA(())],
        input_output_aliases={2: 0},
        compiler_params=_cparams("arbitrary"),
        name="dispatch",
    )(dest, h2, xs)


def _ffn_kernel(te_ref, nu_ref, xs_ref, wu_ref, bu_ref, wd_ref, bd_ref, ys_ref, wu16, wd16):
    i = pl.program_id(0)
    n_up = wu_ref.shape[1]
    active = i < nu_ref[0]
    fresh = jnp.logical_or(i == 0, te_ref[i] != te_ref[jnp.maximum(i - 1, 0)])

    @pl.when(jnp.logical_and(active, fresh))
    def _():
        src = _iota((GROUP, GROUP), 0)
        dst = _iota((GROUP, GROUP), 1)
        want = jnp.where(dst < LANES, 2 * dst, 2 * (dst - LANES) + 1)
        perm = (src == want).astype(BF16)
        for b in range(n_up // GROUP):
            cs = slice(b * GROUP, (b + 1) * GROUP)
            wu16[:, cs] = _dot(wu_ref[:, cs].astype(BF16), perm).astype(BF16)
        wd16[...] = wd_ref[...].astype(BF16)

    @pl.when(active)
    def _():
        u = _dot(xs_ref[...].astype(BF16), wu16[...]) + bu_ref[...]
        acts = []
        for b in range(n_up // GROUP):
            glu = jnp.minimum(u[:, b * GROUP:b * GROUP + LANES], SWIGLU_LIMIT)
            lin = jnp.clip(u[:, b * GROUP + LANES:(b + 1) * GROUP], -SWIGLU_LIMIT, SWIGLU_LIMIT)
            acts.append((glu * jax.nn.sigmoid(SWIGLU_ALPHA * glu) * (lin + 1.0)).astype(BF16))
        act = jnp.concatenate(acts, axis=1)
        ys_ref[...] = _dot(act, wd16[...]) + bd_ref[...]

    @pl.when(jnp.logical_not(active))
    def _():
        ys_ref[...] = jnp.zeros_like(ys_ref)


def _ffn(tile_e, n_used, xs, wu, bu, wd, bd_, tm):
    rows, d = xs.shape
    n_exp, _, n_up = wu.shape
    dff = wd.shape[1]
    row_blk = lambda i, te, nu: (jnp.minimum(i, nu[0] - 1), 0)
    exp_blk = lambda i, te, nu: (te[i], 0, 0)
    grid_spec = pltpu.PrefetchScalarGridSpec(
        num_scalar_prefetch=2,
        grid=(rows // tm,),
        in_specs=[pl.BlockSpec((tm, d), row_blk),
                  pl.BlockSpec((None, d, n_up), exp_blk),
                  pl.BlockSpec((None, 1, n_up), exp_blk),
                  pl.BlockSpec((None, dff, d), exp_blk),
                  pl.BlockSpec((None, 1, d), exp_blk)],
        out_specs=pl.BlockSpec((tm, d), lambda i, te, nu: (i, 0)),
        scratch_shapes=[pltpu.VMEM((d, n_up), BF16), pltpu.VMEM((dff, d), BF16)],
    )
    return pl.pallas_call(
        _ffn_kernel,
        grid_spec=grid_spec,
        out_shape=jax.ShapeDtypeStruct((rows, d), F32),
        compiler_params=_cparams("arbitrary"),
        name="expert_ffn",
    )(tile_e, n_used, xs, wu, bu.reshape(n_exp, 1, n_up), wd, bd_.reshape(n_exp, 1, d))


def _combine_kernel(src_ref, srcn_ref, gate_ref, x1_ref, gt2_ref, ys_ref, o_ref, buf, sem, *, tk):
    i = pl.program_id(1)
    n = pl.num_programs(1)
    flat = pl.program_id(0) * n + i
    total = pl.num_programs(0) * n
    slot = lax.rem(flat, 2)

    def issue(rows_ref, s):
        def body(t8, carry):
            base = pl.multiple_of(t8 * ROW_UNROLL, ROW_UNROLL)
            for r in range(ROW_UNROLL):
                for j in range(TOP_K):
                    src = rows_ref[j, base + r]
                    pltpu.make_async_copy(ys_ref.at[pl.ds(src, 1)], buf.at[s, j, pl.ds(base + r, 1)],
                                          sem.at[s]).start()
            return carry
        lax.fori_loop(0, tk // ROW_UNROLL, body, 0)

    @pl.when(flat == 0)
    def _():
        issue(src_ref, 0)

    @pl.when(flat + 1 < total)
    def _():
        issue(srcn_ref, 1 - slot)

    for j in range(TOP_K):
        pltpu.make_async_copy(ys_ref.at[pl.ds(0, tk)], buf.at[slot, j], sem.at[slot]).wait()

    gate = gate_ref[...]
    acc = gate[:, 0:1] * buf[slot, 0]
    for j in range(1, TOP_K):
        acc = acc + gate[:, j:j + 1] * buf[slot, j]
    o_ref[...] = x1_ref[...] + gt2_ref[...] * acc


def _combine(dest, gates_t, x1, gt2, ys, tk):
    g, r, d = x1.shape
    per = r // tk
    last = g * per - 1
    mr = gt2.shape[1]
    mrow = (lambda b, i: (b, 0, 0)) if mr == 1 else (lambda b, i: (b, i, 0))
    mblk = 1 if mr == 1 else tk
    cur = pl.BlockSpec((TOP_K, tk), lambda b, i: (0, b * per + i), memory_space=pltpu.SMEM)
    nxt = pl.BlockSpec((TOP_K, tk), lambda b, i: (0, jnp.minimum(b * per + i + 1, last)),
                       memory_space=pltpu.SMEM)
    kern = functools.partial(_combine_kernel, tk=tk)
    return pl.pallas_call(
        kern,
        grid=(g, per),
        in_specs=[cur, nxt,
                  pl.BlockSpec((tk, TOP_K), lambda b, i: (b * per + i, 0)),
                  pl.BlockSpec((None, tk, d), lambda b, i: (b, i, 0)),
                  pl.BlockSpec((None, mblk, d), mrow),
                  pl.BlockSpec(memory_space=pl.ANY)],
        out_specs=pl.BlockSpec((None, tk, d), lambda b, i: (b, i, 0)),
        out_shape=jax.ShapeDtypeStruct((g, r, d), F32),
        scratch_shapes=[pltpu.VMEM((2, TOP_K, tk, d), F32), pltpu.SemaphoreType.DMA((2,))],
        compiler_params=_cparams("arbitrary", "arbitrary"),
        name="combine",
    )(dest, dest, gates_t, x1, gt2, ys)


def _tile_plan(counts, tm, n_tiles):
    cnt = counts.reshape(-1).astype(I32)
    tiles = (cnt + tm - 1) // tm
    tile_end = jnp.cumsum(tiles)
    pstart = ((tile_end - tiles) * tm).astype(I32)
    n_used = tile_end[-1:].astype(I32)
    idx = jnp.minimum(jnp.arange(n_tiles, dtype=I32), n_used[0] - 1)
    tile_e = jnp.sum((tile_end[None, :] <= idx[:, None]).astype(I32), axis=1)
    return pstart, jnp.minimum(tile_e, N_EXPERTS - 1), n_used


def kernel(x_prompt, x_sample, state_shift, state_wkv, cache_win_k, cache_win_v, c_prompt, c_sample, rel_bias, norm1_g, norm2_g, w_ada, b_ada, w_in, w_out, rwkv_mu, rwkv_w0, rwkv_w_up, rwkv_a0, rwkv_a_up, rwkv_g_up, rwkv_k_k, rwkv_k_a, rwkv_r_k, rwkv_ln_g, rwkv_ln_b, q_norm, k_norm, sinks, router_w, router_b, exp_w_up, exp_b_up, exp_w_down, exp_b_down):
    bp, tp, d = x_prompt.shape
    bs, ts, _ = x_sample.shape
    depth = w_in.shape[0]
    assert depth == 1, "single-layer trunk"
    layer = 0
    lp = dict(rwkv_mu=rwkv_mu[layer], rwkv_w0=rwkv_w0[layer], rwkv_w_up=rwkv_w_up[layer],
              rwkv_a0=rwkv_a0[layer], rwkv_a_up=rwkv_a_up[layer], rwkv_g_up=rwkv_g_up[layer],
              rwkv_k_k=rwkv_k_k[layer], rwkv_k_a=rwkv_k_a[layer], rwkv_r_k=rwkv_r_k[layer].reshape(-1),
              rwkv_ln_g=rwkv_ln_g[layer], rwkv_ln_b=rwkv_ln_b[layer])

    rows = bp + bs
    pad = (-rows) % 8
    c_all = jnp.concatenate([c_prompt, c_sample, jnp.zeros((pad, d), F32)], axis=0)
    mod = _modulation(c_all, w_ada[layer], b_ada[layer])
    sh1, sc1, gt1, sh2, sc2, gt2 = [mod[:, i * d:(i + 1) * d] for i in range(6)]
    pm = lambda m: m[:bp].reshape(bp, 1, d)
    sm = lambda m: jnp.repeat(m[bp:bp + bs], ts, axis=0).reshape(1, bs * ts, d)

    w_in16 = w_in[layer].astype(BF16)
    wo16 = w_out[layer].astype(BF16)
    wo_r16, wo_a16 = wo16[:RWKV_WIDTH], wo16[RWKV_WIDTH:]
    wr_t = router_w[layer].T
    rb = router_b[layer].reshape(N_EXPERTS, 1)
    wu = exp_w_up.reshape(exp_w_up.shape[1:])
    wd = exp_w_down.reshape(exp_w_down.shape[1:])
    n_up = wu.shape[-1]
    bu = exp_b_up[layer].reshape(N_EXPERTS, n_up // GROUP, LANES, 2).transpose(0, 1, 3, 2).reshape(N_EXPERTS, n_up)

    pr_p, pa_p = _inproj(x_prompt, norm1_g[layer], pm(sh1), pm(sc1), w_in16, TM_PROJ)
    zero_prev = jnp.zeros((bp, 1, RWKV_PROJ), F32)
    zero_state = jnp.zeros((bp, RWKV_WIDTH // GROUP, GROUP, GROUP), F32)
    yr_p, s_p = _rwkv(pr_p, zero_prev, zero_state, lp, RWKV_SUB, None)
    zero_kv = jnp.zeros((bp, WINDOW, KV_WIDTH), F32)
    ya_p, kn_p = _swa(pa_p, zero_kv, zero_kv, rel_bias, sinks[layer], q_norm[layer], k_norm[layer],
                      CHUNK, SWA_QC, False)

    ns = bs * ts
    xs_flat = x_sample.reshape(1, ns, d)
    pr_s, pa_s = _inproj(xs_flat, norm1_g[layer], sm(sh1), sm(sc1), w_in16, ns)
    pr_s = pr_s.reshape(bs, ts, RWKV_PROJ)
    pa_s = pa_s.reshape(bs, ts, ATTN_PROJ)
    t_pad = -(-ts // CHUNK) * CHUNK
    pr_s_pad = jnp.pad(pr_s, ((0, 0), (0, t_pad - ts), (0, 0)))
    yr_s, s_s = _rwkv(pr_s_pad, state_shift[layer].reshape(bs, 1, RWKV_PROJ), _state_to_bd(state_wkv[layer]),
                      lp, t_pad // CHUNK, ts)
    yr_s = yr_s[:, :ts]
    ya_s, kn_s = _swa(pa_s, cache_win_k[layer].reshape(bs, WINDOW, KV_WIDTH),
                      cache_win_v[layer].reshape(bs, WINDOW, KV_WIDTH), rel_bias, sinks[layer],
                      q_norm[layer], k_norm[layer], ts, 1, True)

    base0 = jnp.zeros((N_EXPERTS, 1), F32)
    x1_p, h2_p, te_p, gate_p, pos_p, cnt_p = _mid(yr_p, ya_p, x_prompt, pm(gt1), pm(sh2), pm(sc2), norm2_g[layer],
                                                  wo_r16, wo_a16, wr_t, rb, base0, TM_PROJ)
    x1_s, h2_s, te_s, gate_s, pos_s, cnt = _mid(yr_s.reshape(1, ns, RWKV_WIDTH), ya_s.reshape(1, ns, ATTN_WIDTH),
                                                xs_flat, sm(gt1), sm(sh2), sm(sc2), norm2_g[layer],
                                                wo_r16, wo_a16, wr_t, rb, cnt_p, ns)

    n_p = bp * tp
    n_rows = (n_p + ns) * TOP_K + N_EXPERTS * (TM_FFN - 1)
    n_tiles = -(-n_rows // TM_FFN)
    pstart, tile_e, n_used = _tile_plan(cnt, TM_FFN, n_tiles)
    xs_buf = jnp.zeros((n_tiles * TM_FFN, d), F32)
    dest_p = _slot_rows(te_p, pos_p, pstart)
    dest_s = _slot_rows(te_s, pos_s, pstart)
    xs_buf = _dispatch(dest_p, h2_p.reshape(n_p, d), xs_buf, TK_DISPATCH)
    xs_buf = _dispatch(dest_s, h2_s.reshape(ns, d), xs_buf, ns)
    ys_buf = _ffn(tile_e, n_used, xs_buf, wu, bu, wd, exp_b_down[layer], TM_FFN)
    y_p = _combine(dest_p, gate_p.T, x1_p, pm(gt2), ys_buf, TK_COMBINE)
    y_s = _combine(dest_s, gate_s.T, x1_s, sm(gt2), ys_buf, TK_COMBINE)

    kv4 = lambda z, rows_: z.reshape(z.shape[0], rows_, KV_HEADS, HEAD)[None]
    v_p = pa_p[:, tp - WINDOW:, ATTN_WIDTH + KV_WIDTH:]
    v_s = pa_s[:, :, ATTN_WIDTH + KV_WIDTH:]
    return (y_p, y_s.reshape(bs, ts, d),
            pr_p[:, tp - 1][None], _state_from_bd(s_p)[None], kv4(kn_p[:, tp - WINDOW:], WINDOW), kv4(v_p, WINDOW),
            pr_s[:, ts - 1][None], _state_from_bd(s_s)[None], kv4(kn_s, ts), kv4(v_s, ts))
```

```python
import functools
import math

import jax
import jax.numpy as jnp
from jax import lax
from jax.experimental import pallas as pl
from jax.experimental.pallas import tpu as pltpu

F32 = jnp.float32
BF16 = jnp.bfloat16
I32 = jnp.int32

HEAD = 64
RWKV_HEADS = 8
RWKV_WIDTH = RWKV_HEADS * HEAD
DECAY_RANK = 64
ICLR_RANK = 64
GATE_RANK = 128
RWKV_PROJ = 3 * RWKV_WIDTH + DECAY_RANK + ICLR_RANK + GATE_RANK
ATTN_HEADS = 8
KV_HEADS = 2
ATTN_WIDTH = ATTN_HEADS * HEAD
KV_WIDTH = KV_HEADS * HEAD
ATTN_PROJ = ATTN_WIDTH + 2 * KV_WIDTH
WINDOW = 128
CHUNK = 64
REL_BUCKETS = 32
REL_MAX_DIST = 128
N_EXPERTS = 32
TOP_K = 4
GN_EPS = 64e-5
RMS_EPS = 1e-6
NEG_INF = -1e30
ATTN_SCALE = HEAD ** -0.5
SWIGLU_ALPHA = 1.702
SWIGLU_LIMIT = 7.0

LANES = 128
GROUP = 256
VMEM_LIMIT = 56 * 1024 * 1024

TM_PROJ = 512
RWKV_SUB = 4
SWA_QC = 4
TM_FFN = 512
TK_DISPATCH = 1024
TK_COMBINE = 128
ROW_UNROLL = 8


def _cparams(*sem):
    return pltpu.CompilerParams(dimension_semantics=sem, vmem_limit_bytes=VMEM_LIMIT)


def _dot(a, b, nt=False):
    dims = (((1,), (1,)), ((), ())) if nt else (((1,), (0,)), ((), ()))
    return lax.dot_general(a, b, dims, preferred_element_type=F32)


def _split(x):
    hi = x.astype(BF16)
    lo = (x - hi.astype(F32)).astype(BF16)
    return hi, lo


def _mm(a, b, passes=1, nt=False):
    if passes == 1:
        return _dot(a.astype(BF16), b.astype(BF16), nt)
    ah, al = _split(a)
    bh, bl = _split(b)
    return _dot(ah, bh, nt) + _dot(al, bh, nt) + _dot(ah, bl, nt)


def _mm_exact_rhs(a, b16, nt=False):
    ah, al = _split(a)
    return _dot(ah, b16, nt) + _dot(al, b16, nt)


def _iota(shape, dim):
    return lax.broadcasted_iota(I32, shape, dim)


def _head_ones(n):
    r = lax.shift_right_logical(_iota((n, n), 0), 6)
    c = lax.shift_right_logical(_iota((n, n), 1), 6)
    return (r == c).astype(BF16)


def _head_sum(x):
    width = x.shape[1]
    if width <= GROUP:
        return _mm_exact_rhs(x, _head_ones(width))
    ones = _head_ones(GROUP)
    parts = [_mm_exact_rhs(x[:, g:g + GROUP], ones) for g in range(0, width, GROUP)]
    return jnp.concatenate(parts, axis=1)


def _rms(x, eps):
    return x * lax.rsqrt(jnp.mean(x * x, axis=-1, keepdims=True) + eps)


def _mod_kernel(c_ref, w_ref, b_ref, o_ref):
    c = c_ref[...]
    s = c * jax.nn.sigmoid(c)
    o_ref[...] = _mm(s, w_ref[...], passes=3) + b_ref[...]


def _modulation(c_all, w_ada, b_ada):
    rows, d = c_all.shape
    n = w_ada.shape[1]
    tn = 1536
    return pl.pallas_call(
        _mod_kernel,
        grid=(n // tn,),
        in_specs=[pl.BlockSpec((rows, d), lambda j: (0, 0)),
                  pl.BlockSpec((d, tn), lambda j: (0, j)),
                  pl.BlockSpec((1, tn), lambda j: (0, j))],
        out_specs=pl.BlockSpec((rows, tn), lambda j: (0, j)),
        out_shape=jax.ShapeDtypeStruct((rows, n), F32),
        compiler_params=_cparams("arbitrary"),
        name="modulation",
    )(c_all, w_ada, b_ada.reshape(1, n))


def _inproj_kernel(x_ref, g_ref, sh_ref, sc_ref, w_ref, pr_ref, pa_ref):
    x = x_ref[...]
    h = _rms(x, RMS_EPS) * g_ref[...] * (1.0 + sc_ref[...]) + sh_ref[...]
    p = _dot(h.astype(BF16), w_ref[...])
    pr_ref[...] = p[:, :RWKV_PROJ]
    pa_ref[...] = p[:, RWKV_PROJ:]


def _inproj(x3, gain, sh3, sc3, w_in16, tm):
    g, r, d = x3.shape
    mr = sh3.shape[1]
    mrow = (lambda b, i: (b, 0, 0)) if mr == 1 else (lambda b, i: (b, i, 0))
    mblk = 1 if mr == 1 else tm
    return pl.pallas_call(
        _inproj_kernel,
        grid=(g, r // tm),
        in_specs=[pl.BlockSpec((None, tm, d), lambda b, i: (b, i, 0)),
                  pl.BlockSpec((1, d), lambda b, i: (0, 0)),
                  pl.BlockSpec((None, mblk, d), mrow),
                  pl.BlockSpec((None, mblk, d), mrow),
                  pl.BlockSpec(w_in16.shape, lambda b, i: (0, 0))],
        out_specs=[pl.BlockSpec((None, tm, RWKV_PROJ), lambda b, i: (b, i, 0)),
                   pl.BlockSpec((None, tm, ATTN_PROJ), lambda b, i: (b, i, 0))],
        out_shape=[jax.ShapeDtypeStruct((g, r, RWKV_PROJ), F32),
                   jax.ShapeDtypeStruct((g, r, ATTN_PROJ), F32)],
        compiler_params=_cparams("arbitrary", "arbitrary"),
        name="inproj",
    )(x3, gain.reshape(1, d), sh3, sc3, w_in16)


def _softplus(x):
    return jnp.maximum(x, 0.0) + jnp.log1p(jnp.exp(-jnp.abs(x)))


def _rwkv_kernel(p_ref, prev0_ref, s0_ref, mu_ref, w0_ref, wwa_ref, a0_ref, gup_ref, kk_ref, ka_ref,
                 rk_ref, lng_ref, lnb_ref, y_ref, sout_ref, prev_sc, s_sc, *, nsub, t_valid):
    tc = nsub * CHUNK
    c = pl.program_id(1)

    @pl.when(c == 0)
    def _():
        prev_sc[...] = prev0_ref[...]
        s_sc[...] = s0_ref[...]

    p = p_ref[...]
    row = _iota((tc, 1), 0)
    p_prev = jnp.where(row == 0, prev_sc[...], pltpu.roll(p, 1, axis=0))
    prev_sc[...] = p_ref[tc - 1:tc, :]
    xp = p + (p_prev - p) * mu_ref[...]

    w3 = 3 * RWKV_WIDTH
    r = xp[:, 0:RWKV_WIDTH]
    k = xp[:, RWKV_WIDTH:2 * RWKV_WIDTH]
    v = xp[:, 2 * RWKV_WIDTH:w3]
    z = xp[:, w3:w3 + DECAY_RANK + ICLR_RANK]
    gd = xp[:, w3 + DECAY_RANK + ICLR_RANK:]
    zl = _iota((1, DECAY_RANK + ICLR_RANK), 1)
    zz = jnp.where(zl < DECAY_RANK, jnp.tanh(z), z)
    wa = _mm(zz, wwa_ref[...], passes=3)
    w_log = -_softplus(-(w0_ref[...] + wa[:, :RWKV_WIDTH])) - 0.5
    lw = -jnp.exp(w_log)
    a = jax.nn.sigmoid(a0_ref[...] + wa[:, RWKV_WIDTH:])
    g = _mm(jax.nn.sigmoid(gd), gup_ref[...])
    kk = k * kk_ref[...]
    kk = kk * lax.rsqrt(jnp.maximum(_head_sum(kk * kk), 1e-24))
    k2 = k * (1.0 + (a - 1.0) * ka_ref[...])
    bvec = kk * a
    if t_valid is not None:
        ok = (c * tc + row) < t_valid
        lw = jnp.where(ok, lw, 0.0)
        kk = jnp.where(ok, kk, 0.0)
        bvec = jnp.where(ok, bvec, 0.0)
        k2u = jnp.where(ok, k2, 0.0)
        vu = jnp.where(ok, v, 0.0)
    else:
        k2u, vu = k2, v

    tril16 = (_iota((CHUNK, CHUNK), 0) >= _iota((CHUNK, CHUNK), 1)).astype(BF16)
    bd_mask = (lax.shift_right_logical(_iota((GROUP, GROUP), 0), 6)
               == lax.shift_right_logical(_iota((GROUP, GROUP), 1), 6))
    t_idx = _iota((CHUNK, GROUP), 0)
    s_idx = jnp.bitwise_and(_iota((CHUNK, GROUP), 1), CHUNK - 1)
    strict = t_idx > s_idx
    incl = t_idx >= s_idx
    eye = (t_idx == s_idx).astype(F32)

    bd16 = bd_mask.astype(BF16)

    def bd(x):
        x16 = x.astype(BF16)
        return jnp.concatenate([x16, x16, x16, x16], axis=0) * bd16

    def b16(x):
        return x.astype(BF16)

    ng = RWKV_WIDTH // GROUP
    pairs = [(j, gi) for j in range(nsub) for gi in range(ng)]
    gsl = lambda gi: slice(gi * GROUP, (gi + 1) * GROUP)

    at, bt, kt, rt, bw, kw, vj, w_last = [], [], [], [], [], [], [], []
    for j in range(nsub):
        sl = slice(j * CHUNK, (j + 1) * CHUNK)
        lwj = lw[sl]
        h1 = lwj.astype(BF16)
        r1 = lwj - h1.astype(F32)
        h2 = r1.astype(BF16)
        h3 = (r1 - h2.astype(F32)).astype(BF16)
        cum = _dot(tril16, h1) + _dot(tril16, h2) + _dot(tril16, h3)
        e_cum = jnp.exp(cum)
        e_inv = jnp.exp(-cum)
        wl = e_cum[CHUNK - 1:CHUNK, :]
        at.append(-kk[sl] * jnp.exp(cum - lwj))
        bt.append(bvec[sl] * e_inv)
        kt.append(k2u[sl] * e_inv)
        rt.append(r[sl] * e_cum)
        bw.append(bt[j] * wl)
        kw.append(kt[j] * wl)
        vj.append(vu[sl])
        w_last.append(wl)

    a_ab, a_ak, a_rb, a_rk, vbd = {}, {}, {}, {}, {}
    for (j, gi) in pairs:
        gs = gsl(gi)
        q2 = b16(jnp.concatenate([at[j][:, gs], rt[j][:, gs]], axis=0))
        sc = _dot(q2, jnp.concatenate([bd(bt[j][:, gs]), bd(kt[j][:, gs])], axis=0), nt=True)
        a_ab[j, gi] = jnp.where(strict, sc[:CHUNK, :GROUP], 0.0)
        a_ak[j, gi] = jnp.where(strict, sc[:CHUNK, GROUP:], 0.0)
        a_rb[j, gi] = jnp.where(incl, sc[CHUNK:, :GROUP], 0.0)
        a_rk[j, gi] = jnp.where(incl, sc[CHUNK:, GROUP:], 0.0)
        vbd[j, gi] = bd(vj[j][:, gs])

    tinv = {p: eye + a_ab[p] for p in pairs}
    xpow = {p: _dot(b16(a_ab[p]), bd(a_ab[p])) for p in pairs}
    for step in range(1, 6):
        for p in pairs:
            if step < 5:
                both = _dot(b16(xpow[p]), jnp.concatenate([bd(tinv[p]), bd(xpow[p])], axis=1))
                tinv[p] = tinv[p] + both[:, :GROUP]
                xpow[p] = both[:, GROUP:]
            else:
                tinv[p] = tinv[p] + _dot(b16(xpow[p]), bd(tinv[p]))

    akv = {p: _dot(b16(a_ak[p]), vbd[p]) for p in pairs}
    pq = {(j, gi): _dot(b16(tinv[j, gi]), jnp.concatenate([bd(at[j][:, gsl(gi)]), bd(akv[j, gi])], axis=1))
          for (j, gi) in pairs}
    r2, y0, m_lr, n_add = {}, {}, {}, {}
    for (j, gi) in pairs:
        gs = gsl(gi)
        pm, qm = pq[j, gi][:, :GROUP], pq[j, gi][:, GROUP:]
        rr = _dot(b16(a_rb[j, gi]), jnp.concatenate([bd(pm), bd(qm)], axis=1))
        r2[j, gi] = b16(rt[j][:, gs] + rr[:, :GROUP])
        y0[j, gi] = rr[:, GROUP:] + _dot(b16(a_rk[j, gi]), vbd[j, gi])
        m_lr[j, gi] = b16(jnp.where(bd_mask, _dot(b16(pm.T), b16(bw[j][:, gs])), 0.0))
        qv = jnp.concatenate([qm, vj[j][:, gs]], axis=0)
        bk = jnp.concatenate([bw[j][:, gs], kw[j][:, gs]], axis=0)
        n_add[j, gi] = jnp.where(bd_mask, _dot(b16(qv.T), b16(bk)), 0.0)

    s_in = {}
    for gi in range(ng):
        s = s_sc[gi]
        for j in range(nsub):
            s16 = b16(s)
            s_in[j, gi] = s16
            s = s * w_last[j][:, gsl(gi)] + _dot(s16, m_lr[j, gi]) + n_add[j, gi]
        s_sc[gi] = s

    ys = []
    for j in range(nsub):
        yg = [_dot(r2[j, gi], s_in[j, gi], nt=True) + y0[j, gi] for gi in range(ng)]
        ys.append(jnp.concatenate(yg, axis=1))
    y = jnp.concatenate(ys, axis=0) if nsub > 1 else ys[0]

    inv_n = 1.0 / HEAD
    yc = y - _head_sum(y) * inv_n
    yn = yc * lax.rsqrt(_head_sum(yc * yc) * inv_n + GN_EPS)
    yn = yn * lng_ref[...] + lnb_ref[...]
    bonus = _head_sum(r * k2 * rk_ref[...]) * v
    y_ref[...] = (yn + bonus) * g
    sout_ref[...] = s_sc[...]


def _rwkv(p_r, prev0, s0_bd, lp, nsub, t_valid):
    b, t, _ = p_r.shape
    tc = nsub * CHUNK
    ng = RWKV_WIDTH // GROUP
    row = lambda a: a.reshape(1, -1)
    zeros = jnp.zeros((DECAY_RANK, RWKV_WIDTH), F32)
    wwa = jnp.concatenate([jnp.concatenate([lp["rwkv_w_up"], zeros], axis=1),
                           jnp.concatenate([zeros, lp["rwkv_a_up"]], axis=1)], axis=0)
    const = lambda shape: pl.BlockSpec(shape, lambda bi, ci: (0,) * len(shape))
    kern = functools.partial(_rwkv_kernel, nsub=nsub, t_valid=t_valid)
    return pl.pallas_call(
        kern,
        grid=(b, t // tc),
        in_specs=[pl.BlockSpec((None, tc, RWKV_PROJ), lambda bi, ci: (bi, ci, 0)),
                  pl.BlockSpec((None, 1, RWKV_PROJ), lambda bi, ci: (bi, 0, 0)),
                  pl.BlockSpec((None, ng, GROUP, GROUP), lambda bi, ci: (bi, 0, 0, 0)),
                  const((1, RWKV_PROJ)), const((1, RWKV_WIDTH)),
                  const((DECAY_RANK + ICLR_RANK, 2 * RWKV_WIDTH)), const((1, RWKV_WIDTH)),
                  const((GATE_RANK, RWKV_WIDTH)), const((1, RWKV_WIDTH)), const((1, RWKV_WIDTH)),
                  const((1, RWKV_WIDTH)), const((1, RWKV_WIDTH)), const((1, RWKV_WIDTH))],
        out_specs=[pl.BlockSpec((None, tc, RWKV_WIDTH), lambda bi, ci: (bi, ci, 0)),
                   pl.BlockSpec((None, ng, GROUP, GROUP), lambda bi, ci: (bi, 0, 0, 0))],
        out_shape=[jax.ShapeDtypeStruct((b, t, RWKV_WIDTH), F32),
                   jax.ShapeDtypeStruct((b, ng, GROUP, GROUP), F32)],
        scratch_shapes=[pltpu.VMEM((1, RWKV_PROJ), F32), pltpu.VMEM((ng, GROUP, GROUP), F32)],
        compiler_params=_cparams("arbitrary", "arbitrary"),
        name="rwkv7",
    )(p_r, prev0, s0_bd, row(lp["rwkv_mu"]), row(lp["rwkv_w0"]), wwa, row(lp["rwkv_a0"]),
      lp["rwkv_g_up"], row(lp["rwkv_k_k"]), row(lp["rwkv_k_a"]), row(lp["rwkv_r_k"]),
      row(lp["rwkv_ln_g"]), row(lp["rwkv_ln_b"]))


def _state_to_bd(s):
    b = s.shape[0]
    hg = GROUP // HEAD
    s5 = s.reshape(b, RWKV_HEADS // hg, hg, HEAD, HEAD)
    out = jnp.einsum("bghvk,hj->bghvjk", s5, jnp.eye(hg, dtype=s.dtype))
    return out.reshape(b, RWKV_HEADS // hg, GROUP, GROUP)


def _state_from_bd(s_bd):
    b = s_bd.shape[0]
    hg = GROUP // HEAD
    s6 = s_bd.reshape(b, RWKV_HEADS // hg, hg, HEAD, hg, HEAD)
    out = jnp.einsum("bghvjk,hj->bghvk", s6, jnp.eye(hg, dtype=s_bd.dtype))
    return out.reshape(b, RWKV_HEADS, HEAD, HEAD)


def _swa_kernel(tab_ref, sink_ref, pa_ref, ck_ref, cv_ref, bkt_ref, qg_ref, kg_ref,
                ya_ref, kn_ref, kall, vall, bias_sc, *, cq, nqc, has_cache):
    tq = cq * nqc
    band = WINDOW + cq
    bi = pl.program_id(0)
    i = pl.program_id(1)
    lane = _iota((1, LANES), 1)
    low = lane < HEAD

    @pl.when(jnp.logical_and(bi == 0, i == 0))
    def _():
        bkt = bkt_ref[...]
        for h in range(ATTN_HEADS):
            def body(j, acc, h=h):
                return acc + jnp.where(bkt == j, tab_ref[j, h], 0.0)
            bh = lax.fori_loop(0, REL_BUCKETS, body, jnp.zeros((cq, band), F32))
            bias_sc[h // 2, (h % 2) * cq:(h % 2 + 1) * cq, :] = bh

    @pl.when(i == 0)
    def _():
        kall[0:WINDOW, :] = ck_ref[...]
        vall[0:WINDOW, :] = cv_ref[...]

    pa = pa_ref[...]
    q = pa[:, :ATTN_WIDTH]
    k = pa[:, ATTN_WIDTH:ATTN_WIDTH + KV_WIDTH]
    v = pa[:, ATTN_WIDTH + KV_WIDTH:]
    inv_n = 1.0 / HEAD
    qn = q * lax.rsqrt(_head_sum(q * q) * inv_n + RMS_EPS) * qg_ref[...]
    kn = k * lax.rsqrt(_head_sum(k * k) * inv_n + RMS_EPS) * kg_ref[...]
    kn_ref[...] = kn
    kall[WINDOW:WINDOW + tq, :] = kn
    vall[WINDOW:WINDOW + tq, :] = v

    row2 = _iota((2 * cq, 1), 0)
    col = _iota((1, band), 1)
    thr = jnp.where(i == 0, WINDOW, 0)
    k_all = kall[...]
    v_all = vall[...]
    k_rot = pltpu.roll(k_all, HEAD, axis=1)
    v_rot = pltpu.roll(v_all, HEAD, axis=1)
    kd = [jnp.where(low, k_all, k_rot).astype(BF16), jnp.where(low, k_rot, k_all).astype(BF16)]
    vd = [jnp.where(low, v_all, v_rot).astype(BF16), jnp.where(low, v_rot, v_all).astype(BF16)]

    work = [(qc, pi) for qc in range(nqc) for pi in range(ATTN_HEADS // 2)]
    kv_of = lambda pi: (2 * pi) // (ATTN_HEADS // KV_HEADS)
    scores = {}
    for (qc, pi) in work:
        qp = qn[qc * cq:(qc + 1) * cq, pi * LANES:(pi + 1) * LANES]
        qs = jnp.concatenate([jnp.where(low, qp, 0.0), jnp.where(low, 0.0, qp)], axis=0)
        s = _dot(qs.astype(BF16), kd[kv_of(pi)][qc * cq:qc * cq + band], nt=True) * ATTN_SCALE + bias_sc[pi]
        if (not has_cache) and qc * cq < WINDOW:
            s = jnp.where((col + qc * cq) < thr, NEG_INF, s)
        scores[qc, pi] = s
    probs = {}
    for (qc, pi) in work:
        s = scores[qc, pi]
        sink = jnp.where(row2 < cq, sink_ref[0, 2 * pi], sink_ref[0, 2 * pi + 1])
        m = jnp.maximum(jnp.max(s, axis=-1, keepdims=True), sink)
        e = jnp.exp(s - m)
        den = jnp.sum(e, axis=-1, keepdims=True) + jnp.exp(sink - m)
        probs[qc, pi] = (e / den).astype(BF16)
    for (qc, pi) in work:
        o = _dot(probs[qc, pi], vd[kv_of(pi)][qc * cq:qc * cq + band])
        ya_ref[qc * cq:(qc + 1) * cq, pi * LANES:(pi + 1) * LANES] = jnp.where(low, o[:cq], o[cq:])

    if tq >= WINDOW:
        knext = kall[tq:tq + WINDOW, :]
        vnext = vall[tq:tq + WINDOW, :]
        kall[0:WINDOW, :] = knext
        vall[0:WINDOW, :] = vnext


def _t5_bucket(rel):
    half = REL_BUCKETS // 2
    max_exact = half // 2
    n = jnp.abs(rel)
    log_ratio = jnp.log(jnp.maximum(n, 1).astype(F32) / max_exact) / math.log(REL_MAX_DIST / max_exact)
    large = jnp.minimum(max_exact + (log_ratio * (half - max_exact)).astype(I32), half - 1)
    return jnp.where(rel > 0, half, 0) + jnp.where(n < max_exact, n, large)


def _swa(p_a, cache_k, cache_v, rel_table, sinks, q_norm, k_norm, cq, nqc, has_cache):
    b, t, _ = p_a.shape
    tq = cq * nqc
    band = WINDOW + cq
    rel = (jnp.arange(band) - WINDOW)[None, :] - jnp.arange(cq)[:, None]
    bkt = _t5_bucket(rel).astype(I32)
    qg = jnp.tile(q_norm, ATTN_HEADS).reshape(1, ATTN_WIDTH)
    kg = jnp.tile(k_norm, KV_HEADS).reshape(1, KV_WIDTH)
    kern = functools.partial(_swa_kernel, cq=cq, nqc=nqc, has_cache=has_cache)
    smem = pl.BlockSpec(memory_space=pltpu.SMEM)
    return pl.pallas_call(
        kern,
        grid=(b, t // tq),
        in_specs=[smem, smem,
                  pl.BlockSpec((None, tq, ATTN_PROJ), lambda bi, i: (bi, i, 0)),
                  pl.BlockSpec((None, WINDOW, KV_WIDTH), lambda bi, i: (bi, 0, 0)),
                  pl.BlockSpec((None, WINDOW, KV_WIDTH), lambda bi, i: (bi, 0, 0)),
                  pl.BlockSpec((cq, band), lambda bi, i: (0, 0)),
                  pl.BlockSpec((1, ATTN_WIDTH), lambda bi, i: (0, 0)),
                  pl.BlockSpec((1, KV_WIDTH), lambda bi, i: (0, 0))],
        out_specs=[pl.BlockSpec((None, tq, ATTN_WIDTH), lambda bi, i: (bi, i, 0)),
                   pl.BlockSpec((None, tq, KV_WIDTH), lambda bi, i: (bi, i, 0))],
        out_shape=[jax.ShapeDtypeStruct((b, t, ATTN_WIDTH), F32),
                   jax.ShapeDtypeStruct((b, t, KV_WIDTH), F32)],
        scratch_shapes=[pltpu.VMEM((WINDOW + tq, KV_WIDTH), F32),
                        pltpu.VMEM((WINDOW + tq, KV_WIDTH), F32),
                        pltpu.VMEM((ATTN_HEADS // 2, 2 * cq, band), F32)],
        compiler_params=_cparams("arbitrary", "arbitrary"),
        name="swa",
    )(rel_table, sinks.reshape(1, ATTN_HEADS), p_a, cache_k, cache_v, bkt, qg, kg)


def _mid_kernel(yr_ref, ya_ref, x_ref, gt1_ref, sh2_ref, sc2_ref, g2_ref, wor_ref, woa_ref, wrt_ref,
                rb_ref, base0_ref, x1_ref, h2_ref, te_ref, gate_ref, pos_ref, cnt_ref, base_sc, *, tm):
    @pl.when(jnp.logical_and(pl.program_id(0) == 0, pl.program_id(1) == 0))
    def _():
        base_sc[...] = base0_ref[...]

    mix = _dot(yr_ref[...].astype(BF16), wor_ref[...]) + _dot(ya_ref[...].astype(BF16), woa_ref[...])
    x1 = x_ref[...] + gt1_ref[...] * mix
    x1_ref[...] = x1
    h2 = _rms(x1, RMS_EPS) * g2_ref[...] * (1.0 + sc2_ref[...]) + sh2_ref[...]
    h2_ref[...] = h2

    logits = _mm(wrt_ref[...], h2, passes=3, nt=True) + rb_ref[...]
    eidx = _iota((N_EXPERTS, tm), 0)
    work = logits
    tops, hots = [], []
    for j in range(TOP_K):
        m = jnp.max(work, axis=0, keepdims=True)
        idx = jnp.min(jnp.where(work == m, eidx, N_EXPERTS), axis=0, keepdims=True)
        hot = eidx == idx
        work = jnp.where(hot, -jnp.inf, work)
        tops.append(m)
        hots.append(hot)
        te_ref[j:j + 1, :] = idx
    exps = [jnp.exp(tj - tops[0]) for tj in tops]
    den = exps[0] + exps[1] + exps[2] + exps[3]
    for j in range(TOP_K):
        gate_ref[j:j + 1, :] = exps[j] / den

    member = (hots[0] | hots[1] | hots[2] | hots[3]).astype(BF16)
    upper = (_iota((tm, tm), 0) < _iota((tm, tm), 1)).astype(BF16)
    before = _dot(member, upper) + base_sc[...]
    for j in range(TOP_K):
        pos_ref[j:j + 1, :] = jnp.sum(jnp.where(hots[j], before, 0.0), axis=0, keepdims=True).astype(I32)
    base_sc[...] = base_sc[...] + jnp.sum(member.astype(F32), axis=1, keepdims=True)
    cnt_ref[...] = base_sc[...]


def _mid(y_r, y_a, x3, gt1, sh2, sc2, gain2, wo_r16, wo_a16, wr_t, rb, base0, tm):
    g, r, d = x3.shape
    n = g * r
    per = r // tm
    mr = gt1.shape[1]
    mrow = (lambda b, i: (b, 0, 0)) if mr == 1 else (lambda b, i: (b, i, 0))
    mblk = 1 if mr == 1 else tm
    tok = lambda w: pl.BlockSpec((None, tm, w), lambda b, i: (b, i, 0))
    mod = pl.BlockSpec((None, mblk, d), mrow)
    const = lambda shape: pl.BlockSpec(shape, lambda b, i: (0,) * len(shape))
    lane_out = pl.BlockSpec((TOP_K, tm), lambda b, i: (0, b * per + i))
    kern = functools.partial(_mid_kernel, tm=tm)
    return pl.pallas_call(
        kern,
        grid=(g, per),
        in_specs=[tok(RWKV_WIDTH), tok(ATTN_WIDTH), tok(d), mod, mod, mod, const((1, d)),
                  const((RWKV_WIDTH, d)), const((ATTN_WIDTH, d)), const((N_EXPERTS, d)),
                  const((N_EXPERTS, 1)), const((N_EXPERTS, 1))],
        out_specs=[tok(d), tok(d), lane_out, lane_out, lane_out, const((N_EXPERTS, 1))],
        out_shape=[jax.ShapeDtypeStruct((g, r, d), F32), jax.ShapeDtypeStruct((g, r, d), F32),
                   jax.ShapeDtypeStruct((TOP_K, n), I32), jax.ShapeDtypeStruct((TOP_K, n), F32),
                   jax.ShapeDtypeStruct((TOP_K, n), I32), jax.ShapeDtypeStruct((N_EXPERTS, 1), F32)],
        scratch_shapes=[pltpu.VMEM((N_EXPERTS, 1), F32)],
        compiler_params=_cparams("arbitrary", "arbitrary"),
        name="mid",
    )(y_r, y_a, x3, gt1, sh2, sc2, gain2.reshape(1, d), wo_r16, wo_a16, wr_t, rb, base0)


def _slot_kernel(pstart_ref, te_ref, pos_ref, o_ref):
    te = te_ref[...]
    acc = pos_ref[...]
    for e in range(N_EXPERTS):
        acc = acc + jnp.where(te == e, pstart_ref[e], 0)
    o_ref[...] = acc


def _slot_rows(te, pos, pstart):
    k, n = te.shape
    return pl.pallas_call(
        _slot_kernel,
        grid=(1,),
        in_specs=[pl.BlockSpec(memory_space=pltpu.SMEM),
                  pl.BlockSpec((k, n), lambda i: (0, 0)), pl.BlockSpec((k, n), lambda i: (0, 0))],
        out_specs=pl.BlockSpec((k, n), lambda i: (0, 0)),
        out_shape=jax.ShapeDtypeStruct((k, n), I32),
        compiler_params=_cparams("arbitrary"),
        name="slot_rows",
    )(pstart, te, pos)


def _dispatch_kernel(dst_ref, pad_ref, h2a_ref, h2b_ref, xs_ref, ztile, sem, zsem, *, tk, steps_a):
    i = pl.program_id(0)

    @pl.when(i == 0)
    def _():
        ztile[...] = jnp.zeros_like(ztile)
        zrow = ztile.at[pl.ds(0, 1)]
        tm = ztile.shape[0]

        def per_expert(e, total):
            first = pad_ref[0, e]
            count = pad_ref[1, e]

            def per_row(r, c):
                pltpu.make_async_copy(zrow, xs_ref.at[pl.ds(first + r, 1)], zsem).start()
                return c
            lax.fori_loop(0, count, per_row, 0)
            return total + count
        n_pad = lax.fori_loop(0, N_EXPERTS, per_expert, 0)

        def drain(r, c):
            pltpu.make_async_copy(zrow, xs_ref.at[pl.ds(0, 1)], zsem).wait()
            return c
        lax.fori_loop(0, n_pad, drain, 0)

        def spare(t, c):
            row0 = pl.multiple_of(t * tm, tm)
            pltpu.make_async_copy(ztile, xs_ref.at[pl.ds(row0, tm)], zsem).start()
            return c
        lax.fori_loop(pad_ref[2, 0], pad_ref[2, 1], spare, 0)

        def spare_wait(t, c):
            pltpu.make_async_copy(ztile, xs_ref.at[pl.ds(0, tm)], zsem).wait()
            return c
        lax.fori_loop(pad_ref[2, 0], pad_ref[2, 1], spare_wait, 0)

    def scatter(h2_ref):
        rows = h2_ref.shape[0]

        def body(t8, carry):
            base = pl.multiple_of(t8 * ROW_UNROLL, ROW_UNROLL)
            for r in range(ROW_UNROLL):
                for j in range(TOP_K):
                    dst = dst_ref[j, base + r]
                    pltpu.make_async_copy(h2_ref.at[pl.ds(base + r, 1)], xs_ref.at[pl.ds(dst, 1)], sem).start()
            return carry

        lax.fori_loop(0, rows // ROW_UNROLL, body, 0)
        for _ in range(TOP_K):
            pltpu.make_async_copy(h2_ref, xs_ref.at[pl.ds(0, rows)], sem).wait()

    pl.when(i < steps_a)(functools.partial(scatter, h2a_ref))
    pl.when(i == steps_a)(functools.partial(scatter, h2b_ref))


def _dispatch(dest, pads, h2a, h2b, n_rows, tk):
    na, d = h2a.shape
    nb = h2b.shape[0]
    steps_a = na // tk
    dest = jnp.pad(dest, ((0, 0), (0, (steps_a + 1) * tk - na - nb)))
    kern = functools.partial(_dispatch_kernel, tk=tk, steps_a=steps_a)
    return pl.pallas_call(
        kern,
        grid=(steps_a + 1,),
        in_specs=[pl.BlockSpec((TOP_K, tk), lambda i: (0, i), memory_space=pltpu.SMEM),
                  pl.BlockSpec(memory_space=pltpu.SMEM),
                  pl.BlockSpec((tk, d), lambda i: (jnp.minimum(i, steps_a - 1), 0)),
                  pl.BlockSpec((nb, d), lambda i: (0, 0))],
        out_specs=pl.BlockSpec(memory_space=pl.ANY),
        out_shape=jax.ShapeDtypeStruct((n_rows, d), F32),
        scratch_shapes=[pltpu.VMEM((TM_FFN, d), F32), pltpu.SemaphoreType.DMA(()), pltpu.SemaphoreType.DMA(())],
        compiler_params=_cparams("arbitrary"),
        name="dispatch",
    )(dest, pads, h2a, h2b)


def _ffn_kernel(te_ref, nu_ref, nx_ref, xs_ref, wu_hbm, bu_ref, wd_hbm, bd_ref, ys_ref,
                wu32, wd32, wu16, wd16, sem):
    i = pl.program_id(0)
    n_up = wu32.shape[1]
    active = i < nu_ref[0]
    fresh = jnp.logical_or(i == 0, te_ref[i] != te_ref[jnp.maximum(i - 1, 0)])

    def weight_copies(e):
        return (pltpu.make_async_copy(wu_hbm.at[e], wu32, sem.at[0]),
                pltpu.make_async_copy(wd_hbm.at[e], wd32, sem.at[1]))

    @pl.when(i == 0)
    def _():
        for cp in weight_copies(te_ref[0]):
            cp.start()

    @pl.when(jnp.logical_and(active, fresh))
    def _():
        for cp in weight_copies(te_ref[i]):
            cp.wait()
        src = _iota((GROUP, GROUP), 0)
        dst = _iota((GROUP, GROUP), 1)
        want = jnp.where(dst < LANES, 2 * dst, 2 * (dst - LANES) + 1)
        perm = (src == want).astype(BF16)
        for b in range(n_up // GROUP):
            cs = slice(b * GROUP, (b + 1) * GROUP)
            wu16[:, cs] = _dot(wu32[:, cs].astype(BF16), perm).astype(BF16)
        wd16[...] = wd32[...].astype(BF16)

        @pl.when(nx_ref[i] >= 0)
        def _():
            for cp in weight_copies(nx_ref[i]):
                cp.start()

    @pl.when(active)
    def _():
        u = _dot(xs_ref[...].astype(BF16), wu16[...]) + bu_ref[...]
        acts = []
        for b in range(n_up // GROUP):
            glu = jnp.minimum(u[:, b * GROUP:b * GROUP + LANES], SWIGLU_LIMIT)
            lin = jnp.clip(u[:, b * GROUP + LANES:(b + 1) * GROUP], -SWIGLU_LIMIT, SWIGLU_LIMIT)
            acts.append((glu * jax.nn.sigmoid(SWIGLU_ALPHA * glu) * (lin + 1.0)).astype(BF16))
        act = jnp.concatenate(acts, axis=1)
        ys_ref[...] = _dot(act, wd16[...]) + bd_ref[...]

    @pl.when(jnp.logical_not(active))
    def _():
        ys_ref[...] = jnp.zeros_like(ys_ref)


def _ffn(tile_e, n_used, next_e, xs, wu, bu, wd, bd_, tm):
    rows, d = xs.shape
    n_exp, _, n_up = wu.shape
    dff = wd.shape[1]
    row_blk = lambda i, te, nu, nx: (jnp.minimum(i, nu[0] - 1), 0)
    exp_blk = lambda i, te, nu, nx: (te[i], 0, 0)
    grid_spec = pltpu.PrefetchScalarGridSpec(
        num_scalar_prefetch=3,
        grid=(rows // tm,),
        in_specs=[pl.BlockSpec((tm, d), row_blk),
                  pl.BlockSpec(memory_space=pl.ANY),
                  pl.BlockSpec((None, 1, n_up), exp_blk),
                  pl.BlockSpec(memory_space=pl.ANY),
                  pl.BlockSpec((None, 1, d), exp_blk)],
        out_specs=pl.BlockSpec((tm, d), lambda i, te, nu, nx: (i, 0)),
        scratch_shapes=[pltpu.VMEM((d, n_up), F32), pltpu.VMEM((dff, d), F32),
                        pltpu.VMEM((d, n_up), BF16), pltpu.VMEM((dff, d), BF16),
                        pltpu.SemaphoreType.DMA((2,))],
    )
    return pl.pallas_call(
        _ffn_kernel,
        grid_spec=grid_spec,
        out_shape=jax.ShapeDtypeStruct((rows, d), F32),
        compiler_params=_cparams("arbitrary"),
        name="expert_ffn",
    )(tile_e, n_used, next_e, xs, wu, bu.reshape(n_exp, 1, n_up), wd, bd_.reshape(n_exp, 1, d))


def _combine_kernel(src_ref, srcn_ref, gate_ref, x1_ref, gt2_ref, ys_ref, o_ref, buf, sem, *, tk):
    i = pl.program_id(1)
    n = pl.num_programs(1)
    flat = pl.program_id(0) * n + i
    total = pl.num_programs(0) * n
    slot = lax.rem(flat, 2)

    def issue(rows_ref, s):
        def body(t8, carry):
            base = pl.multiple_of(t8 * ROW_UNROLL, ROW_UNROLL)
            for r in range(ROW_UNROLL):
                for j in range(TOP_K):
                    src = rows_ref[j, base + r]
                    pltpu.make_async_copy(ys_ref.at[pl.ds(src, 1)], buf.at[s, j, pl.ds(base + r, 1)],
                                          sem.at[s]).start()
            return carry
        lax.fori_loop(0, tk // ROW_UNROLL, body, 0)

    @pl.when(flat == 0)
    def _():
        issue(src_ref, 0)

    def step(cur):
        @pl.when(flat + 1 < total)
        def _():
            issue(srcn_ref, 1 - cur)

        for j in range(TOP_K):
            pltpu.make_async_copy(ys_ref.at[pl.ds(0, tk)], buf.at[cur, j], sem.at[cur]).wait()
        gate = gate_ref[...]
        acc = gate[:, 0:1] * buf[cur, 0]
        for j in range(1, TOP_K):
            acc = acc + gate[:, j:j + 1] * buf[cur, j]
        o_ref[...] = x1_ref[...] + gt2_ref[...] * acc

    for cur in range(2):
        pl.when(slot == cur)(functools.partial(step, cur))


def _combine(dest, gates_t, x1, gt2, ys, tk):
    g, r, d = x1.shape
    per = r // tk
    last = g * per - 1
    mr = gt2.shape[1]
    mrow = (lambda b, i: (b, 0, 0)) if mr == 1 else (lambda b, i: (b, i, 0))
    mblk = 1 if mr == 1 else tk
    cur = pl.BlockSpec((TOP_K, tk), lambda b, i: (0, b * per + i), memory_space=pltpu.SMEM)
    nxt = pl.BlockSpec((TOP_K, tk), lambda b, i: (0, jnp.minimum(b * per + i + 1, last)),
                       memory_space=pltpu.SMEM)
    kern = functools.partial(_combine_kernel, tk=tk)
    return pl.pallas_call(
        kern,
        grid=(g, per),
        in_specs=[cur, nxt,
                  pl.BlockSpec((tk, TOP_K), lambda b, i: (b * per + i, 0)),
                  pl.BlockSpec((None, tk, d), lambda b, i: (b, i, 0)),
                  pl.BlockSpec((None, mblk, d), mrow),
                  pl.BlockSpec(memory_space=pl.ANY)],
        out_specs=pl.BlockSpec((None, tk, d), lambda b, i: (b, i, 0)),
        out_shape=jax.ShapeDtypeStruct((g, r, d), F32),
        scratch_shapes=[pltpu.VMEM((2, TOP_K, tk, d), F32), pltpu.SemaphoreType.DMA((2,))],
        compiler_params=_cparams("arbitrary", "arbitrary"),
        name="combine",
    )(dest, dest, gates_t, x1, gt2, ys)


def _tile_plan(counts, tm, n_tiles):
    cnt = counts.reshape(-1).astype(I32)
    tiles = (cnt + tm - 1) // tm
    tile_end = jnp.cumsum(tiles)
    pstart = ((tile_end - tiles) * tm).astype(I32)
    n_used = tile_end[-1:].astype(I32)
    idx = jnp.minimum(jnp.arange(n_tiles, dtype=I32), n_used[0] - 1)
    tile_e = jnp.sum((tile_end[None, :] <= idx[:, None]).astype(I32), axis=1)
    spare = jnp.zeros((N_EXPERTS,), I32).at[0].set(n_used[0]).at[1].set(n_tiles)
    pads = jnp.stack([pstart + cnt, tiles * tm - cnt, spare]).astype(I32)
    tile_e = jnp.minimum(tile_e, N_EXPERTS - 1)
    end_of = tile_end[tile_e]
    next_e = jnp.where(end_of < n_used[0], tile_e[jnp.minimum(end_of, n_tiles - 1)], -1).astype(I32)
    return pstart, tile_e, n_used, next_e, pads


def kernel(x_prompt, x_sample, state_shift, state_wkv, cache_win_k, cache_win_v, c_prompt, c_sample, rel_bias, norm1_g, norm2_g, w_ada, b_ada, w_in, w_out, rwkv_mu, rwkv_w0, rwkv_w_up, rwkv_a0, rwkv_a_up, rwkv_g_up, rwkv_k_k, rwkv_k_a, rwkv_r_k, rwkv_ln_g, rwkv_ln_b, q_norm, k_norm, sinks, router_w, router_b, exp_w_up, exp_b_up, exp_w_down, exp_b_down):
    bp, tp, d = x_prompt.shape
    bs, ts, _ = x_sample.shape
    depth = w_in.shape[0]
    assert depth == 1, "single-layer trunk"
    layer = 0
    lp = dict(rwkv_mu=rwkv_mu[layer], rwkv_w0=rwkv_w0[layer], rwkv_w_up=rwkv_w_up[layer],
              rwkv_a0=rwkv_a0[layer], rwkv_a_up=rwkv_a_up[layer], rwkv_g_up=rwkv_g_up[layer],
              rwkv_k_k=rwkv_k_k[layer], rwkv_k_a=rwkv_k_a[layer], rwkv_r_k=rwkv_r_k[layer].reshape(-1),
              rwkv_ln_g=rwkv_ln_g[layer], rwkv_ln_b=rwkv_ln_b[layer])

    rows = bp + bs
    pad = (-rows) % 8
    c_all = jnp.concatenate([c_prompt, c_sample, jnp.zeros((pad, d), F32)], axis=0)
    mod = _modulation(c_all, w_ada[layer], b_ada[layer])
    sh1, sc1, gt1, sh2, sc2, gt2 = [mod[:, i * d:(i + 1) * d] for i in range(6)]
    pm = lambda m: m[:bp].reshape(bp, 1, d)
    sm = lambda m: jnp.repeat(m[bp:bp + bs], ts, axis=0).reshape(1, bs * ts, d)

    w_in16 = w_in[layer].astype(BF16)
    wo16 = w_out[layer].astype(BF16)
    wo_r16, wo_a16 = wo16[:RWKV_WIDTH], wo16[RWKV_WIDTH:]
    wr_t = router_w[layer].T
    rb = router_b[layer].reshape(N_EXPERTS, 1)
    wu = exp_w_up.reshape(exp_w_up.shape[1:])
    wd = exp_w_down.reshape(exp_w_down.shape[1:])
    n_up = wu.shape[-1]
    bu = exp_b_up[layer].reshape(N_EXPERTS, n_up // GROUP, LANES, 2).transpose(0, 1, 3, 2).reshape(N_EXPERTS, n_up)

    pr_p, pa_p = _inproj(x_prompt, norm1_g[layer], pm(sh1), pm(sc1), w_in16, TM_PROJ)
    zero_prev = jnp.zeros((bp, 1, RWKV_PROJ), F32)
    zero_state = jnp.zeros((bp, RWKV_WIDTH // GROUP, GROUP, GROUP), F32)
    yr_p, s_p = _rwkv(pr_p, zero_prev, zero_state, lp, RWKV_SUB, None)
    zero_kv = jnp.zeros((bp, WINDOW, KV_WIDTH), F32)
    ya_p, kn_p = _swa(pa_p, zero_kv, zero_kv, rel_bias, sinks[layer], q_norm[layer], k_norm[layer],
                      CHUNK, SWA_QC, False)

    ns = bs * ts
    xs_flat = x_sample.reshape(1, ns, d)
    pr_s, pa_s = _inproj(xs_flat, norm1_g[layer], sm(sh1), sm(sc1), w_in16, ns)
    pr_s = pr_s.reshape(bs, ts, RWKV_PROJ)
    pa_s = pa_s.reshape(bs, ts, ATTN_PROJ)
    t_pad = -(-ts // CHUNK) * CHUNK
    pr_s_pad = jnp.pad(pr_s, ((0, 0), (0, t_pad - ts), (0, 0)))
    yr_s, s_s = _rwkv(pr_s_pad, state_shift[layer].reshape(bs, 1, RWKV_PROJ), _state_to_bd(state_wkv[layer]),
                      lp, t_pad // CHUNK, ts)
    yr_s = yr_s[:, :ts]
    ya_s, kn_s = _swa(pa_s, cache_win_k[layer].reshape(bs, WINDOW, KV_WIDTH),
                      cache_win_v[layer].reshape(bs, WINDOW, KV_WIDTH), rel_bias, sinks[layer],
                      q_norm[layer], k_norm[layer], ts, 1, True)

    base0 = jnp.zeros((N_EXPERTS, 1), F32)
    x1_p, h2_p, te_p, gate_p, pos_p, cnt_p = _mid(yr_p, ya_p, x_prompt, pm(gt1), pm(sh2), pm(sc2), norm2_g[layer],
                                                  wo_r16, wo_a16, wr_t, rb, base0, TM_PROJ)
    x1_s, h2_s, te_s, gate_s, pos_s, cnt = _mid(yr_s.reshape(1, ns, RWKV_WIDTH), ya_s.reshape(1, ns, ATTN_WIDTH),
                                                xs_flat, sm(gt1), sm(sh2), sm(sc2), norm2_g[layer],
                                                wo_r16, wo_a16, wr_t, rb, cnt_p, ns)

    n_p = bp * tp
    n_rows = (n_p + ns) * TOP_K + N_EXPERTS * (TM_FFN - 1)
    n_tiles = -(-n_rows // TM_FFN)
    pstart, tile_e, n_used, next_e, pads = _tile_plan(cnt, TM_FFN, n_tiles)
    dest = _slot_rows(jnp.concatenate([te_p, te_s], axis=1), jnp.concatenate([pos_p, pos_s], axis=1), pstart)
    dest_p, dest_s = dest[:, :n_p], dest[:, n_p:]
    xs_buf = _dispatch(dest, pads, h2_p.reshape(n_p, d), h2_s.reshape(ns, d), n_tiles * TM_FFN, TK_DISPATCH)
    ys_buf = _ffn(tile_e, n_used, next_e, xs_buf, wu, bu, wd, exp_b_down[layer], TM_FFN)
    y_p = _combine(dest_p, gate_p.T, x1_p, pm(gt2), ys_buf, TK_COMBINE)
    y_s = _combine(dest_s, gate_s.T, x1_s, sm(gt2), ys_buf, TK_COMBINE)

    kv4 = lambda z, rows_: z.reshape(z.shape[0], rows_, KV_HEADS, HEAD)[None]
    v_p = pa_p[:, tp - WINDOW:, ATTN_WIDTH + KV_WIDTH:]
    v_s = pa_s[:, :, ATTN_WIDTH + KV_WIDTH:]
    return (y_p, y_s.reshape(bs, ts, d),
            pr_p[:, tp - 1][None], _state_from_bd(s_p)[None], kv4(kn_p[:, tp - WINDOW:], WINDOW), kv4(v_p, WINDOW),
            pr_s[:, ts - 1][None], _state_from_bd(s_s)[None], kv4(kn_s, ts), kv4(v_s, ts))
```

```python
import functools
import math

import jax
import jax.numpy as jnp
from jax import lax
from jax.experimental import pallas as pl
from jax.experimental.pallas import tpu as pltpu

F32 = jnp.float32
BF16 = jnp.bfloat16
I32 = jnp.int32

HEAD = 64
RWKV_HEADS = 8
RWKV_WIDTH = RWKV_HEADS * HEAD
DECAY_RANK = 64
ICLR_RANK = 64
GATE_RANK = 128
RWKV_PROJ = 3 * RWKV_WIDTH + DECAY_RANK + ICLR_RANK + GATE_RANK
ATTN_HEADS = 8
KV_HEADS = 2
ATTN_WIDTH = ATTN_HEADS * HEAD
KV_WIDTH = KV_HEADS * HEAD
ATTN_PROJ = ATTN_WIDTH + 2 * KV_WIDTH
WINDOW = 128
CHUNK = 64
REL_BUCKETS = 32
REL_MAX_DIST = 128
N_EXPERTS = 32
TOP_K = 4
GN_EPS = 64e-5
RMS_EPS = 1e-6
NEG_INF = -1e30
ATTN_SCALE = HEAD ** -0.5
SWIGLU_ALPHA = 1.702
SWIGLU_LIMIT = 7.0

LANES = 128
GROUP = 256
VMEM_LIMIT = 56 * 1024 * 1024

TM_PROJ = 512
RWKV_SUB = 4
SWA_QC = 4
TM_FFN = 512
TK_DISPATCH = 512
TK_COMBINE = 128
ROW_UNROLL = 8


def _cparams(*sem):
    return pltpu.CompilerParams(dimension_semantics=sem, vmem_limit_bytes=VMEM_LIMIT)


def _dot(a, b, nt=False):
    dims = (((1,), (1,)), ((), ())) if nt else (((1,), (0,)), ((), ()))
    return lax.dot_general(a, b, dims, preferred_element_type=F32)


def _split(x):
    hi = x.astype(BF16)
    lo = (x - hi.astype(F32)).astype(BF16)
    return hi, lo


def _mm(a, b, passes=1, nt=False):
    if passes == 1:
        return _dot(a.astype(BF16), b.astype(BF16), nt)
    ah, al = _split(a)
    bh, bl = _split(b)
    return _dot(ah, bh, nt) + _dot(al, bh, nt) + _dot(ah, bl, nt)


def _mm_exact_rhs(a, b16, nt=False):
    ah, al = _split(a)
    return _dot(ah, b16, nt) + _dot(al, b16, nt)


def _iota(shape, dim):
    return lax.broadcasted_iota(I32, shape, dim)


def _head_ones(n):
    r = lax.shift_right_logical(_iota((n, n), 0), 6)
    c = lax.shift_right_logical(_iota((n, n), 1), 6)
    return (r == c).astype(BF16)


def _head_sum(x):
    width = x.shape[1]
    if width <= GROUP:
        return _mm_exact_rhs(x, _head_ones(width))
    ones = _head_ones(GROUP)
    parts = [_mm_exact_rhs(x[:, g:g + GROUP], ones) for g in range(0, width, GROUP)]
    return jnp.concatenate(parts, axis=1)


def _rms(x, eps):
    return x * lax.rsqrt(jnp.mean(x * x, axis=-1, keepdims=True) + eps)


def _mod_kernel(c_ref, w_ref, b_ref, o_ref):
    c = c_ref[...]
    s = c * jax.nn.sigmoid(c)
    o_ref[...] = _mm(s, w_ref[...], passes=3) + b_ref[...]


def _modulation(c_all, w_ada, b_ada):
    rows, d = c_all.shape
    n = w_ada.shape[1]
    tn = 1536
    return pl.pallas_call(
        _mod_kernel,
        grid=(n // tn,),
        in_specs=[pl.BlockSpec((rows, d), lambda j: (0, 0)),
                  pl.BlockSpec((d, tn), lambda j: (0, j)),
                  pl.BlockSpec((1, tn), lambda j: (0, j))],
        out_specs=pl.BlockSpec((rows, tn), lambda j: (0, j)),
        out_shape=jax.ShapeDtypeStruct((rows, n), F32),
        compiler_params=_cparams("arbitrary"),
        name="modulation",
    )(c_all, w_ada, b_ada.reshape(1, n))


def _inproj_kernel(x_ref, g_ref, sh_ref, sc_ref, w_ref, pr_ref, pa_ref):
    x = x_ref[...]
    h = _rms(x, RMS_EPS) * g_ref[...] * (1.0 + sc_ref[...]) + sh_ref[...]
    p = _dot(h.astype(BF16), w_ref[...])
    pr_ref[...] = p[:, :RWKV_PROJ]
    pa_ref[...] = p[:, RWKV_PROJ:]


def _inproj(x3, gain, sh3, sc3, w_in16, tm):
    g, r, d = x3.shape
    mr = sh3.shape[1]
    mrow = (lambda b, i: (b, 0, 0)) if mr == 1 else (lambda b, i: (b, i, 0))
    mblk = 1 if mr == 1 else tm
    return pl.pallas_call(
        _inproj_kernel,
        grid=(g, r // tm),
        in_specs=[pl.BlockSpec((None, tm, d), lambda b, i: (b, i, 0)),
                  pl.BlockSpec((1, d), lambda b, i: (0, 0)),
                  pl.BlockSpec((None, mblk, d), mrow),
                  pl.BlockSpec((None, mblk, d), mrow),
                  pl.BlockSpec(w_in16.shape, lambda b, i: (0, 0))],
        out_specs=[pl.BlockSpec((None, tm, RWKV_PROJ), lambda b, i: (b, i, 0)),
                   pl.BlockSpec((None, tm, ATTN_PROJ), lambda b, i: (b, i, 0))],
        out_shape=[jax.ShapeDtypeStruct((g, r, RWKV_PROJ), F32),
                   jax.ShapeDtypeStruct((g, r, ATTN_PROJ), F32)],
        compiler_params=_cparams("arbitrary", "arbitrary"),
        name="inproj",
    )(x3, gain.reshape(1, d), sh3, sc3, w_in16)


def _softplus(x):
    return jnp.maximum(x, 0.0) + jnp.log1p(jnp.exp(-jnp.abs(x)))


def _rwkv_kernel(p_ref, prev0_ref, s0_ref, mu_ref, w0_ref, wwa_ref, a0_ref, gup_ref, kk_ref, ka_ref,
                 rk_ref, lng_ref, lnb_ref, y_ref, sout_ref, prev_sc, s_sc, *, nsub, t_valid):
    tc = nsub * CHUNK
    c = pl.program_id(1)

    @pl.when(c == 0)
    def _():
        prev_sc[...] = prev0_ref[...]
        s_sc[...] = s0_ref[...]

    p = p_ref[...]
    row = _iota((tc, 1), 0)
    p_prev = jnp.where(row == 0, prev_sc[...], pltpu.roll(p, 1, axis=0))
    prev_sc[...] = p_ref[tc - 1:tc, :]
    xp = p + (p_prev - p) * mu_ref[...]

    w3 = 3 * RWKV_WIDTH
    r = xp[:, 0:RWKV_WIDTH]
    k = xp[:, RWKV_WIDTH:2 * RWKV_WIDTH]
    v = xp[:, 2 * RWKV_WIDTH:w3]
    z = xp[:, w3:w3 + DECAY_RANK + ICLR_RANK]
    gd = xp[:, w3 + DECAY_RANK + ICLR_RANK:]
    zl = _iota((1, DECAY_RANK + ICLR_RANK), 1)
    zz = jnp.where(zl < DECAY_RANK, jnp.tanh(z), z)
    wa = _mm(zz, wwa_ref[...], passes=3)
    w_log = -_softplus(-(w0_ref[...] + wa[:, :RWKV_WIDTH])) - 0.5
    lw = -jnp.exp(w_log)
    a = jax.nn.sigmoid(a0_ref[...] + wa[:, RWKV_WIDTH:])
    g = _mm(jax.nn.sigmoid(gd), gup_ref[...])
    kk = k * kk_ref[...]
    kk = kk * lax.rsqrt(jnp.maximum(_head_sum(kk * kk), 1e-24))
    k2 = k * (1.0 + (a - 1.0) * ka_ref[...])
    bvec = kk * a
    if t_valid is not None:
        ok = (c * tc + row) < t_valid
        lw = jnp.where(ok, lw, 0.0)
        kk = jnp.where(ok, kk, 0.0)
        bvec = jnp.where(ok, bvec, 0.0)
        k2u = jnp.where(ok, k2, 0.0)
        vu = jnp.where(ok, v, 0.0)
    else:
        k2u, vu = k2, v

    tril16 = (_iota((CHUNK, CHUNK), 0) >= _iota((CHUNK, CHUNK), 1)).astype(BF16)
    bd_mask = (lax.shift_right_logical(_iota((GROUP, GROUP), 0), 6)
               == lax.shift_right_logical(_iota((GROUP, GROUP), 1), 6))
    t_idx = _iota((CHUNK, GROUP), 0)
    s_idx = jnp.bitwise_and(_iota((CHUNK, GROUP), 1), CHUNK - 1)
    strict = t_idx > s_idx
    incl = t_idx >= s_idx
    eye = (t_idx == s_idx).astype(F32)

    bd16 = bd_mask.astype(BF16)

    def bd(x):
        x16 = x.astype(BF16)
        return jnp.concatenate([x16, x16, x16, x16], axis=0) * bd16

    def b16(x):
        return x.astype(BF16)

    ng = RWKV_WIDTH // GROUP
    pairs = [(j, gi) for j in range(nsub) for gi in range(ng)]
    gsl = lambda gi: slice(gi * GROUP, (gi + 1) * GROUP)

    at, bt, kt, rt, bw, kw, vj, w_last = [], [], [], [], [], [], [], []
    for j in range(nsub):
        sl = slice(j * CHUNK, (j + 1) * CHUNK)
        lwj = lw[sl]
        h1 = lwj.astype(BF16)
        r1 = lwj - h1.astype(F32)
        h2 = r1.astype(BF16)
        h3 = (r1 - h2.astype(F32)).astype(BF16)
        cum = _dot(tril16, h1) + _dot(tril16, h2) + _dot(tril16, h3)
        e_cum = jnp.exp(cum)
        e_inv = jnp.exp(-cum)
        wl = e_cum[CHUNK - 1:CHUNK, :]
        at.append(-kk[sl] * jnp.exp(cum - lwj))
        bt.append(bvec[sl] * e_inv)
        kt.append(k2u[sl] * e_inv)
        rt.append(r[sl] * e_cum)
        bw.append(bt[j] * wl)
        kw.append(kt[j] * wl)
        vj.append(vu[sl])
        w_last.append(wl)

    a_ab, a_ak, a_rb, a_rk, vbd = {}, {}, {}, {}, {}
    for (j, gi) in pairs:
        gs = gsl(gi)
        q2 = b16(jnp.concatenate([at[j][:, gs], rt[j][:, gs]], axis=0))
        sc = _dot(q2, jnp.concatenate([bd(bt[j][:, gs]), bd(kt[j][:, gs])], axis=0), nt=True)
        a_ab[j, gi] = jnp.where(strict, sc[:CHUNK, :GROUP], 0.0)
        a_ak[j, gi] = jnp.where(strict, sc[:CHUNK, GROUP:], 0.0)
        a_rb[j, gi] = jnp.where(incl, sc[CHUNK:, :GROUP], 0.0)
        a_rk[j, gi] = jnp.where(incl, sc[CHUNK:, GROUP:], 0.0)
        vbd[j, gi] = bd(vj[j][:, gs])

    tinv = {p: eye + a_ab[p] for p in pairs}
    xpow = {p: _dot(b16(a_ab[p]), bd(a_ab[p])) for p in pairs}
    for step in range(1, 6):
        for p in pairs:
            if step < 5:
                both = _dot(b16(xpow[p]), jnp.concatenate([bd(tinv[p]), bd(xpow[p])], axis=1))
                tinv[p] = tinv[p] + both[:, :GROUP]
                xpow[p] = both[:, GROUP:]
            else:
                tinv[p] = tinv[p] + _dot(b16(xpow[p]), bd(tinv[p]))

    akv = {p: _dot(b16(a_ak[p]), vbd[p]) for p in pairs}
    pq = {(j, gi): _dot(b16(tinv[j, gi]), jnp.concatenate([bd(at[j][:, gsl(gi)]), bd(akv[j, gi])], axis=1))
          for (j, gi) in pairs}
    r2, y0, m_lr, n_add = {}, {}, {}, {}
    for (j, gi) in pairs:
        gs = gsl(gi)
        pm, qm = pq[j, gi][:, :GROUP], pq[j, gi][:, GROUP:]
        rr = _dot(b16(a_rb[j, gi]), jnp.concatenate([bd(pm), bd(qm)], axis=1))
        r2[j, gi] = b16(rt[j][:, gs] + rr[:, :GROUP])
        y0[j, gi] = rr[:, GROUP:] + _dot(b16(a_rk[j, gi]), vbd[j, gi])
        m_lr[j, gi] = b16(jnp.where(bd_mask, _dot(b16(pm.T), b16(bw[j][:, gs])), 0.0))
        qv = jnp.concatenate([qm, vj[j][:, gs]], axis=0)
        bk = jnp.concatenate([bw[j][:, gs], kw[j][:, gs]], axis=0)
        n_add[j, gi] = jnp.where(bd_mask, _dot(b16(qv.T), b16(bk)), 0.0)

    s_in = {}
    for gi in range(ng):
        s = s_sc[gi]
        for j in range(nsub):
            s16 = b16(s)
            s_in[j, gi] = s16
            s = s * w_last[j][:, gsl(gi)] + _dot(s16, m_lr[j, gi]) + n_add[j, gi]
        s_sc[gi] = s

    ys = []
    for j in range(nsub):
        yg = [_dot(r2[j, gi], s_in[j, gi], nt=True) + y0[j, gi] for gi in range(ng)]
        ys.append(jnp.concatenate(yg, axis=1))
    y = jnp.concatenate(ys, axis=0) if nsub > 1 else ys[0]

    inv_n = 1.0 / HEAD
    yc = y - _head_sum(y) * inv_n
    yn = yc * lax.rsqrt(_head_sum(yc * yc) * inv_n + GN_EPS)
    yn = yn * lng_ref[...] + lnb_ref[...]
    bonus = _head_sum(r * k2 * rk_ref[...]) * v
    y_ref[...] = (yn + bonus) * g
    sout_ref[...] = s_sc[...]


def _rwkv(p_r, prev0, s0_bd, lp, nsub, t_valid):
    b, t, _ = p_r.shape
    tc = nsub * CHUNK
    ng = RWKV_WIDTH // GROUP
    row = lambda a: a.reshape(1, -1)
    zeros = jnp.zeros((DECAY_RANK, RWKV_WIDTH), F32)
    wwa = jnp.concatenate([jnp.concatenate([lp["rwkv_w_up"], zeros], axis=1),
                           jnp.concatenate([zeros, lp["rwkv_a_up"]], axis=1)], axis=0)
    const = lambda shape: pl.BlockSpec(shape, lambda bi, ci: (0,) * len(shape))
    kern = functools.partial(_rwkv_kernel, nsub=nsub, t_valid=t_valid)
    return pl.pallas_call(
        kern,
        grid=(b, t // tc),
        in_specs=[pl.BlockSpec((None, tc, RWKV_PROJ), lambda bi, ci: (bi, ci, 0)),
                  pl.BlockSpec((None, 1, RWKV_PROJ), lambda bi, ci: (bi, 0, 0)),
                  pl.BlockSpec((None, ng, GROUP, GROUP), lambda bi, ci: (bi, 0, 0, 0)),
                  const((1, RWKV_PROJ)), const((1, RWKV_WIDTH)),
                  const((DECAY_RANK + ICLR_RANK, 2 * RWKV_WIDTH)), const((1, RWKV_WIDTH)),
                  const((GATE_RANK, RWKV_WIDTH)), const((1, RWKV_WIDTH)), const((1, RWKV_WIDTH)),
                  const((1, RWKV_WIDTH)), const((1, RWKV_WIDTH)), const((1, RWKV_WIDTH))],
        out_specs=[pl.BlockSpec((None, tc, RWKV_WIDTH), lambda bi, ci: (bi, ci, 0)),
                   pl.BlockSpec((None, ng, GROUP, GROUP), lambda bi, ci: (bi, 0, 0, 0))],
        out_shape=[jax.ShapeDtypeStruct((b, t, RWKV_WIDTH), F32),
                   jax.ShapeDtypeStruct((b, ng, GROUP, GROUP), F32)],
        scratch_shapes=[pltpu.VMEM((1, RWKV_PROJ), F32), pltpu.VMEM((ng, GROUP, GROUP), F32)],
        compiler_params=_cparams("arbitrary", "arbitrary"),
        name="rwkv7",
    )(p_r, prev0, s0_bd, row(lp["rwkv_mu"]), row(lp["rwkv_w0"]), wwa, row(lp["rwkv_a0"]),
      lp["rwkv_g_up"], row(lp["rwkv_k_k"]), row(lp["rwkv_k_a"]), row(lp["rwkv_r_k"]),
      row(lp["rwkv_ln_g"]), row(lp["rwkv_ln_b"]))


def _state_to_bd(s):
    b = s.shape[0]
    hg = GROUP // HEAD
    s5 = s.reshape(b, RWKV_HEADS // hg, hg, HEAD, HEAD)
    out = jnp.einsum("bghvk,hj->bghvjk", s5, jnp.eye(hg, dtype=s.dtype))
    return out.reshape(b, RWKV_HEADS // hg, GROUP, GROUP)


def _state_from_bd(s_bd):
    b = s_bd.shape[0]
    hg = GROUP // HEAD
    s6 = s_bd.reshape(b, RWKV_HEADS // hg, hg, HEAD, hg, HEAD)
    out = jnp.einsum("bghvjk,hj->bghvk", s6, jnp.eye(hg, dtype=s_bd.dtype))
    return out.reshape(b, RWKV_HEADS, HEAD, HEAD)


def _swa_kernel(tab_ref, sink_ref, pa_ref, ck_ref, cv_ref, bkt_ref, qg_ref, kg_ref,
                ya_ref, kn_ref, kall, vall, bias_sc, *, cq, nqc, has_cache):
    tq = cq * nqc
    band = WINDOW + cq
    bi = pl.program_id(0)
    i = pl.program_id(1)
    lane = _iota((1, LANES), 1)
    low = lane < HEAD

    @pl.when(jnp.logical_and(bi == 0, i == 0))
    def _():
        bkt = bkt_ref[...]
        for h in range(ATTN_HEADS):
            def body(j, acc, h=h):
                return acc + jnp.where(bkt == j, tab_ref[j, h], 0.0)
            bh = lax.fori_loop(0, REL_BUCKETS, body, jnp.zeros((cq, band), F32))
            bias_sc[h // 2, (h % 2) * cq:(h % 2 + 1) * cq, :] = bh

    @pl.when(i == 0)
    def _():
        kall[0:WINDOW, :] = ck_ref[...]
        vall[0:WINDOW, :] = cv_ref[...]

    pa = pa_ref[...]
    q = pa[:, :ATTN_WIDTH]
    k = pa[:, ATTN_WIDTH:ATTN_WIDTH + KV_WIDTH]
    v = pa[:, ATTN_WIDTH + KV_WIDTH:]
    inv_n = 1.0 / HEAD
    qn = q * lax.rsqrt(_head_sum(q * q) * inv_n + RMS_EPS) * qg_ref[...]
    kn = k * lax.rsqrt(_head_sum(k * k) * inv_n + RMS_EPS) * kg_ref[...]
    kn_ref[...] = kn
    kall[WINDOW:WINDOW + tq, :] = kn
    vall[WINDOW:WINDOW + tq, :] = v

    row2 = _iota((2 * cq, 1), 0)
    col = _iota((1, band), 1)
    thr = jnp.where(i == 0, WINDOW, 0)
    k_all = kall[...]
    v_all = vall[...]
    k_rot = pltpu.roll(k_all, HEAD, axis=1)
    v_rot = pltpu.roll(v_all, HEAD, axis=1)
    kd = [jnp.where(low, k_all, k_rot).astype(BF16), jnp.where(low, k_rot, k_all).astype(BF16)]
    vd = [jnp.where(low, v_all, v_rot).astype(BF16), jnp.where(low, v_rot, v_all).astype(BF16)]

    work = [(qc, pi) for qc in range(nqc) for pi in range(ATTN_HEADS // 2)]
    kv_of = lambda pi: (2 * pi) // (ATTN_HEADS // KV_HEADS)
    scores = {}
    for (qc, pi) in work:
        qp = qn[qc * cq:(qc + 1) * cq, pi * LANES:(pi + 1) * LANES]
        qs = jnp.concatenate([jnp.where(low, qp, 0.0), jnp.where(low, 0.0, qp)], axis=0)
        s = _dot(qs.astype(BF16), kd[kv_of(pi)][qc * cq:qc * cq + band], nt=True) * ATTN_SCALE + bias_sc[pi]
        if (not has_cache) and qc * cq < WINDOW:
            s = jnp.where((col + qc * cq) < thr, NEG_INF, s)
        scores[qc, pi] = s
    probs = {}
    for (qc, pi) in work:
        s = scores[qc, pi]
        sink = jnp.where(row2 < cq, sink_ref[0, 2 * pi], sink_ref[0, 2 * pi + 1])
        m = jnp.maximum(jnp.max(s, axis=-1, keepdims=True), sink)
        e = jnp.exp(s - m)
        den = jnp.sum(e, axis=-1, keepdims=True) + jnp.exp(sink - m)
        probs[qc, pi] = (e / den).astype(BF16)
    for (qc, pi) in work:
        o = _dot(probs[qc, pi], vd[kv_of(pi)][qc * cq:qc * cq + band])
        ya_ref[qc * cq:(qc + 1) * cq, pi * LANES:(pi + 1) * LANES] = jnp.where(low, o[:cq], o[cq:])

    if tq >= WINDOW:
        knext = kall[tq:tq + WINDOW, :]
        vnext = vall[tq:tq + WINDOW, :]
        kall[0:WINDOW, :] = knext
        vall[0:WINDOW, :] = vnext


def _t5_bucket(rel):
    half = REL_BUCKETS // 2
    max_exact = half // 2
    n = jnp.abs(rel)
    log_ratio = jnp.log(jnp.maximum(n, 1).astype(F32) / max_exact) / math.log(REL_MAX_DIST / max_exact)
    large = jnp.minimum(max_exact + (log_ratio * (half - max_exact)).astype(I32), half - 1)
    return jnp.where(rel > 0, half, 0) + jnp.where(n < max_exact, n, large)


def _swa(p_a, cache_k, cache_v, rel_table, sinks, q_norm, k_norm, cq, nqc, has_cache):
    b, t, _ = p_a.shape
    tq = cq * nqc
    band = WINDOW + cq
    rel = (jnp.arange(band) - WINDOW)[None, :] - jnp.arange(cq)[:, None]
    bkt = _t5_bucket(rel).astype(I32)
    qg = jnp.tile(q_norm, ATTN_HEADS).reshape(1, ATTN_WIDTH)
    kg = jnp.tile(k_norm, KV_HEADS).reshape(1, KV_WIDTH)
    kern = functools.partial(_swa_kernel, cq=cq, nqc=nqc, has_cache=has_cache)
    smem = pl.BlockSpec(memory_space=pltpu.SMEM)
    return pl.pallas_call(
        kern,
        grid=(b, t // tq),
        in_specs=[smem, smem,
                  pl.BlockSpec((None, tq, ATTN_PROJ), lambda bi, i: (bi, i, 0)),
                  pl.BlockSpec((None, WINDOW, KV_WIDTH), lambda bi, i: (bi, 0, 0)),
                  pl.BlockSpec((None, WINDOW, KV_WIDTH), lambda bi, i: (bi, 0, 0)),
                  pl.BlockSpec((cq, band), lambda bi, i: (0, 0)),
                  pl.BlockSpec((1, ATTN_WIDTH), lambda bi, i: (0, 0)),
                  pl.BlockSpec((1, KV_WIDTH), lambda bi, i: (0, 0))],
        out_specs=[pl.BlockSpec((None, tq, ATTN_WIDTH), lambda bi, i: (bi, i, 0)),
                   pl.BlockSpec((None, tq, KV_WIDTH), lambda bi, i: (bi, i, 0))],
        out_shape=[jax.ShapeDtypeStruct((b, t, ATTN_WIDTH), F32),
                   jax.ShapeDtypeStruct((b, t, KV_WIDTH), F32)],
        scratch_shapes=[pltpu.VMEM((WINDOW + tq, KV_WIDTH), F32),
                        pltpu.VMEM((WINDOW + tq, KV_WIDTH), F32),
                        pltpu.VMEM((ATTN_HEADS // 2, 2 * cq, band), F32)],
        compiler_params=_cparams("arbitrary", "arbitrary"),
        name="swa",
    )(rel_table, sinks.reshape(1, ATTN_HEADS), p_a, cache_k, cache_v, bkt, qg, kg)


def _mid_kernel(yr_ref, ya_ref, x_ref, gt1_ref, sh2_ref, sc2_ref, g2_ref, wor_ref, woa_ref, wrt_ref,
                rb_ref, base0_ref, x1_ref, h2_ref, te_ref, gate_ref, pos_ref, cnt_ref, base_sc, *, tm):
    @pl.when(jnp.logical_and(pl.program_id(0) == 0, pl.program_id(1) == 0))
    def _():
        base_sc[...] = base0_ref[...]

    mix = _dot(yr_ref[...].astype(BF16), wor_ref[...]) + _dot(ya_ref[...].astype(BF16), woa_ref[...])
    x1 = x_ref[...] + gt1_ref[...] * mix
    x1_ref[...] = x1
    h2 = _rms(x1, RMS_EPS) * g2_ref[...] * (1.0 + sc2_ref[...]) + sh2_ref[...]
    h2_ref[...] = h2

    logits = _mm(wrt_ref[...], h2, passes=3, nt=True) + rb_ref[...]
    eidx = _iota((N_EXPERTS, tm), 0)
    work = logits
    tops, hots = [], []
    for j in range(TOP_K):
        m = jnp.max(work, axis=0, keepdims=True)
        idx = jnp.min(jnp.where(work == m, eidx, N_EXPERTS), axis=0, keepdims=True)
        hot = eidx == idx
        work = jnp.where(hot, -jnp.inf, work)
        tops.append(m)
        hots.append(hot)
        te_ref[j:j + 1, :] = idx
    exps = [jnp.exp(tj - tops[0]) for tj in tops]
    den = exps[0] + exps[1] + exps[2] + exps[3]
    for j in range(TOP_K):
        gate_ref[j:j + 1, :] = exps[j] / den

    member = (hots[0] | hots[1] | hots[2] | hots[3]).astype(BF16)
    upper = (_iota((tm, tm), 0) < _iota((tm, tm), 1)).astype(BF16)
    before = _dot(member, upper) + base_sc[...]
    for j in range(TOP_K):
        pos_ref[j:j + 1, :] = jnp.sum(jnp.where(hots[j], before, 0.0), axis=0, keepdims=True).astype(I32)
    base_sc[...] = base_sc[...] + jnp.sum(member.astype(F32), axis=1, keepdims=True)
    cnt_ref[...] = base_sc[...]


def _mid(y_r, y_a, x3, gt1, sh2, sc2, gain2, wo_r16, wo_a16, wr_t, rb, base0, tm):
    g, r, d = x3.shape
    n = g * r
    per = r // tm
    mr = gt1.shape[1]
    mrow = (lambda b, i: (b, 0, 0)) if mr == 1 else (lambda b, i: (b, i, 0))
    mblk = 1 if mr == 1 else tm
    tok = lambda w: pl.BlockSpec((None, tm, w), lambda b, i: (b, i, 0))
    mod = pl.BlockSpec((None, mblk, d), mrow)
    const = lambda shape: pl.BlockSpec(shape, lambda b, i: (0,) * len(shape))
    lane_out = pl.BlockSpec((TOP_K, tm), lambda b, i: (0, b * per + i))
    kern = functools.partial(_mid_kernel, tm=tm)
    return pl.pallas_call(
        kern,
        grid=(g, per),
        in_specs=[tok(RWKV_WIDTH), tok(ATTN_WIDTH), tok(d), mod, mod, mod, const((1, d)),
                  const((RWKV_WIDTH, d)), const((ATTN_WIDTH, d)), const((N_EXPERTS, d)),
                  const((N_EXPERTS, 1)), const((N_EXPERTS, 1))],
        out_specs=[tok(d), tok(d), lane_out, lane_out, lane_out, const((N_EXPERTS, 1))],
        out_shape=[jax.ShapeDtypeStruct((g, r, d), F32), jax.ShapeDtypeStruct((g, r, d), F32),
                   jax.ShapeDtypeStruct((TOP_K, n), I32), jax.ShapeDtypeStruct((TOP_K, n), F32),
                   jax.ShapeDtypeStruct((TOP_K, n), I32), jax.ShapeDtypeStruct((N_EXPERTS, 1), F32)],
        scratch_shapes=[pltpu.VMEM((N_EXPERTS, 1), F32)],
        compiler_params=_cparams("arbitrary", "arbitrary"),
        name="mid",
    )(y_r, y_a, x3, gt1, sh2, sc2, gain2.reshape(1, d), wo_r16, wo_a16, wr_t, rb, base0)


def _slot_kernel(pstart_ref, te_ref, pos_ref, o_ref):
    te = te_ref[...]
    acc = pos_ref[...]
    for e in range(N_EXPERTS):
        acc = acc + jnp.where(te == e, pstart_ref[e], 0)
    o_ref[...] = acc


def _slot_rows(te, pos, pstart):
    k, n = te.shape
    return pl.pallas_call(
        _slot_kernel,
        grid=(1,),
        in_specs=[pl.BlockSpec(memory_space=pltpu.SMEM),
                  pl.BlockSpec((k, n), lambda i: (0, 0)), pl.BlockSpec((k, n), lambda i: (0, 0))],
        out_specs=pl.BlockSpec((k, n), lambda i: (0, 0)),
        out_shape=jax.ShapeDtypeStruct((k, n), I32),
        compiler_params=_cparams("arbitrary"),
        name="slot_rows",
    )(pstart, te, pos)


def _dispatch_kernel(dst_ref, pad_ref, h2a_ref, h2b_ref, xs_ref, ztile, sem, zsem, *, tk, steps_a):
    i = pl.program_id(0)

    @pl.when(i == 0)
    def _():
        ztile[...] = jnp.zeros_like(ztile)
        zrow = ztile.at[pl.ds(0, 1)]
        tm = ztile.shape[0]

        def per_expert(e, total):
            first = pad_ref[0, e]
            count = pad_ref[1, e]

            def per_row(r, c):
                pltpu.make_async_copy(zrow, xs_ref.at[pl.ds(first + r, 1)], zsem).start()
                return c
            lax.fori_loop(0, count, per_row, 0)
            return total + count
        n_pad = lax.fori_loop(0, N_EXPERTS, per_expert, 0)

        def drain(r, c):
            pltpu.make_async_copy(zrow, xs_ref.at[pl.ds(0, 1)], zsem).wait()
            return c
        lax.fori_loop(0, n_pad, drain, 0)

        def spare(t, c):
            row0 = pl.multiple_of(t * tm, tm)
            pltpu.make_async_copy(ztile, xs_ref.at[pl.ds(row0, tm)], zsem).start()
            return c
        lax.fori_loop(pad_ref[2, 0], pad_ref[2, 1], spare, 0)

        def spare_wait(t, c):
            pltpu.make_async_copy(ztile, xs_ref.at[pl.ds(0, tm)], zsem).wait()
            return c
        lax.fori_loop(pad_ref[2, 0], pad_ref[2, 1], spare_wait, 0)

    def scatter(h2_ref):
        for t in range(h2_ref.shape[0] * ROW_UNROLL):
            for j in range(TOP_K):
                dst = dst_ref[t * TOP_K + j]
                pltpu.make_async_copy(h2_ref.at[t // ROW_UNROLL, pl.ds(t % ROW_UNROLL, 1)],
                                      xs_ref.at[pl.ds(dst, 1)], sem).start()
        for _ in range(TOP_K):
            pltpu.make_async_copy(h2_ref, h2_ref, sem).wait()

    pl.when(i < steps_a)(functools.partial(scatter, h2a_ref))
    pl.when(i == steps_a)(functools.partial(scatter, h2b_ref))


def _dispatch(dest, pads, h2a, h2b, n_rows, tk):
    na, d = h2a.shape
    nb = h2b.shape[0]
    steps_a = na // tk
    dest = jnp.pad(dest.T.reshape(-1), (0, ((steps_a + 1) * tk - na - nb) * TOP_K))
    h2a = h2a.reshape(na // ROW_UNROLL, ROW_UNROLL, d)
    h2b = h2b.reshape(nb // ROW_UNROLL, ROW_UNROLL, d)
    kern = functools.partial(_dispatch_kernel, tk=tk, steps_a=steps_a)
    return pl.pallas_call(
        kern,
        grid=(steps_a + 1,),
        in_specs=[pl.BlockSpec((TOP_K * tk,), lambda i: (i,), memory_space=pltpu.SMEM),
                  pl.BlockSpec(memory_space=pltpu.SMEM),
                  pl.BlockSpec((tk // ROW_UNROLL, ROW_UNROLL, d), lambda i: (jnp.minimum(i, steps_a - 1), 0, 0)),
                  pl.BlockSpec((nb // ROW_UNROLL, ROW_UNROLL, d), lambda i: (0, 0, 0))],
        out_specs=pl.BlockSpec(memory_space=pl.ANY),
        out_shape=jax.ShapeDtypeStruct((n_rows, d), F32),
        scratch_shapes=[pltpu.VMEM((TM_FFN, d), F32), pltpu.SemaphoreType.DMA(()), pltpu.SemaphoreType.DMA(())],
        compiler_params=_cparams("arbitrary"),
        name="dispatch",
    )(dest, pads, h2a, h2b)


def _ffn_kernel(te_ref, nu_ref, nx_ref, xs_ref, wu_hbm, bu_ref, wd_hbm, bd_ref, ys_ref,
                wu32, wd32, wu16, wd16, sem):
    i = pl.program_id(0)
    n_up = wu32.shape[1]
    active = i < nu_ref[0]
    fresh = jnp.logical_or(i == 0, te_ref[i] != te_ref[jnp.maximum(i - 1, 0)])

    def weight_copies(e):
        return (pltpu.make_async_copy(wu_hbm.at[e], wu32, sem.at[0]),
                pltpu.make_async_copy(wd_hbm.at[e], wd32, sem.at[1]))

    @pl.when(i == 0)
    def _():
        for cp in weight_copies(te_ref[0]):
            cp.start()

    @pl.when(jnp.logical_and(active, fresh))
    def _():
        for cp in weight_copies(te_ref[i]):
            cp.wait()
        src = _iota((GROUP, GROUP), 0)
        dst = _iota((GROUP, GROUP), 1)
        want = jnp.where(dst < LANES, 2 * dst, 2 * (dst - LANES) + 1)
        perm = (src == want).astype(BF16)
        for b in range(n_up // GROUP):
            cs = slice(b * GROUP, (b + 1) * GROUP)
            wu16[:, cs] = _dot(wu32[:, cs].astype(BF16), perm).astype(BF16)
        wd16[...] = wd32[...].astype(BF16)

        @pl.when(nx_ref[i] >= 0)
        def _():
            for cp in weight_copies(nx_ref[i]):
                cp.start()

    @pl.when(active)
    def _():
        u = _dot(xs_ref[...].astype(BF16), wu16[...]) + bu_ref[...]
        acts = []
        for b in range(n_up // GROUP):
            glu = jnp.minimum(u[:, b * GROUP:b * GROUP + LANES], SWIGLU_LIMIT)
            lin = jnp.clip(u[:, b * GROUP + LANES:(b + 1) * GROUP], -SWIGLU_LIMIT, SWIGLU_LIMIT)
            acts.append((glu * jax.nn.sigmoid(SWIGLU_ALPHA * glu) * (lin + 1.0)).astype(BF16))
        act = jnp.concatenate(acts, axis=1)
        ys_ref[...] = _dot(act, wd16[...]) + bd_ref[...]

    @pl.when(jnp.logical_not(active))
    def _():
        ys_ref[...] = jnp.zeros_like(ys_ref)


def _ffn(tile_e, n_used, next_e, xs, wu, bu, wd, bd_, tm):
    rows, d = xs.shape
    n_exp, _, n_up = wu.shape
    dff = wd.shape[1]
    row_blk = lambda i, te, nu, nx: (jnp.minimum(i, nu[0] - 1), 0)
    exp_blk = lambda i, te, nu, nx: (te[i], 0, 0)
    grid_spec = pltpu.PrefetchScalarGridSpec(
        num_scalar_prefetch=3,
        grid=(rows // tm,),
        in_specs=[pl.BlockSpec((tm, d), row_blk),
                  pl.BlockSpec(memory_space=pl.ANY),
                  pl.BlockSpec((None, 1, n_up), exp_blk),
                  pl.BlockSpec(memory_space=pl.ANY),
                  pl.BlockSpec((None, 1, d), exp_blk)],
        out_specs=pl.BlockSpec((tm, d), lambda i, te, nu, nx: (i, 0)),
        scratch_shapes=[pltpu.VMEM((d, n_up), F32), pltpu.VMEM((dff, d), F32),
                        pltpu.VMEM((d, n_up), BF16), pltpu.VMEM((dff, d), BF16),
                        pltpu.SemaphoreType.DMA((2,))],
    )
    return pl.pallas_call(
        _ffn_kernel,
        grid_spec=grid_spec,
        out_shape=jax.ShapeDtypeStruct((rows, d), F32),
        compiler_params=_cparams("arbitrary"),
        name="expert_ffn",
    )(tile_e, n_used, next_e, xs, wu, bu.reshape(n_exp, 1, n_up), wd, bd_.reshape(n_exp, 1, d))


def _combine_kernel(src_ref, srcn_ref, gate_ref, x1_ref, gt2_ref, ys_ref, o_ref, buf, sem, *, tk):
    i = pl.program_id(1)
    n = pl.num_programs(1)
    flat = pl.program_id(0) * n + i
    total = pl.num_programs(0) * n
    slot = lax.rem(flat, 2)

    def issue(rows_ref, s):
        for t in range(tk):
            for j in range(TOP_K):
                src = rows_ref[t * TOP_K + j]
                pltpu.make_async_copy(ys_ref.at[pl.ds(src, 1)],
                                      buf.at[s, j, t // ROW_UNROLL, pl.ds(t % ROW_UNROLL, 1)],
                                      sem.at[s]).start()

    @pl.when(flat == 0)
    def _():
        issue(src_ref, 0)

    def step(cur):
        @pl.when(flat + 1 < total)
        def _():
            issue(srcn_ref, 1 - cur)

        for j in range(TOP_K):
            pltpu.make_async_copy(buf.at[cur, j], buf.at[cur, j], sem.at[cur]).wait()
        gate = gate_ref[...]
        rows = lambda j: buf[cur, j].reshape(tk, buf.shape[-1])
        acc = gate[:, 0:1] * rows(0)
        for j in range(1, TOP_K):
            acc = acc + gate[:, j:j + 1] * rows(j)
        o_ref[...] = x1_ref[...] + gt2_ref[...] * acc

    for cur in range(2):
        pl.when(slot == cur)(functools.partial(step, cur))


def _combine(dest, gates_t, x1, gt2, ys, tk):
    g, r, d = x1.shape
    per = r // tk
    last = g * per - 1
    mr = gt2.shape[1]
    mrow = (lambda b, i: (b, 0, 0)) if mr == 1 else (lambda b, i: (b, i, 0))
    mblk = 1 if mr == 1 else tk
    dest = dest.T.reshape(-1)
    cur = pl.BlockSpec((TOP_K * tk,), lambda b, i: (b * per + i,), memory_space=pltpu.SMEM)
    nxt = pl.BlockSpec((TOP_K * tk,), lambda b, i: (jnp.minimum(b * per + i + 1, last),),
                       memory_space=pltpu.SMEM)
    kern = functools.partial(_combine_kernel, tk=tk)
    return pl.pallas_call(
        kern,
        grid=(g, per),
        in_specs=[cur, nxt,
                  pl.BlockSpec((tk, TOP_K), lambda b, i: (b * per + i, 0)),
                  pl.BlockSpec((None, tk, d), lambda b, i: (b, i, 0)),
                  pl.BlockSpec((None, mblk, d), mrow),
                  pl.BlockSpec(memory_space=pl.ANY)],
        out_specs=pl.BlockSpec((None, tk, d), lambda b, i: (b, i, 0)),
        out_shape=jax.ShapeDtypeStruct((g, r, d), F32),
        scratch_shapes=[pltpu.VMEM((2, TOP_K, tk // ROW_UNROLL, ROW_UNROLL, d), F32),
                        pltpu.SemaphoreType.DMA((2,))],
        compiler_params=_cparams("arbitrary", "arbitrary"),
        name="combine",
    )(dest, dest, gates_t, x1, gt2, ys)


def _tile_plan(counts, tm, n_tiles):
    cnt = counts.reshape(-1).astype(I32)
    tiles = (cnt + tm - 1) // tm
    tile_end = jnp.cumsum(tiles)
    pstart = ((tile_end - tiles) * tm).astype(I32)
    n_used = tile_end[-1:].astype(I32)
    idx = jnp.minimum(jnp.arange(n_tiles, dtype=I32), n_used[0] - 1)
    tile_e = jnp.sum((tile_end[None, :] <= idx[:, None]).astype(I32), axis=1)
    spare = jnp.zeros((N_EXPERTS,), I32).at[0].set(n_used[0]).at[1].set(n_tiles)
    pads = jnp.stack([pstart + cnt, tiles * tm - cnt, spare]).astype(I32)
    tile_e = jnp.minimum(tile_e, N_EXPERTS - 1)
    end_of = tile_end[tile_e]
    next_e = jnp.where(end_of < n_used[0], tile_e[jnp.minimum(end_of, n_tiles - 1)], -1).astype(I32)
    return pstart, tile_e, n_used, next_e, pads


def kernel(x_prompt, x_sample, state_shift, state_wkv, cache_win_k, cache_win_v, c_prompt, c_sample, rel_bias, norm1_g, norm2_g, w_ada, b_ada, w_in, w_out, rwkv_mu, rwkv_w0, rwkv_w_up, rwkv_a0, rwkv_a_up, rwkv_g_up, rwkv_k_k, rwkv_k_a, rwkv_r_k, rwkv_ln_g, rwkv_ln_b, q_norm, k_norm, sinks, router_w, router_b, exp_w_up, exp_b_up, exp_w_down, exp_b_down):
    bp, tp, d = x_prompt.shape
    bs, ts, _ = x_sample.shape
    depth = w_in.shape[0]
    assert depth == 1, "single-layer trunk"
    layer = 0
    lp = dict(rwkv_mu=rwkv_mu[layer], rwkv_w0=rwkv_w0[layer], rwkv_w_up=rwkv_w_up[layer],
              rwkv_a0=rwkv_a0[layer], rwkv_a_up=rwkv_a_up[layer], rwkv_g_up=rwkv_g_up[layer],
              rwkv_k_k=rwkv_k_k[layer], rwkv_k_a=rwkv_k_a[layer], rwkv_r_k=rwkv_r_k[layer].reshape(-1),
              rwkv_ln_g=rwkv_ln_g[layer], rwkv_ln_b=rwkv_ln_b[layer])

    rows = bp + bs
    pad = (-rows) % 8
    c_all = jnp.concatenate([c_prompt, c_sample, jnp.zeros((pad, d), F32)], axis=0)
    mod = _modulation(c_all, w_ada[layer], b_ada[layer])
    sh1, sc1, gt1, sh2, sc2, gt2 = [mod[:, i * d:(i + 1) * d] for i in range(6)]
    pm = lambda m: m[:bp].reshape(bp, 1, d)
    sm = lambda m: jnp.repeat(m[bp:bp + bs], ts, axis=0).reshape(1, bs * ts, d)

    w_in16 = w_in[layer].astype(BF16)
    wo16 = w_out[layer].astype(BF16)
    wo_r16, wo_a16 = wo16[:RWKV_WIDTH], wo16[RWKV_WIDTH:]
    wr_t = router_w[layer].T
    rb = router_b[layer].reshape(N_EXPERTS, 1)
    wu = exp_w_up.reshape(exp_w_up.shape[1:])
    wd = exp_w_down.reshape(exp_w_down.shape[1:])
    n_up = wu.shape[-1]
    bu = exp_b_up[layer].reshape(N_EXPERTS, n_up // GROUP, LANES, 2).transpose(0, 1, 3, 2).reshape(N_EXPERTS, n_up)

    pr_p, pa_p = _inproj(x_prompt, norm1_g[layer], pm(sh1), pm(sc1), w_in16, TM_PROJ)
    zero_prev = jnp.zeros((bp, 1, RWKV_PROJ), F32)
    zero_state = jnp.zeros((bp, RWKV_WIDTH // GROUP, GROUP, GROUP), F32)
    yr_p, s_p = _rwkv(pr_p, zero_prev, zero_state, lp, RWKV_SUB, None)
    zero_kv = jnp.zeros((bp, WINDOW, KV_WIDTH), F32)
    ya_p, kn_p = _swa(pa_p, zero_kv, zero_kv, rel_bias, sinks[layer], q_norm[layer], k_norm[layer],
                      CHUNK, SWA_QC, False)

    ns = bs * ts
    xs_flat = x_sample.reshape(1, ns, d)
    pr_s, pa_s = _inproj(xs_flat, norm1_g[layer], sm(sh1), sm(sc1), w_in16, ns)
    pr_s = pr_s.reshape(bs, ts, RWKV_PROJ)
    pa_s = pa_s.reshape(bs, ts, ATTN_PROJ)
    t_pad = -(-ts // CHUNK) * CHUNK
    pr_s_pad = jnp.pad(pr_s, ((0, 0), (0, t_pad - ts), (0, 0)))
    yr_s, s_s = _rwkv(pr_s_pad, state_shift[layer].reshape(bs, 1, RWKV_PROJ), _state_to_bd(state_wkv[layer]),
                      lp, t_pad // CHUNK, ts)
    yr_s = yr_s[:, :ts]
    ya_s, kn_s = _swa(pa_s, cache_win_k[layer].reshape(bs, WINDOW, KV_WIDTH),
                      cache_win_v[layer].reshape(bs, WINDOW, KV_WIDTH), rel_bias, sinks[layer],
                      q_norm[layer], k_norm[layer], ts, 1, True)

    base0 = jnp.zeros((N_EXPERTS, 1), F32)
    x1_p, h2_p, te_p, gate_p, pos_p, cnt_p = _mid(yr_p, ya_p, x_prompt, pm(gt1), pm(sh2), pm(sc2), norm2_g[layer],
                                                  wo_r16, wo_a16, wr_t, rb, base0, TM_PROJ)
    x1_s, h2_s, te_s, gate_s, pos_s, cnt = _mid(yr_s.reshape(1, ns, RWKV_WIDTH), ya_s.reshape(1, ns, ATTN_WIDTH),
                                                xs_flat, sm(gt1), sm(sh2), sm(sc2), norm2_g[layer],
                                                wo_r16, wo_a16, wr_t, rb, cnt_p, ns)

    n_p = bp * tp
    n_rows = (n_p + ns) * TOP_K + N_EXPERTS * (TM_FFN - 1)
    n_tiles = -(-n_rows // TM_FFN)
    pstart, tile_e, n_used, next_e, pads = _tile_plan(cnt, TM_FFN, n_tiles)
    dest = _slot_rows(jnp.concatenate([te_p, te_s], axis=1), jnp.concatenate([pos_p, pos_s], axis=1), pstart)
    dest_p, dest_s = dest[:, :n_p], dest[:, n_p:]
    xs_buf = _dispatch(dest, pads, h2_p.reshape(n_p, d), h2_s.reshape(ns, d), n_tiles * TM_FFN, TK_DISPATCH)
    ys_buf = _ffn(tile_e, n_used, next_e, xs_buf, wu, bu, wd, exp_b_down[layer], TM_FFN)
    y_p = _combine(dest_p, gate_p.T, x1_p, pm(gt2), ys_buf, TK_COMBINE)
    y_s = _combine(dest_s, gate_s.T, x1_s, sm(gt2), ys_buf, TK_COMBINE)

    kv4 = lambda z, rows_: z.reshape(z.shape[0], rows_, KV_HEADS, HEAD)[None]
    v_p = pa_p[:, tp - WINDOW:, ATTN_WIDTH + KV_WIDTH:]
    v_s = pa_s[:, :, ATTN_WIDTH + KV_WIDTH:]
    return (y_p, y_s.reshape(bs, ts, d),
            pr_p[:, tp - 1][None], _state_from_bd(s_p)[None], kv4(kn_p[:, tp - WINDOW:], WINDOW), kv4(v_p, WINDOW),
            pr_s[:, ts - 1][None], _state_from_bd(s_s)[None], kv4(kn_s, ts), kv4(v_s, ts))
```

```python
import functools
import math

import jax
import jax.numpy as jnp
from jax import lax
from jax.experimental import pallas as pl
from jax.experimental.pallas import tpu as pltpu

F32 = jnp.float32
BF16 = jnp.bfloat16
I32 = jnp.int32

HEAD = 64
RWKV_HEADS = 8
RWKV_WIDTH = RWKV_HEADS * HEAD
DECAY_RANK = 64
ICLR_RANK = 64
GATE_RANK = 128
RWKV_PROJ = 3 * RWKV_WIDTH + DECAY_RANK + ICLR_RANK + GATE_RANK
ATTN_HEADS = 8
KV_HEADS = 2
ATTN_WIDTH = ATTN_HEADS * HEAD
KV_WIDTH = KV_HEADS * HEAD
ATTN_PROJ = ATTN_WIDTH + 2 * KV_WIDTH
WINDOW = 128
CHUNK = 64
REL_BUCKETS = 32
REL_MAX_DIST = 128
N_EXPERTS = 32
TOP_K = 4
GN_EPS = 64e-5
RMS_EPS = 1e-6
NEG_INF = -1e30
ATTN_SCALE = HEAD ** -0.5
SWIGLU_ALPHA = 1.702
SWIGLU_LIMIT = 7.0

LANES = 128
GROUP = 256
VMEM_LIMIT = 56 * 1024 * 1024

TM_PROJ = 512
RWKV_SUB = 4
SWA_QC = 4
TM_FFN = 512
TK_DISPATCH = 512
TK_COMBINE = 256
ROW_UNROLL = 8


def _cparams(*sem):
    return pltpu.CompilerParams(dimension_semantics=sem, vmem_limit_bytes=VMEM_LIMIT)


def _dot(a, b, nt=False):
    dims = (((1,), (1,)), ((), ())) if nt else (((1,), (0,)), ((), ()))
    return lax.dot_general(a, b, dims, preferred_element_type=F32)


def _split(x):
    hi = x.astype(BF16)
    lo = (x - hi.astype(F32)).astype(BF16)
    return hi, lo


def _mm(a, b, passes=1, nt=False):
    if passes == 1:
        return _dot(a.astype(BF16), b.astype(BF16), nt)
    ah, al = _split(a)
    bh, bl = _split(b)
    return _dot(ah, bh, nt) + _dot(al, bh, nt) + _dot(ah, bl, nt)


def _iota(shape, dim):
    return lax.broadcasted_iota(I32, shape, dim)


def _head_ones(n):
    r = lax.shift_right_logical(_iota((n, n), 0), 6)
    c = lax.shift_right_logical(_iota((n, n), 1), 6)
    return (r == c).astype(BF16)


def _head_sum(x):
    width = x.shape[1]
    x16 = x.astype(BF16)
    if width <= GROUP:
        return _dot(x16, _head_ones(width))
    ones = _head_ones(GROUP)
    parts = [_dot(x16[:, g:g + GROUP], ones) for g in range(0, width, GROUP)]
    return jnp.concatenate(parts, axis=1)


def _rms(x, eps):
    return x * lax.rsqrt(jnp.mean(x * x, axis=-1, keepdims=True) + eps)


def _mod_kernel(c_ref, w_ref, b_ref, o_ref):
    c = c_ref[...]
    s = c * jax.nn.sigmoid(c)
    o_ref[...] = _mm(s, w_ref[...], passes=3) + b_ref[...]


def _modulation(c_all, w_ada, b_ada):
    rows, d = c_all.shape
    n = w_ada.shape[1]
    tn = 1536
    return pl.pallas_call(
        _mod_kernel,
        grid=(n // tn,),
        in_specs=[pl.BlockSpec((rows, d), lambda j: (0, 0)),
                  pl.BlockSpec((d, tn), lambda j: (0, j)),
                  pl.BlockSpec((1, tn), lambda j: (0, j))],
        out_specs=pl.BlockSpec((rows, tn), lambda j: (0, j)),
        out_shape=jax.ShapeDtypeStruct((rows, n), F32),
        compiler_params=_cparams("arbitrary"),
        name="modulation",
    )(c_all, w_ada, b_ada.reshape(1, n))


def _inproj_kernel(x_ref, g_ref, sh_ref, sc_ref, w_ref, pr_ref, pa_ref):
    x = x_ref[...]
    h = _rms(x, RMS_EPS) * g_ref[...] * (1.0 + sc_ref[...]) + sh_ref[...]
    p = _dot(h.astype(BF16), w_ref[...])
    pr_ref[...] = p[:, :RWKV_PROJ]
    pa_ref[...] = p[:, RWKV_PROJ:]


def _inproj(x3, gain, sh3, sc3, w_in16, tm):
    g, r, d = x3.shape
    mr = sh3.shape[1]
    mrow = (lambda b, i: (b, 0, 0)) if mr == 1 else (lambda b, i: (b, i, 0))
    mblk = 1 if mr == 1 else tm
    return pl.pallas_call(
        _inproj_kernel,
        grid=(g, r // tm),
        in_specs=[pl.BlockSpec((None, tm, d), lambda b, i: (b, i, 0)),
                  pl.BlockSpec((1, d), lambda b, i: (0, 0)),
                  pl.BlockSpec((None, mblk, d), mrow),
                  pl.BlockSpec((None, mblk, d), mrow),
                  pl.BlockSpec(w_in16.shape, lambda b, i: (0, 0))],
        out_specs=[pl.BlockSpec((None, tm, RWKV_PROJ), lambda b, i: (b, i, 0)),
                   pl.BlockSpec((None, tm, ATTN_PROJ), lambda b, i: (b, i, 0))],
        out_shape=[jax.ShapeDtypeStruct((g, r, RWKV_PROJ), F32),
                   jax.ShapeDtypeStruct((g, r, ATTN_PROJ), F32)],
        compiler_params=_cparams("arbitrary", "arbitrary"),
        name="inproj",
    )(x3, gain.reshape(1, d), sh3, sc3, w_in16)


def _softplus(x):
    return jnp.maximum(x, 0.0) + jnp.log1p(jnp.exp(-jnp.abs(x)))


def _rwkv_kernel(p_ref, prev0_ref, s0_ref, mu_ref, w0_ref, wwa_ref, a0_ref, gup_ref, kk_ref, ka_ref,
                 rk_ref, lng_ref, lnb_ref, y_ref, sout_ref, prev_sc, s_sc, *, nsub, t_valid):
    tc = nsub * CHUNK
    c = pl.program_id(1)

    @pl.when(c == 0)
    def _():
        prev_sc[...] = prev0_ref[...]
        s_sc[...] = s0_ref[...]

    p = p_ref[...]
    row = _iota((tc, 1), 0)
    p_prev = jnp.where(row == 0, prev_sc[...], pltpu.roll(p, 1, axis=0))
    prev_sc[...] = p_ref[tc - 1:tc, :]
    xp = p + (p_prev - p) * mu_ref[...]

    w3 = 3 * RWKV_WIDTH
    r = xp[:, 0:RWKV_WIDTH]
    k = xp[:, RWKV_WIDTH:2 * RWKV_WIDTH]
    v = xp[:, 2 * RWKV_WIDTH:w3]
    z = xp[:, w3:w3 + DECAY_RANK + ICLR_RANK]
    gd = xp[:, w3 + DECAY_RANK + ICLR_RANK:]
    zl = _iota((1, DECAY_RANK + ICLR_RANK), 1)
    zz = jnp.where(zl < DECAY_RANK, jnp.tanh(z), z)
    wa = _mm(zz, wwa_ref[...], passes=3)
    w_log = -_softplus(-(w0_ref[...] + wa[:, :RWKV_WIDTH])) - 0.5
    lw = -jnp.exp(w_log)
    a = jax.nn.sigmoid(a0_ref[...] + wa[:, RWKV_WIDTH:])
    g = _mm(jax.nn.sigmoid(gd), gup_ref[...])
    kk = k * kk_ref[...]
    kk = kk * lax.rsqrt(jnp.maximum(_head_sum(kk * kk), 1e-24))
    k2 = k * (1.0 + (a - 1.0) * ka_ref[...])
    bvec = kk * a
    if t_valid is not None:
        ok = (c * tc + row) < t_valid
        lw = jnp.where(ok, lw, 0.0)
        kk = jnp.where(ok, kk, 0.0)
        bvec = jnp.where(ok, bvec, 0.0)
        k2u = jnp.where(ok, k2, 0.0)
        vu = jnp.where(ok, v, 0.0)
    else:
        k2u, vu = k2, v

    tril16 = (_iota((CHUNK, CHUNK), 0) >= _iota((CHUNK, CHUNK), 1)).astype(BF16)
    bd_mask = (lax.shift_right_logical(_iota((GROUP, GROUP), 0), 6)
               == lax.shift_right_logical(_iota((GROUP, GROUP), 1), 6))
    t_idx = _iota((CHUNK, GROUP), 0)
    s_idx = jnp.bitwise_and(_iota((CHUNK, GROUP), 1), CHUNK - 1)
    strict = t_idx > s_idx
    incl = t_idx >= s_idx
    eye = (t_idx == s_idx).astype(F32)

    bd16 = bd_mask.astype(BF16)

    def bd(x):
        x16 = x.astype(BF16)
        return jnp.concatenate([x16, x16, x16, x16], axis=0) * bd16

    def b16(x):
        return x.astype(BF16)

    ng = RWKV_WIDTH // GROUP
    pairs = [(j, gi) for j in range(nsub) for gi in range(ng)]
    gsl = lambda gi: slice(gi * GROUP, (gi + 1) * GROUP)

    at, bt, kt, rt, bw, kw, vj, w_last = [], [], [], [], [], [], [], []
    for j in range(nsub):
        sl = slice(j * CHUNK, (j + 1) * CHUNK)
        lwj = lw[sl]
        h1 = lwj.astype(BF16)
        r1 = lwj - h1.astype(F32)
        h2 = r1.astype(BF16)
        h3 = (r1 - h2.astype(F32)).astype(BF16)
        cum = _dot(tril16, h1) + _dot(tril16, h2) + _dot(tril16, h3)
        e_cum = jnp.exp(cum)
        e_inv = jnp.exp(-cum)
        wl = e_cum[CHUNK - 1:CHUNK, :]
        at.append(-kk[sl] * jnp.exp(cum - lwj))
        bt.append(bvec[sl] * e_inv)
        kt.append(k2u[sl] * e_inv)
        rt.append(r[sl] * e_cum)
        bw.append(bt[j] * wl)
        kw.append(kt[j] * wl)
        vj.append(vu[sl])
        w_last.append(wl)

    a_ab, a_ak, a_rb, a_rk, vbd = {}, {}, {}, {}, {}
    for (j, gi) in pairs:
        gs = gsl(gi)
        q2 = b16(jnp.concatenate([at[j][:, gs], rt[j][:, gs]], axis=0))
        sc = _dot(q2, jnp.concatenate([bd(bt[j][:, gs]), bd(kt[j][:, gs])], axis=0), nt=True)
        a_ab[j, gi] = jnp.where(strict, sc[:CHUNK, :GROUP], 0.0)
        a_ak[j, gi] = jnp.where(strict, sc[:CHUNK, GROUP:], 0.0)
        a_rb[j, gi] = jnp.where(incl, sc[CHUNK:, :GROUP], 0.0)
        a_rk[j, gi] = jnp.where(incl, sc[CHUNK:, GROUP:], 0.0)
        vbd[j, gi] = bd(vj[j][:, gs])

    tinv = {p: eye + a_ab[p] for p in pairs}
    xpow = {p: _dot(b16(a_ab[p]), bd(a_ab[p])) for p in pairs}
    for step in range(1, 6):
        for p in pairs:
            if step < 5:
                both = _dot(b16(xpow[p]), jnp.concatenate([bd(tinv[p]), bd(xpow[p])], axis=1))
                tinv[p] = tinv[p] + both[:, :GROUP]
                xpow[p] = both[:, GROUP:]
            else:
                tinv[p] = tinv[p] + _dot(b16(xpow[p]), bd(tinv[p]))

    akv = {p: _dot(b16(a_ak[p]), vbd[p]) for p in pairs}
    pq = {(j, gi): _dot(b16(tinv[j, gi]), jnp.concatenate([bd(at[j][:, gsl(gi)]), bd(akv[j, gi])], axis=1))
          for (j, gi) in pairs}
    r2, y0, m_lr, n_add = {}, {}, {}, {}
    for (j, gi) in pairs:
        gs = gsl(gi)
        pm, qm = pq[j, gi][:, :GROUP], pq[j, gi][:, GROUP:]
        rr = _dot(b16(a_rb[j, gi]), jnp.concatenate([bd(pm), bd(qm)], axis=1))
        r2[j, gi] = b16(rt[j][:, gs] + rr[:, :GROUP])
        y0[j, gi] = rr[:, GROUP:] + _dot(b16(a_rk[j, gi]), vbd[j, gi])
        m_lr[j, gi] = b16(jnp.where(bd_mask, _dot(b16(pm.T), b16(bw[j][:, gs])), 0.0))
        qv = jnp.concatenate([qm, vj[j][:, gs]], axis=0)
        bk = jnp.concatenate([bw[j][:, gs], kw[j][:, gs]], axis=0)
        n_add[j, gi] = jnp.where(bd_mask, _dot(b16(qv.T), b16(bk)), 0.0)

    s_in = {}
    for gi in range(ng):
        s = s_sc[gi]
        for j in range(nsub):
            s16 = b16(s)
            s_in[j, gi] = s16
            s = s * w_last[j][:, gsl(gi)] + _dot(s16, m_lr[j, gi]) + n_add[j, gi]
        s_sc[gi] = s

    ys = []
    for j in range(nsub):
        yg = [_dot(r2[j, gi], s_in[j, gi], nt=True) + y0[j, gi] for gi in range(ng)]
        ys.append(jnp.concatenate(yg, axis=1))
    y = jnp.concatenate(ys, axis=0) if nsub > 1 else ys[0]

    inv_n = 1.0 / HEAD
    yc = y - _head_sum(y) * inv_n
    yn = yc * lax.rsqrt(_head_sum(yc * yc) * inv_n + GN_EPS)
    yn = yn * lng_ref[...] + lnb_ref[...]
    bonus = _head_sum(r * k2 * rk_ref[...]) * v
    y_ref[...] = (yn + bonus) * g
    sout_ref[...] = s_sc[...]


def _rwkv(p_r, prev0, s0_bd, lp, nsub, t_valid):
    b, t, _ = p_r.shape
    tc = nsub * CHUNK
    ng = RWKV_WIDTH // GROUP
    row = lambda a: a.reshape(1, -1)
    zeros = jnp.zeros((DECAY_RANK, RWKV_WIDTH), F32)
    wwa = jnp.concatenate([jnp.concatenate([lp["rwkv_w_up"], zeros], axis=1),
                           jnp.concatenate([zeros, lp["rwkv_a_up"]], axis=1)], axis=0)
    const = lambda shape: pl.BlockSpec(shape, lambda bi, ci: (0,) * len(shape))
    kern = functools.partial(_rwkv_kernel, nsub=nsub, t_valid=t_valid)
    return pl.pallas_call(
        kern,
        grid=(b, t // tc),
        in_specs=[pl.BlockSpec((None, tc, RWKV_PROJ), lambda bi, ci: (bi, ci, 0)),
                  pl.BlockSpec((None, 1, RWKV_PROJ), lambda bi, ci: (bi, 0, 0)),
                  pl.BlockSpec((None, ng, GROUP, GROUP), lambda bi, ci: (bi, 0, 0, 0)),
                  const((1, RWKV_PROJ)), const((1, RWKV_WIDTH)),
                  const((DECAY_RANK + ICLR_RANK, 2 * RWKV_WIDTH)), const((1, RWKV_WIDTH)),
                  const((GATE_RANK, RWKV_WIDTH)), const((1, RWKV_WIDTH)), const((1, RWKV_WIDTH)),
                  const((1, RWKV_WIDTH)), const((1, RWKV_WIDTH)), const((1, RWKV_WIDTH))],
        out_specs=[pl.BlockSpec((None, tc, RWKV_WIDTH), lambda bi, ci: (bi, ci, 0)),
                   pl.BlockSpec((None, ng, GROUP, GROUP), lambda bi, ci: (bi, 0, 0, 0))],
        out_shape=[jax.ShapeDtypeStruct((b, t, RWKV_WIDTH), F32),
                   jax.ShapeDtypeStruct((b, ng, GROUP, GROUP), F32)],
        scratch_shapes=[pltpu.VMEM((1, RWKV_PROJ), F32), pltpu.VMEM((ng, GROUP, GROUP), F32)],
        compiler_params=_cparams("arbitrary", "arbitrary"),
        name="rwkv7",
    )(p_r, prev0, s0_bd, row(lp["rwkv_mu"]), row(lp["rwkv_w0"]), wwa, row(lp["rwkv_a0"]),
      lp["rwkv_g_up"], row(lp["rwkv_k_k"]), row(lp["rwkv_k_a"]), row(lp["rwkv_r_k"]),
      row(lp["rwkv_ln_g"]), row(lp["rwkv_ln_b"]))


def _state_to_bd(s):
    b = s.shape[0]
    hg = GROUP // HEAD
    s5 = s.reshape(b, RWKV_HEADS // hg, hg, HEAD, HEAD)
    out = jnp.einsum("bghvk,hj->bghvjk", s5, jnp.eye(hg, dtype=s.dtype))
    return out.reshape(b, RWKV_HEADS // hg, GROUP, GROUP)


def _state_from_bd(s_bd):
    b = s_bd.shape[0]
    hg = GROUP // HEAD
    s6 = s_bd.reshape(b, RWKV_HEADS // hg, hg, HEAD, hg, HEAD)
    out = jnp.einsum("bghvjk,hj->bghvk", s6, jnp.eye(hg, dtype=s_bd.dtype))
    return out.reshape(b, RWKV_HEADS, HEAD, HEAD)


def _swa_kernel(tab_ref, sink_ref, pa_ref, ck_ref, cv_ref, bkt_ref, qg_ref, kg_ref,
                ya_ref, kn_ref, kall, vall, bias_sc, *, cq, nqc, has_cache):
    tq = cq * nqc
    band = WINDOW + cq
    bi = pl.program_id(0)
    i = pl.program_id(1)
    lane = _iota((1, LANES), 1)
    low = lane < HEAD

    @pl.when(jnp.logical_and(bi == 0, i == 0))
    def _():
        bkt = bkt_ref[...]
        for h in range(ATTN_HEADS):
            def body(j, acc, h=h):
                return acc + jnp.where(bkt == j, tab_ref[j, h], 0.0)
            bh = lax.fori_loop(0, REL_BUCKETS, body, jnp.zeros((cq, band), F32))
            bias_sc[h // 2, (h % 2) * cq:(h % 2 + 1) * cq, :] = bh

    @pl.when(i == 0)
    def _():
        kall[0:WINDOW, :] = ck_ref[...]
        vall[0:WINDOW, :] = cv_ref[...]

    pa = pa_ref[...]
    q = pa[:, :ATTN_WIDTH]
    k = pa[:, ATTN_WIDTH:ATTN_WIDTH + KV_WIDTH]
    v = pa[:, ATTN_WIDTH + KV_WIDTH:]
    inv_n = 1.0 / HEAD
    qn = q * lax.rsqrt(_head_sum(q * q) * inv_n + RMS_EPS) * qg_ref[...]
    kn = k * lax.rsqrt(_head_sum(k * k) * inv_n + RMS_EPS) * kg_ref[...]
    kn_ref[...] = kn
    kall[WINDOW:WINDOW + tq, :] = kn
    vall[WINDOW:WINDOW + tq, :] = v

    row2 = _iota((2 * cq, 1), 0)
    col = _iota((1, band), 1)
    thr = jnp.where(i == 0, WINDOW, 0)
    k_all = kall[...]
    v_all = vall[...]
    k_rot = pltpu.roll(k_all, HEAD, axis=1)
    v_rot = pltpu.roll(v_all, HEAD, axis=1)
    kd = [jnp.where(low, k_all, k_rot).astype(BF16), jnp.where(low, k_rot, k_all).astype(BF16)]
    vd = [jnp.where(low, v_all, v_rot).astype(BF16), jnp.where(low, v_rot, v_all).astype(BF16)]

    work = [(qc, pi) for qc in range(nqc) for pi in range(ATTN_HEADS // 2)]
    kv_of = lambda pi: (2 * pi) // (ATTN_HEADS // KV_HEADS)
    scores = {}
    for (qc, pi) in work:
        qp = qn[qc * cq:(qc + 1) * cq, pi * LANES:(pi + 1) * LANES]
        qs = jnp.concatenate([jnp.where(low, qp, 0.0), jnp.where(low, 0.0, qp)], axis=0)
        s = _dot(qs.astype(BF16), kd[kv_of(pi)][qc * cq:qc * cq + band], nt=True) * ATTN_SCALE + bias_sc[pi]
        if (not has_cache) and qc * cq < WINDOW:
            s = jnp.where((col + qc * cq) < thr, NEG_INF, s)
        scores[qc, pi] = s
    probs = {}
    for (qc, pi) in work:
        s = scores[qc, pi]
        sink = jnp.where(row2 < cq, sink_ref[0, 2 * pi], sink_ref[0, 2 * pi + 1])
        m = jnp.maximum(jnp.max(s, axis=-1, keepdims=True), sink)
        e = jnp.exp(s - m)
        den = jnp.sum(e, axis=-1, keepdims=True) + jnp.exp(sink - m)
        probs[qc, pi] = (e / den).astype(BF16)
    for (qc, pi) in work:
        o = _dot(probs[qc, pi], vd[kv_of(pi)][qc * cq:qc * cq + band])
        ya_ref[qc * cq:(qc + 1) * cq, pi * LANES:(pi + 1) * LANES] = jnp.where(low, o[:cq], o[cq:])

    if tq >= WINDOW:
        knext = kall[tq:tq + WINDOW, :]
        vnext = vall[tq:tq + WINDOW, :]
        kall[0:WINDOW, :] = knext
        vall[0:WINDOW, :] = vnext


def _t5_bucket(rel):
    half = REL_BUCKETS // 2
    max_exact = half // 2
    n = jnp.abs(rel)
    log_ratio = jnp.log(jnp.maximum(n, 1).astype(F32) / max_exact) / math.log(REL_MAX_DIST / max_exact)
    large = jnp.minimum(max_exact + (log_ratio * (half - max_exact)).astype(I32), half - 1)
    return jnp.where(rel > 0, half, 0) + jnp.where(n < max_exact, n, large)


def _swa(p_a, cache_k, cache_v, rel_table, sinks, q_norm, k_norm, cq, nqc, has_cache):
    b, t, _ = p_a.shape
    tq = cq * nqc
    band = WINDOW + cq
    rel = (jnp.arange(band) - WINDOW)[None, :] - jnp.arange(cq)[:, None]
    bkt = _t5_bucket(rel).astype(I32)
    qg = jnp.tile(q_norm, ATTN_HEADS).reshape(1, ATTN_WIDTH)
    kg = jnp.tile(k_norm, KV_HEADS).reshape(1, KV_WIDTH)
    kern = functools.partial(_swa_kernel, cq=cq, nqc=nqc, has_cache=has_cache)
    smem = pl.BlockSpec(memory_space=pltpu.SMEM)
    return pl.pallas_call(
        kern,
        grid=(b, t // tq),
        in_specs=[smem, smem,
                  pl.BlockSpec((None, tq, ATTN_PROJ), lambda bi, i: (bi, i, 0)),
                  pl.BlockSpec((None, WINDOW, KV_WIDTH), lambda bi, i: (bi, 0, 0)),
                  pl.BlockSpec((None, WINDOW, KV_WIDTH), lambda bi, i: (bi, 0, 0)),
                  pl.BlockSpec((cq, band), lambda bi, i: (0, 0)),
                  pl.BlockSpec((1, ATTN_WIDTH), lambda bi, i: (0, 0)),
                  pl.BlockSpec((1, KV_WIDTH), lambda bi, i: (0, 0))],
        out_specs=[pl.BlockSpec((None, tq, ATTN_WIDTH), lambda bi, i: (bi, i, 0)),
                   pl.BlockSpec((None, tq, KV_WIDTH), lambda bi, i: (bi, i, 0))],
        out_shape=[jax.ShapeDtypeStruct((b, t, ATTN_WIDTH), F32),
                   jax.ShapeDtypeStruct((b, t, KV_WIDTH), F32)],
        scratch_shapes=[pltpu.VMEM((WINDOW + tq, KV_WIDTH), F32),
                        pltpu.VMEM((WINDOW + tq, KV_WIDTH), F32),
                        pltpu.VMEM((ATTN_HEADS // 2, 2 * cq, band), F32)],
        compiler_params=_cparams("arbitrary", "arbitrary"),
        name="swa",
    )(rel_table, sinks.reshape(1, ATTN_HEADS), p_a, cache_k, cache_v, bkt, qg, kg)


def _mid_kernel(yr_ref, ya_ref, x_ref, gt1_ref, sh2_ref, sc2_ref, g2_ref, wor_ref, woa_ref, wrt_ref,
                rb_ref, base0_ref, x1_ref, h2_ref, te_ref, gate_ref, pos_ref, cnt_ref, base_sc, *, tm):
    @pl.when(jnp.logical_and(pl.program_id(0) == 0, pl.program_id(1) == 0))
    def _():
        base_sc[...] = base0_ref[...]

    mix = _dot(yr_ref[...].astype(BF16), wor_ref[...]) + _dot(ya_ref[...].astype(BF16), woa_ref[...])
    x1 = x_ref[...] + gt1_ref[...] * mix
    x1_ref[...] = x1
    h2 = _rms(x1, RMS_EPS) * g2_ref[...] * (1.0 + sc2_ref[...]) + sh2_ref[...]
    h2_ref[...] = h2

    logits = _mm(wrt_ref[...], h2, passes=3, nt=True) + rb_ref[...]
    eidx = _iota((N_EXPERTS, tm), 0)
    work = logits
    tops, hots = [], []
    for j in range(TOP_K):
        m = jnp.max(work, axis=0, keepdims=True)
        idx = jnp.min(jnp.where(work == m, eidx, N_EXPERTS), axis=0, keepdims=True)
        hot = eidx == idx
        work = jnp.where(hot, -jnp.inf, work)
        tops.append(m)
        hots.append(hot)
        te_ref[j:j + 1, :] = idx
    exps = [jnp.exp(tj - tops[0]) for tj in tops]
    den = exps[0] + exps[1] + exps[2] + exps[3]
    for j in range(TOP_K):
        gate_ref[j:j + 1, :] = exps[j] / den

    member = (hots[0] | hots[1] | hots[2] | hots[3]).astype(BF16)
    upper = (_iota((tm, tm), 0) < _iota((tm, tm), 1)).astype(BF16)
    before = _dot(member, upper) + base_sc[...]
    for j in range(TOP_K):
        pos_ref[j:j + 1, :] = jnp.sum(jnp.where(hots[j], before, 0.0), axis=0, keepdims=True).astype(I32)
    base_sc[...] = base_sc[...] + jnp.sum(member.astype(F32), axis=1, keepdims=True)
    cnt_ref[...] = base_sc[...]


def _mid(y_r, y_a, x3, gt1, sh2, sc2, gain2, wo_r16, wo_a16, wr_t, rb, base0, tm):
    g, r, d = x3.shape
    n = g * r
    per = r // tm
    mr = gt1.shape[1]
    mrow = (lambda b, i: (b, 0, 0)) if mr == 1 else (lambda b, i: (b, i, 0))
    mblk = 1 if mr == 1 else tm
    tok = lambda w: pl.BlockSpec((None, tm, w), lambda b, i: (b, i, 0))
    mod = pl.BlockSpec((None, mblk, d), mrow)
    const = lambda shape: pl.BlockSpec(shape, lambda b, i: (0,) * len(shape))
    lane_out = pl.BlockSpec((TOP_K, tm), lambda b, i: (0, b * per + i))
    kern = functools.partial(_mid_kernel, tm=tm)
    return pl.pallas_call(
        kern,
        grid=(g, per),
        in_specs=[tok(RWKV_WIDTH), tok(ATTN_WIDTH), tok(d), mod, mod, mod, const((1, d)),
                  const((RWKV_WIDTH, d)), const((ATTN_WIDTH, d)), const((N_EXPERTS, d)),
                  const((N_EXPERTS, 1)), const((N_EXPERTS, 1))],
        out_specs=[tok(d), tok(d), lane_out, lane_out, lane_out, const((N_EXPERTS, 1))],
        out_shape=[jax.ShapeDtypeStruct((g, r, d), F32), jax.ShapeDtypeStruct((g, r, d), F32),
                   jax.ShapeDtypeStruct((TOP_K, n), I32), jax.ShapeDtypeStruct((TOP_K, n), F32),
                   jax.ShapeDtypeStruct((TOP_K, n), I32), jax.ShapeDtypeStruct((N_EXPERTS, 1), F32)],
        scratch_shapes=[pltpu.VMEM((N_EXPERTS, 1), F32)],
        compiler_params=_cparams("arbitrary", "arbitrary"),
        name="mid",
    )(y_r, y_a, x3, gt1, sh2, sc2, gain2.reshape(1, d), wo_r16, wo_a16, wr_t, rb, base0)


def _slot_kernel(pstart_ref, te_ref, pos_ref, o_ref):
    te = te_ref[...]
    acc = pos_ref[...]
    for e in range(N_EXPERTS):
        acc = acc + jnp.where(te == e, pstart_ref[e], 0)
    o_ref[...] = acc


def _slot_rows(te, pos, pstart):
    k, n = te.shape
    return pl.pallas_call(
        _slot_kernel,
        grid=(1,),
        in_specs=[pl.BlockSpec(memory_space=pltpu.SMEM),
                  pl.BlockSpec((k, n), lambda i: (0, 0)), pl.BlockSpec((k, n), lambda i: (0, 0))],
        out_specs=pl.BlockSpec((k, n), lambda i: (0, 0)),
        out_shape=jax.ShapeDtypeStruct((k, n), I32),
        compiler_params=_cparams("arbitrary"),
        name="slot_rows",
    )(pstart, te, pos)


def _dispatch_kernel(dst_ref, pad_ref, h2a_ref, h2b_ref, xs_ref, ztile, sem, zsem, *, tk, steps_a):
    i = pl.program_id(0)

    @pl.when(i == 0)
    def _():
        ztile[...] = jnp.zeros_like(ztile)
        zrow = ztile.at[pl.ds(0, 1)]
        tm = ztile.shape[0]

        def zero_row(row, priority):
            pltpu.make_async_copy(zrow, xs_ref.at[pl.ds(row, 1)], zsem).start(priority=priority)

        def per_expert(e, total):
            first = pad_ref[0, e]
            count = pad_ref[1, e]

            def per_pair(r, c):
                zero_row(first + 2 * r, 0)
                zero_row(first + 2 * r + 1, 1)
                return c
            lax.fori_loop(0, count // 2, per_pair, 0)

            @pl.when(count % 2 == 1)
            def _():
                zero_row(first + count - 1, 0)
            return total + count
        n_pad = lax.fori_loop(0, N_EXPERTS, per_expert, 0)

        blk = 64
        def drain_blk(r, c):
            pltpu.make_async_copy(ztile.at[pl.ds(0, blk)], xs_ref.at[pl.ds(0, blk)], zsem).wait()
            return c
        lax.fori_loop(0, n_pad // blk, drain_blk, 0)

        def drain_row(r, c):
            pltpu.make_async_copy(zrow, xs_ref.at[pl.ds(0, 1)], zsem).wait()
            return c
        lax.fori_loop(0, n_pad % blk, drain_row, 0)

        def spare(t, c):
            row0 = pl.multiple_of(t * tm, tm)
            pltpu.make_async_copy(ztile, xs_ref.at[pl.ds(row0, tm)], zsem).start()
            return c
        lax.fori_loop(pad_ref[2, 0], pad_ref[2, 1], spare, 0)

        def spare_wait(t, c):
            pltpu.make_async_copy(ztile, xs_ref.at[pl.ds(0, tm)], zsem).wait()
            return c
        lax.fori_loop(pad_ref[2, 0], pad_ref[2, 1], spare_wait, 0)

    def scatter(h2_ref):
        for t in range(h2_ref.shape[0] * ROW_UNROLL):
            for j in range(TOP_K):
                dst = dst_ref[j, t]
                pltpu.make_async_copy(h2_ref.at[t // ROW_UNROLL, pl.ds(t % ROW_UNROLL, 1)],
                                      xs_ref.at[pl.ds(dst, 1)], sem).start(priority=j % 2)
        for _ in range(TOP_K):
            pltpu.make_async_copy(h2_ref, h2_ref, sem).wait()

    pl.when(i < steps_a)(functools.partial(scatter, h2a_ref))
    pl.when(i == steps_a)(functools.partial(scatter, h2b_ref))


def _dispatch(dest, pads, h2a, h2b, n_rows, tk):
    na, d = h2a.shape
    nb = h2b.shape[0]
    steps_a = na // tk
    dest = jnp.pad(dest, ((0, 0), (0, (steps_a + 1) * tk - na - nb)))
    h2a = h2a.reshape(na // ROW_UNROLL, ROW_UNROLL, d)
    h2b = h2b.reshape(nb // ROW_UNROLL, ROW_UNROLL, d)
    kern = functools.partial(_dispatch_kernel, tk=tk, steps_a=steps_a)
    return pl.pallas_call(
        kern,
        grid=(steps_a + 1,),
        in_specs=[pl.BlockSpec((TOP_K, tk), lambda i: (0, i), memory_space=pltpu.SMEM),
                  pl.BlockSpec(memory_space=pltpu.SMEM),
                  pl.BlockSpec((tk // ROW_UNROLL, ROW_UNROLL, d), lambda i: (jnp.minimum(i, steps_a - 1), 0, 0)),
                  pl.BlockSpec((nb // ROW_UNROLL, ROW_UNROLL, d), lambda i: (0, 0, 0))],
        out_specs=pl.BlockSpec(memory_space=pl.ANY),
        out_shape=jax.ShapeDtypeStruct((n_rows, d), F32),
        scratch_shapes=[pltpu.VMEM((TM_FFN, d), F32), pltpu.SemaphoreType.DMA(()), pltpu.SemaphoreType.DMA(())],
        compiler_params=_cparams("arbitrary"),
        name="dispatch",
    )(dest, pads, h2a, h2b)


def _ffn_kernel(te_ref, nu_ref, nx_ref, xs_ref, wu_hbm, bu_ref, wd_hbm, bd_ref, ys_ref,
                wu32, wd32, wu16, wd16, sem):
    i = pl.program_id(0)
    n_up = wu32.shape[1]
    active = i < nu_ref[0]
    fresh = jnp.logical_or(i == 0, te_ref[i] != te_ref[jnp.maximum(i - 1, 0)])

    def weight_copies(e):
        return (pltpu.make_async_copy(wu_hbm.at[e], wu32, sem.at[0]),
                pltpu.make_async_copy(wd_hbm.at[e], wd32, sem.at[1]))

    @pl.when(i == 0)
    def _():
        for cp in weight_copies(te_ref[0]):
            cp.start()

    @pl.when(jnp.logical_and(active, fresh))
    def _():
        for cp in weight_copies(te_ref[i]):
            cp.wait()
        src = _iota((GROUP, GROUP), 0)
        dst = _iota((GROUP, GROUP), 1)
        want = jnp.where(dst < LANES, 2 * dst, 2 * (dst - LANES) + 1)
        perm = (src == want).astype(BF16)
        for b in range(n_up // GROUP):
            cs = slice(b * GROUP, (b + 1) * GROUP)
            wu16[:, cs] = _dot(wu32[:, cs].astype(BF16), perm).astype(BF16)
        wd16[...] = wd32[...].astype(BF16)

        @pl.when(nx_ref[i] >= 0)
        def _():
            for cp in weight_copies(nx_ref[i]):
                cp.start()

    @pl.when(active)
    def _():
        u = _dot(xs_ref[...].astype(BF16), wu16[...]) + bu_ref[...]
        acts = []
        for b in range(n_up // GROUP):
            glu = jnp.minimum(u[:, b * GROUP:b * GROUP + LANES], SWIGLU_LIMIT)
            lin = jnp.clip(u[:, b * GROUP + LANES:(b + 1) * GROUP], -SWIGLU_LIMIT, SWIGLU_LIMIT)
            acts.append((glu * jax.nn.sigmoid(SWIGLU_ALPHA * glu) * (lin + 1.0)).astype(BF16))
        act = jnp.concatenate(acts, axis=1)
        ys_ref[...] = _dot(act, wd16[...]) + bd_ref[...]

    @pl.when(jnp.logical_not(active))
    def _():
        ys_ref[...] = jnp.zeros_like(ys_ref)


def _ffn(tile_e, n_used, next_e, xs, wu, bu, wd, bd_, tm):
    rows, d = xs.shape
    n_exp, _, n_up = wu.shape
    dff = wd.shape[1]
    row_blk = lambda i, te, nu, nx: (jnp.minimum(i, nu[0] - 1), 0)
    exp_blk = lambda i, te, nu, nx: (te[i], 0, 0)
    grid_spec = pltpu.PrefetchScalarGridSpec(
        num_scalar_prefetch=3,
        grid=(rows // tm,),
        in_specs=[pl.BlockSpec((tm, d), row_blk),
                  pl.BlockSpec(memory_space=pl.ANY),
                  pl.BlockSpec((None, 1, n_up), exp_blk),
                  pl.BlockSpec(memory_space=pl.ANY),
                  pl.BlockSpec((None, 1, d), exp_blk)],
        out_specs=pl.BlockSpec((tm, d), lambda i, te, nu, nx: (i, 0)),
        scratch_shapes=[pltpu.VMEM((d, n_up), F32), pltpu.VMEM((dff, d), F32),
                        pltpu.VMEM((d, n_up), BF16), pltpu.VMEM((dff, d), BF16),
                        pltpu.SemaphoreType.DMA((2,))],
    )
    return pl.pallas_call(
        _ffn_kernel,
        grid_spec=grid_spec,
        out_shape=jax.ShapeDtypeStruct((rows, d), F32),
        compiler_params=_cparams("arbitrary"),
        name="expert_ffn",
    )(tile_e, n_used, next_e, xs, wu, bu.reshape(n_exp, 1, n_up), wd, bd_.reshape(n_exp, 1, d))


def _combine_kernel(src_ref, srcn_ref, gate_ref, x1_ref, gt2_ref, ys_ref, o_ref, buf, sem, *, tk):
    i = pl.program_id(1)
    n = pl.num_programs(1)
    flat = pl.program_id(0) * n + i
    total = pl.num_programs(0) * n
    slot = lax.rem(flat, 2)

    def issue(rows_ref, s):
        for t in range(tk):
            for j in range(TOP_K):
                src = rows_ref[j, t]
                pltpu.make_async_copy(ys_ref.at[pl.ds(src, 1)],
                                      buf.at[s, j, t // ROW_UNROLL, pl.ds(t % ROW_UNROLL, 1)],
                                      sem.at[s]).start()

    @pl.when(flat == 0)
    def _():
        issue(src_ref, 0)

    def step(cur):
        @pl.when(flat + 1 < total)
        def _():
            issue(srcn_ref, 1 - cur)

        for j in range(TOP_K):
            pltpu.make_async_copy(buf.at[cur, j], buf.at[cur, j], sem.at[cur]).wait()
        eye = (_iota((tk, tk), 0) == _iota((tk, tk), 1)).astype(BF16)
        g_hi, g_lo = _split(gate_ref[...])
        gate = _dot(eye, g_hi, nt=True) + _dot(eye, g_lo, nt=True)
        rows = lambda j: buf[cur, j].reshape(tk, buf.shape[-1])
        acc = gate[:, 0:1] * rows(0)
        for j in range(1, TOP_K):
            acc = acc + gate[:, j:j + 1] * rows(j)
        o_ref[...] = x1_ref[...] + gt2_ref[...] * acc

    for cur in range(2):
        pl.when(slot == cur)(functools.partial(step, cur))


def _combine(dest, gates, x1, gt2, ys, tk):
    g, r, d = x1.shape
    per = r // tk
    last = g * per - 1
    mr = gt2.shape[1]
    mrow = (lambda b, i: (b, 0, 0)) if mr == 1 else (lambda b, i: (b, i, 0))
    mblk = 1 if mr == 1 else tk
    cur = pl.BlockSpec((TOP_K, tk), lambda b, i: (0, b * per + i), memory_space=pltpu.SMEM)
    nxt = pl.BlockSpec((TOP_K, tk), lambda b, i: (0, jnp.minimum(b * per + i + 1, last)),
                       memory_space=pltpu.SMEM)
    kern = functools.partial(_combine_kernel, tk=tk)
    return pl.pallas_call(
        kern,
        grid=(g, per),
        in_specs=[cur, nxt,
                  pl.BlockSpec((TOP_K, tk), lambda b, i: (0, b * per + i)),
                  pl.BlockSpec((None, tk, d), lambda b, i: (b, i, 0)),
                  pl.BlockSpec((None, mblk, d), mrow),
                  pl.BlockSpec(memory_space=pl.ANY)],
        out_specs=pl.BlockSpec((None, tk, d), lambda b, i: (b, i, 0)),
        out_shape=jax.ShapeDtypeStruct((g, r, d), F32),
        scratch_shapes=[pltpu.VMEM((2, TOP_K, tk // ROW_UNROLL, ROW_UNROLL, d), F32),
                        pltpu.SemaphoreType.DMA((2,))],
        compiler_params=_cparams("arbitrary", "arbitrary"),
        name="combine",
    )(dest, dest, gates, x1, gt2, ys)


def _tile_plan(counts, tm, n_tiles):
    cnt = counts.reshape(-1).astype(I32)
    tiles = (cnt + tm - 1) // tm
    tile_end = jnp.cumsum(tiles)
    pstart = ((tile_end - tiles) * tm).astype(I32)
    n_used = tile_end[-1:].astype(I32)
    idx = jnp.minimum(jnp.arange(n_tiles, dtype=I32), n_used[0] - 1)
    tile_e = jnp.sum((tile_end[None, :] <= idx[:, None]).astype(I32), axis=1)
    spare = jnp.zeros((N_EXPERTS,), I32).at[0].set(n_used[0]).at[1].set(n_tiles)
    pads = jnp.stack([pstart + cnt, tiles * tm - cnt, spare]).astype(I32)
    tile_e = jnp.minimum(tile_e, N_EXPERTS - 1)
    end_of = tile_end[tile_e]
    next_e = jnp.where(end_of < n_used[0], tile_e[jnp.minimum(end_of, n_tiles - 1)], -1).astype(I32)
    return pstart, tile_e, n_used, next_e, pads


def kernel(x_prompt, x_sample, state_shift, state_wkv, cache_win_k, cache_win_v, c_prompt, c_sample, rel_bias, norm1_g, norm2_g, w_ada, b_ada, w_in, w_out, rwkv_mu, rwkv_w0, rwkv_w_up, rwkv_a0, rwkv_a_up, rwkv_g_up, rwkv_k_k, rwkv_k_a, rwkv_r_k, rwkv_ln_g, rwkv_ln_b, q_norm, k_norm, sinks, router_w, router_b, exp_w_up, exp_b_up, exp_w_down, exp_b_down):
    bp, tp, d = x_prompt.shape
    bs, ts, _ = x_sample.shape
    depth = w_in.shape[0]
    assert depth == 1, "single-layer trunk"
    layer = 0
    lp = dict(rwkv_mu=rwkv_mu[layer], rwkv_w0=rwkv_w0[layer], rwkv_w_up=rwkv_w_up[layer],
              rwkv_a0=rwkv_a0[layer], rwkv_a_up=rwkv_a_up[layer], rwkv_g_up=rwkv_g_up[layer],
              rwkv_k_k=rwkv_k_k[layer], rwkv_k_a=rwkv_k_a[layer], rwkv_r_k=rwkv_r_k[layer].reshape(-1),
              rwkv_ln_g=rwkv_ln_g[layer], rwkv_ln_b=rwkv_ln_b[layer])

    rows = bp + bs
    pad = (-rows) % 8
    c_all = jnp.concatenate([c_prompt, c_sample, jnp.zeros((pad, d), F32)], axis=0)
    mod = _modulation(c_all, w_ada[layer], b_ada[layer])
    sh1, sc1, gt1, sh2, sc2, gt2 = [mod[:, i * d:(i + 1) * d] for i in range(6)]
    pm = lambda m: m[:bp].reshape(bp, 1, d)
    sm = lambda m: jnp.repeat(m[bp:bp + bs], ts, axis=0).reshape(1, bs * ts, d)

    w_in16 = w_in[layer].astype(BF16)
    wo16 = w_out[layer].astype(BF16)
    wo_r16, wo_a16 = wo16[:RWKV_WIDTH], wo16[RWKV_WIDTH:]
    wr_t = router_w[layer].T
    rb = router_b[layer].reshape(N_EXPERTS, 1)
    wu = exp_w_up.reshape(exp_w_up.shape[1:])
    wd = exp_w_down.reshape(exp_w_down.shape[1:])
    n_up = wu.shape[-1]
    bu = exp_b_up[layer].reshape(N_EXPERTS, n_up // GROUP, LANES, 2).transpose(0, 1, 3, 2).reshape(N_EXPERTS, n_up)

    pr_p, pa_p = _inproj(x_prompt, norm1_g[layer], pm(sh1), pm(sc1), w_in16, TM_PROJ)
    zero_prev = jnp.zeros((bp, 1, RWKV_PROJ), F32)
    zero_state = jnp.zeros((bp, RWKV_WIDTH // GROUP, GROUP, GROUP), F32)
    yr_p, s_p = _rwkv(pr_p, zero_prev, zero_state, lp, RWKV_SUB, None)
    zero_kv = jnp.zeros((bp, WINDOW, KV_WIDTH), F32)
    ya_p, kn_p = _swa(pa_p, zero_kv, zero_kv, rel_bias, sinks[layer], q_norm[layer], k_norm[layer],
                      CHUNK, SWA_QC, False)

    ns = bs * ts
    xs_flat = x_sample.reshape(1, ns, d)
    pr_s, pa_s = _inproj(xs_flat, norm1_g[layer], sm(sh1), sm(sc1), w_in16, ns)
    pr_s = pr_s.reshape(bs, ts, RWKV_PROJ)
    pa_s = pa_s.reshape(bs, ts, ATTN_PROJ)
    t_pad = -(-ts // CHUNK) * CHUNK
    pr_s_pad = jnp.pad(pr_s, ((0, 0), (0, t_pad - ts), (0, 0)))
    yr_s, s_s = _rwkv(pr_s_pad, state_shift[layer].reshape(bs, 1, RWKV_PROJ), _state_to_bd(state_wkv[layer]),
                      lp, t_pad // CHUNK, ts)
    yr_s = yr_s[:, :ts]
    ya_s, kn_s = _swa(pa_s, cache_win_k[layer].reshape(bs, WINDOW, KV_WIDTH),
                      cache_win_v[layer].reshape(bs, WINDOW, KV_WIDTH), rel_bias, sinks[layer],
                      q_norm[layer], k_norm[layer], ts, 1, True)

    base0 = jnp.zeros((N_EXPERTS, 1), F32)
    x1_p, h2_p, te_p, gate_p, pos_p, cnt_p = _mid(yr_p, ya_p, x_prompt, pm(gt1), pm(sh2), pm(sc2), norm2_g[layer],
                                                  wo_r16, wo_a16, wr_t, rb, base0, TM_PROJ)
    x1_s, h2_s, te_s, gate_s, pos_s, cnt = _mid(yr_s.reshape(1, ns, RWKV_WIDTH), ya_s.reshape(1, ns, ATTN_WIDTH),
                                                xs_flat, sm(gt1), sm(sh2), sm(sc2), norm2_g[layer],
                                                wo_r16, wo_a16, wr_t, rb, cnt_p, ns)

    n_p = bp * tp
    n_rows = (n_p + ns) * TOP_K + N_EXPERTS * (TM_FFN - 1)
    n_tiles = -(-n_rows // TM_FFN)
    pstart, tile_e, n_used, next_e, pads = _tile_plan(cnt, TM_FFN, n_tiles)
    dest = _slot_rows(jnp.concatenate([te_p, te_s], axis=1), jnp.concatenate([pos_p, pos_s], axis=1), pstart)
    dest_p, dest_s = dest[:, :n_p], dest[:, n_p:]
    xs_buf = _dispatch(dest, pads, h2_p.reshape(n_p, d), h2_s.reshape(ns, d), n_tiles * TM_FFN, TK_DISPATCH)
    ys_buf = _ffn(tile_e, n_used, next_e, xs_buf, wu, bu, wd, exp_b_down[layer], TM_FFN)
    y_p = _combine(dest_p, gate_p, x1_p, pm(gt2), ys_buf, TK_COMBINE)
    y_s = _combine(dest_s, gate_s, x1_s, sm(gt2), ys_buf, TK_COMBINE)

    kv4 = lambda z, rows_: z.reshape(z.shape[0], rows_, KV_HEADS, HEAD)[None]
    v_p = pa_p[:, tp - WINDOW:, ATTN_WIDTH + KV_WIDTH:]
    v_s = pa_s[:, :, ATTN_WIDTH + KV_WIDTH:]
    return (y_p, y_s.reshape(bs, ts, d),
            pr_p[:, tp - 1][None], _state_from_bd(s_p)[None], kv4(kn_p[:, tp - WINDOW:], WINDOW), kv4(v_p, WINDOW),
            pr_s[:, ts - 1][None], _state_from_bd(s_s)[None], kv4(kn_s, ts), kv4(v_s, ts))
```

```python
import functools
import math

import jax
import jax.numpy as jnp
from jax import lax
from jax.experimental import pallas as pl
from jax.experimental.pallas import tpu as pltpu

F32 = jnp.float32
BF16 = jnp.bfloat16
I32 = jnp.int32

HEAD = 64
RWKV_HEADS = 8
RWKV_WIDTH = RWKV_HEADS * HEAD
DECAY_RANK = 64
ICLR_RANK = 64
GATE_RANK = 128
RWKV_PROJ = 3 * RWKV_WIDTH + DECAY_RANK + ICLR_RANK + GATE_RANK
ATTN_HEADS = 8
KV_HEADS = 2
ATTN_WIDTH = ATTN_HEADS * HEAD
KV_WIDTH = KV_HEADS * HEAD
ATTN_PROJ = ATTN_WIDTH + 2 * KV_WIDTH
WINDOW = 128
CHUNK = 64
REL_BUCKETS = 32
REL_MAX_DIST = 128
N_EXPERTS = 32
TOP_K = 4
GN_EPS = 64e-5
RMS_EPS = 1e-6
NEG_INF = -1e30
ATTN_SCALE = HEAD ** -0.5
SWIGLU_ALPHA = 1.702
SWIGLU_LIMIT = 7.0

LANES = 128
GROUP = 256
VMEM_LIMIT = 56 * 1024 * 1024

TM_PROJ = 512
RWKV_SUB = 4
SWA_QC = 4
TM_FFN = 512
TK_DISPATCH = 512
TK_COMBINE = 256
ROW_UNROLL = 8


def _cparams(*sem):
    return pltpu.CompilerParams(dimension_semantics=sem, vmem_limit_bytes=VMEM_LIMIT)


def _dot(a, b, nt=False):
    dims = (((1,), (1,)), ((), ())) if nt else (((1,), (0,)), ((), ()))
    return lax.dot_general(a, b, dims, preferred_element_type=F32)


def _split(x):
    hi = x.astype(BF16)
    lo = (x - hi.astype(F32)).astype(BF16)
    return hi, lo


def _mm(a, b, passes=1, nt=False):
    if passes == 1:
        return _dot(a.astype(BF16), b.astype(BF16), nt)
    ah, al = _split(a)
    bh, bl = _split(b)
    return _dot(ah, bh, nt) + _dot(al, bh, nt) + _dot(ah, bl, nt)


def _iota(shape, dim):
    return lax.broadcasted_iota(I32, shape, dim)


def _head_ones(n):
    r = lax.shift_right_logical(_iota((n, n), 0), 6)
    c = lax.shift_right_logical(_iota((n, n), 1), 6)
    return (r == c).astype(BF16)


def _head_sum(x):
    width = x.shape[1]
    x16 = x.astype(BF16)
    if width <= GROUP:
        return _dot(x16, _head_ones(width))
    ones = _head_ones(GROUP)
    parts = [_dot(x16[:, g:g + GROUP], ones) for g in range(0, width, GROUP)]
    return jnp.concatenate(parts, axis=1)


def _rms(x, eps):
    return x * lax.rsqrt(jnp.mean(x * x, axis=-1, keepdims=True) + eps)


def _mod_kernel(c_ref, w_ref, b_ref, o_ref):
    c = c_ref[...]
    s = c * jax.nn.sigmoid(c)
    o_ref[...] = _mm(s, w_ref[...], passes=3) + b_ref[...]


def _modulation(c_all, w_ada, b_ada):
    rows, d = c_all.shape
    n = w_ada.shape[1]
    tn = 1536
    return pl.pallas_call(
        _mod_kernel,
        grid=(n // tn,),
        in_specs=[pl.BlockSpec((rows, d), lambda j: (0, 0)),
                  pl.BlockSpec((d, tn), lambda j: (0, j)),
                  pl.BlockSpec((1, tn), lambda j: (0, j))],
        out_specs=pl.BlockSpec((rows, tn), lambda j: (0, j)),
        out_shape=jax.ShapeDtypeStruct((rows, n), F32),
        compiler_params=_cparams("arbitrary"),
        name="modulation",
    )(c_all, w_ada, b_ada.reshape(1, n))


def _inproj_kernel(x_ref, g_ref, sh_ref, sc_ref, w_ref, pr_ref, pa_ref):
    x = x_ref[...]
    h = _rms(x, RMS_EPS) * g_ref[...] * (1.0 + sc_ref[...]) + sh_ref[...]
    p = _dot(h.astype(BF16), w_ref[...])
    pr_ref[...] = p[:, :RWKV_PROJ]
    pa_ref[...] = p[:, RWKV_PROJ:]


def _inproj(x3, gain, sh3, sc3, w_in16, tm):
    g, r, d = x3.shape
    mr = sh3.shape[1]
    mrow = (lambda b, i: (b, 0, 0)) if mr == 1 else (lambda b, i: (b, i, 0))
    mblk = 1 if mr == 1 else tm
    return pl.pallas_call(
        _inproj_kernel,
        grid=(g, r // tm),
        in_specs=[pl.BlockSpec((None, tm, d), lambda b, i: (b, i, 0)),
                  pl.BlockSpec((1, d), lambda b, i: (0, 0)),
                  pl.BlockSpec((None, mblk, d), mrow),
                  pl.BlockSpec((None, mblk, d), mrow),
                  pl.BlockSpec(w_in16.shape, lambda b, i: (0, 0))],
        out_specs=[pl.BlockSpec((None, tm, RWKV_PROJ), lambda b, i: (b, i, 0)),
                   pl.BlockSpec((None, tm, ATTN_PROJ), lambda b, i: (b, i, 0))],
        out_shape=[jax.ShapeDtypeStruct((g, r, RWKV_PROJ), F32),
                   jax.ShapeDtypeStruct((g, r, ATTN_PROJ), F32)],
        compiler_params=_cparams("arbitrary", "arbitrary"),
        name="inproj",
    )(x3, gain.reshape(1, d), sh3, sc3, w_in16)


def _softplus(x):
    return jnp.maximum(x, 0.0) + jnp.log(1.0 + jnp.exp(-jnp.abs(x)))


def _rwkv_kernel(p_ref, prev0_ref, s0_ref, mu_ref, w0_ref, wwa_ref, a0_ref, gup_ref, kk_ref, ka_ref,
                 rk_ref, lng_ref, lnb_ref, y_ref, sout_ref, prev_sc, s_sc, *, nsub, t_valid):
    tc = nsub * CHUNK
    c = pl.program_id(1)

    @pl.when(c == 0)
    def _():
        prev_sc[...] = prev0_ref[...]
        s_sc[...] = s0_ref[...]

    p = p_ref[...]
    row = _iota((tc, 1), 0)
    p_prev = jnp.where(row == 0, prev_sc[...], pltpu.roll(p, 1, axis=0))
    prev_sc[...] = p_ref[tc - 1:tc, :]
    xp = p + (p_prev - p) * mu_ref[...]

    w3 = 3 * RWKV_WIDTH
    r = xp[:, 0:RWKV_WIDTH]
    k = xp[:, RWKV_WIDTH:2 * RWKV_WIDTH]
    v = xp[:, 2 * RWKV_WIDTH:w3]
    z = xp[:, w3:w3 + DECAY_RANK + ICLR_RANK]
    gd = xp[:, w3 + DECAY_RANK + ICLR_RANK:]
    zl = _iota((1, DECAY_RANK + ICLR_RANK), 1)
    zz = jnp.where(zl < DECAY_RANK, jnp.tanh(z), z)
    wa = _mm(zz, wwa_ref[...], passes=3)
    w_log = -_softplus(-(w0_ref[...] + wa[:, :RWKV_WIDTH])) - 0.5
    lw = -jnp.exp(w_log)
    a = jax.nn.sigmoid(a0_ref[...] + wa[:, RWKV_WIDTH:])
    g = _mm(jax.nn.sigmoid(gd), gup_ref[...])
    kk = k * kk_ref[...]
    kk = kk * lax.rsqrt(jnp.maximum(_head_sum(kk * kk), 1e-24))
    k2 = k * (1.0 + (a - 1.0) * ka_ref[...])
    bvec = kk * a
    if t_valid is not None:
        ok = (c * tc + row) < t_valid
        lw = jnp.where(ok, lw, 0.0)
        kk = jnp.where(ok, kk, 0.0)
        bvec = jnp.where(ok, bvec, 0.0)
        k2u = jnp.where(ok, k2, 0.0)
        vu = jnp.where(ok, v, 0.0)
    else:
        k2u, vu = k2, v

    tril16 = (_iota((CHUNK, CHUNK), 0) >= _iota((CHUNK, CHUNK), 1)).astype(BF16)
    bd_mask = (lax.shift_right_logical(_iota((GROUP, GROUP), 0), 6)
               == lax.shift_right_logical(_iota((GROUP, GROUP), 1), 6))
    t_idx = _iota((CHUNK, GROUP), 0)
    s_idx = jnp.bitwise_and(_iota((CHUNK, GROUP), 1), CHUNK - 1)
    strict = t_idx > s_idx
    incl = t_idx >= s_idx
    eye = (t_idx == s_idx).astype(F32)

    bd16 = bd_mask.astype(BF16)

    def bd(x):
        x16 = x.astype(BF16)
        return jnp.concatenate([x16, x16, x16, x16], axis=0) * bd16

    def b16(x):
        return x.astype(BF16)

    ng = RWKV_WIDTH // GROUP
    pairs = [(j, gi) for j in range(nsub) for gi in range(ng)]
    gsl = lambda gi: slice(gi * GROUP, (gi + 1) * GROUP)

    at, bt, kt, rt, bw, kw, vj, w_last = [], [], [], [], [], [], [], []
    for j in range(nsub):
        sl = slice(j * CHUNK, (j + 1) * CHUNK)
        lwj = lw[sl]
        h1 = lwj.astype(BF16)
        r1 = lwj - h1.astype(F32)
        h2 = r1.astype(BF16)
        h3 = (r1 - h2.astype(F32)).astype(BF16)
        cum = _dot(tril16, h1) + _dot(tril16, h2) + _dot(tril16, h3)
        e_cum = jnp.exp(cum)
        e_inv = jnp.exp(-cum)
        wl = e_cum[CHUNK - 1:CHUNK, :]
        at.append(-kk[sl] * jnp.exp(cum - lwj))
        bt.append(bvec[sl] * e_inv)
        kt.append(k2u[sl] * e_inv)
        rt.append(r[sl] * e_cum)
        bw.append(bt[j] * wl)
        kw.append(kt[j] * wl)
        vj.append(vu[sl])
        w_last.append(wl)

    a_ab, a_ak, a_rb, a_rk, vbd = {}, {}, {}, {}, {}
    for (j, gi) in pairs:
        gs = gsl(gi)
        q2 = b16(jnp.concatenate([at[j][:, gs], rt[j][:, gs]], axis=0))
        sc = _dot(q2, jnp.concatenate([bd(bt[j][:, gs]), bd(kt[j][:, gs])], axis=0), nt=True)
        a_ab[j, gi] = jnp.where(strict, sc[:CHUNK, :GROUP], 0.0)
        a_ak[j, gi] = jnp.where(strict, sc[:CHUNK, GROUP:], 0.0)
        a_rb[j, gi] = jnp.where(incl, sc[CHUNK:, :GROUP], 0.0)
        a_rk[j, gi] = jnp.where(incl, sc[CHUNK:, GROUP:], 0.0)
        vbd[j, gi] = bd(vj[j][:, gs])

    tinv = {p: eye + a_ab[p] for p in pairs}
    xpow = {p: _dot(b16(a_ab[p]), bd(a_ab[p])) for p in pairs}
    for step in range(1, 6):
        for p in pairs:
            if step < 5:
                both = _dot(b16(xpow[p]), jnp.concatenate([bd(tinv[p]), bd(xpow[p])], axis=1))
                tinv[p] = tinv[p] + both[:, :GROUP]
                xpow[p] = both[:, GROUP:]
            else:
                tinv[p] = tinv[p] + _dot(b16(xpow[p]), bd(tinv[p]))

    akv = {p: _dot(b16(a_ak[p]), vbd[p]) for p in pairs}
    pq = {(j, gi): _dot(b16(tinv[j, gi]), jnp.concatenate([bd(at[j][:, gsl(gi)]), bd(akv[j, gi])], axis=1))
          for (j, gi) in pairs}
    r2, y0, m_lr, n_add = {}, {}, {}, {}
    for (j, gi) in pairs:
        gs = gsl(gi)
        pm, qm = pq[j, gi][:, :GROUP], pq[j, gi][:, GROUP:]
        rr = _dot(b16(a_rb[j, gi]), jnp.concatenate([bd(pm), bd(qm)], axis=1))
        r2[j, gi] = b16(rt[j][:, gs] + rr[:, :GROUP])
        y0[j, gi] = rr[:, GROUP:] + _dot(b16(a_rk[j, gi]), vbd[j, gi])
        m_lr[j, gi] = b16(jnp.where(bd_mask, _dot(b16(pm.T), b16(bw[j][:, gs])), 0.0))
        qv = jnp.concatenate([qm, vj[j][:, gs]], axis=0)
        bk = jnp.concatenate([bw[j][:, gs], kw[j][:, gs]], axis=0)
        n_add[j, gi] = jnp.where(bd_mask, _dot(b16(qv.T), b16(bk)), 0.0)

    s_in = {}
    for gi in range(ng):
        s = s_sc[gi]
        for j in range(nsub):
            s16 = b16(s)
            s_in[j, gi] = s16
            s = s * w_last[j][:, gsl(gi)] + _dot(s16, m_lr[j, gi]) + n_add[j, gi]
        s_sc[gi] = s

    ys = []
    for j in range(nsub):
        yg = [_dot(r2[j, gi], s_in[j, gi], nt=True) + y0[j, gi] for gi in range(ng)]
        ys.append(jnp.concatenate(yg, axis=1))
    y = jnp.concatenate(ys, axis=0) if nsub > 1 else ys[0]

    inv_n = 1.0 / HEAD
    yc = y - _head_sum(y) * inv_n
    yn = yc * lax.rsqrt(_head_sum(yc * yc) * inv_n + GN_EPS)
    yn = yn * lng_ref[...] + lnb_ref[...]
    bonus = _head_sum(r * k2 * rk_ref[...]) * v
    y_ref[...] = (yn + bonus) * g
    sout_ref[...] = s_sc[...]


def _rwkv(p_r, prev0, s0_bd, lp, nsub, t_valid):
    b, t, _ = p_r.shape
    tc = nsub * CHUNK
    ng = RWKV_WIDTH // GROUP
    row = lambda a: a.reshape(1, -1)
    zeros = jnp.zeros((DECAY_RANK, RWKV_WIDTH), F32)
    wwa = jnp.concatenate([jnp.concatenate([lp["rwkv_w_up"], zeros], axis=1),
                           jnp.concatenate([zeros, lp["rwkv_a_up"]], axis=1)], axis=0)
    const = lambda shape: pl.BlockSpec(shape, lambda bi, ci: (0,) * len(shape))
    kern = functools.partial(_rwkv_kernel, nsub=nsub, t_valid=t_valid)
    return pl.pallas_call(
        kern,
        grid=(b, t // tc),
        in_specs=[pl.BlockSpec((None, tc, RWKV_PROJ), lambda bi, ci: (bi, ci, 0)),
                  pl.BlockSpec((None, 1, RWKV_PROJ), lambda bi, ci: (bi, 0, 0)),
                  pl.BlockSpec((None, ng, GROUP, GROUP), lambda bi, ci: (bi, 0, 0, 0)),
                  const((1, RWKV_PROJ)), const((1, RWKV_WIDTH)),
                  const((DECAY_RANK + ICLR_RANK, 2 * RWKV_WIDTH)), const((1, RWKV_WIDTH)),
                  const((GATE_RANK, RWKV_WIDTH)), const((1, RWKV_WIDTH)), const((1, RWKV_WIDTH)),
                  const((1, RWKV_WIDTH)), const((1, RWKV_WIDTH)), const((1, RWKV_WIDTH))],
        out_specs=[pl.BlockSpec((None, tc, RWKV_WIDTH), lambda bi, ci: (bi, ci, 0)),
                   pl.BlockSpec((None, ng, GROUP, GROUP), lambda bi, ci: (bi, 0, 0, 0))],
        out_shape=[jax.ShapeDtypeStruct((b, t, RWKV_WIDTH), F32),
                   jax.ShapeDtypeStruct((b, ng, GROUP, GROUP), F32)],
        scratch_shapes=[pltpu.VMEM((1, RWKV_PROJ), F32), pltpu.VMEM((ng, GROUP, GROUP), F32)],
        compiler_params=_cparams("arbitrary", "arbitrary"),
        name="rwkv7",
    )(p_r, prev0, s0_bd, row(lp["rwkv_mu"]), row(lp["rwkv_w0"]), wwa, row(lp["rwkv_a0"]),
      lp["rwkv_g_up"], row(lp["rwkv_k_k"]), row(lp["rwkv_k_a"]), row(lp["rwkv_r_k"]),
      row(lp["rwkv_ln_g"]), row(lp["rwkv_ln_b"]))


def _state_to_bd(s):
    b = s.shape[0]
    hg = GROUP // HEAD
    s5 = s.reshape(b, RWKV_HEADS // hg, hg, HEAD, HEAD)
    out = jnp.einsum("bghvk,hj->bghvjk", s5, jnp.eye(hg, dtype=s.dtype))
    return out.reshape(b, RWKV_HEADS // hg, GROUP, GROUP)


def _state_from_bd(s_bd):
    b = s_bd.shape[0]
    hg = GROUP // HEAD
    s6 = s_bd.reshape(b, RWKV_HEADS // hg, hg, HEAD, hg, HEAD)
    out = jnp.einsum("bghvjk,hj->bghvk", s6, jnp.eye(hg, dtype=s_bd.dtype))
    return out.reshape(b, RWKV_HEADS, HEAD, HEAD)


def _swa_kernel(tab_ref, sink_ref, pa_ref, ck_ref, cv_ref, bkt_ref, qg_ref, kg_ref,
                ya_ref, kn_ref, kall, vall, bias_sc, *, cq, nqc, has_cache):
    tq = cq * nqc
    band = WINDOW + cq
    bi = pl.program_id(0)
    i = pl.program_id(1)
    lane = _iota((1, LANES), 1)
    low = lane < HEAD

    @pl.when(jnp.logical_and(bi == 0, i == 0))
    def _():
        bkt = bkt_ref[...]
        for h in range(ATTN_HEADS):
            def body(j, acc, h=h):
                return acc + jnp.where(bkt == j, tab_ref[j, h], 0.0)
            bh = lax.fori_loop(0, REL_BUCKETS, body, jnp.zeros((cq, band), F32))
            bias_sc[h // 2, (h % 2) * cq:(h % 2 + 1) * cq, :] = bh

    @pl.when(i == 0)
    def _():
        kall[0:WINDOW, :] = ck_ref[...]
        vall[0:WINDOW, :] = cv_ref[...]

    pa = pa_ref[...]
    q = pa[:, :ATTN_WIDTH]
    k = pa[:, ATTN_WIDTH:ATTN_WIDTH + KV_WIDTH]
    v = pa[:, ATTN_WIDTH + KV_WIDTH:]
    inv_n = 1.0 / HEAD
    qn = q * lax.rsqrt(_head_sum(q * q) * inv_n + RMS_EPS) * qg_ref[...]
    kn = k * lax.rsqrt(_head_sum(k * k) * inv_n + RMS_EPS) * kg_ref[...]
    kn_ref[...] = kn
    kall[WINDOW:WINDOW + tq, :] = kn
    vall[WINDOW:WINDOW + tq, :] = v

    row2 = _iota((2 * cq, 1), 0)
    col = _iota((1, band), 1)
    thr = jnp.where(i == 0, WINDOW, 0)
    k_all = kall[...]
    v_all = vall[...]
    k_rot = pltpu.roll(k_all, HEAD, axis=1)
    v_rot = pltpu.roll(v_all, HEAD, axis=1)
    kd = [jnp.where(low, k_all, k_rot).astype(BF16), jnp.where(low, k_rot, k_all).astype(BF16)]
    vd = [jnp.where(low, v_all, v_rot).astype(BF16), jnp.where(low, v_rot, v_all).astype(BF16)]

    work = [(qc, pi) for qc in range(nqc) for pi in range(ATTN_HEADS // 2)]
    kv_of = lambda pi: (2 * pi) // (ATTN_HEADS // KV_HEADS)
    scores = {}
    for (qc, pi) in work:
        qp = qn[qc * cq:(qc + 1) * cq, pi * LANES:(pi + 1) * LANES]
        qs = jnp.concatenate([jnp.where(low, qp, 0.0), jnp.where(low, 0.0, qp)], axis=0)
        s = _dot(qs.astype(BF16), kd[kv_of(pi)][qc * cq:qc * cq + band], nt=True) * ATTN_SCALE + bias_sc[pi]
        if (not has_cache) and qc * cq < WINDOW:
            s = jnp.where((col + qc * cq) < thr, NEG_INF, s)
        scores[qc, pi] = s
    probs = {}
    for (qc, pi) in work:
        s = scores[qc, pi]
        sink = jnp.where(row2 < cq, sink_ref[0, 2 * pi], sink_ref[0, 2 * pi + 1])
        m = jnp.maximum(jnp.max(s, axis=-1, keepdims=True), sink)
        e = jnp.exp(s - m)
        den = jnp.sum(e, axis=-1, keepdims=True) + jnp.exp(sink - m)
        probs[qc, pi] = (e / den).astype(BF16)
    for (qc, pi) in work:
        o = _dot(probs[qc, pi], vd[kv_of(pi)][qc * cq:qc * cq + band])
        ya_ref[qc * cq:(qc + 1) * cq, pi * LANES:(pi + 1) * LANES] = jnp.where(low, o[:cq], o[cq:])

    if tq >= WINDOW:
        knext = kall[tq:tq + WINDOW, :]
        vnext = vall[tq:tq + WINDOW, :]
        kall[0:WINDOW, :] = knext
        vall[0:WINDOW, :] = vnext


def _t5_bucket(rel):
    half = REL_BUCKETS // 2
    max_exact = half // 2
    n = jnp.abs(rel)
    log_ratio = jnp.log(jnp.maximum(n, 1).astype(F32) / max_exact) / math.log(REL_MAX_DIST / max_exact)
    large = jnp.minimum(max_exact + (log_ratio * (half - max_exact)).astype(I32), half - 1)
    return jnp.where(rel > 0, half, 0) + jnp.where(n < max_exact, n, large)


def _swa(p_a, cache_k, cache_v, rel_table, sinks, q_norm, k_norm, cq, nqc, has_cache):
    b, t, _ = p_a.shape
    tq = cq * nqc
    band = WINDOW + cq
    rel = (jnp.arange(band) - WINDOW)[None, :] - jnp.arange(cq)[:, None]
    bkt = _t5_bucket(rel).astype(I32)
    qg = jnp.tile(q_norm, ATTN_HEADS).reshape(1, ATTN_WIDTH)
    kg = jnp.tile(k_norm, KV_HEADS).reshape(1, KV_WIDTH)
    kern = functools.partial(_swa_kernel, cq=cq, nqc=nqc, has_cache=has_cache)
    smem = pl.BlockSpec(memory_space=pltpu.SMEM)
    return pl.pallas_call(
        kern,
        grid=(b, t // tq),
        in_specs=[smem, smem,
                  pl.BlockSpec((None, tq, ATTN_PROJ), lambda bi, i: (bi, i, 0)),
                  pl.BlockSpec((None, WINDOW, KV_WIDTH), lambda bi, i: (bi, 0, 0)),
                  pl.BlockSpec((None, WINDOW, KV_WIDTH), lambda bi, i: (bi, 0, 0)),
                  pl.BlockSpec((cq, band), lambda bi, i: (0, 0)),
                  pl.BlockSpec((1, ATTN_WIDTH), lambda bi, i: (0, 0)),
                  pl.BlockSpec((1, KV_WIDTH), lambda bi, i: (0, 0))],
        out_specs=[pl.BlockSpec((None, tq, ATTN_WIDTH), lambda bi, i: (bi, i, 0)),
                   pl.BlockSpec((None, tq, KV_WIDTH), lambda bi, i: (bi, i, 0))],
        out_shape=[jax.ShapeDtypeStruct((b, t, ATTN_WIDTH), F32),
                   jax.ShapeDtypeStruct((b, t, KV_WIDTH), F32)],
        scratch_shapes=[pltpu.VMEM((WINDOW + tq, KV_WIDTH), F32),
                        pltpu.VMEM((WINDOW + tq, KV_WIDTH), F32),
                        pltpu.VMEM((ATTN_HEADS // 2, 2 * cq, band), F32)],
        compiler_params=_cparams("arbitrary", "arbitrary"),
        name="swa",
    )(rel_table, sinks.reshape(1, ATTN_HEADS), p_a, cache_k, cache_v, bkt, qg, kg)


def _mid_kernel(yr_ref, ya_ref, x_ref, gt1_ref, sh2_ref, sc2_ref, g2_ref, wor_ref, woa_ref, wrt_ref,
                rb_ref, base0_ref, x1_ref, h2_ref, te_ref, gate_ref, pos_ref, cnt_ref, base_sc, *, tm):
    @pl.when(jnp.logical_and(pl.program_id(0) == 0, pl.program_id(1) == 0))
    def _():
        base_sc[...] = base0_ref[...]

    mix = _dot(yr_ref[...].astype(BF16), wor_ref[...]) + _dot(ya_ref[...].astype(BF16), woa_ref[...])
    x1 = x_ref[...] + gt1_ref[...] * mix
    x1_ref[...] = x1
    h2 = _rms(x1, RMS_EPS) * g2_ref[...] * (1.0 + sc2_ref[...]) + sh2_ref[...]
    h2_ref[...] = h2

    logits = _mm(wrt_ref[...], h2, passes=3, nt=True) + rb_ref[...]
    eidx = _iota((N_EXPERTS, tm), 0)
    work = logits
    tops, hots = [], []
    for j in range(TOP_K):
        m = jnp.max(work, axis=0, keepdims=True)
        idx = jnp.min(jnp.where(work == m, eidx, N_EXPERTS), axis=0, keepdims=True)
        hot = eidx == idx
        work = jnp.where(hot, -jnp.inf, work)
        tops.append(m)
        hots.append(hot)
        te_ref[j:j + 1, :] = idx
    exps = [jnp.exp(tj - tops[0]) for tj in tops]
    den = exps[0] + exps[1] + exps[2] + exps[3]
    for j in range(TOP_K):
        gate_ref[j:j + 1, :] = exps[j] / den

    member = (hots[0] | hots[1] | hots[2] | hots[3]).astype(BF16)
    upper = (_iota((tm, tm), 0) < _iota((tm, tm), 1)).astype(BF16)
    before = _dot(member, upper) + base_sc[...]
    for j in range(TOP_K):
        pos_ref[j:j + 1, :] = jnp.sum(jnp.where(hots[j], before, 0.0), axis=0, keepdims=True).astype(I32)
    base_sc[...] = base_sc[...] + jnp.sum(member.astype(F32), axis=1, keepdims=True)
    cnt_ref[...] = base_sc[...]


def _mid(y_r, y_a, x3, gt1, sh2, sc2, gain2, wo_r16, wo_a16, wr_t, rb, base0, tm):
    g, r, d = x3.shape
    n = g * r
    per = r // tm
    mr = gt1.shape[1]
    mrow = (lambda b, i: (b, 0, 0)) if mr == 1 else (lambda b, i: (b, i, 0))
    mblk = 1 if mr == 1 else tm
    tok = lambda w: pl.BlockSpec((None, tm, w), lambda b, i: (b, i, 0))
    mod = pl.BlockSpec((None, mblk, d), mrow)
    const = lambda shape: pl.BlockSpec(shape, lambda b, i: (0,) * len(shape))
    lane_out = pl.BlockSpec((TOP_K, tm), lambda b, i: (0, b * per + i))
    kern = functools.partial(_mid_kernel, tm=tm)
    return pl.pallas_call(
        kern,
        grid=(g, per),
        in_specs=[tok(RWKV_WIDTH), tok(ATTN_WIDTH), tok(d), mod, mod, mod, const((1, d)),
                  const((RWKV_WIDTH, d)), const((ATTN_WIDTH, d)), const((N_EXPERTS, d)),
                  const((N_EXPERTS, 1)), const((N_EXPERTS, 1))],
        out_specs=[tok(d), tok(d), lane_out, lane_out, lane_out, const((N_EXPERTS, 1))],
        out_shape=[jax.ShapeDtypeStruct((g, r, d), F32), jax.ShapeDtypeStruct((g, r, d), F32),
                   jax.ShapeDtypeStruct((TOP_K, n), I32), jax.ShapeDtypeStruct((TOP_K, n), F32),
                   jax.ShapeDtypeStruct((TOP_K, n), I32), jax.ShapeDtypeStruct((N_EXPERTS, 1), F32)],
        scratch_shapes=[pltpu.VMEM((N_EXPERTS, 1), F32)],
        compiler_params=_cparams("arbitrary", "arbitrary"),
        name="mid",
    )(y_r, y_a, x3, gt1, sh2, sc2, gain2.reshape(1, d), wo_r16, wo_a16, wr_t, rb, base0)


def _slot_kernel(pstart_ref, te_ref, pos_ref, o_ref):
    te = te_ref[...]
    acc = pos_ref[...]
    for e in range(N_EXPERTS):
        acc = acc + jnp.where(te == e, pstart_ref[e], 0)
    o_ref[...] = acc


def _slot_rows(te, pos, pstart):
    k, n = te.shape
    return pl.pallas_call(
        _slot_kernel,
        grid=(1,),
        in_specs=[pl.BlockSpec(memory_space=pltpu.SMEM),
                  pl.BlockSpec((k, n), lambda i: (0, 0)), pl.BlockSpec((k, n), lambda i: (0, 0))],
        out_specs=pl.BlockSpec((k, n), lambda i: (0, 0)),
        out_shape=jax.ShapeDtypeStruct((k, n), I32),
        compiler_params=_cparams("arbitrary"),
        name="slot_rows",
    )(pstart, te, pos)


def _dispatch_kernel(dst_ref, pad_ref, h2a_ref, h2b_ref, xs_ref, ztile, sem, zsem, *, tk, steps_a):
    i = pl.program_id(0)

    @pl.when(i == 0)
    def _():
        ztile[...] = jnp.zeros_like(ztile)
        zrow = ztile.at[pl.ds(0, 1)]
        tm = ztile.shape[0]

        def zero_row(row, priority):
            pltpu.make_async_copy(zrow, xs_ref.at[pl.ds(row, 1)], zsem).start(priority=priority)

        def per_expert(e, total):
            first = pad_ref[0, e]
            count = pad_ref[1, e]

            def per_pair(r, c):
                zero_row(first + 2 * r, 0)
                zero_row(first + 2 * r + 1, 1)
                return c
            lax.fori_loop(0, count // 2, per_pair, 0)

            @pl.when(count % 2 == 1)
            def _():
                zero_row(first + count - 1, 0)
            return total + count
        n_pad = lax.fori_loop(0, N_EXPERTS, per_expert, 0)

        blk = 64
        def drain_blk(r, c):
            pltpu.make_async_copy(ztile.at[pl.ds(0, blk)], xs_ref.at[pl.ds(0, blk)], zsem).wait()
            return c
        lax.fori_loop(0, n_pad // blk, drain_blk, 0)

        def drain_row(r, c):
            pltpu.make_async_copy(zrow, xs_ref.at[pl.ds(0, 1)], zsem).wait()
            return c
        lax.fori_loop(0, n_pad % blk, drain_row, 0)

        def spare(t, c):
            row0 = pl.multiple_of(t * tm, tm)
            pltpu.make_async_copy(ztile, xs_ref.at[pl.ds(row0, tm)], zsem).start()
            return c
        lax.fori_loop(pad_ref[2, 0], pad_ref[2, 1], spare, 0)

        def spare_wait(t, c):
            pltpu.make_async_copy(ztile, xs_ref.at[pl.ds(0, tm)], zsem).wait()
            return c
        lax.fori_loop(pad_ref[2, 0], pad_ref[2, 1], spare_wait, 0)

    def scatter(h2_ref):
        for t in range(h2_ref.shape[0] * ROW_UNROLL):
            for j in range(TOP_K):
                dst = dst_ref[j, t]
                pltpu.make_async_copy(h2_ref.at[t // ROW_UNROLL, pl.ds(t % ROW_UNROLL, 1)],
                                      xs_ref.at[pl.ds(dst, 1)], sem).start(priority=j % 2)
        for _ in range(TOP_K):
            pltpu.make_async_copy(h2_ref, h2_ref, sem).wait()

    pl.when(i < steps_a)(functools.partial(scatter, h2a_ref))
    pl.when(i == steps_a)(functools.partial(scatter, h2b_ref))


def _dispatch(dest, pads, h2a, h2b, n_rows, tk):
    na, d = h2a.shape
    nb = h2b.shape[0]
    steps_a = na // tk
    dest = jnp.pad(dest, ((0, 0), (0, (steps_a + 1) * tk - na - nb)))
    h2a = h2a.reshape(na // ROW_UNROLL, ROW_UNROLL, d)
    h2b = h2b.reshape(nb // ROW_UNROLL, ROW_UNROLL, d)
    kern = functools.partial(_dispatch_kernel, tk=tk, steps_a=steps_a)
    return pl.pallas_call(
        kern,
        grid=(steps_a + 1,),
        in_specs=[pl.BlockSpec((TOP_K, tk), lambda i: (0, i), memory_space=pltpu.SMEM),
                  pl.BlockSpec(memory_space=pltpu.SMEM),
                  pl.BlockSpec((tk // ROW_UNROLL, ROW_UNROLL, d), lambda i: (jnp.minimum(i, steps_a - 1), 0, 0)),
                  pl.BlockSpec((nb // ROW_UNROLL, ROW_UNROLL, d), lambda i: (0, 0, 0))],
        out_specs=pl.BlockSpec(memory_space=pl.ANY),
        out_shape=jax.ShapeDtypeStruct((n_rows, d), F32),
        scratch_shapes=[pltpu.VMEM((TM_FFN, d), F32), pltpu.SemaphoreType.DMA(()), pltpu.SemaphoreType.DMA(())],
        compiler_params=_cparams("arbitrary"),
        name="dispatch",
    )(dest, pads, h2a, h2b)


def _ffn_kernel(te_ref, nu_ref, nx_ref, xs_ref, wu_hbm, bu_ref, wd_hbm, bd_ref, ys_ref,
                wu32, wd32, wu16, wd16, sem):
    i = pl.program_id(0)
    n_up = wu32.shape[1]
    active = i < nu_ref[0]
    fresh = jnp.logical_or(i == 0, te_ref[i] != te_ref[jnp.maximum(i - 1, 0)])

    def weight_copies(e):
        return (pltpu.make_async_copy(wu_hbm.at[e], wu32, sem.at[0]),
                pltpu.make_async_copy(wd_hbm.at[e], wd32, sem.at[1]))

    @pl.when(i == 0)
    def _():
        for cp in weight_copies(te_ref[0]):
            cp.start()

    @pl.when(jnp.logical_and(active, fresh))
    def _():
        for cp in weight_copies(te_ref[i]):
            cp.wait()
        src = _iota((GROUP, GROUP), 0)
        dst = _iota((GROUP, GROUP), 1)
        want = jnp.where(dst < LANES, 2 * dst, 2 * (dst - LANES) + 1)
        perm = (src == want).astype(BF16)
        for b in range(n_up // GROUP):
            cs = slice(b * GROUP, (b + 1) * GROUP)
            wu16[:, cs] = _dot(wu32[:, cs].astype(BF16), perm).astype(BF16)
        wd16[...] = wd32[...].astype(BF16)

        @pl.when(nx_ref[i] >= 0)
        def _():
            for cp in weight_copies(nx_ref[i]):
                cp.start()

    @pl.when(active)
    def _():
        u = _dot(xs_ref[...].astype(BF16), wu16[...]) + bu_ref[...]
        acts = []
        for b in range(n_up // GROUP):
            glu = jnp.minimum(u[:, b * GROUP:b * GROUP + LANES], SWIGLU_LIMIT)
            lin = jnp.clip(u[:, b * GROUP + LANES:(b + 1) * GROUP], -SWIGLU_LIMIT, SWIGLU_LIMIT)
            acts.append((glu * jax.nn.sigmoid(SWIGLU_ALPHA * glu) * (lin + 1.0)).astype(BF16))
        act = jnp.concatenate(acts, axis=1)
        ys_ref[...] = _dot(act, wd16[...]) + bd_ref[...]

    @pl.when(jnp.logical_not(active))
    def _():
        ys_ref[...] = jnp.zeros_like(ys_ref)


def _ffn(tile_e, n_used, next_e, xs, wu, bu, wd, bd_, tm):
    rows, d = xs.shape
    n_exp, _, n_up = wu.shape
    dff = wd.shape[1]
    row_blk = lambda i, te, nu, nx: (jnp.minimum(i, nu[0] - 1), 0)
    exp_blk = lambda i, te, nu, nx: (te[i], 0, 0)
    grid_spec = pltpu.PrefetchScalarGridSpec(
        num_scalar_prefetch=3,
        grid=(rows // tm,),
        in_specs=[pl.BlockSpec((tm, d), row_blk),
                  pl.BlockSpec(memory_space=pl.ANY),
                  pl.BlockSpec((None, 1, n_up), exp_blk),
                  pl.BlockSpec(memory_space=pl.ANY),
                  pl.BlockSpec((None, 1, d), exp_blk)],
        out_specs=pl.BlockSpec((tm, d), lambda i, te, nu, nx: (i, 0)),
        scratch_shapes=[pltpu.VMEM((d, n_up), F32), pltpu.VMEM((dff, d), F32),
                        pltpu.VMEM((d, n_up), BF16), pltpu.VMEM((dff, d), BF16),
                        pltpu.SemaphoreType.DMA((2,))],
    )
    return pl.pallas_call(
        _ffn_kernel,
        grid_spec=grid_spec,
        out_shape=jax.ShapeDtypeStruct((rows, d), F32),
        compiler_params=_cparams("arbitrary"),
        name="expert_ffn",
    )(tile_e, n_used, next_e, xs, wu, bu.reshape(n_exp, 1, n_up), wd, bd_.reshape(n_exp, 1, d))


def _combine_kernel(src_ref, srcn_ref, gate_ref, x1_ref, gt2_ref, ys_ref, o_ref, buf, sem, *, tk):
    i = pl.program_id(1)
    n = pl.num_programs(1)
    flat = pl.program_id(0) * n + i
    total = pl.num_programs(0) * n
    slot = lax.rem(flat, 2)

    def issue(rows_ref, s, t0, t1):
        for t in range(t0, t1):
            for j in range(TOP_K):
                src = rows_ref[j, t]
                pltpu.make_async_copy(ys_ref.at[pl.ds(src, 1)],
                                      buf.at[s, j, t // ROW_UNROLL, pl.ds(t % ROW_UNROLL, 1)],
                                      sem.at[s]).start(priority=j % 2)

    def wait_slot(s):
        for j in range(TOP_K):
            pltpu.make_async_copy(buf.at[s, j], buf.at[s, j], sem.at[s]).wait()

    @pl.when(flat == 0)
    def _():
        issue(src_ref, 0, 0, tk)

    def step(cur):
        wait_slot(cur)
        eye = (_iota((tk, tk), 0) == _iota((tk, tk), 1)).astype(BF16)
        g_hi, g_lo = _split(gate_ref[...])
        gate = _dot(eye, g_hi, nt=True) + _dot(eye, g_lo, nt=True)
        for g in range(tk // ROW_UNROLL):
            rs = slice(g * ROW_UNROLL, (g + 1) * ROW_UNROLL)
            acc = gate[rs, 0:1] * buf[cur, 0, g]
            for j in range(1, TOP_K):
                acc = acc + gate[rs, j:j + 1] * buf[cur, j, g]
            gt2 = gt2_ref[...] if gt2_ref.shape[0] == 1 else gt2_ref[rs, :]
            o_ref[rs, :] = x1_ref[rs, :] + gt2 * acc
            issue(srcn_ref, 1 - cur, g * ROW_UNROLL, (g + 1) * ROW_UNROLL)

        @pl.when(flat + 1 == total)
        def _():
            wait_slot(1 - cur)

    for cur in range(2):
        pl.when(slot == cur)(functools.partial(step, cur))


def _combine(dest, gates, x1, gt2, ys, tk):
    g, r, d = x1.shape
    per = r // tk
    last = g * per - 1
    mr = gt2.shape[1]
    mrow = (lambda b, i: (b, 0, 0)) if mr == 1 else (lambda b, i: (b, i, 0))
    mblk = 1 if mr == 1 else tk
    cur = pl.BlockSpec((TOP_K, tk), lambda b, i: (0, b * per + i), memory_space=pltpu.SMEM)
    nxt = pl.BlockSpec((TOP_K, tk), lambda b, i: (0, jnp.minimum(b * per + i + 1, last)),
                       memory_space=pltpu.SMEM)
    kern = functools.partial(_combine_kernel, tk=tk)
    return pl.pallas_call(
        kern,
        grid=(g, per),
        in_specs=[cur, nxt,
                  pl.BlockSpec((TOP_K, tk), lambda b, i: (0, b * per + i)),
                  pl.BlockSpec((None, tk, d), lambda b, i: (b, i, 0)),
                  pl.BlockSpec((None, mblk, d), mrow),
                  pl.BlockSpec(memory_space=pl.ANY)],
        out_specs=pl.BlockSpec((None, tk, d), lambda b, i: (b, i, 0)),
        out_shape=jax.ShapeDtypeStruct((g, r, d), F32),
        scratch_shapes=[pltpu.VMEM((2, TOP_K, tk // ROW_UNROLL, ROW_UNROLL, d), F32),
                        pltpu.SemaphoreType.DMA((2,))],
        compiler_params=_cparams("arbitrary", "arbitrary"),
        name="combine",
    )(dest, dest, gates, x1, gt2, ys)


def _tile_plan(counts, tm, n_tiles):
    cnt = counts.reshape(-1).astype(I32)
    tiles = (cnt + tm - 1) // tm
    tile_end = jnp.cumsum(tiles)
    pstart = ((tile_end - tiles) * tm).astype(I32)
    n_used = tile_end[-1:].astype(I32)
    idx = jnp.minimum(jnp.arange(n_tiles, dtype=I32), n_used[0] - 1)
    tile_e = jnp.sum((tile_end[None, :] <= idx[:, None]).astype(I32), axis=1)
    spare = jnp.zeros((N_EXPERTS,), I32).at[0].set(n_used[0]).at[1].set(n_tiles)
    pads = jnp.stack([pstart + cnt, tiles * tm - cnt, spare]).astype(I32)
    tile_e = jnp.minimum(tile_e, N_EXPERTS - 1)
    end_of = tile_end[tile_e]
    next_e = jnp.where(end_of < n_used[0], tile_e[jnp.minimum(end_of, n_tiles - 1)], -1).astype(I32)
    return pstart, tile_e, n_used, next_e, pads


def kernel(x_prompt, x_sample, state_shift, state_wkv, cache_win_k, cache_win_v, c_prompt, c_sample, rel_bias, norm1_g, norm2_g, w_ada, b_ada, w_in, w_out, rwkv_mu, rwkv_w0, rwkv_w_up, rwkv_a0, rwkv_a_up, rwkv_g_up, rwkv_k_k, rwkv_k_a, rwkv_r_k, rwkv_ln_g, rwkv_ln_b, q_norm, k_norm, sinks, router_w, router_b, exp_w_up, exp_b_up, exp_w_down, exp_b_down):
    bp, tp, d = x_prompt.shape
    bs, ts, _ = x_sample.shape
    depth = w_in.shape[0]
    assert depth == 1, "single-layer trunk"
    layer = 0
    lp = dict(rwkv_mu=rwkv_mu[layer], rwkv_w0=rwkv_w0[layer], rwkv_w_up=rwkv_w_up[layer],
              rwkv_a0=rwkv_a0[layer], rwkv_a_up=rwkv_a_up[layer], rwkv_g_up=rwkv_g_up[layer],
              rwkv_k_k=rwkv_k_k[layer], rwkv_k_a=rwkv_k_a[layer], rwkv_r_k=rwkv_r_k[layer].reshape(-1),
              rwkv_ln_g=rwkv_ln_g[layer], rwkv_ln_b=rwkv_ln_b[layer])

    rows = bp + bs
    pad = (-rows) % 8
    c_all = jnp.concatenate([c_prompt, c_sample, jnp.zeros((pad, d), F32)], axis=0)
    mod = _modulation(c_all, w_ada[layer], b_ada[layer])
    sh1, sc1, gt1, sh2, sc2, gt2 = [mod[:, i * d:(i + 1) * d] for i in range(6)]
    pm = lambda m: m[:bp].reshape(bp, 1, d)
    sm = lambda m: jnp.repeat(m[bp:bp + bs], ts, axis=0).reshape(1, bs * ts, d)

    w_in16 = w_in[layer].astype(BF16)
    wo16 = w_out[layer].astype(BF16)
    wo_r16, wo_a16 = wo16[:RWKV_WIDTH], wo16[RWKV_WIDTH:]
    wr_t = router_w[layer].T
    rb = router_b[layer].reshape(N_EXPERTS, 1)
    wu = exp_w_up.reshape(exp_w_up.shape[1:])
    wd = exp_w_down.reshape(exp_w_down.shape[1:])
    n_up = wu.shape[-1]
    bu = exp_b_up[layer].reshape(N_EXPERTS, n_up // GROUP, LANES, 2).transpose(0, 1, 3, 2).reshape(N_EXPERTS, n_up)

    pr_p, pa_p = _inproj(x_prompt, norm1_g[layer], pm(sh1), pm(sc1), w_in16, TM_PROJ)
    zero_prev = jnp.zeros((bp, 1, RWKV_PROJ), F32)
    zero_state = jnp.zeros((bp, RWKV_WIDTH // GROUP, GROUP, GROUP), F32)
    yr_p, s_p = _rwkv(pr_p, zero_prev, zero_state, lp, RWKV_SUB, None)
    zero_kv = jnp.zeros((bp, WINDOW, KV_WIDTH), F32)
    ya_p, kn_p = _swa(pa_p, zero_kv, zero_kv, rel_bias, sinks[layer], q_norm[layer], k_norm[layer],
                      CHUNK, SWA_QC, False)

    ns = bs * ts
    xs_flat = x_sample.reshape(1, ns, d)
    pr_s, pa_s = _inproj(xs_flat, norm1_g[layer], sm(sh1), sm(sc1), w_in16, ns)
    pr_s = pr_s.reshape(bs, ts, RWKV_PROJ)
    pa_s = pa_s.reshape(bs, ts, ATTN_PROJ)
    t_pad = -(-ts // CHUNK) * CHUNK
    pr_s_pad = jnp.pad(pr_s, ((0, 0), (0, t_pad - ts), (0, 0)))
    yr_s, s_s = _rwkv(pr_s_pad, state_shift[layer].reshape(bs, 1, RWKV_PROJ), _state_to_bd(state_wkv[layer]),
                      lp, t_pad // CHUNK, ts)
    yr_s = yr_s[:, :ts]
    ya_s, kn_s = _swa(pa_s, cache_win_k[layer].reshape(bs, WINDOW, KV_WIDTH),
                      cache_win_v[layer].reshape(bs, WINDOW, KV_WIDTH), rel_bias, sinks[layer],
                      q_norm[layer], k_norm[layer], ts, 1, True)

    base0 = jnp.zeros((N_EXPERTS, 1), F32)
    x1_p, h2_p, te_p, gate_p, pos_p, cnt_p = _mid(yr_p, ya_p, x_prompt, pm(gt1), pm(sh2), pm(sc2), norm2_g[layer],
                                                  wo_r16, wo_a16, wr_t, rb, base0, TM_PROJ)
    x1_s, h2_s, te_s, gate_s, pos_s, cnt = _mid(yr_s.reshape(1, ns, RWKV_WIDTH), ya_s.reshape(1, ns, ATTN_WIDTH),
                                                xs_flat, sm(gt1), sm(sh2), sm(sc2), norm2_g[layer],
                                                wo_r16, wo_a16, wr_t, rb, cnt_p, ns)

    n_p = bp * tp
    n_rows = (n_p + ns) * TOP_K + N_EXPERTS * (TM_FFN - 1)
    n_tiles = -(-n_rows // TM_FFN)
    pstart, tile_e, n_used, next_e, pads = _tile_plan(cnt, TM_FFN, n_tiles)
    dest = _slot_rows(jnp.concatenate([te_p, te_s], axis=1), jnp.concatenate([pos_p, pos_s], axis=1), pstart)
    dest_p, dest_s = dest[:, :n_p], dest[:, n_p:]
    xs_buf = _dispatch(dest, pads, h2_p.reshape(n_p, d), h2_s.reshape(ns, d), n_tiles * TM_FFN, TK_DISPATCH)
    ys_buf = _ffn(tile_e, n_used, next_e, xs_buf, wu, bu, wd, exp_b_down[layer], TM_FFN)
    y_p = _combine(dest_p, gate_p, x1_p, pm(gt2), ys_buf, TK_COMBINE)
    y_s = _combine(dest_s, gate_s, x1_s, sm(gt2), ys_buf, TK_COMBINE)

    kv4 = lambda z, rows_: z.reshape(z.shape[0], rows_, KV_HEADS, HEAD)[None]
    v_p = pa_p[:, tp - WINDOW:, ATTN_WIDTH + KV_WIDTH:]
    v_s = pa_s[:, :, ATTN_WIDTH + KV_WIDTH:]
    return (y_p, y_s.reshape(bs, ts, d),
            pr_p[:, tp - 1][None], _state_from_bd(s_p)[None], kv4(kn_p[:, tp - WINDOW:], WINDOW), kv4(v_p, WINDOW),
            pr_s[:, ts - 1][None], _state_from_bd(s_s)[None], kv4(kn_s, ts), kv4(v_s, ts))
```

```python
import functools
import math

import jax
import jax.numpy as jnp
from jax import lax
from jax.experimental import pallas as pl
from jax.experimental.pallas import tpu as pltpu

F32 = jnp.float32
BF16 = jnp.bfloat16
I32 = jnp.int32

HEAD = 64
RWKV_HEADS = 8
RWKV_WIDTH = RWKV_HEADS * HEAD
DECAY_RANK = 64
ICLR_RANK = 64
GATE_RANK = 128
RWKV_PROJ = 3 * RWKV_WIDTH + DECAY_RANK + ICLR_RANK + GATE_RANK
ATTN_HEADS = 8
KV_HEADS = 2
ATTN_WIDTH = ATTN_HEADS * HEAD
KV_WIDTH = KV_HEADS * HEAD
ATTN_PROJ = ATTN_WIDTH + 2 * KV_WIDTH
WINDOW = 128
CHUNK = 64
REL_BUCKETS = 32
REL_MAX_DIST = 128
N_EXPERTS = 32
TOP_K = 4
GN_EPS = 64e-5
RMS_EPS = 1e-6
NEG_INF = -1e30
ATTN_SCALE = HEAD ** -0.5
SWIGLU_ALPHA = 1.702
SWIGLU_LIMIT = 7.0

LANES = 128
GROUP = 256
VMEM_LIMIT = 56 * 1024 * 1024

TM_PROJ = 512
RWKV_SUB = 4
SWA_QC = 4
TM_FFN = 512
TK_DISPATCH = 512
TK_COMBINE = 256
ROW_UNROLL = 8


def _cparams(*sem):
    return pltpu.CompilerParams(dimension_semantics=sem, vmem_limit_bytes=VMEM_LIMIT)


def _dot(a, b, nt=False):
    dims = (((1,), (1,)), ((), ())) if nt else (((1,), (0,)), ((), ()))
    return lax.dot_general(a, b, dims, preferred_element_type=F32)


def _split(x):
    hi = x.astype(BF16)
    lo = (x - hi.astype(F32)).astype(BF16)
    return hi, lo


def _mm(a, b, passes=1, nt=False):
    if passes == 1:
        return _dot(a.astype(BF16), b.astype(BF16), nt)
    ah, al = _split(a)
    bh, bl = _split(b)
    return _dot(ah, bh, nt) + _dot(al, bh, nt) + _dot(ah, bl, nt)


def _iota(shape, dim):
    return lax.broadcasted_iota(I32, shape, dim)


def _head_ones(n):
    r = lax.shift_right_logical(_iota((n, n), 0), 6)
    c = lax.shift_right_logical(_iota((n, n), 1), 6)
    return (r == c).astype(BF16)


def _head_sum(x):
    width = x.shape[1]
    x16 = x.astype(BF16)
    if width <= GROUP:
        return _dot(x16, _head_ones(width))
    ones = _head_ones(GROUP)
    parts = [_dot(x16[:, g:g + GROUP], ones) for g in range(0, width, GROUP)]
    return jnp.concatenate(parts, axis=1)


def _rms(x, eps):
    return x * lax.rsqrt(jnp.mean(x * x, axis=-1, keepdims=True) + eps)


def _mod_kernel(c_ref, w_ref, b_ref, o_ref):
    c = c_ref[...]
    s = c * jax.nn.sigmoid(c)
    o_ref[...] = _mm(s, w_ref[...], passes=3) + b_ref[...]


def _modulation(c_all, w_ada, b_ada):
    rows, d = c_all.shape
    n = w_ada.shape[1]
    tn = 1536
    return pl.pallas_call(
        _mod_kernel,
        grid=(n // tn,),
        in_specs=[pl.BlockSpec((rows, d), lambda j: (0, 0)),
                  pl.BlockSpec((d, tn), lambda j: (0, j)),
                  pl.BlockSpec((1, tn), lambda j: (0, j))],
        out_specs=pl.BlockSpec((rows, tn), lambda j: (0, j)),
        out_shape=jax.ShapeDtypeStruct((rows, n), F32),
        compiler_params=_cparams("arbitrary"),
        name="modulation",
    )(c_all, w_ada, b_ada.reshape(1, n))


def _inproj_kernel(x_ref, g_ref, sh_ref, sc_ref, w_ref, pr_ref, pa_ref):
    x = x_ref[...]
    h = _rms(x, RMS_EPS) * g_ref[...] * (1.0 + sc_ref[...]) + sh_ref[...]
    p = _dot(h.astype(BF16), w_ref[...])
    pr_ref[...] = p[:, :RWKV_PROJ]
    pa_ref[...] = p[:, RWKV_PROJ:]


def _inproj(x3, gain, sh3, sc3, w_in16, tm):
    g, r, d = x3.shape
    mr = sh3.shape[1]
    mrow = (lambda b, i: (b, 0, 0)) if mr == 1 else (lambda b, i: (b, i, 0))
    mblk = 1 if mr == 1 else tm
    return pl.pallas_call(
        _inproj_kernel,
        grid=(g, r // tm),
        in_specs=[pl.BlockSpec((None, tm, d), lambda b, i: (b, i, 0)),
                  pl.BlockSpec((1, d), lambda b, i: (0, 0)),
                  pl.BlockSpec((None, mblk, d), mrow),
                  pl.BlockSpec((None, mblk, d), mrow),
                  pl.BlockSpec(w_in16.shape, lambda b, i: (0, 0))],
        out_specs=[pl.BlockSpec((None, tm, RWKV_PROJ), lambda b, i: (b, i, 0)),
                   pl.BlockSpec((None, tm, ATTN_PROJ), lambda b, i: (b, i, 0))],
        out_shape=[jax.ShapeDtypeStruct((g, r, RWKV_PROJ), F32),
                   jax.ShapeDtypeStruct((g, r, ATTN_PROJ), F32)],
        compiler_params=_cparams("arbitrary", "arbitrary"),
        name="inproj",
    )(x3, gain.reshape(1, d), sh3, sc3, w_in16)


def _softplus(x):
    return jnp.maximum(x, 0.0) + jnp.log(1.0 + jnp.exp(-jnp.abs(x)))


def _rwkv_kernel(p_ref, prev0_ref, s0_ref, mu_ref, w0_ref, wwa_ref, a0_ref, gup_ref, kk_ref, ka_ref,
                 rk_ref, lng_ref, lnb_ref, y_ref, sout_ref, prev_sc, s_sc, *, nsub, t_valid):
    tc = nsub * CHUNK
    c = pl.program_id(1)

    @pl.when(c == 0)
    def _():
        prev_sc[...] = prev0_ref[...]
        s_sc[...] = s0_ref[...]

    p = p_ref[...]
    row = _iota((tc, 1), 0)
    p_prev = jnp.where(row == 0, prev_sc[...], pltpu.roll(p, 1, axis=0))
    prev_sc[...] = p_ref[tc - 1:tc, :]
    xp = p + (p_prev - p) * mu_ref[...]

    w3 = 3 * RWKV_WIDTH
    r = xp[:, 0:RWKV_WIDTH]
    k = xp[:, RWKV_WIDTH:2 * RWKV_WIDTH]
    v = xp[:, 2 * RWKV_WIDTH:w3]
    z = xp[:, w3:w3 + DECAY_RANK + ICLR_RANK]
    gd = xp[:, w3 + DECAY_RANK + ICLR_RANK:]
    zl = _iota((1, DECAY_RANK + ICLR_RANK), 1)
    zz = jnp.where(zl < DECAY_RANK, jnp.tanh(z), z)
    wa = _mm(zz, wwa_ref[...], passes=3)
    w_log = -_softplus(-(w0_ref[...] + wa[:, :RWKV_WIDTH])) - 0.5
    lw = -jnp.exp(w_log)
    a = jax.nn.sigmoid(a0_ref[...] + wa[:, RWKV_WIDTH:])
    g = _mm(jax.nn.sigmoid(gd), gup_ref[...])
    kk = k * kk_ref[...]
    kk = kk * lax.rsqrt(jnp.maximum(_head_sum(kk * kk), 1e-24))
    k2 = k * (1.0 + (a - 1.0) * ka_ref[...])
    bvec = kk * a
    if t_valid is not None:
        ok = (c * tc + row) < t_valid
        lw = jnp.where(ok, lw, 0.0)
        kk = jnp.where(ok, kk, 0.0)
        bvec = jnp.where(ok, bvec, 0.0)
        k2u = jnp.where(ok, k2, 0.0)
        vu = jnp.where(ok, v, 0.0)
    else:
        k2u, vu = k2, v

    tril16 = (_iota((CHUNK, CHUNK), 0) >= _iota((CHUNK, CHUNK), 1)).astype(BF16)
    bd_mask = (lax.shift_right_logical(_iota((GROUP, GROUP), 0), 6)
               == lax.shift_right_logical(_iota((GROUP, GROUP), 1), 6))
    t_idx = _iota((CHUNK, GROUP), 0)
    s_idx = jnp.bitwise_and(_iota((CHUNK, GROUP), 1), CHUNK - 1)
    strict = t_idx > s_idx
    incl = t_idx >= s_idx
    eye = (t_idx == s_idx).astype(F32)

    bd16 = bd_mask.astype(BF16)

    def bd(x):
        x16 = x.astype(BF16)
        return jnp.concatenate([x16, x16, x16, x16], axis=0) * bd16

    def b16(x):
        return x.astype(BF16)

    ng = RWKV_WIDTH // GROUP
    pairs = [(j, gi) for j in range(nsub) for gi in range(ng)]
    gsl = lambda gi: slice(gi * GROUP, (gi + 1) * GROUP)

    at, bt, kt, rt, bw, kw, vj, w_last = [], [], [], [], [], [], [], []
    for j in range(nsub):
        sl = slice(j * CHUNK, (j + 1) * CHUNK)
        lwj = lw[sl]
        h1 = lwj.astype(BF16)
        r1 = lwj - h1.astype(F32)
        h2 = r1.astype(BF16)
        h3 = (r1 - h2.astype(F32)).astype(BF16)
        cum = _dot(tril16, h1) + _dot(tril16, h2) + _dot(tril16, h3)
        e_cum = jnp.exp(cum)
        e_inv = jnp.exp(-cum)
        wl = e_cum[CHUNK - 1:CHUNK, :]
        at.append(-kk[sl] * jnp.exp(cum - lwj))
        bt.append(bvec[sl] * e_inv)
        kt.append(k2u[sl] * e_inv)
        rt.append(r[sl] * e_cum)
        bw.append(bt[j] * wl)
        kw.append(kt[j] * wl)
        vj.append(vu[sl])
        w_last.append(wl)

    a_ab, a_ak, a_rb, a_rk, vbd = {}, {}, {}, {}, {}
    for (j, gi) in pairs:
        gs = gsl(gi)
        q2 = b16(jnp.concatenate([at[j][:, gs], rt[j][:, gs]], axis=0))
        sc = _dot(q2, jnp.concatenate([bd(bt[j][:, gs]), bd(kt[j][:, gs])], axis=0), nt=True)
        a_ab[j, gi] = jnp.where(strict, sc[:CHUNK, :GROUP], 0.0)
        a_ak[j, gi] = jnp.where(strict, sc[:CHUNK, GROUP:], 0.0)
        a_rb[j, gi] = jnp.where(incl, sc[CHUNK:, :GROUP], 0.0)
        a_rk[j, gi] = jnp.where(incl, sc[CHUNK:, GROUP:], 0.0)
        vbd[j, gi] = bd(vj[j][:, gs])

    tinv = {p: eye + a_ab[p] for p in pairs}
    xpow = {p: _dot(b16(a_ab[p]), bd(a_ab[p])) for p in pairs}
    for step in range(1, 6):
        for p in pairs:
            if step < 5:
                both = _dot(b16(xpow[p]), jnp.concatenate([bd(tinv[p]), bd(xpow[p])], axis=1))
                tinv[p] = tinv[p] + both[:, :GROUP]
                xpow[p] = both[:, GROUP:]
            else:
                tinv[p] = tinv[p] + _dot(b16(xpow[p]), bd(tinv[p]))

    akv = {p: _dot(b16(a_ak[p]), vbd[p]) for p in pairs}
    pq = {(j, gi): _dot(b16(tinv[j, gi]), jnp.concatenate([bd(at[j][:, gsl(gi)]), bd(akv[j, gi])], axis=1))
          for (j, gi) in pairs}
    r2, y0, m_lr, n_add = {}, {}, {}, {}
    for (j, gi) in pairs:
        gs = gsl(gi)
        pm, qm = pq[j, gi][:, :GROUP], pq[j, gi][:, GROUP:]
        rr = _dot(b16(a_rb[j, gi]), jnp.concatenate([bd(pm), bd(qm)], axis=1))
        r2[j, gi] = b16(rt[j][:, gs] + rr[:, :GROUP])
        y0[j, gi] = rr[:, GROUP:] + _dot(b16(a_rk[j, gi]), vbd[j, gi])
        m_lr[j, gi] = b16(jnp.where(bd_mask, _dot(b16(pm.T), b16(bw[j][:, gs])), 0.0))
        qv = jnp.concatenate([qm, vj[j][:, gs]], axis=0)
        bk = jnp.concatenate([bw[j][:, gs], kw[j][:, gs]], axis=0)
        n_add[j, gi] = jnp.where(bd_mask, _dot(b16(qv.T), b16(bk)), 0.0)

    s_in = {}
    for gi in range(ng):
        s = s_sc[gi]
        for j in range(nsub):
            s16 = b16(s)
            s_in[j, gi] = s16
            s = s * w_last[j][:, gsl(gi)] + _dot(s16, m_lr[j, gi]) + n_add[j, gi]
        s_sc[gi] = s

    ys = []
    for j in range(nsub):
        yg = [_dot(r2[j, gi], s_in[j, gi], nt=True) + y0[j, gi] for gi in range(ng)]
        ys.append(jnp.concatenate(yg, axis=1))
    y = jnp.concatenate(ys, axis=0) if nsub > 1 else ys[0]

    inv_n = 1.0 / HEAD
    yc = y - _head_sum(y) * inv_n
    yn = yc * lax.rsqrt(_head_sum(yc * yc) * inv_n + GN_EPS)
    yn = yn * lng_ref[...] + lnb_ref[...]
    bonus = _head_sum(r * k2 * rk_ref[...]) * v
    y_ref[...] = (yn + bonus) * g
    sout_ref[...] = s_sc[...]


def _rwkv(p_r, prev0, s0_bd, lp, nsub, t_valid):
    b, t, _ = p_r.shape
    tc = nsub * CHUNK
    ng = RWKV_WIDTH // GROUP
    row = lambda a: a.reshape(1, -1)
    zeros = jnp.zeros((DECAY_RANK, RWKV_WIDTH), F32)
    wwa = jnp.concatenate([jnp.concatenate([lp["rwkv_w_up"], zeros], axis=1),
                           jnp.concatenate([zeros, lp["rwkv_a_up"]], axis=1)], axis=0)
    const = lambda shape: pl.BlockSpec(shape, lambda bi, ci: (0,) * len(shape))
    kern = functools.partial(_rwkv_kernel, nsub=nsub, t_valid=t_valid)
    return pl.pallas_call(
        kern,
        grid=(b, t // tc),
        in_specs=[pl.BlockSpec((None, tc, RWKV_PROJ), lambda bi, ci: (bi, ci, 0)),
                  pl.BlockSpec((None, 1, RWKV_PROJ), lambda bi, ci: (bi, 0, 0)),
                  pl.BlockSpec((None, ng, GROUP, GROUP), lambda bi, ci: (bi, 0, 0, 0)),
                  const((1, RWKV_PROJ)), const((1, RWKV_WIDTH)),
                  const((DECAY_RANK + ICLR_RANK, 2 * RWKV_WIDTH)), const((1, RWKV_WIDTH)),
                  const((GATE_RANK, RWKV_WIDTH)), const((1, RWKV_WIDTH)), const((1, RWKV_WIDTH)),
                  const((1, RWKV_WIDTH)), const((1, RWKV_WIDTH)), const((1, RWKV_WIDTH))],
        out_specs=[pl.BlockSpec((None, tc, RWKV_WIDTH), lambda bi, ci: (bi, ci, 0)),
                   pl.BlockSpec((None, ng, GROUP, GROUP), lambda bi, ci: (bi, 0, 0, 0))],
        out_shape=[jax.ShapeDtypeStruct((b, t, RWKV_WIDTH), F32),
                   jax.ShapeDtypeStruct((b, ng, GROUP, GROUP), F32)],
        scratch_shapes=[pltpu.VMEM((1, RWKV_PROJ), F32), pltpu.VMEM((ng, GROUP, GROUP), F32)],
        compiler_params=_cparams("arbitrary", "arbitrary"),
        name="rwkv7",
    )(p_r, prev0, s0_bd, row(lp["rwkv_mu"]), row(lp["rwkv_w0"]), wwa, row(lp["rwkv_a0"]),
      lp["rwkv_g_up"], row(lp["rwkv_k_k"]), row(lp["rwkv_k_a"]), row(lp["rwkv_r_k"]),
      row(lp["rwkv_ln_g"]), row(lp["rwkv_ln_b"]))


def _state_to_bd(s):
    b = s.shape[0]
    hg = GROUP // HEAD
    s5 = s.reshape(b, RWKV_HEADS // hg, hg, HEAD, HEAD)
    out = jnp.einsum("bghvk,hj->bghvjk", s5, jnp.eye(hg, dtype=s.dtype))
    return out.reshape(b, RWKV_HEADS // hg, GROUP, GROUP)


def _state_from_bd(s_bd):
    b = s_bd.shape[0]
    hg = GROUP // HEAD
    s6 = s_bd.reshape(b, RWKV_HEADS // hg, hg, HEAD, hg, HEAD)
    out = jnp.einsum("bghvjk,hj->bghvk", s6, jnp.eye(hg, dtype=s_bd.dtype))
    return out.reshape(b, RWKV_HEADS, HEAD, HEAD)


def _swa_kernel(tab_ref, sink_ref, pa_ref, ck_ref, cv_ref, bkt_ref, qg_ref, kg_ref,
                ya_ref, kn_ref, kall, vall, bias_sc, *, cq, nqc, has_cache):
    tq = cq * nqc
    band = WINDOW + cq
    bi = pl.program_id(0)
    i = pl.program_id(1)
    lane = _iota((1, LANES), 1)
    low = lane < HEAD

    @pl.when(jnp.logical_and(bi == 0, i == 0))
    def _():
        bkt = bkt_ref[...]
        for h in range(ATTN_HEADS):
            def body(j, acc, h=h):
                return acc + jnp.where(bkt == j, tab_ref[j, h], 0.0)
            bh = lax.fori_loop(0, REL_BUCKETS, body, jnp.zeros((cq, band), F32))
            bias_sc[h // 2, (h % 2) * cq:(h % 2 + 1) * cq, :] = bh

    @pl.when(i == 0)
    def _():
        kall[0:WINDOW, :] = ck_ref[...]
        vall[0:WINDOW, :] = cv_ref[...]

    pa = pa_ref[...]
    q = pa[:, :ATTN_WIDTH]
    k = pa[:, ATTN_WIDTH:ATTN_WIDTH + KV_WIDTH]
    v = pa[:, ATTN_WIDTH + KV_WIDTH:]
    inv_n = 1.0 / HEAD
    qn = q * lax.rsqrt(_head_sum(q * q) * inv_n + RMS_EPS) * qg_ref[...]
    kn = k * lax.rsqrt(_head_sum(k * k) * inv_n + RMS_EPS) * kg_ref[...]
    kn_ref[...] = kn
    kall[WINDOW:WINDOW + tq, :] = kn
    vall[WINDOW:WINDOW + tq, :] = v

    row2 = _iota((2 * cq, 1), 0)
    col = _iota((1, band), 1)
    thr = jnp.where(i == 0, WINDOW, 0)
    k_all = kall[...]
    v_all = vall[...]
    k_rot = pltpu.roll(k_all, HEAD, axis=1)
    v_rot = pltpu.roll(v_all, HEAD, axis=1)
    kd = [jnp.where(low, k_all, k_rot).astype(BF16), jnp.where(low, k_rot, k_all).astype(BF16)]
    vd = [jnp.where(low, v_all, v_rot).astype(BF16), jnp.where(low, v_rot, v_all).astype(BF16)]

    work = [(qc, pi) for qc in range(nqc) for pi in range(ATTN_HEADS // 2)]
    kv_of = lambda pi: (2 * pi) // (ATTN_HEADS // KV_HEADS)
    scores = {}
    for (qc, pi) in work:
        qp = qn[qc * cq:(qc + 1) * cq, pi * LANES:(pi + 1) * LANES]
        qs = jnp.concatenate([jnp.where(low, qp, 0.0), jnp.where(low, 0.0, qp)], axis=0)
        s = _dot(qs.astype(BF16), kd[kv_of(pi)][qc * cq:qc * cq + band], nt=True) * ATTN_SCALE + bias_sc[pi]
        if (not has_cache) and qc * cq < WINDOW:
            s = jnp.where((col + qc * cq) < thr, NEG_INF, s)
        scores[qc, pi] = s
    probs = {}
    for (qc, pi) in work:
        s = scores[qc, pi]
        sink = jnp.where(row2 < cq, sink_ref[0, 2 * pi], sink_ref[0, 2 * pi + 1])
        m = jnp.maximum(jnp.max(s, axis=-1, keepdims=True), sink)
        e = jnp.exp(s - m)
        den = jnp.sum(e, axis=-1, keepdims=True) + jnp.exp(sink - m)
        probs[qc, pi] = (e / den).astype(BF16)
    for (qc, pi) in work:
        o = _dot(probs[qc, pi], vd[kv_of(pi)][qc * cq:qc * cq + band])
        ya_ref[qc * cq:(qc + 1) * cq, pi * LANES:(pi + 1) * LANES] = jnp.where(low, o[:cq], o[cq:])

    if tq >= WINDOW:
        knext = kall[tq:tq + WINDOW, :]
        vnext = vall[tq:tq + WINDOW, :]
        kall[0:WINDOW, :] = knext
        vall[0:WINDOW, :] = vnext


def _t5_bucket(rel):
    half = REL_BUCKETS // 2
    max_exact = half // 2
    n = jnp.abs(rel)
    log_ratio = jnp.log(jnp.maximum(n, 1).astype(F32) / max_exact) / math.log(REL_MAX_DIST / max_exact)
    large = jnp.minimum(max_exact + (log_ratio * (half - max_exact)).astype(I32), half - 1)
    return jnp.where(rel > 0, half, 0) + jnp.where(n < max_exact, n, large)


def _swa(p_a, cache_k, cache_v, rel_table, sinks, q_norm, k_norm, cq, nqc, has_cache):
    b, t, _ = p_a.shape
    tq = cq * nqc
    band = WINDOW + cq
    rel = (jnp.arange(band) - WINDOW)[None, :] - jnp.arange(cq)[:, None]
    bkt = _t5_bucket(rel).astype(I32)
    qg = jnp.tile(q_norm, ATTN_HEADS).reshape(1, ATTN_WIDTH)
    kg = jnp.tile(k_norm, KV_HEADS).reshape(1, KV_WIDTH)
    kern = functools.partial(_swa_kernel, cq=cq, nqc=nqc, has_cache=has_cache)
    smem = pl.BlockSpec(memory_space=pltpu.SMEM)
    return pl.pallas_call(
        kern,
        grid=(b, t // tq),
        in_specs=[smem, smem,
                  pl.BlockSpec((None, tq, ATTN_PROJ), lambda bi, i: (bi, i, 0)),
                  pl.BlockSpec((None, WINDOW, KV_WIDTH), lambda bi, i: (bi, 0, 0)),
                  pl.BlockSpec((None, WINDOW, KV_WIDTH), lambda bi, i: (bi, 0, 0)),
                  pl.BlockSpec((cq, band), lambda bi, i: (0, 0)),
                  pl.BlockSpec((1, ATTN_WIDTH), lambda bi, i: (0, 0)),
                  pl.BlockSpec((1, KV_WIDTH), lambda bi, i: (0, 0))],
        out_specs=[pl.BlockSpec((None, tq, ATTN_WIDTH), lambda bi, i: (bi, i, 0)),
                   pl.BlockSpec((None, tq, KV_WIDTH), lambda bi, i: (bi, i, 0))],
        out_shape=[jax.ShapeDtypeStruct((b, t, ATTN_WIDTH), F32),
                   jax.ShapeDtypeStruct((b, t, KV_WIDTH), F32)],
        scratch_shapes=[pltpu.VMEM((WINDOW + tq, KV_WIDTH), F32),
                        pltpu.VMEM((WINDOW + tq, KV_WIDTH), F32),
                        pltpu.VMEM((ATTN_HEADS // 2, 2 * cq, band), F32)],
        compiler_params=_cparams("arbitrary", "arbitrary"),
        name="swa",
    )(rel_table, sinks.reshape(1, ATTN_HEADS), p_a, cache_k, cache_v, bkt, qg, kg)


def _mid_kernel(yr_ref, ya_ref, x_ref, gt1_ref, sh2_ref, sc2_ref, g2_ref, wor_ref, woa_ref, wrt_ref,
                rb_ref, base0_ref, x1_ref, h2_ref, te_ref, gate_ref, pos_ref, cnt_ref, base_sc, *, tm):
    @pl.when(jnp.logical_and(pl.program_id(0) == 0, pl.program_id(1) == 0))
    def _():
        base_sc[...] = base0_ref[...]

    mix = _dot(yr_ref[...].astype(BF16), wor_ref[...]) + _dot(ya_ref[...].astype(BF16), woa_ref[...])
    x1 = x_ref[...] + gt1_ref[...] * mix
    x1_ref[...] = x1
    h2 = _rms(x1, RMS_EPS) * g2_ref[...] * (1.0 + sc2_ref[...]) + sh2_ref[...]
    h2_ref[...] = h2

    logits = _mm(wrt_ref[...], h2, passes=3, nt=True) + rb_ref[...]
    eidx = _iota((N_EXPERTS, tm), 0)
    work = logits
    tops, hots = [], []
    for j in range(TOP_K):
        m = jnp.max(work, axis=0, keepdims=True)
        idx = jnp.min(jnp.where(work == m, eidx, N_EXPERTS), axis=0, keepdims=True)
        hot = eidx == idx
        work = jnp.where(hot, -jnp.inf, work)
        tops.append(m)
        hots.append(hot)
        te_ref[j:j + 1, :] = idx
    exps = [jnp.exp(tj - tops[0]) for tj in tops]
    den = exps[0] + exps[1] + exps[2] + exps[3]
    for j in range(TOP_K):
        gate_ref[j:j + 1, :] = exps[j] / den

    member = (hots[0] | hots[1] | hots[2] | hots[3]).astype(BF16)
    upper = (_iota((tm, tm), 0) < _iota((tm, tm), 1)).astype(BF16)
    before = _dot(member, upper) + base_sc[...]
    for j in range(TOP_K):
        pos_ref[j:j + 1, :] = jnp.sum(jnp.where(hots[j], before, 0.0), axis=0, keepdims=True).astype(I32)
    base_sc[...] = base_sc[...] + jnp.sum(member.astype(F32), axis=1, keepdims=True)
    cnt_ref[...] = base_sc[...]


def _mid(y_r, y_a, x3, gt1, sh2, sc2, gain2, wo_r16, wo_a16, wr_t, rb, base0, tm):
    g, r, d = x3.shape
    n = g * r
    per = r // tm
    mr = gt1.shape[1]
    mrow = (lambda b, i: (b, 0, 0)) if mr == 1 else (lambda b, i: (b, i, 0))
    mblk = 1 if mr == 1 else tm
    tok = lambda w: pl.BlockSpec((None, tm, w), lambda b, i: (b, i, 0))
    mod = pl.BlockSpec((None, mblk, d), mrow)
    const = lambda shape: pl.BlockSpec(shape, lambda b, i: (0,) * len(shape))
    lane_out = pl.BlockSpec((TOP_K, tm), lambda b, i: (0, b * per + i))
    kern = functools.partial(_mid_kernel, tm=tm)
    return pl.pallas_call(
        kern,
        grid=(g, per),
        in_specs=[tok(RWKV_WIDTH), tok(ATTN_WIDTH), tok(d), mod, mod, mod, const((1, d)),
                  const((RWKV_WIDTH, d)), const((ATTN_WIDTH, d)), const((N_EXPERTS, d)),
                  const((N_EXPERTS, 1)), const((N_EXPERTS, 1))],
        out_specs=[tok(d), tok(d), lane_out, lane_out, lane_out, const((N_EXPERTS, 1))],
        out_shape=[jax.ShapeDtypeStruct((g, r, d), F32), jax.ShapeDtypeStruct((g, r, d), F32),
                   jax.ShapeDtypeStruct((TOP_K, n), I32), jax.ShapeDtypeStruct((TOP_K, n), F32),
                   jax.ShapeDtypeStruct((TOP_K, n), I32), jax.ShapeDtypeStruct((N_EXPERTS, 1), F32)],
        scratch_shapes=[pltpu.VMEM((N_EXPERTS, 1), F32)],
        compiler_params=_cparams("arbitrary", "arbitrary"),
        name="mid",
    )(y_r, y_a, x3, gt1, sh2, sc2, gain2.reshape(1, d), wo_r16, wo_a16, wr_t, rb, base0)


def _slot_kernel(pstart_ref, te_ref, pos_ref, o_ref):
    te = te_ref[...]
    acc = pos_ref[...]
    for e in range(N_EXPERTS):
        acc = acc + jnp.where(te == e, pstart_ref[e], 0)
    o_ref[...] = acc


def _slot_rows(te, pos, pstart):
    k, n = te.shape
    return pl.pallas_call(
        _slot_kernel,
        grid=(1,),
        in_specs=[pl.BlockSpec(memory_space=pltpu.SMEM),
                  pl.BlockSpec((k, n), lambda i: (0, 0)), pl.BlockSpec((k, n), lambda i: (0, 0))],
        out_specs=pl.BlockSpec((k, n), lambda i: (0, 0)),
        out_shape=jax.ShapeDtypeStruct((k, n), I32),
        compiler_params=_cparams("arbitrary"),
        name="slot_rows",
    )(pstart, te, pos)


def _dispatch_kernel(dst_ref, pad_ref, h2a_ref, h2b_ref, xs_ref, ztile, sem, zsem, *, tk, steps_a):
    i = pl.program_id(0)

    @pl.when(i == 0)
    def _():
        ztile[...] = jnp.zeros_like(ztile)
        zrow = ztile.at[pl.ds(0, 1)]
        tm = ztile.shape[0]

        def zero_row(row, priority):
            pltpu.make_async_copy(zrow, xs_ref.at[pl.ds(row, 1)], zsem).start(priority=priority)

        def per_expert(e, total):
            first = pad_ref[0, e]
            count = pad_ref[1, e]

            def per_pair(r, c):
                zero_row(first + 2 * r, 0)
                zero_row(first + 2 * r + 1, 1)
                return c
            lax.fori_loop(0, count // 2, per_pair, 0)

            @pl.when(count % 2 == 1)
            def _():
                zero_row(first + count - 1, 0)
            return total + count
        n_pad = lax.fori_loop(0, N_EXPERTS, per_expert, 0)

        blk = 64
        def drain_blk(r, c):
            pltpu.make_async_copy(ztile.at[pl.ds(0, blk)], xs_ref.at[pl.ds(0, blk)], zsem).wait()
            return c
        lax.fori_loop(0, n_pad // blk, drain_blk, 0)

        def drain_row(r, c):
            pltpu.make_async_copy(zrow, xs_ref.at[pl.ds(0, 1)], zsem).wait()
            return c
        lax.fori_loop(0, n_pad % blk, drain_row, 0)

        def spare(t, c):
            row0 = pl.multiple_of(t * tm, tm)
            pltpu.make_async_copy(ztile, xs_ref.at[pl.ds(row0, tm)], zsem).start()
            return c
        lax.fori_loop(pad_ref[2, 0], pad_ref[2, 1], spare, 0)

        def spare_wait(t, c):
            pltpu.make_async_copy(ztile, xs_ref.at[pl.ds(0, tm)], zsem).wait()
            return c
        lax.fori_loop(pad_ref[2, 0], pad_ref[2, 1], spare_wait, 0)

    def scatter(h2_ref):
        for t in range(h2_ref.shape[0] * ROW_UNROLL):
            for j in range(TOP_K):
                dst = dst_ref[j, t]
                pltpu.make_async_copy(h2_ref.at[t // ROW_UNROLL, pl.ds(t % ROW_UNROLL, 1)],
                                      xs_ref.at[pl.ds(dst, 1)], sem).start(priority=j % 2)
        for _ in range(TOP_K):
            pltpu.make_async_copy(h2_ref, h2_ref, sem).wait()

    pl.when(i < steps_a)(functools.partial(scatter, h2a_ref))
    pl.when(i == steps_a)(functools.partial(scatter, h2b_ref))


def _dispatch(dest, pads, h2a, h2b, n_rows, tk):
    na, d = h2a.shape
    nb = h2b.shape[0]
    steps_a = na // tk
    dest = jnp.pad(dest, ((0, 0), (0, (steps_a + 1) * tk - na - nb)))
    h2a = h2a.reshape(na // ROW_UNROLL, ROW_UNROLL, d)
    h2b = h2b.reshape(nb // ROW_UNROLL, ROW_UNROLL, d)
    kern = functools.partial(_dispatch_kernel, tk=tk, steps_a=steps_a)
    return pl.pallas_call(
        kern,
        grid=(steps_a + 1,),
        in_specs=[pl.BlockSpec((TOP_K, tk), lambda i: (0, i), memory_space=pltpu.SMEM),
                  pl.BlockSpec(memory_space=pltpu.SMEM),
                  pl.BlockSpec((tk // ROW_UNROLL, ROW_UNROLL, d), lambda i: (jnp.minimum(i, steps_a - 1), 0, 0)),
                  pl.BlockSpec((nb // ROW_UNROLL, ROW_UNROLL, d), lambda i: (0, 0, 0))],
        out_specs=pl.BlockSpec(memory_space=pl.ANY),
        out_shape=jax.ShapeDtypeStruct((n_rows, d), F32),
        scratch_shapes=[pltpu.VMEM((TM_FFN, d), F32), pltpu.SemaphoreType.DMA(()), pltpu.SemaphoreType.DMA(())],
        compiler_params=_cparams("arbitrary"),
        name="dispatch",
    )(dest, pads, h2a, h2b)


def _ffn_kernel(te_ref, nu_ref, nx_ref, xs_ref, wu_hbm, bu_ref, wd_hbm, bd_ref, ys_ref,
                wu32, wd32, wu16, wd16, sem):
    i = pl.program_id(0)
    n_up = wu32.shape[1]
    active = i < nu_ref[0]
    fresh = jnp.logical_or(i == 0, te_ref[i] != te_ref[jnp.maximum(i - 1, 0)])

    def weight_copies(e):
        return (pltpu.make_async_copy(wu_hbm.at[e], wu32, sem.at[0]),
                pltpu.make_async_copy(wd_hbm.at[e], wd32, sem.at[1]))

    @pl.when(i == 0)
    def _():
        for cp in weight_copies(te_ref[0]):
            cp.start()

    @pl.when(jnp.logical_and(active, fresh))
    def _():
        for cp in weight_copies(te_ref[i]):
            cp.wait()
        src = _iota((GROUP, GROUP), 0)
        dst = _iota((GROUP, GROUP), 1)
        want = jnp.where(dst < LANES, 2 * dst, 2 * (dst - LANES) + 1)
        perm = (src == want).astype(BF16)
        for b in range(n_up // GROUP):
            cs = slice(b * GROUP, (b + 1) * GROUP)
            wu16[:, cs] = _dot(wu32[:, cs].astype(BF16), perm).astype(BF16)
        wd16[...] = wd32[...].astype(BF16)

        @pl.when(nx_ref[i] >= 0)
        def _():
            for cp in weight_copies(nx_ref[i]):
                cp.start()

    @pl.when(active)
    def _():
        u = _dot(xs_ref[...].astype(BF16), wu16[...]) + bu_ref[...]
        acts = []
        for b in range(n_up // GROUP):
            glu = jnp.minimum(u[:, b * GROUP:b * GROUP + LANES], SWIGLU_LIMIT)
            lin = jnp.clip(u[:, b * GROUP + LANES:(b + 1) * GROUP], -SWIGLU_LIMIT, SWIGLU_LIMIT)
            acts.append((glu * jax.nn.sigmoid(SWIGLU_ALPHA * glu) * (lin + 1.0)).astype(BF16))
        act = jnp.concatenate(acts, axis=1)
        ys_ref[...] = _dot(act, wd16[...]) + bd_ref[...]

    @pl.when(jnp.logical_not(active))
    def _():
        ys_ref[...] = jnp.zeros_like(ys_ref)


def _ffn(tile_e, n_used, next_e, xs, wu, bu, wd, bd_, tm):
    rows, d = xs.shape
    n_exp, _, n_up = wu.shape
    dff = wd.shape[1]
    row_blk = lambda i, te, nu, nx: (jnp.minimum(i, nu[0] - 1), 0)
    exp_blk = lambda i, te, nu, nx: (te[i], 0, 0)
    grid_spec = pltpu.PrefetchScalarGridSpec(
        num_scalar_prefetch=3,
        grid=(rows // tm,),
        in_specs=[pl.BlockSpec((tm, d), row_blk),
                  pl.BlockSpec(memory_space=pl.ANY),
                  pl.BlockSpec((None, 1, n_up), exp_blk),
                  pl.BlockSpec(memory_space=pl.ANY),
                  pl.BlockSpec((None, 1, d), exp_blk)],
        out_specs=pl.BlockSpec((tm, d), lambda i, te, nu, nx: (i, 0)),
        scratch_shapes=[pltpu.VMEM((d, n_up), F32), pltpu.VMEM((dff, d), F32),
                        pltpu.VMEM((d, n_up), BF16), pltpu.VMEM((dff, d), BF16),
                        pltpu.SemaphoreType.DMA((2,))],
    )
    return pl.pallas_call(
        _ffn_kernel,
        grid_spec=grid_spec,
        out_shape=jax.ShapeDtypeStruct((rows, d), F32),
        compiler_params=_cparams("arbitrary"),
        name="expert_ffn",
    )(tile_e, n_used, next_e, xs, wu, bu.reshape(n_exp, 1, n_up), wd, bd_.reshape(n_exp, 1, d))


def _combine_kernel(src_ref, srcn_ref, gate_ref, x1_ref, gt2_ref, ys_ref, o_ref, buf, sem, *, tk):
    i = pl.program_id(1)
    n = pl.num_programs(1)
    flat = pl.program_id(0) * n + i
    total = pl.num_programs(0) * n
    slot = lax.rem(flat, 2)

    def issue(rows_ref, s, t0, t1):
        for t in range(t0, t1):
            for j in range(TOP_K):
                src = rows_ref[j, t]
                pltpu.make_async_copy(ys_ref.at[pl.ds(src, 1)],
                                      buf.at[s, j, t // ROW_UNROLL, pl.ds(t % ROW_UNROLL, 1)],
                                      sem.at[s]).start()

    def wait_slot(s):
        for j in range(TOP_K):
            pltpu.make_async_copy(buf.at[s, j], buf.at[s, j], sem.at[s]).wait()

    @pl.when(flat == 0)
    def _():
        issue(src_ref, 0, 0, tk)

    def step(cur):
        wait_slot(cur)
        eye = (_iota((tk, tk), 0) == _iota((tk, tk), 1)).astype(BF16)
        g_hi, g_lo = _split(gate_ref[...])
        gate = _dot(eye, g_hi, nt=True) + _dot(eye, g_lo, nt=True)
        for g in range(tk // ROW_UNROLL):
            rs = slice(g * ROW_UNROLL, (g + 1) * ROW_UNROLL)
            acc = gate[rs, 0:1] * buf[cur, 0, g]
            for j in range(1, TOP_K):
                acc = acc + gate[rs, j:j + 1] * buf[cur, j, g]
            gt2 = gt2_ref[...] if gt2_ref.shape[0] == 1 else gt2_ref[rs, :]
            o_ref[rs, :] = x1_ref[rs, :] + gt2 * acc
            issue(srcn_ref, 1 - cur, g * ROW_UNROLL, (g + 1) * ROW_UNROLL)

        @pl.when(flat + 1 == total)
        def _():
            wait_slot(1 - cur)

    for cur in range(2):
        pl.when(slot == cur)(functools.partial(step, cur))


def _combine(dest, gates, x1, gt2, ys, tk):
    g, r, d = x1.shape
    per = r // tk
    last = g * per - 1
    mr = gt2.shape[1]
    mrow = (lambda b, i: (b, 0, 0)) if mr == 1 else (lambda b, i: (b, i, 0))
    mblk = 1 if mr == 1 else tk
    cur = pl.BlockSpec((TOP_K, tk), lambda b, i: (0, b * per + i), memory_space=pltpu.SMEM)
    nxt = pl.BlockSpec((TOP_K, tk), lambda b, i: (0, jnp.minimum(b * per + i + 1, last)),
                       memory_space=pltpu.SMEM)
    kern = functools.partial(_combine_kernel, tk=tk)
    return pl.pallas_call(
        kern,
        grid=(g, per),
        in_specs=[cur, nxt,
                  pl.BlockSpec((TOP_K, tk), lambda b, i: (0, b * per + i)),
                  pl.BlockSpec((None, tk, d), lambda b, i: (b, i, 0)),
                  pl.BlockSpec((None, mblk, d), mrow),
                  pl.BlockSpec(memory_space=pl.ANY)],
        out_specs=pl.BlockSpec((None, tk, d), lambda b, i: (b, i, 0)),
        out_shape=jax.ShapeDtypeStruct((g, r, d), F32),
        scratch_shapes=[pltpu.VMEM((2, TOP_K, tk // ROW_UNROLL, ROW_UNROLL, d), F32),
                        pltpu.SemaphoreType.DMA((2,))],
        compiler_params=_cparams("arbitrary", "arbitrary"),
        name="combine",
    )(dest, dest, gates, x1, gt2, ys)


def _tile_plan(counts, tm, n_tiles):
    cnt = counts.reshape(-1).astype(I32)
    tiles = (cnt + tm - 1) // tm
    tile_end = jnp.cumsum(tiles)
    pstart = ((tile_end - tiles) * tm).astype(I32)
    n_used = tile_end[-1:].astype(I32)
    idx = jnp.minimum(jnp.arange(n_tiles, dtype=I32), n_used[0] - 1)
    tile_e = jnp.sum((tile_end[None, :] <= idx[:, None]).astype(I32), axis=1)
    spare = jnp.zeros((N_EXPERTS,), I32).at[0].set(n_used[0]).at[1].set(n_tiles)
    pads = jnp.stack([pstart + cnt, tiles * tm - cnt, spare]).astype(I32)
    tile_e = jnp.minimum(tile_e, N_EXPERTS - 1)
    end_of = tile_end[tile_e]
    next_e = jnp.where(end_of < n_used[0], tile_e[jnp.minimum(end_of, n_tiles - 1)], -1).astype(I32)
    return pstart, tile_e, n_used, next_e, pads


def kernel(x_prompt, x_sample, state_shift, state_wkv, cache_win_k, cache_win_v, c_prompt, c_sample, rel_bias, norm1_g, norm2_g, w_ada, b_ada, w_in, w_out, rwkv_mu, rwkv_w0, rwkv_w_up, rwkv_a0, rwkv_a_up, rwkv_g_up, rwkv_k_k, rwkv_k_a, rwkv_r_k, rwkv_ln_g, rwkv_ln_b, q_norm, k_norm, sinks, router_w, router_b, exp_w_up, exp_b_up, exp_w_down, exp_b_down):
    bp, tp, d = x_prompt.shape
    bs, ts, _ = x_sample.shape
    depth = w_in.shape[0]
    assert depth == 1, "single-layer trunk"
    layer = 0
    lp = dict(rwkv_mu=rwkv_mu[layer], rwkv_w0=rwkv_w0[layer], rwkv_w_up=rwkv_w_up[layer],
              rwkv_a0=rwkv_a0[layer], rwkv_a_up=rwkv_a_up[layer], rwkv_g_up=rwkv_g_up[layer],
              rwkv_k_k=rwkv_k_k[layer], rwkv_k_a=rwkv_k_a[layer], rwkv_r_k=rwkv_r_k[layer].reshape(-1),
              rwkv_ln_g=rwkv_ln_g[layer], rwkv_ln_b=rwkv_ln_b[layer])

    rows = bp + bs
    pad = (-rows) % 8
    c_all = jnp.concatenate([c_prompt, c_sample, jnp.zeros((pad, d), F32)], axis=0)
    mod = _modulation(c_all, w_ada[layer], b_ada[layer])
    sh1, sc1, gt1, sh2, sc2, gt2 = [mod[:, i * d:(i + 1) * d] for i in range(6)]
    pm = lambda m: m[:bp].reshape(bp, 1, d)
    sm = lambda m: jnp.repeat(m[bp:bp + bs], ts, axis=0).reshape(1, bs * ts, d)

    w_in16 = w_in[layer].astype(BF16)
    wo16 = w_out[layer].astype(BF16)
    wo_r16, wo_a16 = wo16[:RWKV_WIDTH], wo16[RWKV_WIDTH:]
    wr_t = router_w[layer].T
    rb = router_b[layer].reshape(N_EXPERTS, 1)
    wu = exp_w_up.reshape(exp_w_up.shape[1:])
    wd = exp_w_down.reshape(exp_w_down.shape[1:])
    n_up = wu.shape[-1]
    bu = exp_b_up[layer].reshape(N_EXPERTS, n_up // GROUP, LANES, 2).transpose(0, 1, 3, 2).reshape(N_EXPERTS, n_up)

    pr_p, pa_p = _inproj(x_prompt, norm1_g[layer], pm(sh1), pm(sc1), w_in16, TM_PROJ)
    zero_prev = jnp.zeros((bp, 1, RWKV_PROJ), F32)
    zero_state = jnp.zeros((bp, RWKV_WIDTH // GROUP, GROUP, GROUP), F32)
    yr_p, s_p = _rwkv(pr_p, zero_prev, zero_state, lp, RWKV_SUB, None)
    zero_kv = jnp.zeros((bp, WINDOW, KV_WIDTH), F32)
    ya_p, kn_p = _swa(pa_p, zero_kv, zero_kv, rel_bias, sinks[layer], q_norm[layer], k_norm[layer],
                      CHUNK, SWA_QC, False)

    ns = bs * ts
    xs_flat = x_sample.reshape(1, ns, d)
    pr_s, pa_s = _inproj(xs_flat, norm1_g[layer], sm(sh1), sm(sc1), w_in16, ns)
    pr_s = pr_s.reshape(bs, ts, RWKV_PROJ)
    pa_s = pa_s.reshape(bs, ts, ATTN_PROJ)
    t_pad = -(-ts // CHUNK) * CHUNK
    pr_s_pad = jnp.pad(pr_s, ((0, 0), (0, t_pad - ts), (0, 0)))
    yr_s, s_s = _rwkv(pr_s_pad, state_shift[layer].reshape(bs, 1, RWKV_PROJ), _state_to_bd(state_wkv[layer]),
                      lp, t_pad // CHUNK, ts)
    yr_s = yr_s[:, :ts]
    ya_s, kn_s = _swa(pa_s, cache_win_k[layer].reshape(bs, WINDOW, KV_WIDTH),
                      cache_win_v[layer].reshape(bs, WINDOW, KV_WIDTH), rel_bias, sinks[layer],
                      q_norm[layer], k_norm[layer], ts, 1, True)

    base0 = jnp.zeros((N_EXPERTS, 1), F32)
    x1_p, h2_p, te_p, gate_p, pos_p, cnt_p = _mid(yr_p, ya_p, x_prompt, pm(gt1), pm(sh2), pm(sc2), norm2_g[layer],
                                                  wo_r16, wo_a16, wr_t, rb, base0, TM_PROJ)
    x1_s, h2_s, te_s, gate_s, pos_s, cnt = _mid(yr_s.reshape(1, ns, RWKV_WIDTH), ya_s.reshape(1, ns, ATTN_WIDTH),
                                                xs_flat, sm(gt1), sm(sh2), sm(sc2), norm2_g[layer],
                                                wo_r16, wo_a16, wr_t, rb, cnt_p, ns)

    n_p = bp * tp
    n_rows = (n_p + ns) * TOP_K + N_EXPERTS * (TM_FFN - 1)
    n_tiles = -(-n_rows // TM_FFN)
    pstart, tile_e, n_used, next_e, pads = _tile_plan(cnt, TM_FFN, n_tiles)
    dest = _slot_rows(jnp.concatenate([te_p, te_s], axis=1), jnp.concatenate([pos_p, pos_s], axis=1), pstart)
    dest_p, dest_s = dest[:, :n_p], dest[:, n_p:]
    xs_buf = _dispatch(dest, pads, h2_p.reshape(n_p, d), h2_s.reshape(ns, d), n_tiles * TM_FFN, TK_DISPATCH)
    ys_buf = _ffn(tile_e, n_used, next_e, xs_buf, wu, bu, wd, exp_b_down[layer], TM_FFN)
    y_p = _combine(dest_p, gate_p, x1_p, pm(gt2), ys_buf, TK_COMBINE)
    y_s = _combine(dest_s, gate_s, x1_s, sm(gt2), ys_buf, TK_COMBINE)

    kv4 = lambda z, rows_: z.reshape(z.shape[0], rows_, KV_HEADS, HEAD)[None]
    v_p = pa_p[:, tp - WINDOW:, ATTN_WIDTH + KV_WIDTH:]
    v_s = pa_s[:, :, ATTN_WIDTH + KV_WIDTH:]
    return (y_p, y_s.reshape(bs, ts, d),
            pr_p[:, tp - 1][None], _state_from_bd(s_p)[None], kv4(kn_p[:, tp - WINDOW:], WINDOW), kv4(v_p, WINDOW),
            pr_s[:, ts - 1][None], _state_from_bd(s_s)[None], kv4(kn_s, ts), kv4(v_s, ts))
```

```python
import functools
import math

import jax
import jax.numpy as jnp
from jax import lax
from jax.experimental import pallas as pl
from jax.experimental.pallas import tpu as pltpu

F32 = jnp.float32
BF16 = jnp.bfloat16
I32 = jnp.int32

HEAD = 64
RWKV_HEADS = 8
RWKV_WIDTH = RWKV_HEADS * HEAD
DECAY_RANK = 64
ICLR_RANK = 64
GATE_RANK = 128
RWKV_PROJ = 3 * RWKV_WIDTH + DECAY_RANK + ICLR_RANK + GATE_RANK
ATTN_HEADS = 8
KV_HEADS = 2
ATTN_WIDTH = ATTN_HEADS * HEAD
KV_WIDTH = KV_HEADS * HEAD
ATTN_PROJ = ATTN_WIDTH + 2 * KV_WIDTH
WINDOW = 128
CHUNK = 64
REL_BUCKETS = 32
REL_MAX_DIST = 128
N_EXPERTS = 32
TOP_K = 4
GN_EPS = 64e-5
RMS_EPS = 1e-6
NEG_INF = -1e30
ATTN_SCALE = HEAD ** -0.5
SWIGLU_ALPHA = 1.702
SWIGLU_LIMIT = 7.0

LANES = 128
GROUP = 256
VMEM_LIMIT = 56 * 1024 * 1024

TM_PROJ = 512
RWKV_SUB = 4
SWA_QC = 4
TM_FFN = 512
TK_DISPATCH = 512
TK_COMBINE = 256
ROW_UNROLL = 8


def _cparams(*sem):
    return pltpu.CompilerParams(dimension_semantics=sem, vmem_limit_bytes=VMEM_LIMIT)


def _dot(a, b, nt=False):
    dims = (((1,), (1,)), ((), ())) if nt else (((1,), (0,)), ((), ()))
    return lax.dot_general(a, b, dims, preferred_element_type=F32)


def _split(x):
    hi = x.astype(BF16)
    lo = (x - hi.astype(F32)).astype(BF16)
    return hi, lo


def _mm(a, b, passes=1, nt=False):
    if passes == 1:
        return _dot(a.astype(BF16), b.astype(BF16), nt)
    ah, al = _split(a)
    bh, bl = _split(b)
    return _dot(ah, bh, nt) + _dot(al, bh, nt) + _dot(ah, bl, nt)


def _iota(shape, dim):
    return lax.broadcasted_iota(I32, shape, dim)


def _head_ones(n):
    r = lax.shift_right_logical(_iota((n, n), 0), 6)
    c = lax.shift_right_logical(_iota((n, n), 1), 6)
    return (r == c).astype(BF16)


def _head_sum(x):
    width = x.shape[1]
    x16 = x.astype(BF16)
    if width <= GROUP:
        return _dot(x16, _head_ones(width))
    ones = _head_ones(GROUP)
    parts = [_dot(x16[:, g:g + GROUP], ones) for g in range(0, width, GROUP)]
    return jnp.concatenate(parts, axis=1)


def _rms(x, eps):
    return x * lax.rsqrt(jnp.mean(x * x, axis=-1, keepdims=True) + eps)


def _mod_kernel(c_ref, w_ref, b_ref, o_ref):
    c = c_ref[...]
    s = c * jax.nn.sigmoid(c)
    o_ref[...] = _mm(s, w_ref[...], passes=3) + b_ref[...]


def _modulation(c_all, w_ada, b_ada):
    rows, d = c_all.shape
    n = w_ada.shape[1]
    tn = 1536
    return pl.pallas_call(
        _mod_kernel,
        grid=(n // tn,),
        in_specs=[pl.BlockSpec((rows, d), lambda j: (0, 0)),
                  pl.BlockSpec((d, tn), lambda j: (0, j)),
                  pl.BlockSpec((1, tn), lambda j: (0, j))],
        out_specs=pl.BlockSpec((rows, tn), lambda j: (0, j)),
        out_shape=jax.ShapeDtypeStruct((rows, n), F32),
        compiler_params=_cparams("arbitrary"),
        name="modulation",
    )(c_all, w_ada, b_ada.reshape(1, n))


def _inproj_kernel(x_ref, g_ref, sh_ref, sc_ref, w_ref, pr_ref, pa_ref):
    x = x_ref[...]
    h = _rms(x, RMS_EPS) * g_ref[...] * (1.0 + sc_ref[...]) + sh_ref[...]
    p = _dot(h.astype(BF16), w_ref[...])
    pr_ref[...] = p[:, :RWKV_PROJ]
    pa_ref[...] = p[:, RWKV_PROJ:]


def _inproj(x3, gain, sh3, sc3, w_in16, tm):
    g, r, d = x3.shape
    mr = sh3.shape[1]
    mrow = (lambda b, i: (b, 0, 0)) if mr == 1 else (lambda b, i: (b, i, 0))
    mblk = 1 if mr == 1 else tm
    return pl.pallas_call(
        _inproj_kernel,
        grid=(g, r // tm),
        in_specs=[pl.BlockSpec((None, tm, d), lambda b, i: (b, i, 0)),
                  pl.BlockSpec((1, d), lambda b, i: (0, 0)),
                  pl.BlockSpec((None, mblk, d), mrow),
                  pl.BlockSpec((None, mblk, d), mrow),
                  pl.BlockSpec(w_in16.shape, lambda b, i: (0, 0))],
        out_specs=[pl.BlockSpec((None, tm, RWKV_PROJ), lambda b, i: (b, i, 0)),
                   pl.BlockSpec((None, tm, ATTN_PROJ), lambda b, i: (b, i, 0))],
        out_shape=[jax.ShapeDtypeStruct((g, r, RWKV_PROJ), F32),
                   jax.ShapeDtypeStruct((g, r, ATTN_PROJ), F32)],
        compiler_params=_cparams("arbitrary", "arbitrary"),
        name="inproj",
    )(x3, gain.reshape(1, d), sh3, sc3, w_in16)


def _softplus(x):
    return jnp.maximum(x, 0.0) + jnp.log(1.0 + jnp.exp(-jnp.abs(x)))


def _rwkv_kernel(p_ref, prev0_ref, s0_ref, mu_ref, w0_ref, wwa_ref, a0_ref, gup_ref, kk_ref, ka_ref,
                 rk_ref, lng_ref, lnb_ref, y_ref, sout_ref, prev_sc, s_sc, *, nsub, t_valid):
    tc = nsub * CHUNK
    c = pl.program_id(1)

    @pl.when(c == 0)
    def _():
        prev_sc[...] = prev0_ref[...]
        s_sc[...] = s0_ref[...]

    p = p_ref[...]
    row = _iota((tc, 1), 0)
    p_prev = jnp.where(row == 0, prev_sc[...], pltpu.roll(p, 1, axis=0))
    prev_sc[...] = p_ref[tc - 1:tc, :]
    xp = p + (p_prev - p) * mu_ref[...]

    w3 = 3 * RWKV_WIDTH
    r = xp[:, 0:RWKV_WIDTH]
    k = xp[:, RWKV_WIDTH:2 * RWKV_WIDTH]
    v = xp[:, 2 * RWKV_WIDTH:w3]
    z = xp[:, w3:w3 + DECAY_RANK + ICLR_RANK]
    gd = xp[:, w3 + DECAY_RANK + ICLR_RANK:]
    zl = _iota((1, DECAY_RANK + ICLR_RANK), 1)
    zz = jnp.where(zl < DECAY_RANK, jnp.tanh(z), z)
    wa = _mm(zz, wwa_ref[...], passes=3)
    w_log = -_softplus(-(w0_ref[...] + wa[:, :RWKV_WIDTH])) - 0.5
    lw = -jnp.exp(w_log)
    a = jax.nn.sigmoid(a0_ref[...] + wa[:, RWKV_WIDTH:])
    g = _mm(jax.nn.sigmoid(gd), gup_ref[...])
    kk = k * kk_ref[...]
    kk = kk * lax.rsqrt(jnp.maximum(_head_sum(kk * kk), 1e-24))
    k2 = k * (1.0 + (a - 1.0) * ka_ref[...])
    bvec = kk * a
    if t_valid is not None:
        ok = (c * tc + row) < t_valid
        lw = jnp.where(ok, lw, 0.0)
        kk = jnp.where(ok, kk, 0.0)
        bvec = jnp.where(ok, bvec, 0.0)
        k2u = jnp.where(ok, k2, 0.0)
        vu = jnp.where(ok, v, 0.0)
    else:
        k2u, vu = k2, v

    tril16 = (_iota((CHUNK, CHUNK), 0) >= _iota((CHUNK, CHUNK), 1)).astype(BF16)
    bd_mask = (lax.shift_right_logical(_iota((GROUP, GROUP), 0), 6)
               == lax.shift_right_logical(_iota((GROUP, GROUP), 1), 6))
    t_idx = _iota((CHUNK, GROUP), 0)
    s_idx = jnp.bitwise_and(_iota((CHUNK, GROUP), 1), CHUNK - 1)
    strict = t_idx > s_idx
    incl = t_idx >= s_idx
    eye = (t_idx == s_idx).astype(F32)

    bd16 = bd_mask.astype(BF16)

    def bd(x):
        x16 = x.astype(BF16)
        return jnp.concatenate([x16, x16, x16, x16], axis=0) * bd16

    def b16(x):
        return x.astype(BF16)

    ng = RWKV_WIDTH // GROUP
    pairs = [(j, gi) for j in range(nsub) for gi in range(ng)]
    gsl = lambda gi: slice(gi * GROUP, (gi + 1) * GROUP)

    at, bt, kt, rt, bw, kw, vj, w_last = [], [], [], [], [], [], [], []
    for j in range(nsub):
        sl = slice(j * CHUNK, (j + 1) * CHUNK)
        lwj = lw[sl]
        h1 = lwj.astype(BF16)
        r1 = lwj - h1.astype(F32)
        h2 = r1.astype(BF16)
        h3 = (r1 - h2.astype(F32)).astype(BF16)
        cum = _dot(tril16, h1) + _dot(tril16, h2) + _dot(tril16, h3)
        e_cum = jnp.exp(cum)
        e_inv = jnp.exp(-cum)
        wl = e_cum[CHUNK - 1:CHUNK, :]
        at.append(-kk[sl] * jnp.exp(cum - lwj))
        bt.append(bvec[sl] * e_inv)
        kt.append(k2u[sl] * e_inv)
        rt.append(r[sl] * e_cum)
        bw.append(bt[j] * wl)
        kw.append(kt[j] * wl)
        vj.append(vu[sl])
        w_last.append(wl)

    a_ab, a_ak, a_rb, a_rk, vbd = {}, {}, {}, {}, {}
    for (j, gi) in pairs:
        gs = gsl(gi)
        q2 = b16(jnp.concatenate([at[j][:, gs], rt[j][:, gs]], axis=0))
        sc = _dot(q2, jnp.concatenate([bd(bt[j][:, gs]), bd(kt[j][:, gs])], axis=0), nt=True)
        a_ab[j, gi] = jnp.where(strict, sc[:CHUNK, :GROUP], 0.0)
        a_ak[j, gi] = jnp.where(strict, sc[:CHUNK, GROUP:], 0.0)
        a_rb[j, gi] = jnp.where(incl, sc[CHUNK:, :GROUP], 0.0)
        a_rk[j, gi] = jnp.where(incl, sc[CHUNK:, GROUP:], 0.0)
        vbd[j, gi] = bd(vj[j][:, gs])

    tinv = {p: eye + a_ab[p] for p in pairs}
    xpow = {p: _dot(b16(a_ab[p]), bd(a_ab[p])) for p in pairs}
    for step in range(1, 6):
        for p in pairs:
            if step < 5:
                both = _dot(b16(xpow[p]), jnp.concatenate([bd(tinv[p]), bd(xpow[p])], axis=1))
                tinv[p] = tinv[p] + both[:, :GROUP]
                xpow[p] = both[:, GROUP:]
            else:
                tinv[p] = tinv[p] + _dot(b16(xpow[p]), bd(tinv[p]))

    akv = {p: _dot(b16(a_ak[p]), vbd[p]) for p in pairs}
    pq = {(j, gi): _dot(b16(tinv[j, gi]), jnp.concatenate([bd(at[j][:, gsl(gi)]), bd(akv[j, gi])], axis=1))
          for (j, gi) in pairs}
    r2, y0, m_lr, n_add = {}, {}, {}, {}
    for (j, gi) in pairs:
        gs = gsl(gi)
        pm, qm = pq[j, gi][:, :GROUP], pq[j, gi][:, GROUP:]
        rr = _dot(b16(a_rb[j, gi]), jnp.concatenate([bd(pm), bd(qm)], axis=1))
        r2[j, gi] = b16(rt[j][:, gs] + rr[:, :GROUP])
        y0[j, gi] = rr[:, GROUP:] + _dot(b16(a_rk[j, gi]), vbd[j, gi])
        m_lr[j, gi] = b16(jnp.where(bd_mask, _dot(b16(pm.T), b16(bw[j][:, gs])), 0.0))
        qv = jnp.concatenate([qm, vj[j][:, gs]], axis=0)
        bk = jnp.concatenate([bw[j][:, gs], kw[j][:, gs]], axis=0)
        n_add[j, gi] = jnp.where(bd_mask, _dot(b16(qv.T), b16(bk)), 0.0)

    s_in = {}
    for gi in range(ng):
        s = s_sc[gi]
        for j in range(nsub):
            s16 = b16(s)
            s_in[j, gi] = s16
            s = s * w_last[j][:, gsl(gi)] + _dot(s16, m_lr[j, gi]) + n_add[j, gi]
        s_sc[gi] = s

    ys = []
    for j in range(nsub):
        yg = [_dot(r2[j, gi], s_in[j, gi], nt=True) + y0[j, gi] for gi in range(ng)]
        ys.append(jnp.concatenate(yg, axis=1))
    y = jnp.concatenate(ys, axis=0) if nsub > 1 else ys[0]

    inv_n = 1.0 / HEAD
    yc = y - _head_sum(y) * inv_n
    yn = yc * lax.rsqrt(_head_sum(yc * yc) * inv_n + GN_EPS)
    yn = yn * lng_ref[...] + lnb_ref[...]
    bonus = _head_sum(r * k2 * rk_ref[...]) * v
    y_ref[...] = (yn + bonus) * g
    sout_ref[...] = s_sc[...]


def _rwkv(p_r, prev0, s0_bd, lp, nsub, t_valid):
    b, t, _ = p_r.shape
    tc = nsub * CHUNK
    ng = RWKV_WIDTH // GROUP
    row = lambda a: a.reshape(1, -1)
    zeros = jnp.zeros((DECAY_RANK, RWKV_WIDTH), F32)
    wwa = jnp.concatenate([jnp.concatenate([lp["rwkv_w_up"], zeros], axis=1),
                           jnp.concatenate([zeros, lp["rwkv_a_up"]], axis=1)], axis=0)
    const = lambda shape: pl.BlockSpec(shape, lambda bi, ci: (0,) * len(shape))
    kern = functools.partial(_rwkv_kernel, nsub=nsub, t_valid=t_valid)
    return pl.pallas_call(
        kern,
        grid=(b, t // tc),
        in_specs=[pl.BlockSpec((None, tc, RWKV_PROJ), lambda bi, ci: (bi, ci, 0)),
                  pl.BlockSpec((None, 1, RWKV_PROJ), lambda bi, ci: (bi, 0, 0)),
                  pl.BlockSpec((None, ng, GROUP, GROUP), lambda bi, ci: (bi, 0, 0, 0)),
                  const((1, RWKV_PROJ)), const((1, RWKV_WIDTH)),
                  const((DECAY_RANK + ICLR_RANK, 2 * RWKV_WIDTH)), const((1, RWKV_WIDTH)),
                  const((GATE_RANK, RWKV_WIDTH)), const((1, RWKV_WIDTH)), const((1, RWKV_WIDTH)),
                  const((1, RWKV_WIDTH)), const((1, RWKV_WIDTH)), const((1, RWKV_WIDTH))],
        out_specs=[pl.BlockSpec((None, tc, RWKV_WIDTH), lambda bi, ci: (bi, ci, 0)),
                   pl.BlockSpec((None, ng, GROUP, GROUP), lambda bi, ci: (bi, 0, 0, 0))],
        out_shape=[jax.ShapeDtypeStruct((b, t, RWKV_WIDTH), F32),
                   jax.ShapeDtypeStruct((b, ng, GROUP, GROUP), F32)],
        scratch_shapes=[pltpu.VMEM((1, RWKV_PROJ), F32), pltpu.VMEM((ng, GROUP, GROUP), F32)],
        compiler_params=_cparams("arbitrary", "arbitrary"),
        name="rwkv7",
    )(p_r, prev0, s0_bd, row(lp["rwkv_mu"]), row(lp["rwkv_w0"]), wwa, row(lp["rwkv_a0"]),
      lp["rwkv_g_up"], row(lp["rwkv_k_k"]), row(lp["rwkv_k_a"]), row(lp["rwkv_r_k"]),
      row(lp["rwkv_ln_g"]), row(lp["rwkv_ln_b"]))


def _state_to_bd(s):
    b = s.shape[0]
    hg = GROUP // HEAD
    s5 = s.reshape(b, RWKV_HEADS // hg, hg, HEAD, HEAD)
    out = jnp.einsum("bghvk,hj->bghvjk", s5, jnp.eye(hg, dtype=s.dtype))
    return out.reshape(b, RWKV_HEADS // hg, GROUP, GROUP)


def _state_from_bd(s_bd):
    b = s_bd.shape[0]
    hg = GROUP // HEAD
    s6 = s_bd.reshape(b, RWKV_HEADS // hg, hg, HEAD, hg, HEAD)
    out = jnp.einsum("bghvjk,hj->bghvk", s6, jnp.eye(hg, dtype=s_bd.dtype))
    return out.reshape(b, RWKV_HEADS, HEAD, HEAD)


def _swa_kernel(tab_ref, sink_ref, pa_ref, ck_ref, cv_ref, bkt_ref, qg_ref, kg_ref,
                ya_ref, kn_ref, kall, vall, bias_sc, *, cq, nqc, has_cache):
    tq = cq * nqc
    band = WINDOW + cq
    bi = pl.program_id(0)
    i = pl.program_id(1)
    lane = _iota((1, LANES), 1)
    low = lane < HEAD

    @pl.when(jnp.logical_and(bi == 0, i == 0))
    def _():
        bkt = bkt_ref[...]
        for h in range(ATTN_HEADS):
            def body(j, acc, h=h):
                return acc + jnp.where(bkt == j, tab_ref[j, h], 0.0)
            bh = lax.fori_loop(0, REL_BUCKETS, body, jnp.zeros((cq, band), F32))
            bias_sc[h // 2, (h % 2) * cq:(h % 2 + 1) * cq, :] = bh

    @pl.when(i == 0)
    def _():
        kall[0:WINDOW, :] = ck_ref[...]
        vall[0:WINDOW, :] = cv_ref[...]

    pa = pa_ref[...]
    q = pa[:, :ATTN_WIDTH]
    k = pa[:, ATTN_WIDTH:ATTN_WIDTH + KV_WIDTH]
    v = pa[:, ATTN_WIDTH + KV_WIDTH:]
    inv_n = 1.0 / HEAD
    qn = q * lax.rsqrt(_head_sum(q * q) * inv_n + RMS_EPS) * qg_ref[...]
    kn = k * lax.rsqrt(_head_sum(k * k) * inv_n + RMS_EPS) * kg_ref[...]
    kn_ref[...] = kn
    kall[WINDOW:WINDOW + tq, :] = kn
    vall[WINDOW:WINDOW + tq, :] = v

    row2 = _iota((2 * cq, 1), 0)
    col = _iota((1, band), 1)
    thr = jnp.where(i == 0, WINDOW, 0)
    k_all = kall[...]
    v_all = vall[...]
    k_rot = pltpu.roll(k_all, HEAD, axis=1)
    v_rot = pltpu.roll(v_all, HEAD, axis=1)
    kd = [jnp.where(low, k_all, k_rot).astype(BF16), jnp.where(low, k_rot, k_all).astype(BF16)]
    vd = [jnp.where(low, v_all, v_rot).astype(BF16), jnp.where(low, v_rot, v_all).astype(BF16)]

    work = [(qc, pi) for qc in range(nqc) for pi in range(ATTN_HEADS // 2)]
    kv_of = lambda pi: (2 * pi) // (ATTN_HEADS // KV_HEADS)
    scores = {}
    for (qc, pi) in work:
        qp = qn[qc * cq:(qc + 1) * cq, pi * LANES:(pi + 1) * LANES]
        qs = jnp.concatenate([jnp.where(low, qp, 0.0), jnp.where(low, 0.0, qp)], axis=0)
        s = _dot(qs.astype(BF16), kd[kv_of(pi)][qc * cq:qc * cq + band], nt=True) * ATTN_SCALE + bias_sc[pi]
        if (not has_cache) and qc * cq < WINDOW:
            s = jnp.where((col + qc * cq) < thr, NEG_INF, s)
        scores[qc, pi] = s
    probs = {}
    for (qc, pi) in work:
        s = scores[qc, pi]
        sink = jnp.where(row2 < cq, sink_ref[0, 2 * pi], sink_ref[0, 2 * pi + 1])
        m = jnp.maximum(jnp.max(s, axis=-1, keepdims=True), sink)
        e = jnp.exp(s - m)
        den = jnp.sum(e, axis=-1, keepdims=True) + jnp.exp(sink - m)
        probs[qc, pi] = (e / den).astype(BF16)
    for (qc, pi) in work:
        o = _dot(probs[qc, pi], vd[kv_of(pi)][qc * cq:qc * cq + band])
        ya_ref[qc * cq:(qc + 1) * cq, pi * LANES:(pi + 1) * LANES] = jnp.where(low, o[:cq], o[cq:])

    if tq >= WINDOW:
        knext = kall[tq:tq + WINDOW, :]
        vnext = vall[tq:tq + WINDOW, :]
        kall[0:WINDOW, :] = knext
        vall[0:WINDOW, :] = vnext


def _t5_bucket(rel):
    half = REL_BUCKETS // 2
    max_exact = half // 2
    n = jnp.abs(rel)
    log_ratio = jnp.log(jnp.maximum(n, 1).astype(F32) / max_exact) / math.log(REL_MAX_DIST / max_exact)
    large = jnp.minimum(max_exact + (log_ratio * (half - max_exact)).astype(I32), half - 1)
    return jnp.where(rel > 0, half, 0) + jnp.where(n < max_exact, n, large)


def _swa(p_a, cache_k, cache_v, rel_table, sinks, q_norm, k_norm, cq, nqc, has_cache):
    b, t, _ = p_a.shape
    tq = cq * nqc
    band = WINDOW + cq
    rel = (jnp.arange(band) - WINDOW)[None, :] - jnp.arange(cq)[:, None]
    bkt = _t5_bucket(rel).astype(I32)
    qg = jnp.tile(q_norm, ATTN_HEADS).reshape(1, ATTN_WIDTH)
    kg = jnp.tile(k_norm, KV_HEADS).reshape(1, KV_WIDTH)
    kern = functools.partial(_swa_kernel, cq=cq, nqc=nqc, has_cache=has_cache)
    smem = pl.BlockSpec(memory_space=pltpu.SMEM)
    return pl.pallas_call(
        kern,
        grid=(b, t // tq),
        in_specs=[smem, smem,
                  pl.BlockSpec((None, tq, ATTN_PROJ), lambda bi, i: (bi, i, 0)),
                  pl.BlockSpec((None, WINDOW, KV_WIDTH), lambda bi, i: (bi, 0, 0)),
                  pl.BlockSpec((None, WINDOW, KV_WIDTH), lambda bi, i: (bi, 0, 0)),
                  pl.BlockSpec((cq, band), lambda bi, i: (0, 0)),
                  pl.BlockSpec((1, ATTN_WIDTH), lambda bi, i: (0, 0)),
                  pl.BlockSpec((1, KV_WIDTH), lambda bi, i: (0, 0))],
        out_specs=[pl.BlockSpec((None, tq, ATTN_WIDTH), lambda bi, i: (bi, i, 0)),
                   pl.BlockSpec((None, tq, KV_WIDTH), lambda bi, i: (bi, i, 0))],
        out_shape=[jax.ShapeDtypeStruct((b, t, ATTN_WIDTH), F32),
                   jax.ShapeDtypeStruct((b, t, KV_WIDTH), F32)],
        scratch_shapes=[pltpu.VMEM((WINDOW + tq, KV_WIDTH), F32),
                        pltpu.VMEM((WINDOW + tq, KV_WIDTH), F32),
                        pltpu.VMEM((ATTN_HEADS // 2, 2 * cq, band), F32)],
        compiler_params=_cparams("arbitrary", "arbitrary"),
        name="swa",
    )(rel_table, sinks.reshape(1, ATTN_HEADS), p_a, cache_k, cache_v, bkt, qg, kg)


def _mid_kernel(yr_ref, ya_ref, x_ref, gt1_ref, sh2_ref, sc2_ref, g2_ref, wor_ref, woa_ref, wrt_ref,
                rb_ref, base0_ref, x1_ref, h2_ref, te_ref, gate_ref, pos_ref, cnt_ref, base_sc, *, tm):
    @pl.when(jnp.logical_and(pl.program_id(0) == 0, pl.program_id(1) == 0))
    def _():
        base_sc[...] = base0_ref[...]

    mix = _dot(yr_ref[...].astype(BF16), wor_ref[...]) + _dot(ya_ref[...].astype(BF16), woa_ref[...])
    x1 = x_ref[...] + gt1_ref[...] * mix
    x1_ref[...] = x1
    h2 = _rms(x1, RMS_EPS) * g2_ref[...] * (1.0 + sc2_ref[...]) + sh2_ref[...]
    h2_ref[...] = h2

    logits = _mm(wrt_ref[...], h2, passes=3, nt=True) + rb_ref[...]
    eidx = _iota((N_EXPERTS, tm), 0)
    work = logits
    tops, hots = [], []
    for j in range(TOP_K):
        m = jnp.max(work, axis=0, keepdims=True)
        idx = jnp.min(jnp.where(work == m, eidx, N_EXPERTS), axis=0, keepdims=True)
        hot = eidx == idx
        work = jnp.where(hot, -jnp.inf, work)
        tops.append(m)
        hots.append(hot)
        te_ref[j:j + 1, :] = idx
    exps = [jnp.exp(tj - tops[0]) for tj in tops]
    den = exps[0] + exps[1] + exps[2] + exps[3]
    for j in range(TOP_K):
        gate_ref[j:j + 1, :] = exps[j] / den

    member = (hots[0] | hots[1] | hots[2] | hots[3]).astype(BF16)
    upper = (_iota((tm, tm), 0) < _iota((tm, tm), 1)).astype(BF16)
    before = _dot(member, upper) + base_sc[...]
    for j in range(TOP_K):
        pos_ref[j:j + 1, :] = jnp.sum(jnp.where(hots[j], before, 0.0), axis=0, keepdims=True).astype(I32)
    base_sc[...] = base_sc[...] + jnp.sum(member.astype(F32), axis=1, keepdims=True)
    cnt_ref[...] = base_sc[...]


def _mid(y_r, y_a, x3, gt1, sh2, sc2, gain2, wo_r16, wo_a16, wr_t, rb, base0, tm):
    g, r, d = x3.shape
    n = g * r
    per = r // tm
    mr = gt1.shape[1]
    mrow = (lambda b, i: (b, 0, 0)) if mr == 1 else (lambda b, i: (b, i, 0))
    mblk = 1 if mr == 1 else tm
    tok = lambda w: pl.BlockSpec((None, tm, w), lambda b, i: (b, i, 0))
    mod = pl.BlockSpec((None, mblk, d), mrow)
    const = lambda shape: pl.BlockSpec(shape, lambda b, i: (0,) * len(shape))
    lane_out = pl.BlockSpec((TOP_K, tm), lambda b, i: (0, b * per + i))
    kern = functools.partial(_mid_kernel, tm=tm)
    return pl.pallas_call(
        kern,
        grid=(g, per),
        in_specs=[tok(RWKV_WIDTH), tok(ATTN_WIDTH), tok(d), mod, mod, mod, const((1, d)),
                  const((RWKV_WIDTH, d)), const((ATTN_WIDTH, d)), const((N_EXPERTS, d)),
                  const((N_EXPERTS, 1)), const((N_EXPERTS, 1))],
        out_specs=[tok(d), tok(d), lane_out, lane_out, lane_out, const((N_EXPERTS, 1))],
        out_shape=[jax.ShapeDtypeStruct((g, r, d), F32), jax.ShapeDtypeStruct((g, r, d), F32),
                   jax.ShapeDtypeStruct((TOP_K, n), I32), jax.ShapeDtypeStruct((TOP_K, n), F32),
                   jax.ShapeDtypeStruct((TOP_K, n), I32), jax.ShapeDtypeStruct((N_EXPERTS, 1), F32)],
        scratch_shapes=[pltpu.VMEM((N_EXPERTS, 1), F32)],
        compiler_params=_cparams("arbitrary", "arbitrary"),
        name="mid",
    )(y_r, y_a, x3, gt1, sh2, sc2, gain2.reshape(1, d), wo_r16, wo_a16, wr_t, rb, base0)


def _slot_kernel(pstart_ref, te_ref, pos_ref, o_ref):
    te = te_ref[...]
    acc = pos_ref[...]
    for e in range(N_EXPERTS):
        acc = acc + jnp.where(te == e, pstart_ref[e], 0)
    o_ref[...] = acc


def _slot_rows(te, pos, pstart):
    k, n = te.shape
    return pl.pallas_call(
        _slot_kernel,
        grid=(1,),
        in_specs=[pl.BlockSpec(memory_space=pltpu.SMEM),
                  pl.BlockSpec((k, n), lambda i: (0, 0)), pl.BlockSpec((k, n), lambda i: (0, 0))],
        out_specs=pl.BlockSpec((k, n), lambda i: (0, 0)),
        out_shape=jax.ShapeDtypeStruct((k, n), I32),
        compiler_params=_cparams("arbitrary"),
        name="slot_rows",
    )(pstart, te, pos)


def _dispatch_kernel(dst_ref, pad_ref, h2a_ref, h2b_ref, xs_ref, ztile, sem, zsem, *, tk, steps_a):
    i = pl.program_id(0)

    @pl.when(i == 0)
    def _():
        ztile[...] = jnp.zeros_like(ztile)
        zrow = ztile.at[pl.ds(0, 1)]
        tm = ztile.shape[0]

        def zero_row(row, priority):
            pltpu.make_async_copy(zrow, xs_ref.at[pl.ds(row, 1)], zsem).start(priority=priority)

        def per_expert(e, total):
            first = pad_ref[0, e]
            count = pad_ref[1, e]

            def per_pair(r, c):
                zero_row(first + 2 * r, 0)
                zero_row(first + 2 * r + 1, 1)
                return c
            lax.fori_loop(0, count // 2, per_pair, 0)

            @pl.when(count % 2 == 1)
            def _():
                zero_row(first + count - 1, 0)
            return total + count
        n_pad = lax.fori_loop(0, N_EXPERTS, per_expert, 0)

        blk = 64
        def drain_blk(r, c):
            pltpu.make_async_copy(ztile.at[pl.ds(0, blk)], xs_ref.at[pl.ds(0, blk)], zsem).wait()
            return c
        lax.fori_loop(0, n_pad // blk, drain_blk, 0)

        def drain_row(r, c):
            pltpu.make_async_copy(zrow, xs_ref.at[pl.ds(0, 1)], zsem).wait()
            return c
        lax.fori_loop(0, n_pad % blk, drain_row, 0)

        def spare(t, c):
            row0 = pl.multiple_of(t * tm, tm)
            pltpu.make_async_copy(ztile, xs_ref.at[pl.ds(row0, tm)], zsem).start()
            return c
        lax.fori_loop(pad_ref[2, 0], pad_ref[2, 1], spare, 0)

        def spare_wait(t, c):
            pltpu.make_async_copy(ztile, xs_ref.at[pl.ds(0, tm)], zsem).wait()
            return c
        lax.fori_loop(pad_ref[2, 0], pad_ref[2, 1], spare_wait, 0)

    def scatter(h2_ref):
        for t in range(h2_ref.shape[0] * ROW_UNROLL):
            for j in range(TOP_K):
                dst = dst_ref[j, t]
                pltpu.make_async_copy(h2_ref.at[t // ROW_UNROLL, pl.ds(t % ROW_UNROLL, 1)],
                                      xs_ref.at[pl.ds(dst, 1)], sem).start(priority=j % 2)
        for _ in range(TOP_K):
            pltpu.make_async_copy(h2_ref, h2_ref, sem).wait()

    pl.when(i < steps_a)(functools.partial(scatter, h2a_ref))
    pl.when(i == steps_a)(functools.partial(scatter, h2b_ref))


def _dispatch(dest, pads, h2a, h2b, n_rows, tk):
    na, d = h2a.shape
    nb = h2b.shape[0]
    steps_a = na // tk
    dest = jnp.pad(dest, ((0, 0), (0, (steps_a + 1) * tk - na - nb)))
    h2a = h2a.reshape(na // ROW_UNROLL, ROW_UNROLL, d)
    h2b = h2b.reshape(nb // ROW_UNROLL, ROW_UNROLL, d)
    kern = functools.partial(_dispatch_kernel, tk=tk, steps_a=steps_a)
    return pl.pallas_call(
        kern,
        grid=(steps_a + 1,),
        in_specs=[pl.BlockSpec((TOP_K, tk), lambda i: (0, i), memory_space=pltpu.SMEM),
                  pl.BlockSpec(memory_space=pltpu.SMEM),
                  pl.BlockSpec((tk // ROW_UNROLL, ROW_UNROLL, d), lambda i: (jnp.minimum(i, steps_a - 1), 0, 0)),
                  pl.BlockSpec((nb // ROW_UNROLL, ROW_UNROLL, d), lambda i: (0, 0, 0))],
        out_specs=pl.BlockSpec(memory_space=pl.ANY),
        out_shape=jax.ShapeDtypeStruct((n_rows, d), F32),
        scratch_shapes=[pltpu.VMEM((TM_FFN, d), F32), pltpu.SemaphoreType.DMA(()), pltpu.SemaphoreType.DMA(())],
        compiler_params=_cparams("arbitrary"),
        name="dispatch",
    )(dest, pads, h2a, h2b)


def _ffn_kernel(te_ref, nu_ref, nx_ref, xs_ref, wu_hbm, bu_ref, wd_hbm, bd_ref, ys_ref,
                wu32, wd32, wu16, wd16, sem):
    i = pl.program_id(0)
    n_up = wu32.shape[1]
    active = i < nu_ref[0]
    fresh = jnp.logical_or(i == 0, te_ref[i] != te_ref[jnp.maximum(i - 1, 0)])

    def weight_copies(e):
        return (pltpu.make_async_copy(wu_hbm.at[e], wu32, sem.at[0]),
                pltpu.make_async_copy(wd_hbm.at[e], wd32, sem.at[1]))

    @pl.when(i == 0)
    def _():
        for cp in weight_copies(te_ref[0]):
            cp.start()

    @pl.when(jnp.logical_and(active, fresh))
    def _():
        for cp in weight_copies(te_ref[i]):
            cp.wait()
        src = _iota((GROUP, GROUP), 0)
        dst = _iota((GROUP, GROUP), 1)
        want = jnp.where(dst < LANES, 2 * dst, 2 * (dst - LANES) + 1)
        perm = (src == want).astype(BF16)
        for b in range(n_up // GROUP):
            cs = slice(b * GROUP, (b + 1) * GROUP)
            wu16[:, cs] = _dot(wu32[:, cs].astype(BF16), perm).astype(BF16)
        wd16[...] = wd32[...].astype(BF16)

        @pl.when(nx_ref[i] >= 0)
        def _():
            for cp in weight_copies(nx_ref[i]):
                cp.start()

    @pl.when(active)
    def _():
        u = _dot(xs_ref[...].astype(BF16), wu16[...]) + bu_ref[...]
        acts = []
        for b in range(n_up // GROUP):
            glu = jnp.minimum(u[:, b * GROUP:b * GROUP + LANES], SWIGLU_LIMIT)
            lin = jnp.clip(u[:, b * GROUP + LANES:(b + 1) * GROUP], -SWIGLU_LIMIT, SWIGLU_LIMIT)
            acts.append((glu * jax.nn.sigmoid(SWIGLU_ALPHA * glu) * (lin + 1.0)).astype(BF16))
        act = jnp.concatenate(acts, axis=1)
        ys_ref[...] = _dot(act, wd16[...]) + bd_ref[...]

    @pl.when(jnp.logical_not(active))
    def _():
        ys_ref[...] = jnp.zeros_like(ys_ref)


def _ffn(tile_e, n_used, next_e, xs, wu, bu, wd, bd_, tm):
    rows, d = xs.shape
    n_exp, _, n_up = wu.shape
    dff = wd.shape[1]
    row_blk = lambda i, te, nu, nx: (jnp.minimum(i, nu[0] - 1), 0)
    exp_blk = lambda i, te, nu, nx: (te[i], 0, 0)
    grid_spec = pltpu.PrefetchScalarGridSpec(
        num_scalar_prefetch=3,
        grid=(rows // tm,),
        in_specs=[pl.BlockSpec((tm, d), row_blk),
                  pl.BlockSpec(memory_space=pl.ANY),
                  pl.BlockSpec((None, 1, n_up), exp_blk),
                  pl.BlockSpec(memory_space=pl.ANY),
                  pl.BlockSpec((None, 1, d), exp_blk)],
        out_specs=pl.BlockSpec((tm, d), lambda i, te, nu, nx: (i, 0)),
        scratch_shapes=[pltpu.VMEM((d, n_up), F32), pltpu.VMEM((dff, d), F32),
                        pltpu.VMEM((d, n_up), BF16), pltpu.VMEM((dff, d), BF16),
                        pltpu.SemaphoreType.DMA((2,))],
    )
    return pl.pallas_call(
        _ffn_kernel,
        grid_spec=grid_spec,
        out_shape=jax.ShapeDtypeStruct((rows, d), F32),
        compiler_params=_cparams("arbitrary"),
        name="expert_ffn",
    )(tile_e, n_used, next_e, xs, wu, bu.reshape(n_exp, 1, n_up), wd, bd_.reshape(n_exp, 1, d))


def _combine_kernel(src_ref, srcn_ref, gate_ref, x1_ref, gt2_ref, ys_ref, o_ref, buf, sem, *, tk):
    i = pl.program_id(1)
    n = pl.num_programs(1)
    flat = pl.program_id(0) * n + i
    total = pl.num_programs(0) * n
    slot = lax.rem(flat, 2)

    def issue(rows_ref, s, t0, t1):
        for t in range(t0, t1):
            for j in range(TOP_K):
                src = rows_ref[j, t]
                pltpu.make_async_copy(ys_ref.at[pl.ds(src, 1)],
                                      buf.at[s, j, t // ROW_UNROLL, pl.ds(t % ROW_UNROLL, 1)],
                                      sem.at[s]).start(priority=j % 2)

    @pl.when(flat == 0)
    def _():
        issue(src_ref, 0, 0, tk)

    def step(cur):
        @pl.when(flat + 1 < total)
        def _():
            issue(srcn_ref, 1 - cur, 0, tk)

        for j in range(TOP_K):
            pltpu.make_async_copy(buf.at[cur, j], buf.at[cur, j], sem.at[cur]).wait()
        eye = (_iota((tk, tk), 0) == _iota((tk, tk), 1)).astype(BF16)
        g_hi, g_lo = _split(gate_ref[...])
        gate = _dot(eye, g_hi, nt=True) + _dot(eye, g_lo, nt=True)
        rows = lambda j: buf[cur, j].reshape(tk, buf.shape[-1])
        acc = gate[:, 0:1] * rows(0)
        for j in range(1, TOP_K):
            acc = acc + gate[:, j:j + 1] * rows(j)
        o_ref[...] = x1_ref[...] + gt2_ref[...] * acc

    for cur in range(2):
        pl.when(slot == cur)(functools.partial(step, cur))


def _combine(dest, gates, x1, gt2, ys, tk):
    g, r, d = x1.shape
    per = r // tk
    last = g * per - 1
    mr = gt2.shape[1]
    mrow = (lambda b, i: (b, 0, 0)) if mr == 1 else (lambda b, i: (b, i, 0))
    mblk = 1 if mr == 1 else tk
    cur = pl.BlockSpec((TOP_K, tk), lambda b, i: (0, b * per + i), memory_space=pltpu.SMEM)
    nxt = pl.BlockSpec((TOP_K, tk), lambda b, i: (0, jnp.minimum(b * per + i + 1, last)),
                       memory_space=pltpu.SMEM)
    kern = functools.partial(_combine_kernel, tk=tk)
    return pl.pallas_call(
        kern,
        grid=(g, per),
        in_specs=[cur, nxt,
                  pl.BlockSpec((TOP_K, tk), lambda b, i: (0, b * per + i)),
                  pl.BlockSpec((None, tk, d), lambda b, i: (b, i, 0)),
                  pl.BlockSpec((None, mblk, d), mrow),
                  pl.BlockSpec(memory_space=pl.ANY)],
        out_specs=pl.BlockSpec((None, tk, d), lambda b, i: (b, i, 0)),
        out_shape=jax.ShapeDtypeStruct((g, r, d), F32),
        scratch_shapes=[pltpu.VMEM((2, TOP_K, tk // ROW_UNROLL, ROW_UNROLL, d), F32),
                        pltpu.SemaphoreType.DMA((2,))],
        compiler_params=_cparams("arbitrary", "arbitrary"),
        name="combine",
    )(dest, dest, gates, x1, gt2, ys)


def _tile_plan(counts, tm, n_tiles):
    cnt = counts.reshape(-1).astype(I32)
    tiles = (cnt + tm - 1) // tm
    tile_end = jnp.cumsum(tiles)
    pstart = ((tile_end - tiles) * tm).astype(I32)
    n_used = tile_end[-1:].astype(I32)
    idx = jnp.minimum(jnp.arange(n_tiles, dtype=I32), n_used[0] - 1)
    tile_e = jnp.sum((tile_end[None, :] <= idx[:, None]).astype(I32), axis=1)
    spare = jnp.zeros((N_EXPERTS,), I32).at[0].set(n_used[0]).at[1].set(n_tiles)
    pads = jnp.stack([pstart + cnt, tiles * tm - cnt, spare]).astype(I32)
    tile_e = jnp.minimum(tile_e, N_EXPERTS - 1)
    end_of = tile_end[tile_e]
    next_e = jnp.where(end_of < n_used[0], tile_e[jnp.minimum(end_of, n_tiles - 1)], -1).astype(I32)
    return pstart, tile_e, n_used, next_e, pads


def kernel(x_prompt, x_sample, state_shift, state_wkv, cache_win_k, cache_win_v, c_prompt, c_sample, rel_bias, norm1_g, norm2_g, w_ada, b_ada, w_in, w_out, rwkv_mu, rwkv_w0, rwkv_w_up, rwkv_a0, rwkv_a_up, rwkv_g_up, rwkv_k_k, rwkv_k_a, rwkv_r_k, rwkv_ln_g, rwkv_ln_b, q_norm, k_norm, sinks, router_w, router_b, exp_w_up, exp_b_up, exp_w_down, exp_b_down):
    bp, tp, d = x_prompt.shape
    bs, ts, _ = x_sample.shape
    depth = w_in.shape[0]
    assert depth == 1, "single-layer trunk"
    layer = 0
    lp = dict(rwkv_mu=rwkv_mu[layer], rwkv_w0=rwkv_w0[layer], rwkv_w_up=rwkv_w_up[layer],
              rwkv_a0=rwkv_a0[layer], rwkv_a_up=rwkv_a_up[layer], rwkv_g_up=rwkv_g_up[layer],
              rwkv_k_k=rwkv_k_k[layer], rwkv_k_a=rwkv_k_a[layer], rwkv_r_k=rwkv_r_k[layer].reshape(-1),
              rwkv_ln_g=rwkv_ln_g[layer], rwkv_ln_b=rwkv_ln_b[layer])

    rows = bp + bs
    pad = (-rows) % 8
    c_all = jnp.concatenate([c_prompt, c_sample, jnp.zeros((pad, d), F32)], axis=0)
    mod = _modulation(c_all, w_ada[layer], b_ada[layer])
    sh1, sc1, gt1, sh2, sc2, gt2 = [mod[:, i * d:(i + 1) * d] for i in range(6)]
    pm = lambda m: m[:bp].reshape(bp, 1, d)
    sm = lambda m: jnp.repeat(m[bp:bp + bs], ts, axis=0).reshape(1, bs * ts, d)

    w_in16 = w_in[layer].astype(BF16)
    wo16 = w_out[layer].astype(BF16)
    wo_r16, wo_a16 = wo16[:RWKV_WIDTH], wo16[RWKV_WIDTH:]
    wr_t = router_w[layer].T
    rb = router_b[layer].reshape(N_EXPERTS, 1)
    wu = exp_w_up.reshape(exp_w_up.shape[1:])
    wd = exp_w_down.reshape(exp_w_down.shape[1:])
    n_up = wu.shape[-1]
    bu = exp_b_up[layer].reshape(N_EXPERTS, n_up // GROUP, LANES, 2).transpose(0, 1, 3, 2).reshape(N_EXPERTS, n_up)

    pr_p, pa_p = _inproj(x_prompt, norm1_g[layer], pm(sh1), pm(sc1), w_in16, TM_PROJ)
    zero_prev = jnp.zeros((bp, 1, RWKV_PROJ), F32)
    zero_state = jnp.zeros((bp, RWKV_WIDTH // GROUP, GROUP, GROUP), F32)
    yr_p, s_p = _rwkv(pr_p, zero_prev, zero_state, lp, RWKV_SUB, None)
    zero_kv = jnp.zeros((bp, WINDOW, KV_WIDTH), F32)
    ya_p, kn_p = _swa(pa_p, zero_kv, zero_kv, rel_bias, sinks[layer], q_norm[layer], k_norm[layer],
                      CHUNK, SWA_QC, False)

    ns = bs * ts
    xs_flat = x_sample.reshape(1, ns, d)
    pr_s, pa_s = _inproj(xs_flat, norm1_g[layer], sm(sh1), sm(sc1), w_in16, ns)
    pr_s = pr_s.reshape(bs, ts, RWKV_PROJ)
    pa_s = pa_s.reshape(bs, ts, ATTN_PROJ)
    t_pad = -(-ts // CHUNK) * CHUNK
    pr_s_pad = jnp.pad(pr_s, ((0, 0), (0, t_pad - ts), (0, 0)))
    yr_s, s_s = _rwkv(pr_s_pad, state_shift[layer].reshape(bs, 1, RWKV_PROJ), _state_to_bd(state_wkv[layer]),
                      lp, t_pad // CHUNK, ts)
    yr_s = yr_s[:, :ts]
    ya_s, kn_s = _swa(pa_s, cache_win_k[layer].reshape(bs, WINDOW, KV_WIDTH),
                      cache_win_v[layer].reshape(bs, WINDOW, KV_WIDTH), rel_bias, sinks[layer],
                      q_norm[layer], k_norm[layer], ts, 1, True)

    base0 = jnp.zeros((N_EXPERTS, 1), F32)
    x1_p, h2_p, te_p, gate_p, pos_p, cnt_p = _mid(yr_p, ya_p, x_prompt, pm(gt1), pm(sh2), pm(sc2), norm2_g[layer],
                                                  wo_r16, wo_a16, wr_t, rb, base0, TM_PROJ)
    x1_s, h2_s, te_s, gate_s, pos_s, cnt = _mid(yr_s.reshape(1, ns, RWKV_WIDTH), ya_s.reshape(1, ns, ATTN_WIDTH),
                                                xs_flat, sm(gt1), sm(sh2), sm(sc2), norm2_g[layer],
                                                wo_r16, wo_a16, wr_t, rb, cnt_p, ns)

    n_p = bp * tp
    n_rows = (n_p + ns) * TOP_K + N_EXPERTS * (TM_FFN - 1)
    n_tiles = -(-n_rows // TM_FFN)
    pstart, tile_e, n_used, next_e, pads = _tile_plan(cnt, TM_FFN, n_tiles)
    dest = _slot_rows(jnp.concatenate([te_p, te_s], axis=1), jnp.concatenate([pos_p, pos_s], axis=1), pstart)
    dest_p, dest_s = dest[:, :n_p], dest[:, n_p:]
    xs_buf = _dispatch(dest, pads, h2_p.reshape(n_p, d), h2_s.reshape(ns, d), n_tiles * TM_FFN, TK_DISPATCH)
    ys_buf = _ffn(tile_e, n_used, next_e, xs_buf, wu, bu, wd, exp_b_down[layer], TM_FFN)
    y_p = _combine(dest_p, gate_p, x1_p, pm(gt2), ys_buf, TK_COMBINE)
    y_s = _combine(dest_s, gate_s, x1_s, sm(gt2), ys_buf, TK_COMBINE)

    kv4 = lambda z, rows_: z.reshape(z.shape[0], rows_, KV_HEADS, HEAD)[None]
    v_p = pa_p[:, tp - WINDOW:, ATTN_WIDTH + KV_WIDTH:]
    v_s = pa_s[:, :, ATTN_WIDTH + KV_WIDTH:]
    return (y_p, y_s.reshape(bs, ts, d),
            pr_p[:, tp - 1][None], _state_from_bd(s_p)[None], kv4(kn_p[:, tp - WINDOW:], WINDOW), kv4(v_p, WINDOW),
            pr_s[:, ts - 1][None], _state_from_bd(s_s)[None], kv4(kn_s, ts), kv4(v_s, ts))
```

```python
import functools
import math

import jax
import jax.numpy as jnp
from jax import lax
from jax.experimental import pallas as pl
from jax.experimental.pallas import tpu as pltpu

F32 = jnp.float32
BF16 = jnp.bfloat16
I32 = jnp.int32

HEAD = 64
RWKV_HEADS = 8
RWKV_WIDTH = RWKV_HEADS * HEAD
DECAY_RANK = 64
ICLR_RANK = 64
GATE_RANK = 128
RWKV_PROJ = 3 * RWKV_WIDTH + DECAY_RANK + ICLR_RANK + GATE_RANK
ATTN_HEADS = 8
KV_HEADS = 2
ATTN_WIDTH = ATTN_HEADS * HEAD
KV_WIDTH = KV_HEADS * HEAD
ATTN_PROJ = ATTN_WIDTH + 2 * KV_WIDTH
WINDOW = 128
CHUNK = 64
REL_BUCKETS = 32
REL_MAX_DIST = 128
N_EXPERTS = 32
TOP_K = 4
GN_EPS = 64e-5
RMS_EPS = 1e-6
NEG_INF = -1e30
ATTN_SCALE = HEAD ** -0.5
SWIGLU_ALPHA = 1.702
SWIGLU_LIMIT = 7.0

LANES = 128
GROUP = 256
VMEM_LIMIT = 56 * 1024 * 1024

TM_PROJ = 512
RWKV_SUB = 8
SWA_QC = 8
TM_FFN = 512
TK_DISPATCH = 512
TK_COMBINE = 256
ROW_UNROLL = 8


def _cparams(*sem):
    return pltpu.CompilerParams(dimension_semantics=sem, vmem_limit_bytes=VMEM_LIMIT)


def _dot(a, b, nt=False):
    dims = (((1,), (1,)), ((), ())) if nt else (((1,), (0,)), ((), ()))
    return lax.dot_general(a, b, dims, preferred_element_type=F32)


def _split(x):
    hi = x.astype(BF16)
    lo = (x - hi.astype(F32)).astype(BF16)
    return hi, lo


def _mm(a, b, passes=1, nt=False):
    if passes == 1:
        return _dot(a.astype(BF16), b.astype(BF16), nt)
    ah, al = _split(a)
    bh, bl = _split(b)
    return _dot(ah, bh, nt) + _dot(al, bh, nt) + _dot(ah, bl, nt)


def _iota(shape, dim):
    return lax.broadcasted_iota(I32, shape, dim)


def _head_ones(n):
    r = lax.shift_right_logical(_iota((n, n), 0), 6)
    c = lax.shift_right_logical(_iota((n, n), 1), 6)
    return (r == c).astype(BF16)


def _head_sum(x):
    width = x.shape[1]
    x16 = x.astype(BF16)
    if width <= GROUP:
        return _dot(x16, _head_ones(width))
    ones = _head_ones(GROUP)
    parts = [_dot(x16[:, g:g + GROUP], ones) for g in range(0, width, GROUP)]
    return jnp.concatenate(parts, axis=1)


def _rms(x, eps):
    return x * lax.rsqrt(jnp.mean(x * x, axis=-1, keepdims=True) + eps)


def _mod_kernel(c_ref, w_ref, b_ref, o_ref):
    c = c_ref[...]
    s = c * jax.nn.sigmoid(c)
    o_ref[...] = _mm(s, w_ref[...], passes=3) + b_ref[...]


def _modulation(c_all, w_ada, b_ada):
    rows, d = c_all.shape
    n = w_ada.shape[1]
    tn = 1536
    return pl.pallas_call(
        _mod_kernel,
        grid=(n // tn,),
        in_specs=[pl.BlockSpec((rows, d), lambda j: (0, 0)),
                  pl.BlockSpec((d, tn), lambda j: (0, j)),
                  pl.BlockSpec((1, tn), lambda j: (0, j))],
        out_specs=pl.BlockSpec((rows, tn), lambda j: (0, j)),
        out_shape=jax.ShapeDtypeStruct((rows, n), F32),
        compiler_params=_cparams("arbitrary"),
        name="modulation",
    )(c_all, w_ada, b_ada.reshape(1, n))


def _inproj_kernel(x_ref, g_ref, sh_ref, sc_ref, w_ref, pr_ref, pa_ref):
    x = x_ref[...]
    h = _rms(x, RMS_EPS) * g_ref[...] * (1.0 + sc_ref[...]) + sh_ref[...]
    p = _dot(h.astype(BF16), w_ref[...])
    pr_ref[...] = p[:, :RWKV_PROJ]
    pa_ref[...] = p[:, RWKV_PROJ:]


def _inproj(x3, gain, sh3, sc3, w_in16, tm):
    g, r, d = x3.shape
    mr = sh3.shape[1]
    mrow = (lambda b, i: (b, 0, 0)) if mr == 1 else (lambda b, i: (b, i, 0))
    mblk = 1 if mr == 1 else tm
    return pl.pallas_call(
        _inproj_kernel,
        grid=(g, r // tm),
        in_specs=[pl.BlockSpec((None, tm, d), lambda b, i: (b, i, 0)),
                  pl.BlockSpec((1, d), lambda b, i: (0, 0)),
                  pl.BlockSpec((None, mblk, d), mrow),
                  pl.BlockSpec((None, mblk, d), mrow),
                  pl.BlockSpec(w_in16.shape, lambda b, i: (0, 0))],
        out_specs=[pl.BlockSpec((None, tm, RWKV_PROJ), lambda b, i: (b, i, 0)),
                   pl.BlockSpec((None, tm, ATTN_PROJ), lambda b, i: (b, i, 0))],
        out_shape=[jax.ShapeDtypeStruct((g, r, RWKV_PROJ), F32),
                   jax.ShapeDtypeStruct((g, r, ATTN_PROJ), F32)],
        compiler_params=_cparams("arbitrary", "arbitrary"),
        name="inproj",
    )(x3, gain.reshape(1, d), sh3, sc3, w_in16)


def _softplus(x):
    return jnp.maximum(x, 0.0) + jnp.log(1.0 + jnp.exp(-jnp.abs(x)))


def _rwkv_kernel(p_ref, prev0_ref, s0_ref, mu_ref, w0_ref, wwa_ref, a0_ref, gup_ref, kk_ref, ka_ref,
                 rk_ref, lng_ref, lnb_ref, y_ref, sout_ref, prev_sc, s_sc, *, nsub, t_valid):
    tc = nsub * CHUNK
    c = pl.program_id(1)

    @pl.when(c == 0)
    def _():
        prev_sc[...] = prev0_ref[...]
        s_sc[...] = s0_ref[...]

    p = p_ref[...]
    row = _iota((tc, 1), 0)
    p_prev = jnp.where(row == 0, prev_sc[...], pltpu.roll(p, 1, axis=0))
    prev_sc[...] = p_ref[tc - 1:tc, :]
    xp = p + (p_prev - p) * mu_ref[...]

    w3 = 3 * RWKV_WIDTH
    r = xp[:, 0:RWKV_WIDTH]
    k = xp[:, RWKV_WIDTH:2 * RWKV_WIDTH]
    v = xp[:, 2 * RWKV_WIDTH:w3]
    z = xp[:, w3:w3 + DECAY_RANK + ICLR_RANK]
    gd = xp[:, w3 + DECAY_RANK + ICLR_RANK:]
    zl = _iota((1, DECAY_RANK + ICLR_RANK), 1)
    zz = jnp.where(zl < DECAY_RANK, jnp.tanh(z), z)
    wa = _mm(zz, wwa_ref[...], passes=3)
    w_log = -_softplus(-(w0_ref[...] + wa[:, :RWKV_WIDTH])) - 0.5
    lw = -jnp.exp(w_log)
    a = jax.nn.sigmoid(a0_ref[...] + wa[:, RWKV_WIDTH:])
    g = _mm(jax.nn.sigmoid(gd), gup_ref[...])
    kk = k * kk_ref[...]
    kk = kk * lax.rsqrt(jnp.maximum(_head_sum(kk * kk), 1e-24))
    k2 = k * (1.0 + (a - 1.0) * ka_ref[...])
    bvec = kk * a
    if t_valid is not None:
        ok = (c * tc + row) < t_valid
        lw = jnp.where(ok, lw, 0.0)
        kk = jnp.where(ok, kk, 0.0)
        bvec = jnp.where(ok, bvec, 0.0)
        k2u = jnp.where(ok, k2, 0.0)
        vu = jnp.where(ok, v, 0.0)
    else:
        k2u, vu = k2, v

    tril16 = (_iota((CHUNK, CHUNK), 0) >= _iota((CHUNK, CHUNK), 1)).astype(BF16)
    bd_mask = (lax.shift_right_logical(_iota((GROUP, GROUP), 0), 6)
               == lax.shift_right_logical(_iota((GROUP, GROUP), 1), 6))
    t_idx = _iota((CHUNK, GROUP), 0)
    s_idx = jnp.bitwise_and(_iota((CHUNK, GROUP), 1), CHUNK - 1)
    strict = t_idx > s_idx
    incl = t_idx >= s_idx
    eye = (t_idx == s_idx).astype(F32)

    bd16 = bd_mask.astype(BF16)

    def bd(x):
        x16 = x.astype(BF16)
        return jnp.concatenate([x16, x16, x16, x16], axis=0) * bd16

    def b16(x):
        return x.astype(BF16)

    ng = RWKV_WIDTH // GROUP
    pairs = [(j, gi) for j in range(nsub) for gi in range(ng)]
    gsl = lambda gi: slice(gi * GROUP, (gi + 1) * GROUP)

    at, bt, kt, rt, bw, kw, vj, w_last = [], [], [], [], [], [], [], []
    for j in range(nsub):
        sl = slice(j * CHUNK, (j + 1) * CHUNK)
        lwj = lw[sl]
        h1 = lwj.astype(BF16)
        r1 = lwj - h1.astype(F32)
        h2 = r1.astype(BF16)
        h3 = (r1 - h2.astype(F32)).astype(BF16)
        cum = _dot(tril16, h1) + _dot(tril16, h2) + _dot(tril16, h3)
        e_cum = jnp.exp(cum)
        e_inv = jnp.exp(-cum)
        wl = e_cum[CHUNK - 1:CHUNK, :]
        at.append(-kk[sl] * jnp.exp(cum - lwj))
        bt.append(bvec[sl] * e_inv)
        kt.append(k2u[sl] * e_inv)
        rt.append(r[sl] * e_cum)
        bw.append(bt[j] * wl)
        kw.append(kt[j] * wl)
        vj.append(vu[sl])
        w_last.append(wl)

    a_ab, a_ak, a_rb, a_rk, vbd = {}, {}, {}, {}, {}
    for (j, gi) in pairs:
        gs = gsl(gi)
        q2 = b16(jnp.concatenate([at[j][:, gs], rt[j][:, gs]], axis=0))
        sc = _dot(q2, jnp.concatenate([bd(bt[j][:, gs]), bd(kt[j][:, gs])], axis=0), nt=True)
        a_ab[j, gi] = jnp.where(strict, sc[:CHUNK, :GROUP], 0.0)
        a_ak[j, gi] = jnp.where(strict, sc[:CHUNK, GROUP:], 0.0)
        a_rb[j, gi] = jnp.where(incl, sc[CHUNK:, :GROUP], 0.0)
        a_rk[j, gi] = jnp.where(incl, sc[CHUNK:, GROUP:], 0.0)
        vbd[j, gi] = bd(vj[j][:, gs])

    tinv = {p: eye + a_ab[p] for p in pairs}
    xpow = {p: _dot(b16(a_ab[p]), bd(a_ab[p])) for p in pairs}
    for step in range(1, 6):
        for p in pairs:
            if step < 5:
                both = _dot(b16(xpow[p]), jnp.concatenate([bd(tinv[p]), bd(xpow[p])], axis=1))
                tinv[p] = tinv[p] + both[:, :GROUP]
                xpow[p] = both[:, GROUP:]
            else:
                tinv[p] = tinv[p] + _dot(b16(xpow[p]), bd(tinv[p]))

    akv = {p: _dot(b16(a_ak[p]), vbd[p]) for p in pairs}
    pq = {(j, gi): _dot(b16(tinv[j, gi]), jnp.concatenate([bd(at[j][:, gsl(gi)]), bd(akv[j, gi])], axis=1))
          for (j, gi) in pairs}
    r2, y0, m_lr, n_add = {}, {}, {}, {}
    for (j, gi) in pairs:
        gs = gsl(gi)
        pm, qm = pq[j, gi][:, :GROUP], pq[j, gi][:, GROUP:]
        rr = _dot(b16(a_rb[j, gi]), jnp.concatenate([bd(pm), bd(qm)], axis=1))
        r2[j, gi] = b16(rt[j][:, gs] + rr[:, :GROUP])
        y0[j, gi] = rr[:, GROUP:] + _dot(b16(a_rk[j, gi]), vbd[j, gi])
        m_lr[j, gi] = b16(jnp.where(bd_mask, _dot(b16(pm.T), b16(bw[j][:, gs])), 0.0))
        qv = jnp.concatenate([qm, vj[j][:, gs]], axis=0)
        bk = jnp.concatenate([bw[j][:, gs], kw[j][:, gs]], axis=0)
        n_add[j, gi] = jnp.where(bd_mask, _dot(b16(qv.T), b16(bk)), 0.0)

    s_in = {}
    for gi in range(ng):
        s = s_sc[gi]
        for j in range(nsub):
            s16 = b16(s)
            s_in[j, gi] = s16
            s = s * w_last[j][:, gsl(gi)] + _dot(s16, m_lr[j, gi]) + n_add[j, gi]
        s_sc[gi] = s

    ys = []
    for j in range(nsub):
        yg = [_dot(r2[j, gi], s_in[j, gi], nt=True) + y0[j, gi] for gi in range(ng)]
        ys.append(jnp.concatenate(yg, axis=1))
    y = jnp.concatenate(ys, axis=0) if nsub > 1 else ys[0]

    inv_n = 1.0 / HEAD
    yc = y - _head_sum(y) * inv_n
    yn = yc * lax.rsqrt(_head_sum(yc * yc) * inv_n + GN_EPS)
    yn = yn * lng_ref[...] + lnb_ref[...]
    bonus = _head_sum(r * k2 * rk_ref[...]) * v
    y_ref[...] = (yn + bonus) * g
    sout_ref[...] = s_sc[...]


def _rwkv(p_r, prev0, s0_bd, lp, nsub, t_valid):
    b, t, _ = p_r.shape
    tc = nsub * CHUNK
    ng = RWKV_WIDTH // GROUP
    row = lambda a: a.reshape(1, -1)
    zeros = jnp.zeros((DECAY_RANK, RWKV_WIDTH), F32)
    wwa = jnp.concatenate([jnp.concatenate([lp["rwkv_w_up"], zeros], axis=1),
                           jnp.concatenate([zeros, lp["rwkv_a_up"]], axis=1)], axis=0)
    const = lambda shape: pl.BlockSpec(shape, lambda bi, ci: (0,) * len(shape))
    kern = functools.partial(_rwkv_kernel, nsub=nsub, t_valid=t_valid)
    return pl.pallas_call(
        kern,
        grid=(b, t // tc),
        in_specs=[pl.BlockSpec((None, tc, RWKV_PROJ), lambda bi, ci: (bi, ci, 0)),
                  pl.BlockSpec((None, 1, RWKV_PROJ), lambda bi, ci: (bi, 0, 0)),
                  pl.BlockSpec((None, ng, GROUP, GROUP), lambda bi, ci: (bi, 0, 0, 0)),
                  const((1, RWKV_PROJ)), const((1, RWKV_WIDTH)),
                  const((DECAY_RANK + ICLR_RANK, 2 * RWKV_WIDTH)), const((1, RWKV_WIDTH)),
                  const((GATE_RANK, RWKV_WIDTH)), const((1, RWKV_WIDTH)), const((1, RWKV_WIDTH)),
                  const((1, RWKV_WIDTH)), const((1, RWKV_WIDTH)), const((1, RWKV_WIDTH))],
        out_specs=[pl.BlockSpec((None, tc, RWKV_WIDTH), lambda bi, ci: (bi, ci, 0)),
                   pl.BlockSpec((None, ng, GROUP, GROUP), lambda bi, ci: (bi, 0, 0, 0))],
        out_shape=[jax.ShapeDtypeStruct((b, t, RWKV_WIDTH), F32),
                   jax.ShapeDtypeStruct((b, ng, GROUP, GROUP), F32)],
        scratch_shapes=[pltpu.VMEM((1, RWKV_PROJ), F32), pltpu.VMEM((ng, GROUP, GROUP), F32)],
        compiler_params=_cparams("arbitrary", "arbitrary"),
        name="rwkv7",
    )(p_r, prev0, s0_bd, row(lp["rwkv_mu"]), row(lp["rwkv_w0"]), wwa, row(lp["rwkv_a0"]),
      lp["rwkv_g_up"], row(lp["rwkv_k_k"]), row(lp["rwkv_k_a"]), row(lp["rwkv_r_k"]),
      row(lp["rwkv_ln_g"]), row(lp["rwkv_ln_b"]))


def _state_to_bd(s):
    b = s.shape[0]
    hg = GROUP // HEAD
    s5 = s.reshape(b, RWKV_HEADS // hg, hg, HEAD, HEAD)
    out = jnp.einsum("bghvk,hj->bghvjk", s5, jnp.eye(hg, dtype=s.dtype))
    return out.reshape(b, RWKV_HEADS // hg, GROUP, GROUP)


def _state_from_bd(s_bd):
    b = s_bd.shape[0]
    hg = GROUP // HEAD
    s6 = s_bd.reshape(b, RWKV_HEADS // hg, hg, HEAD, hg, HEAD)
    out = jnp.einsum("bghvjk,hj->bghvk", s6, jnp.eye(hg, dtype=s_bd.dtype))
    return out.reshape(b, RWKV_HEADS, HEAD, HEAD)


def _swa_kernel(tab_ref, sink_ref, pa_ref, ck_ref, cv_ref, bkt_ref, qg_ref, kg_ref,
                ya_ref, kn_ref, kall, vall, bias_sc, *, cq, nqc, has_cache):
    tq = cq * nqc
    band = WINDOW + cq
    bi = pl.program_id(0)
    i = pl.program_id(1)
    lane = _iota((1, LANES), 1)
    low = lane < HEAD

    @pl.when(jnp.logical_and(bi == 0, i == 0))
    def _():
        bkt = bkt_ref[...]
        for h in range(ATTN_HEADS):
            def body(j, acc, h=h):
                return acc + jnp.where(bkt == j, tab_ref[j, h], 0.0)
            bh = lax.fori_loop(0, REL_BUCKETS, body, jnp.zeros((cq, band), F32))
            bias_sc[h // 2, (h % 2) * cq:(h % 2 + 1) * cq, :] = bh

    @pl.when(i == 0)
    def _():
        kall[0:WINDOW, :] = ck_ref[...]
        vall[0:WINDOW, :] = cv_ref[...]

    pa = pa_ref[...]
    q = pa[:, :ATTN_WIDTH]
    k = pa[:, ATTN_WIDTH:ATTN_WIDTH + KV_WIDTH]
    v = pa[:, ATTN_WIDTH + KV_WIDTH:]
    inv_n = 1.0 / HEAD
    qn = q * lax.rsqrt(_head_sum(q * q) * inv_n + RMS_EPS) * qg_ref[...]
    kn = k * lax.rsqrt(_head_sum(k * k) * inv_n + RMS_EPS) * kg_ref[...]
    kn_ref[...] = kn
    kall[WINDOW:WINDOW + tq, :] = kn
    vall[WINDOW:WINDOW + tq, :] = v

    row2 = _iota((2 * cq, 1), 0)
    col = _iota((1, band), 1)
    thr = jnp.where(i == 0, WINDOW, 0)
    k_all = kall[...]
    v_all = vall[...]
    k_rot = pltpu.roll(k_all, HEAD, axis=1)
    v_rot = pltpu.roll(v_all, HEAD, axis=1)
    kd = [jnp.where(low, k_all, k_rot).astype(BF16), jnp.where(low, k_rot, k_all).astype(BF16)]
    vd = [jnp.where(low, v_all, v_rot).astype(BF16), jnp.where(low, v_rot, v_all).astype(BF16)]

    work = [(qc, pi) for qc in range(nqc) for pi in range(ATTN_HEADS // 2)]
    kv_of = lambda pi: (2 * pi) // (ATTN_HEADS // KV_HEADS)
    scores = {}
    for (qc, pi) in work:
        qp = qn[qc * cq:(qc + 1) * cq, pi * LANES:(pi + 1) * LANES]
        qs = jnp.concatenate([jnp.where(low, qp, 0.0), jnp.where(low, 0.0, qp)], axis=0)
        s = _dot(qs.astype(BF16), kd[kv_of(pi)][qc * cq:qc * cq + band], nt=True) * ATTN_SCALE + bias_sc[pi]
        if (not has_cache) and qc * cq < WINDOW:
            s = jnp.where((col + qc * cq) < thr, NEG_INF, s)
        scores[qc, pi] = s
    probs = {}
    for (qc, pi) in work:
        s = scores[qc, pi]
        sink = jnp.where(row2 < cq, sink_ref[0, 2 * pi], sink_ref[0, 2 * pi + 1])
        m = jnp.maximum(jnp.max(s, axis=-1, keepdims=True), sink)
        e = jnp.exp(s - m)
        den = jnp.sum(e, axis=-1, keepdims=True) + jnp.exp(sink - m)
        probs[qc, pi] = (e / den).astype(BF16)
    for (qc, pi) in work:
        o = _dot(probs[qc, pi], vd[kv_of(pi)][qc * cq:qc * cq + band])
        ya_ref[qc * cq:(qc + 1) * cq, pi * LANES:(pi + 1) * LANES] = jnp.where(low, o[:cq], o[cq:])

    if tq >= WINDOW:
        knext = kall[tq:tq + WINDOW, :]
        vnext = vall[tq:tq + WINDOW, :]
        kall[0:WINDOW, :] = knext
        vall[0:WINDOW, :] = vnext


def _t5_bucket(rel):
    half = REL_BUCKETS // 2
    max_exact = half // 2
    n = jnp.abs(rel)
    log_ratio = jnp.log(jnp.maximum(n, 1).astype(F32) / max_exact) / math.log(REL_MAX_DIST / max_exact)
    large = jnp.minimum(max_exact + (log_ratio * (half - max_exact)).astype(I32), half - 1)
    return jnp.where(rel > 0, half, 0) + jnp.where(n < max_exact, n, large)


def _swa(p_a, cache_k, cache_v, rel_table, sinks, q_norm, k_norm, cq, nqc, has_cache):
    b, t, _ = p_a.shape
    tq = cq * nqc
    band = WINDOW + cq
    rel = (jnp.arange(band) - WINDOW)[None, :] - jnp.arange(cq)[:, None]
    bkt = _t5_bucket(rel).astype(I32)
    qg = jnp.tile(q_norm, ATTN_HEADS).reshape(1, ATTN_WIDTH)
    kg = jnp.tile(k_norm, KV_HEADS).reshape(1, KV_WIDTH)
    kern = functools.partial(_swa_kernel, cq=cq, nqc=nqc, has_cache=has_cache)
    smem = pl.BlockSpec(memory_space=pltpu.SMEM)
    return pl.pallas_call(
        kern,
        grid=(b, t // tq),
        in_specs=[smem, smem,
                  pl.BlockSpec((None, tq, ATTN_PROJ), lambda bi, i: (bi, i, 0)),
                  pl.BlockSpec((None, WINDOW, KV_WIDTH), lambda bi, i: (bi, 0, 0)),
                  pl.BlockSpec((None, WINDOW, KV_WIDTH), lambda bi, i: (bi, 0, 0)),
                  pl.BlockSpec((cq, band), lambda bi, i: (0, 0)),
                  pl.BlockSpec((1, ATTN_WIDTH), lambda bi, i: (0, 0)),
                  pl.BlockSpec((1, KV_WIDTH), lambda bi, i: (0, 0))],
        out_specs=[pl.BlockSpec((None, tq, ATTN_WIDTH), lambda bi, i: (bi, i, 0)),
                   pl.BlockSpec((None, tq, KV_WIDTH), lambda bi, i: (bi, i, 0))],
        out_shape=[jax.ShapeDtypeStruct((b, t, ATTN_WIDTH), F32),
                   jax.ShapeDtypeStruct((b, t, KV_WIDTH), F32)],
        scratch_shapes=[pltpu.VMEM((WINDOW + tq, KV_WIDTH), F32),
                        pltpu.VMEM((WINDOW + tq, KV_WIDTH), F32),
                        pltpu.VMEM((ATTN_HEADS // 2, 2 * cq, band), F32)],
        compiler_params=_cparams("arbitrary", "arbitrary"),
        name="swa",
    )(rel_table, sinks.reshape(1, ATTN_HEADS), p_a, cache_k, cache_v, bkt, qg, kg)


def _mid_kernel(yr_ref, ya_ref, x_ref, gt1_ref, sh2_ref, sc2_ref, g2_ref, wor_ref, woa_ref, wrt_ref,
                rb_ref, base0_ref, x1_ref, h2_ref, te_ref, gate_ref, pos_ref, cnt_ref, base_sc, *, tm):
    @pl.when(jnp.logical_and(pl.program_id(0) == 0, pl.program_id(1) == 0))
    def _():
        base_sc[...] = base0_ref[...]

    mix = _dot(yr_ref[...].astype(BF16), wor_ref[...]) + _dot(ya_ref[...].astype(BF16), woa_ref[...])
    x1 = x_ref[...] + gt1_ref[...] * mix
    x1_ref[...] = x1
    h2 = _rms(x1, RMS_EPS) * g2_ref[...] * (1.0 + sc2_ref[...]) + sh2_ref[...]
    h2_ref[...] = h2

    logits = _mm(wrt_ref[...], h2, passes=3, nt=True) + rb_ref[...]
    eidx = _iota((N_EXPERTS, tm), 0)
    work = logits
    tops, hots = [], []
    for j in range(TOP_K):
        m = jnp.max(work, axis=0, keepdims=True)
        idx = jnp.min(jnp.where(work == m, eidx, N_EXPERTS), axis=0, keepdims=True)
        hot = eidx == idx
        work = jnp.where(hot, -jnp.inf, work)
        tops.append(m)
        hots.append(hot)
        te_ref[j:j + 1, :] = idx
    exps = [jnp.exp(tj - tops[0]) for tj in tops]
    den = exps[0] + exps[1] + exps[2] + exps[3]
    for j in range(TOP_K):
        gate_ref[j:j + 1, :] = exps[j] / den

    member = (hots[0] | hots[1] | hots[2] | hots[3]).astype(BF16)
    upper = (_iota((tm, tm), 0) < _iota((tm, tm), 1)).astype(BF16)
    before = _dot(member, upper) + base_sc[...]
    for j in range(TOP_K):
        pos_ref[j:j + 1, :] = jnp.sum(jnp.where(hots[j], before, 0.0), axis=0, keepdims=True).astype(I32)
    base_sc[...] = base_sc[...] + jnp.sum(member.astype(F32), axis=1, keepdims=True)
    cnt_ref[...] = base_sc[...]


def _mid(y_r, y_a, x3, gt1, sh2, sc2, gain2, wo_r16, wo_a16, wr_t, rb, base0, tm):
    g, r, d = x3.shape
    n = g * r
    per = r // tm
    mr = gt1.shape[1]
    mrow = (lambda b, i: (b, 0, 0)) if mr == 1 else (lambda b, i: (b, i, 0))
    mblk = 1 if mr == 1 else tm
    tok = lambda w: pl.BlockSpec((None, tm, w), lambda b, i: (b, i, 0))
    mod = pl.BlockSpec((None, mblk, d), mrow)
    const = lambda shape: pl.BlockSpec(shape, lambda b, i: (0,) * len(shape))
    lane_out = pl.BlockSpec((TOP_K, tm), lambda b, i: (0, b * per + i))
    kern = functools.partial(_mid_kernel, tm=tm)
    return pl.pallas_call(
        kern,
        grid=(g, per),
        in_specs=[tok(RWKV_WIDTH), tok(ATTN_WIDTH), tok(d), mod, mod, mod, const((1, d)),
                  const((RWKV_WIDTH, d)), const((ATTN_WIDTH, d)), const((N_EXPERTS, d)),
                  const((N_EXPERTS, 1)), const((N_EXPERTS, 1))],
        out_specs=[tok(d), tok(d), lane_out, lane_out, lane_out, const((N_EXPERTS, 1))],
        out_shape=[jax.ShapeDtypeStruct((g, r, d), F32), jax.ShapeDtypeStruct((g, r, d), F32),
                   jax.ShapeDtypeStruct((TOP_K, n), I32), jax.ShapeDtypeStruct((TOP_K, n), F32),
                   jax.ShapeDtypeStruct((TOP_K, n), I32), jax.ShapeDtypeStruct((N_EXPERTS, 1), F32)],
        scratch_shapes=[pltpu.VMEM((N_EXPERTS, 1), F32)],
        compiler_params=_cparams("arbitrary", "arbitrary"),
        name="mid",
    )(y_r, y_a, x3, gt1, sh2, sc2, gain2.reshape(1, d), wo_r16, wo_a16, wr_t, rb, base0)


def _slot_kernel(pstart_ref, te_ref, pos_ref, o_ref):
    te = te_ref[...]
    acc = pos_ref[...]
    for e in range(N_EXPERTS):
        acc = acc + jnp.where(te == e, pstart_ref[e], 0)
    o_ref[...] = acc


def _slot_rows(te, pos, pstart):
    k, n = te.shape
    return pl.pallas_call(
        _slot_kernel,
        grid=(1,),
        in_specs=[pl.BlockSpec(memory_space=pltpu.SMEM),
                  pl.BlockSpec((k, n), lambda i: (0, 0)), pl.BlockSpec((k, n), lambda i: (0, 0))],
        out_specs=pl.BlockSpec((k, n), lambda i: (0, 0)),
        out_shape=jax.ShapeDtypeStruct((k, n), I32),
        compiler_params=_cparams("arbitrary"),
        name="slot_rows",
    )(pstart, te, pos)


def _dispatch_kernel(dst_ref, pad_ref, h2a_ref, h2b_ref, xs_ref, ztile, sem, zsem, *, tk, steps_a):
    i = pl.program_id(0)

    @pl.when(i == 0)
    def _():
        ztile[...] = jnp.zeros_like(ztile)
        zrow = ztile.at[pl.ds(0, 1)]
        tm = ztile.shape[0]

        def zero_row(row, priority):
            pltpu.make_async_copy(zrow, xs_ref.at[pl.ds(row, 1)], zsem).start(priority=priority)

        def per_expert(e, total):
            first = pad_ref[0, e]
            count = pad_ref[1, e]

            def per_pair(r, c):
                zero_row(first + 2 * r, 0)
                zero_row(first + 2 * r + 1, 1)
                return c
            lax.fori_loop(0, count // 2, per_pair, 0)

            @pl.when(count % 2 == 1)
            def _():
                zero_row(first + count - 1, 0)
            return total + count
        n_pad = lax.fori_loop(0, N_EXPERTS, per_expert, 0)

        blk = 64
        def drain_blk(r, c):
            pltpu.make_async_copy(ztile.at[pl.ds(0, blk)], xs_ref.at[pl.ds(0, blk)], zsem).wait()
            return c
        lax.fori_loop(0, n_pad // blk, drain_blk, 0)

        def drain_row(r, c):
            pltpu.make_async_copy(zrow, xs_ref.at[pl.ds(0, 1)], zsem).wait()
            return c
        lax.fori_loop(0, n_pad % blk, drain_row, 0)

        def spare(t, c):
            row0 = pl.multiple_of(t * tm, tm)
            pltpu.make_async_copy(ztile, xs_ref.at[pl.ds(row0, tm)], zsem).start()
            return c
        lax.fori_loop(pad_ref[2, 0], pad_ref[2, 1], spare, 0)

        def spare_wait(t, c):
            pltpu.make_async_copy(ztile, xs_ref.at[pl.ds(0, tm)], zsem).wait()
            return c
        lax.fori_loop(pad_ref[2, 0], pad_ref[2, 1], spare_wait, 0)

    def scatter(h2_ref):
        for t in range(h2_ref.shape[0] * ROW_UNROLL):
            for j in range(TOP_K):
                dst = dst_ref[j, t]
                pltpu.make_async_copy(h2_ref.at[t // ROW_UNROLL, pl.ds(t % ROW_UNROLL, 1)],
                                      xs_ref.at[pl.ds(dst, 1)], sem).start(priority=j % 2)
        for _ in range(TOP_K):
            pltpu.make_async_copy(h2_ref, h2_ref, sem).wait()

    pl.when(i < steps_a)(functools.partial(scatter, h2a_ref))
    pl.when(i == steps_a)(functools.partial(scatter, h2b_ref))


def _dispatch(dest, pads, h2a, h2b, n_rows, tk):
    na, d = h2a.shape
    nb = h2b.shape[0]
    steps_a = na // tk
    dest = jnp.pad(dest, ((0, 0), (0, (steps_a + 1) * tk - na - nb)))
    h2a = h2a.reshape(na // ROW_UNROLL, ROW_UNROLL, d)
    h2b = h2b.reshape(nb // ROW_UNROLL, ROW_UNROLL, d)
    kern = functools.partial(_dispatch_kernel, tk=tk, steps_a=steps_a)
    return pl.pallas_call(
        kern,
        grid=(steps_a + 1,),
        in_specs=[pl.BlockSpec((TOP_K, tk), lambda i: (0, i), memory_space=pltpu.SMEM),
                  pl.BlockSpec(memory_space=pltpu.SMEM),
                  pl.BlockSpec((tk // ROW_UNROLL, ROW_UNROLL, d), lambda i: (jnp.minimum(i, steps_a - 1), 0, 0)),
                  pl.BlockSpec((nb // ROW_UNROLL, ROW_UNROLL, d), lambda i: (0, 0, 0))],
        out_specs=pl.BlockSpec(memory_space=pl.ANY),
        out_shape=jax.ShapeDtypeStruct((n_rows, d), F32),
        scratch_shapes=[pltpu.VMEM((TM_FFN, d), F32), pltpu.SemaphoreType.DMA(()), pltpu.SemaphoreType.DMA(())],
        compiler_params=_cparams("arbitrary"),
        name="dispatch",
    )(dest, pads, h2a, h2b)


def _ffn_kernel(te_ref, nu_ref, nx_ref, xs_ref, wu_hbm, bu_ref, wd_hbm, bd_ref, ys_ref,
                wu32, wd32, wu16, wd16, sem):
    i = pl.program_id(0)
    n_up = wu32.shape[1]
    active = i < nu_ref[0]
    fresh = jnp.logical_or(i == 0, te_ref[i] != te_ref[jnp.maximum(i - 1, 0)])

    def weight_copies(e):
        return (pltpu.make_async_copy(wu_hbm.at[e], wu32, sem.at[0]),
                pltpu.make_async_copy(wd_hbm.at[e], wd32, sem.at[1]))

    @pl.when(i == 0)
    def _():
        for cp in weight_copies(te_ref[0]):
            cp.start()

    @pl.when(jnp.logical_and(active, fresh))
    def _():
        for cp in weight_copies(te_ref[i]):
            cp.wait()
        src = _iota((GROUP, GROUP), 0)
        dst = _iota((GROUP, GROUP), 1)
        want = jnp.where(dst < LANES, 2 * dst, 2 * (dst - LANES) + 1)
        perm = (src == want).astype(BF16)
        for b in range(n_up // GROUP):
            cs = slice(b * GROUP, (b + 1) * GROUP)
            wu16[:, cs] = _dot(wu32[:, cs].astype(BF16), perm).astype(BF16)
        wd16[...] = wd32[...].astype(BF16)

        @pl.when(nx_ref[i] >= 0)
        def _():
            for cp in weight_copies(nx_ref[i]):
                cp.start()

    @pl.when(active)
    def _():
        u = _dot(xs_ref[...].astype(BF16), wu16[...]) + bu_ref[...]
        acts = []
        for b in range(n_up // GROUP):
            glu = jnp.minimum(u[:, b * GROUP:b * GROUP + LANES], SWIGLU_LIMIT)
            lin = jnp.clip(u[:, b * GROUP + LANES:(b + 1) * GROUP], -SWIGLU_LIMIT, SWIGLU_LIMIT)
            acts.append((glu * jax.nn.sigmoid(SWIGLU_ALPHA * glu) * (lin + 1.0)).astype(BF16))
        act = jnp.concatenate(acts, axis=1)
        ys_ref[...] = _dot(act, wd16[...]) + bd_ref[...]

    @pl.when(jnp.logical_not(active))
    def _():
        ys_ref[...] = jnp.zeros_like(ys_ref)


def _ffn(tile_e, n_used, next_e, xs, wu, bu, wd, bd_, tm):
    rows, d = xs.shape
    n_exp, _, n_up = wu.shape
    dff = wd.shape[1]
    row_blk = lambda i, te, nu, nx: (jnp.minimum(i, nu[0] - 1), 0)
    exp_blk = lambda i, te, nu, nx: (te[i], 0, 0)
    grid_spec = pltpu.PrefetchScalarGridSpec(
        num_scalar_prefetch=3,
        grid=(rows // tm,),
        in_specs=[pl.BlockSpec((tm, d), row_blk),
                  pl.BlockSpec(memory_space=pl.ANY),
                  pl.BlockSpec((None, 1, n_up), exp_blk),
                  pl.BlockSpec(memory_space=pl.ANY),
                  pl.BlockSpec((None, 1, d), exp_blk)],
        out_specs=pl.BlockSpec((tm, d), lambda i, te, nu, nx: (i, 0)),
        scratch_shapes=[pltpu.VMEM((d, n_up), F32), pltpu.VMEM((dff, d), F32),
                        pltpu.VMEM((d, n_up), BF16), pltpu.VMEM((dff, d), BF16),
                        pltpu.SemaphoreType.DMA((2,))],
    )
    return pl.pallas_call(
        _ffn_kernel,
        grid_spec=grid_spec,
        out_shape=jax.ShapeDtypeStruct((rows, d), F32),
        compiler_params=_cparams("arbitrary"),
        name="expert_ffn",
    )(tile_e, n_used, next_e, xs, wu, bu.reshape(n_exp, 1, n_up), wd, bd_.reshape(n_exp, 1, d))


def _combine_kernel(src_ref, srcn_ref, gate_ref, x1_ref, gt2_ref, ys_ref, o_ref, buf, sem, *, tk):
    i = pl.program_id(1)
    n = pl.num_programs(1)
    flat = pl.program_id(0) * n + i
    total = pl.num_programs(0) * n
    slot = lax.rem(flat, 2)

    def issue(rows_ref, s, t0, t1):
        for t in range(t0, t1):
            for j in range(TOP_K):
                src = rows_ref[j, t]
                pltpu.make_async_copy(ys_ref.at[pl.ds(src, 1)],
                                      buf.at[s, j, t // ROW_UNROLL, pl.ds(t % ROW_UNROLL, 1)],
                                      sem.at[s]).start(priority=j % 2)

    @pl.when(flat == 0)
    def _():
        issue(src_ref, 0, 0, tk)

    def step(cur):
        @pl.when(flat + 1 < total)
        def _():
            issue(srcn_ref, 1 - cur, 0, tk)

        for j in range(TOP_K):
            pltpu.make_async_copy(buf.at[cur, j], buf.at[cur, j], sem.at[cur]).wait()
        eye = (_iota((tk, tk), 0) == _iota((tk, tk), 1)).astype(BF16)
        g_hi, g_lo = _split(gate_ref[...])
        gate = _dot(eye, g_hi, nt=True) + _dot(eye, g_lo, nt=True)
        rows = lambda j: buf[cur, j].reshape(tk, buf.shape[-1])
        acc = gate[:, 0:1] * rows(0)
        for j in range(1, TOP_K):
            acc = acc + gate[:, j:j + 1] * rows(j)
        o_ref[...] = x1_ref[...] + gt2_ref[...] * acc

    for cur in range(2):
        pl.when(slot == cur)(functools.partial(step, cur))


def _combine(dest, gates, x1, gt2, ys, tk):
    g, r, d = x1.shape
    per = r // tk
    last = g * per - 1
    mr = gt2.shape[1]
    mrow = (lambda b, i: (b, 0, 0)) if mr == 1 else (lambda b, i: (b, i, 0))
    mblk = 1 if mr == 1 else tk
    cur = pl.BlockSpec((TOP_K, tk), lambda b, i: (0, b * per + i), memory_space=pltpu.SMEM)
    nxt = pl.BlockSpec((TOP_K, tk), lambda b, i: (0, jnp.minimum(b * per + i + 1, last)),
                       memory_space=pltpu.SMEM)
    kern = functools.partial(_combine_kernel, tk=tk)
    return pl.pallas_call(
        kern,
        grid=(g, per),
        in_specs=[cur, nxt,
                  pl.BlockSpec((TOP_K, tk), lambda b, i: (0, b * per + i)),
                  pl.BlockSpec((None, tk, d), lambda b, i: (b, i, 0)),
                  pl.BlockSpec((None, mblk, d), mrow),
                  pl.BlockSpec(memory_space=pl.ANY)],
        out_specs=pl.BlockSpec((None, tk, d), lambda b, i: (b, i, 0)),
        out_shape=jax.ShapeDtypeStruct((g, r, d), F32),
        scratch_shapes=[pltpu.VMEM((2, TOP_K, tk // ROW_UNROLL, ROW_UNROLL, d), F32),
                        pltpu.SemaphoreType.DMA((2,))],
        compiler_params=_cparams("arbitrary", "arbitrary"),
        name="combine",
    )(dest, dest, gates, x1, gt2, ys)


def _tile_plan(counts, tm, n_tiles):
    cnt = counts.reshape(-1).astype(I32)
    tiles = (cnt + tm - 1) // tm
    tile_end = jnp.cumsum(tiles)
    pstart = ((tile_end - tiles) * tm).astype(I32)
    n_used = tile_end[-1:].astype(I32)
    idx = jnp.minimum(jnp.arange(n_tiles, dtype=I32), n_used[0] - 1)
    tile_e = jnp.sum((tile_end[None, :] <= idx[:, None]).astype(I32), axis=1)
    spare = jnp.zeros((N_EXPERTS,), I32).at[0].set(n_used[0]).at[1].set(n_tiles)
    pads = jnp.stack([pstart + cnt, tiles * tm - cnt, spare]).astype(I32)
    tile_e = jnp.minimum(tile_e, N_EXPERTS - 1)
    end_of = tile_end[tile_e]
    next_e = jnp.where(end_of < n_used[0], tile_e[jnp.minimum(end_of, n_tiles - 1)], -1).astype(I32)
    return pstart, tile_e, n_used, next_e, pads


def kernel(x_prompt, x_sample, state_shift, state_wkv, cache_win_k, cache_win_v, c_prompt, c_sample, rel_bias, norm1_g, norm2_g, w_ada, b_ada, w_in, w_out, rwkv_mu, rwkv_w0, rwkv_w_up, rwkv_a0, rwkv_a_up, rwkv_g_up, rwkv_k_k, rwkv_k_a, rwkv_r_k, rwkv_ln_g, rwkv_ln_b, q_norm, k_norm, sinks, router_w, router_b, exp_w_up, exp_b_up, exp_w_down, exp_b_down):
    bp, tp, d = x_prompt.shape
    bs, ts, _ = x_sample.shape
    depth = w_in.shape[0]
    assert depth == 1, "single-layer trunk"
    layer = 0
    lp = dict(rwkv_mu=rwkv_mu[layer], rwkv_w0=rwkv_w0[layer], rwkv_w_up=rwkv_w_up[layer],
              rwkv_a0=rwkv_a0[layer], rwkv_a_up=rwkv_a_up[layer], rwkv_g_up=rwkv_g_up[layer],
              rwkv_k_k=rwkv_k_k[layer], rwkv_k_a=rwkv_k_a[layer], rwkv_r_k=rwkv_r_k[layer].reshape(-1),
              rwkv_ln_g=rwkv_ln_g[layer], rwkv_ln_b=rwkv_ln_b[layer])

    rows = bp + bs
    pad = (-rows) % 8
    c_all = jnp.concatenate([c_prompt, c_sample, jnp.zeros((pad, d), F32)], axis=0)
    mod = _modulation(c_all, w_ada[layer], b_ada[layer])
    sh1, sc1, gt1, sh2, sc2, gt2 = [mod[:, i * d:(i + 1) * d] for i in range(6)]
    pm = lambda m: m[:bp].reshape(bp, 1, d)
    sm = lambda m: jnp.repeat(m[bp:bp + bs], ts, axis=0).reshape(1, bs * ts, d)

    w_in16 = w_in[layer].astype(BF16)
    wo16 = w_out[layer].astype(BF16)
    wo_r16, wo_a16 = wo16[:RWKV_WIDTH], wo16[RWKV_WIDTH:]
    wr_t = router_w[layer].T
    rb = router_b[layer].reshape(N_EXPERTS, 1)
    wu = exp_w_up.reshape(exp_w_up.shape[1:])
    wd = exp_w_down.reshape(exp_w_down.shape[1:])
    n_up = wu.shape[-1]
    bu = exp_b_up[layer].reshape(N_EXPERTS, n_up // GROUP, LANES, 2).transpose(0, 1, 3, 2).reshape(N_EXPERTS, n_up)

    pr_p, pa_p = _inproj(x_prompt, norm1_g[layer], pm(sh1), pm(sc1), w_in16, TM_PROJ)
    zero_prev = jnp.zeros((bp, 1, RWKV_PROJ), F32)
    zero_state = jnp.zeros((bp, RWKV_WIDTH // GROUP, GROUP, GROUP), F32)
    yr_p, s_p = _rwkv(pr_p, zero_prev, zero_state, lp, RWKV_SUB, None)
    zero_kv = jnp.zeros((bp, WINDOW, KV_WIDTH), F32)
    ya_p, kn_p = _swa(pa_p, zero_kv, zero_kv, rel_bias, sinks[layer], q_norm[layer], k_norm[layer],
                      CHUNK, SWA_QC, False)

    ns = bs * ts
    xs_flat = x_sample.reshape(1, ns, d)
    pr_s, pa_s = _inproj(xs_flat, norm1_g[layer], sm(sh1), sm(sc1), w_in16, ns)
    pr_s = pr_s.reshape(bs, ts, RWKV_PROJ)
    pa_s = pa_s.reshape(bs, ts, ATTN_PROJ)
    t_pad = -(-ts // CHUNK) * CHUNK
    pr_s_pad = jnp.pad(pr_s, ((0, 0), (0, t_pad - ts), (0, 0)))
    yr_s, s_s = _rwkv(pr_s_pad, state_shift[layer].reshape(bs, 1, RWKV_PROJ), _state_to_bd(state_wkv[layer]),
                      lp, t_pad // CHUNK, ts)
    yr_s = yr_s[:, :ts]
    ya_s, kn_s = _swa(pa_s, cache_win_k[layer].reshape(bs, WINDOW, KV_WIDTH),
                      cache_win_v[layer].reshape(bs, WINDOW, KV_WIDTH), rel_bias, sinks[layer],
                      q_norm[layer], k_norm[layer], ts, 1, True)

    base0 = jnp.zeros((N_EXPERTS, 1), F32)
    x1_p, h2_p, te_p, gate_p, pos_p, cnt_p = _mid(yr_p, ya_p, x_prompt, pm(gt1), pm(sh2), pm(sc2), norm2_g[layer],
                                                  wo_r16, wo_a16, wr_t, rb, base0, TM_PROJ)
    x1_s, h2_s, te_s, gate_s, pos_s, cnt = _mid(yr_s.reshape(1, ns, RWKV_WIDTH), ya_s.reshape(1, ns, ATTN_WIDTH),
                                                xs_flat, sm(gt1), sm(sh2), sm(sc2), norm2_g[layer],
                                                wo_r16, wo_a16, wr_t, rb, cnt_p, ns)

    n_p = bp * tp
    n_rows = (n_p + ns) * TOP_K + N_EXPERTS * (TM_FFN - 1)
    n_tiles = -(-n_rows // TM_FFN)
    pstart, tile_e, n_used, next_e, pads = _tile_plan(cnt, TM_FFN, n_tiles)
    dest = _slot_rows(jnp.concatenate([te_p, te_s], axis=1), jnp.concatenate([pos_p, pos_s], axis=1), pstart)
    dest_p, dest_s = dest[:, :n_p], dest[:, n_p:]
    xs_buf = _dispatch(dest, pads, h2_p.reshape(n_p, d), h2_s.reshape(ns, d), n_tiles * TM_FFN, TK_DISPATCH)
    ys_buf = _ffn(tile_e, n_used, next_e, xs_buf, wu, bu, wd, exp_b_down[layer], TM_FFN)
    y_p = _combine(dest_p, gate_p, x1_p, pm(gt2), ys_buf, TK_COMBINE)
    y_s = _combine(dest_s, gate_s, x1_s, sm(gt2), ys_buf, TK_COMBINE)

    kv4 = lambda z, rows_: z.reshape(z.shape[0], rows_, KV_HEADS, HEAD)[None]
    v_p = pa_p[:, tp - WINDOW:, ATTN_WIDTH + KV_WIDTH:]
    v_s = pa_s[:, :, ATTN_WIDTH + KV_WIDTH:]
    return (y_p, y_s.reshape(bs, ts, d),
            pr_p[:, tp - 1][None], _state_from_bd(s_p)[None], kv4(kn_p[:, tp - WINDOW:], WINDOW), kv4(v_p, WINDOW),
            pr_s[:, ts - 1][None], _state_from_bd(s_s)[None], kv4(kn_s, ts), kv4(v_s, ts))
```

```python
import functools
import math

import jax
import jax.numpy as jnp
from jax import lax
from jax.experimental import pallas as pl
from jax.experimental.pallas import tpu as pltpu

F32 = jnp.float32
BF16 = jnp.bfloat16
I32 = jnp.int32

HEAD = 64
RWKV_HEADS = 8
RWKV_WIDTH = RWKV_HEADS * HEAD
DECAY_RANK = 64
ICLR_RANK = 64
GATE_RANK = 128
RWKV_PROJ = 3 * RWKV_WIDTH + DECAY_RANK + ICLR_RANK + GATE_RANK
ATTN_HEADS = 8
KV_HEADS = 2
ATTN_WIDTH = ATTN_HEADS * HEAD
KV_WIDTH = KV_HEADS * HEAD
ATTN_PROJ = ATTN_WIDTH + 2 * KV_WIDTH
WINDOW = 128
CHUNK = 64
REL_BUCKETS = 32
REL_MAX_DIST = 128
N_EXPERTS = 32
TOP_K = 4
GN_EPS = 64e-5
RMS_EPS = 1e-6
NEG_INF = -1e30
ATTN_SCALE = HEAD ** -0.5
SWIGLU_ALPHA = 1.702
SWIGLU_LIMIT = 7.0

LANES = 128
GROUP = 256
VMEM_LIMIT = 56 * 1024 * 1024

TM_PROJ = 512
RWKV_SUB = 4
SWA_QC = 8
TM_FFN = 512
TK_DISPATCH = 512
TK_COMBINE = 256
ROW_UNROLL = 8


def _cparams(*sem):
    return pltpu.CompilerParams(dimension_semantics=sem, vmem_limit_bytes=VMEM_LIMIT)


def _dot(a, b, nt=False):
    dims = (((1,), (1,)), ((), ())) if nt else (((1,), (0,)), ((), ()))
    return lax.dot_general(a, b, dims, preferred_element_type=F32)


def _split(x):
    hi = x.astype(BF16)
    lo = (x - hi.astype(F32)).astype(BF16)
    return hi, lo


def _mm(a, b, passes=1, nt=False):
    if passes == 1:
        return _dot(a.astype(BF16), b.astype(BF16), nt)
    ah, al = _split(a)
    bh, bl = _split(b)
    return _dot(ah, bh, nt) + _dot(al, bh, nt) + _dot(ah, bl, nt)


def _iota(shape, dim):
    return lax.broadcasted_iota(I32, shape, dim)


def _head_ones(n):
    r = lax.shift_right_logical(_iota((n, n), 0), 6)
    c = lax.shift_right_logical(_iota((n, n), 1), 6)
    return (r == c).astype(BF16)


def _head_sum(x):
    width = x.shape[1]
    x16 = x.astype(BF16)
    if width <= GROUP:
        return _dot(x16, _head_ones(width))
    ones = _head_ones(GROUP)
    parts = [_dot(x16[:, g:g + GROUP], ones) for g in range(0, width, GROUP)]
    return jnp.concatenate(parts, axis=1)


def _rms(x, eps):
    return x * lax.rsqrt(jnp.mean(x * x, axis=-1, keepdims=True) + eps)


def _mod_kernel(c_ref, w_ref, b_ref, o_ref):
    c = c_ref[...]
    s = c * jax.nn.sigmoid(c)
    o_ref[...] = _mm(s, w_ref[...], passes=3) + b_ref[...]


def _modulation(c_all, w_ada, b_ada):
    rows, d = c_all.shape
    n = w_ada.shape[1]
    tn = 1536
    return pl.pallas_call(
        _mod_kernel,
        grid=(n // tn,),
        in_specs=[pl.BlockSpec((rows, d), lambda j: (0, 0)),
                  pl.BlockSpec((d, tn), lambda j: (0, j)),
                  pl.BlockSpec((1, tn), lambda j: (0, j))],
        out_specs=pl.BlockSpec((rows, tn), lambda j: (0, j)),
        out_shape=jax.ShapeDtypeStruct((rows, n), F32),
        compiler_params=_cparams("arbitrary"),
        name="modulation",
    )(c_all, w_ada, b_ada.reshape(1, n))


def _inproj_kernel(x_ref, g_ref, sh_ref, sc_ref, w_ref, pr_ref, pa_ref):
    x = x_ref[...]
    h = _rms(x, RMS_EPS) * g_ref[...] * (1.0 + sc_ref[...]) + sh_ref[...]
    p = _dot(h.astype(BF16), w_ref[...])
    pr_ref[...] = p[:, :RWKV_PROJ]
    pa_ref[...] = p[:, RWKV_PROJ:]


def _inproj(x3, gain, sh3, sc3, w_in16, tm):
    g, r, d = x3.shape
    mr = sh3.shape[1]
    mrow = (lambda b, i: (b, 0, 0)) if mr == 1 else (lambda b, i: (b, i, 0))
    mblk = 1 if mr == 1 else tm
    return pl.pallas_call(
        _inproj_kernel,
        grid=(g, r // tm),
        in_specs=[pl.BlockSpec((None, tm, d), lambda b, i: (b, i, 0)),
                  pl.BlockSpec((1, d), lambda b, i: (0, 0)),
                  pl.BlockSpec((None, mblk, d), mrow),
                  pl.BlockSpec((None, mblk, d), mrow),
                  pl.BlockSpec(w_in16.shape, lambda b, i: (0, 0))],
        out_specs=[pl.BlockSpec((None, tm, RWKV_PROJ), lambda b, i: (b, i, 0)),
                   pl.BlockSpec((None, tm, ATTN_PROJ), lambda b, i: (b, i, 0))],
        out_shape=[jax.ShapeDtypeStruct((g, r, RWKV_PROJ), F32),
                   jax.ShapeDtypeStruct((g, r, ATTN_PROJ), F32)],
        compiler_params=_cparams("arbitrary", "arbitrary"),
        name="inproj",
    )(x3, gain.reshape(1, d), sh3, sc3, w_in16)


def _softplus(x):
    return jnp.maximum(x, 0.0) + jnp.log(1.0 + jnp.exp(-jnp.abs(x)))


def _rwkv_kernel(p_ref, prev0_ref, s0_ref, mu_ref, w0_ref, wwa_ref, a0_ref, gup_ref, kk_ref, ka_ref,
                 rk_ref, lng_ref, lnb_ref, y_ref, sout_ref, prev_sc, s_sc, *, nsub, t_valid):
    tc = nsub * CHUNK
    c = pl.program_id(1)

    @pl.when(c == 0)
    def _():
        prev_sc[...] = prev0_ref[...]
        s_sc[...] = s0_ref[...]

    p = p_ref[...]
    row = _iota((tc, 1), 0)
    p_prev = jnp.where(row == 0, prev_sc[...], pltpu.roll(p, 1, axis=0))
    prev_sc[...] = p_ref[tc - 1:tc, :]
    xp = p + (p_prev - p) * mu_ref[...]

    w3 = 3 * RWKV_WIDTH
    r = xp[:, 0:RWKV_WIDTH]
    k = xp[:, RWKV_WIDTH:2 * RWKV_WIDTH]
    v = xp[:, 2 * RWKV_WIDTH:w3]
    z = xp[:, w3:w3 + DECAY_RANK + ICLR_RANK]
    gd = xp[:, w3 + DECAY_RANK + ICLR_RANK:]
    zl = _iota((1, DECAY_RANK + ICLR_RANK), 1)
    zz = jnp.where(zl < DECAY_RANK, jnp.tanh(z), z)
    wa = _mm(zz, wwa_ref[...], passes=3)
    w_log = -_softplus(-(w0_ref[...] + wa[:, :RWKV_WIDTH])) - 0.5
    lw = -jnp.exp(w_log)
    a = jax.nn.sigmoid(a0_ref[...] + wa[:, RWKV_WIDTH:])
    g = _mm(jax.nn.sigmoid(gd), gup_ref[...])
    kk = k * kk_ref[...]
    kk = kk * lax.rsqrt(jnp.maximum(_head_sum(kk * kk), 1e-24))
    k2 = k * (1.0 + (a - 1.0) * ka_ref[...])
    bvec = kk * a
    if t_valid is not None:
        ok = (c * tc + row) < t_valid
        lw = jnp.where(ok, lw, 0.0)
        kk = jnp.where(ok, kk, 0.0)
        bvec = jnp.where(ok, bvec, 0.0)
        k2u = jnp.where(ok, k2, 0.0)
        vu = jnp.where(ok, v, 0.0)
    else:
        k2u, vu = k2, v

    tril16 = (_iota((CHUNK, CHUNK), 0) >= _iota((CHUNK, CHUNK), 1)).astype(BF16)
    bd_mask = (lax.shift_right_logical(_iota((GROUP, GROUP), 0), 6)
               == lax.shift_right_logical(_iota((GROUP, GROUP), 1), 6))
    t_idx = _iota((CHUNK, GROUP), 0)
    s_idx = jnp.bitwise_and(_iota((CHUNK, GROUP), 1), CHUNK - 1)
    strict = t_idx > s_idx
    incl = t_idx >= s_idx
    eye = (t_idx == s_idx).astype(F32)

    bd16 = bd_mask.astype(BF16)

    def bd(x):
        x16 = x.astype(BF16)
        return jnp.concatenate([x16, x16, x16, x16], axis=0) * bd16

    def b16(x):
        return x.astype(BF16)

    ng = RWKV_WIDTH // GROUP
    pairs = [(j, gi) for j in range(nsub) for gi in range(ng)]
    gsl = lambda gi: slice(gi * GROUP, (gi + 1) * GROUP)

    at, bt, kt, rt, bw, kw, vj, w_last = [], [], [], [], [], [], [], []
    for j in range(nsub):
        sl = slice(j * CHUNK, (j + 1) * CHUNK)
        lwj = lw[sl]
        h1 = lwj.astype(BF16)
        r1 = lwj - h1.astype(F32)
        h2 = r1.astype(BF16)
        h3 = (r1 - h2.astype(F32)).astype(BF16)
        cum = _dot(tril16, h1) + _dot(tril16, h2) + _dot(tril16, h3)
        e_cum = jnp.exp(cum)
        e_inv = jnp.exp(-cum)
        wl = e_cum[CHUNK - 1:CHUNK, :]
        at.append(-kk[sl] * jnp.exp(cum - lwj))
        bt.append(bvec[sl] * e_inv)
        kt.append(k2u[sl] * e_inv)
        rt.append(r[sl] * e_cum)
        bw.append(bt[j] * wl)
        kw.append(kt[j] * wl)
        vj.append(vu[sl])
        w_last.append(wl)

    a_ab, a_ak, a_rb, a_rk, vbd = {}, {}, {}, {}, {}
    for (j, gi) in pairs:
        gs = gsl(gi)
        q2 = b16(jnp.concatenate([at[j][:, gs], rt[j][:, gs]], axis=0))
        sc = _dot(q2, jnp.concatenate([bd(bt[j][:, gs]), bd(kt[j][:, gs])], axis=0), nt=True)
        a_ab[j, gi] = jnp.where(strict, sc[:CHUNK, :GROUP], 0.0)
        a_ak[j, gi] = jnp.where(strict, sc[:CHUNK, GROUP:], 0.0)
        a_rb[j, gi] = jnp.where(incl, sc[CHUNK:, :GROUP], 0.0)
        a_rk[j, gi] = jnp.where(incl, sc[CHUNK:, GROUP:], 0.0)
        vbd[j, gi] = bd(vj[j][:, gs])

    tinv = {p: eye + a_ab[p] for p in pairs}
    xpow = {p: _dot(b16(a_ab[p]), bd(a_ab[p])) for p in pairs}
    for step in range(1, 6):
        for p in pairs:
            if step < 5:
                both = _dot(b16(xpow[p]), jnp.concatenate([bd(tinv[p]), bd(xpow[p])], axis=1))
                tinv[p] = tinv[p] + both[:, :GROUP]
                xpow[p] = both[:, GROUP:]
            else:
                tinv[p] = tinv[p] + _dot(b16(xpow[p]), bd(tinv[p]))

    akv = {p: _dot(b16(a_ak[p]), vbd[p]) for p in pairs}
    pq = {(j, gi): _dot(b16(tinv[j, gi]), jnp.concatenate([bd(at[j][:, gsl(gi)]), bd(akv[j, gi])], axis=1))
          for (j, gi) in pairs}
    r2, y0, m_lr, n_add = {}, {}, {}, {}
    for (j, gi) in pairs:
        gs = gsl(gi)
        pm, qm = pq[j, gi][:, :GROUP], pq[j, gi][:, GROUP:]
        rr = _dot(b16(a_rb[j, gi]), jnp.concatenate([bd(pm), bd(qm)], axis=1))
        r2[j, gi] = b16(rt[j][:, gs] + rr[:, :GROUP])
        y0[j, gi] = rr[:, GROUP:] + _dot(b16(a_rk[j, gi]), vbd[j, gi])
        m_lr[j, gi] = b16(jnp.where(bd_mask, _dot(b16(pm.T), b16(bw[j][:, gs])), 0.0))
        qv = jnp.concatenate([qm, vj[j][:, gs]], axis=0)
        bk = jnp.concatenate([bw[j][:, gs], kw[j][:, gs]], axis=0)
        n_add[j, gi] = jnp.where(bd_mask, _dot(b16(qv.T), b16(bk)), 0.0)

    s_in = {}
    for gi in range(ng):
        s = s_sc[gi]
        for j in range(nsub):
            s16 = b16(s)
            s_in[j, gi] = s16
            s = s * w_last[j][:, gsl(gi)] + _dot(s16, m_lr[j, gi]) + n_add[j, gi]
        s_sc[gi] = s

    ys = []
    for j in range(nsub):
        yg = [_dot(r2[j, gi], s_in[j, gi], nt=True) + y0[j, gi] for gi in range(ng)]
        ys.append(jnp.concatenate(yg, axis=1))
    y = jnp.concatenate(ys, axis=0) if nsub > 1 else ys[0]

    inv_n = 1.0 / HEAD
    yc = y - _head_sum(y) * inv_n
    yn = yc * lax.rsqrt(_head_sum(yc * yc) * inv_n + GN_EPS)
    yn = yn * lng_ref[...] + lnb_ref[...]
    bonus = _head_sum(r * k2 * rk_ref[...]) * v
    y_ref[...] = (yn + bonus) * g
    sout_ref[...] = s_sc[...]


def _rwkv(p_r, prev0, s0_bd, lp, nsub, t_valid):
    b, t, _ = p_r.shape
    tc = nsub * CHUNK
    ng = RWKV_WIDTH // GROUP
    row = lambda a: a.reshape(1, -1)
    zeros = jnp.zeros((DECAY_RANK, RWKV_WIDTH), F32)
    wwa = jnp.concatenate([jnp.concatenate([lp["rwkv_w_up"], zeros], axis=1),
                           jnp.concatenate([zeros, lp["rwkv_a_up"]], axis=1)], axis=0)
    const = lambda shape: pl.BlockSpec(shape, lambda bi, ci: (0,) * len(shape))
    kern = functools.partial(_rwkv_kernel, nsub=nsub, t_valid=t_valid)
    return pl.pallas_call(
        kern,
        grid=(b, t // tc),
        in_specs=[pl.BlockSpec((None, tc, RWKV_PROJ), lambda bi, ci: (bi, ci, 0)),
                  pl.BlockSpec((None, 1, RWKV_PROJ), lambda bi, ci: (bi, 0, 0)),
                  pl.BlockSpec((None, ng, GROUP, GROUP), lambda bi, ci: (bi, 0, 0, 0)),
                  const((1, RWKV_PROJ)), const((1, RWKV_WIDTH)),
                  const((DECAY_RANK + ICLR_RANK, 2 * RWKV_WIDTH)), const((1, RWKV_WIDTH)),
                  const((GATE_RANK, RWKV_WIDTH)), const((1, RWKV_WIDTH)), const((1, RWKV_WIDTH)),
                  const((1, RWKV_WIDTH)), const((1, RWKV_WIDTH)), const((1, RWKV_WIDTH))],
        out_specs=[pl.BlockSpec((None, tc, RWKV_WIDTH), lambda bi, ci: (bi, ci, 0)),
                   pl.BlockSpec((None, ng, GROUP, GROUP), lambda bi, ci: (bi, 0, 0, 0))],
        out_shape=[jax.ShapeDtypeStruct((b, t, RWKV_WIDTH), F32),
                   jax.ShapeDtypeStruct((b, ng, GROUP, GROUP), F32)],
        scratch_shapes=[pltpu.VMEM((1, RWKV_PROJ), F32), pltpu.VMEM((ng, GROUP, GROUP), F32)],
        compiler_params=_cparams("arbitrary", "arbitrary"),
        name="rwkv7",
    )(p_r, prev0, s0_bd, row(lp["rwkv_mu"]), row(lp["rwkv_w0"]), wwa, row(lp["rwkv_a0"]),
      lp["rwkv_g_up"], row(lp["rwkv_k_k"]), row(lp["rwkv_k_a"]), row(lp["rwkv_r_k"]),
      row(lp["rwkv_ln_g"]), row(lp["rwkv_ln_b"]))


def _state_to_bd(s):
    b = s.shape[0]
    hg = GROUP // HEAD
    s5 = s.reshape(b, RWKV_HEADS // hg, hg, HEAD, HEAD)
    out = jnp.einsum("bghvk,hj->bghvjk", s5, jnp.eye(hg, dtype=s.dtype))
    return out.reshape(b, RWKV_HEADS // hg, GROUP, GROUP)


def _state_from_bd(s_bd):
    b = s_bd.shape[0]
    hg = GROUP // HEAD
    s6 = s_bd.reshape(b, RWKV_HEADS // hg, hg, HEAD, hg, HEAD)
    out = jnp.einsum("bghvjk,hj->bghvk", s6, jnp.eye(hg, dtype=s_bd.dtype))
    return out.reshape(b, RWKV_HEADS, HEAD, HEAD)


def _swa_kernel(tab_ref, sink_ref, pa_ref, ck_ref, cv_ref, bkt_ref, qg_ref, kg_ref,
                ya_ref, kn_ref, kall, vall, bias_sc, *, cq, nqc, has_cache):
    tq = cq * nqc
    band = WINDOW + cq
    bi = pl.program_id(0)
    i = pl.program_id(1)
    lane = _iota((1, LANES), 1)
    low = lane < HEAD

    @pl.when(jnp.logical_and(bi == 0, i == 0))
    def _():
        bkt = bkt_ref[...]
        for h in range(ATTN_HEADS):
            def body(j, acc, h=h):
                return acc + jnp.where(bkt == j, tab_ref[j, h], 0.0)
            bh = lax.fori_loop(0, REL_BUCKETS, body, jnp.zeros((cq, band), F32))
            bias_sc[h // 2, (h % 2) * cq:(h % 2 + 1) * cq, :] = bh

    @pl.when(i == 0)
    def _():
        kall[0:WINDOW, :] = ck_ref[...]
        vall[0:WINDOW, :] = cv_ref[...]

    pa = pa_ref[...]
    q = pa[:, :ATTN_WIDTH]
    k = pa[:, ATTN_WIDTH:ATTN_WIDTH + KV_WIDTH]
    v = pa[:, ATTN_WIDTH + KV_WIDTH:]
    inv_n = 1.0 / HEAD
    qn = q * lax.rsqrt(_head_sum(q * q) * inv_n + RMS_EPS) * qg_ref[...]
    kn = k * lax.rsqrt(_head_sum(k * k) * inv_n + RMS_EPS) * kg_ref[...]
    kn_ref[...] = kn
    kall[WINDOW:WINDOW + tq, :] = kn
    vall[WINDOW:WINDOW + tq, :] = v

    row2 = _iota((2 * cq, 1), 0)
    col = _iota((1, band), 1)
    thr = jnp.where(i == 0, WINDOW, 0)
    k_all = kall[...]
    v_all = vall[...]
    k_rot = pltpu.roll(k_all, HEAD, axis=1)
    v_rot = pltpu.roll(v_all, HEAD, axis=1)
    kd = [jnp.where(low, k_all, k_rot).astype(BF16), jnp.where(low, k_rot, k_all).astype(BF16)]
    vd = [jnp.where(low, v_all, v_rot).astype(BF16), jnp.where(low, v_rot, v_all).astype(BF16)]

    work = [(qc, pi) for qc in range(nqc) for pi in range(ATTN_HEADS // 2)]
    kv_of = lambda pi: (2 * pi) // (ATTN_HEADS // KV_HEADS)
    scores = {}
    for (qc, pi) in work:
        qp = qn[qc * cq:(qc + 1) * cq, pi * LANES:(pi + 1) * LANES]
        qs = jnp.concatenate([jnp.where(low, qp, 0.0), jnp.where(low, 0.0, qp)], axis=0)
        s = _dot(qs.astype(BF16), kd[kv_of(pi)][qc * cq:qc * cq + band], nt=True) * ATTN_SCALE + bias_sc[pi]
        if (not has_cache) and qc * cq < WINDOW:
            s = jnp.where((col + qc * cq) < thr, NEG_INF, s)
        scores[qc, pi] = s
    probs = {}
    for (qc, pi) in work:
        s = scores[qc, pi]
        sink = jnp.where(row2 < cq, sink_ref[0, 2 * pi], sink_ref[0, 2 * pi + 1])
        m = jnp.maximum(jnp.max(s, axis=-1, keepdims=True), sink)
        e = jnp.exp(s - m)
        den = jnp.sum(e, axis=-1, keepdims=True) + jnp.exp(sink - m)
        probs[qc, pi] = (e / den).astype(BF16)
    for (qc, pi) in work:
        o = _dot(probs[qc, pi], vd[kv_of(pi)][qc * cq:qc * cq + band])
        ya_ref[qc * cq:(qc + 1) * cq, pi * LANES:(pi + 1) * LANES] = jnp.where(low, o[:cq], o[cq:])

    if tq >= WINDOW:
        knext = kall[tq:tq + WINDOW, :]
        vnext = vall[tq:tq + WINDOW, :]
        kall[0:WINDOW, :] = knext
        vall[0:WINDOW, :] = vnext


def _t5_bucket(rel):
    half = REL_BUCKETS // 2
    max_exact = half // 2
    n = jnp.abs(rel)
    log_ratio = jnp.log(jnp.maximum(n, 1).astype(F32) / max_exact) / math.log(REL_MAX_DIST / max_exact)
    large = jnp.minimum(max_exact + (log_ratio * (half - max_exact)).astype(I32), half - 1)
    return jnp.where(rel > 0, half, 0) + jnp.where(n < max_exact, n, large)


def _swa(p_a, cache_k, cache_v, rel_table, sinks, q_norm, k_norm, cq, nqc, has_cache):
    b, t, _ = p_a.shape
    tq = cq * nqc
    band = WINDOW + cq
    rel = (jnp.arange(band) - WINDOW)[None, :] - jnp.arange(cq)[:, None]
    bkt = _t5_bucket(rel).astype(I32)
    qg = jnp.tile(q_norm, ATTN_HEADS).reshape(1, ATTN_WIDTH)
    kg = jnp.tile(k_norm, KV_HEADS).reshape(1, KV_WIDTH)
    kern = functools.partial(_swa_kernel, cq=cq, nqc=nqc, has_cache=has_cache)
    smem = pl.BlockSpec(memory_space=pltpu.SMEM)
    return pl.pallas_call(
        kern,
        grid=(b, t // tq),
        in_specs=[smem, smem,
                  pl.BlockSpec((None, tq, ATTN_PROJ), lambda bi, i: (bi, i, 0)),
                  pl.BlockSpec((None, WINDOW, KV_WIDTH), lambda bi, i: (bi, 0, 0)),
                  pl.BlockSpec((None, WINDOW, KV_WIDTH), lambda bi, i: (bi, 0, 0)),
                  pl.BlockSpec((cq, band), lambda bi, i: (0, 0)),
                  pl.BlockSpec((1, ATTN_WIDTH), lambda bi, i: (0, 0)),
                  pl.BlockSpec((1, KV_WIDTH), lambda bi, i: (0, 0))],
        out_specs=[pl.BlockSpec((None, tq, ATTN_WIDTH), lambda bi, i: (bi, i, 0)),
                   pl.BlockSpec((None, tq, KV_WIDTH), lambda bi, i: (bi, i, 0))],
        out_shape=[jax.ShapeDtypeStruct((b, t, ATTN_WIDTH), F32),
                   jax.ShapeDtypeStruct((b, t, KV_WIDTH), F32)],
        scratch_shapes=[pltpu.VMEM((WINDOW + tq, KV_WIDTH), F32),
                        pltpu.VMEM((WINDOW + tq, KV_WIDTH), F32),
                        pltpu.VMEM((ATTN_HEADS // 2, 2 * cq, band), F32)],
        compiler_params=_cparams("arbitrary", "arbitrary"),
        name="swa",
    )(rel_table, sinks.reshape(1, ATTN_HEADS), p_a, cache_k, cache_v, bkt, qg, kg)


def _mid_kernel(yr_ref, ya_ref, x_ref, gt1_ref, sh2_ref, sc2_ref, g2_ref, wor_ref, woa_ref, wrt_ref,
                rb_ref, base0_ref, x1_ref, h2_ref, te_ref, gate_ref, pos_ref, cnt_ref, base_sc, *, tm):
    @pl.when(jnp.logical_and(pl.program_id(0) == 0, pl.program_id(1) == 0))
    def _():
        base_sc[...] = base0_ref[...]

    mix = _dot(yr_ref[...].astype(BF16), wor_ref[...]) + _dot(ya_ref[...].astype(BF16), woa_ref[...])
    x1 = x_ref[...] + gt1_ref[...] * mix
    x1_ref[...] = x1
    h2 = _rms(x1, RMS_EPS) * g2_ref[...] * (1.0 + sc2_ref[...]) + sh2_ref[...]
    h2_ref[...] = h2

    logits = _mm(wrt_ref[...], h2, passes=3, nt=True) + rb_ref[...]
    eidx = _iota((N_EXPERTS, tm), 0)
    work = logits
    tops, hots = [], []
    for j in range(TOP_K):
        m = jnp.max(work, axis=0, keepdims=True)
        idx = jnp.min(jnp.where(work == m, eidx, N_EXPERTS), axis=0, keepdims=True)
        hot = eidx == idx
        work = jnp.where(hot, -jnp.inf, work)
        tops.append(m)
        hots.append(hot)
        te_ref[j:j + 1, :] = idx
    exps = [jnp.exp(tj - tops[0]) for tj in tops]
    den = exps[0] + exps[1] + exps[2] + exps[3]
    for j in range(TOP_K):
        gate_ref[j:j + 1, :] = exps[j] / den

    member = (hots[0] | hots[1] | hots[2] | hots[3]).astype(BF16)
    upper = (_iota((tm, tm), 0) < _iota((tm, tm), 1)).astype(BF16)
    before = _dot(member, upper) + base_sc[...]
    for j in range(TOP_K):
        pos_ref[j:j + 1, :] = jnp.sum(jnp.where(hots[j], before, 0.0), axis=0, keepdims=True).astype(I32)
    base_sc[...] = base_sc[...] + jnp.sum(member.astype(F32), axis=1, keepdims=True)
    cnt_ref[...] = base_sc[...]


def _mid(y_r, y_a, x3, gt1, sh2, sc2, gain2, wo_r16, wo_a16, wr_t, rb, base0, tm):
    g, r, d = x3.shape
    n = g * r
    per = r // tm
    mr = gt1.shape[1]
    mrow = (lambda b, i: (b, 0, 0)) if mr == 1 else (lambda b, i: (b, i, 0))
    mblk = 1 if mr == 1 else tm
    tok = lambda w: pl.BlockSpec((None, tm, w), lambda b, i: (b, i, 0))
    mod = pl.BlockSpec((None, mblk, d), mrow)
    const = lambda shape: pl.BlockSpec(shape, lambda b, i: (0,) * len(shape))
    lane_out = pl.BlockSpec((TOP_K, tm), lambda b, i: (0, b * per + i))
    kern = functools.partial(_mid_kernel, tm=tm)
    return pl.pallas_call(
        kern,
        grid=(g, per),
        in_specs=[tok(RWKV_WIDTH), tok(ATTN_WIDTH), tok(d), mod, mod, mod, const((1, d)),
                  const((RWKV_WIDTH, d)), const((ATTN_WIDTH, d)), const((N_EXPERTS, d)),
                  const((N_EXPERTS, 1)), const((N_EXPERTS, 1))],
        out_specs=[tok(d), tok(d), lane_out, lane_out, lane_out, const((N_EXPERTS, 1))],
        out_shape=[jax.ShapeDtypeStruct((g, r, d), F32), jax.ShapeDtypeStruct((g, r, d), F32),
                   jax.ShapeDtypeStruct((TOP_K, n), I32), jax.ShapeDtypeStruct((TOP_K, n), F32),
                   jax.ShapeDtypeStruct((TOP_K, n), I32), jax.ShapeDtypeStruct((N_EXPERTS, 1), F32)],
        scratch_shapes=[pltpu.VMEM((N_EXPERTS, 1), F32)],
        compiler_params=_cparams("arbitrary", "arbitrary"),
        name="mid",
    )(y_r, y_a, x3, gt1, sh2, sc2, gain2.reshape(1, d), wo_r16, wo_a16, wr_t, rb, base0)


def _slot_kernel(pstart_ref, te_ref, pos_ref, o_ref):
    te = te_ref[...]
    acc = pos_ref[...]
    for e in range(N_EXPERTS):
        acc = acc + jnp.where(te == e, pstart_ref[e], 0)
    o_ref[...] = acc


def _slot_rows(te, pos, pstart):
    k, n = te.shape
    return pl.pallas_call(
        _slot_kernel,
        grid=(1,),
        in_specs=[pl.BlockSpec(memory_space=pltpu.SMEM),
                  pl.BlockSpec((k, n), lambda i: (0, 0)), pl.BlockSpec((k, n), lambda i: (0, 0))],
        out_specs=pl.BlockSpec((k, n), lambda i: (0, 0)),
        out_shape=jax.ShapeDtypeStruct((k, n), I32),
        compiler_params=_cparams("arbitrary"),
        name="slot_rows",
    )(pstart, te, pos)


def _dispatch_kernel(dst_ref, pad_ref, h2a_ref, h2b_ref, xs_ref, ztile, sem, zsem, *, tk, steps_a):
    i = pl.program_id(0)

    @pl.when(i == 0)
    def _():
        ztile[...] = jnp.zeros_like(ztile)
        zrow = ztile.at[pl.ds(0, 1)]
        tm = ztile.shape[0]

        def zero_row(row, priority):
            pltpu.make_async_copy(zrow, xs_ref.at[pl.ds(row, 1)], zsem).start(priority=priority)

        def per_expert(e, total):
            first = pad_ref[0, e]
            count = pad_ref[1, e]

            def per_pair(r, c):
                zero_row(first + 2 * r, 0)
                zero_row(first + 2 * r + 1, 1)
                return c
            lax.fori_loop(0, count // 2, per_pair, 0)

            @pl.when(count % 2 == 1)
            def _():
                zero_row(first + count - 1, 0)
            return total + count
        n_pad = lax.fori_loop(0, N_EXPERTS, per_expert, 0)

        blk = 64
        def drain_blk(r, c):
            pltpu.make_async_copy(ztile.at[pl.ds(0, blk)], xs_ref.at[pl.ds(0, blk)], zsem).wait()
            return c
        lax.fori_loop(0, n_pad // blk, drain_blk, 0)

        def drain_row(r, c):
            pltpu.make_async_copy(zrow, xs_ref.at[pl.ds(0, 1)], zsem).wait()
            return c
        lax.fori_loop(0, n_pad % blk, drain_row, 0)

        def spare(t, c):
            row0 = pl.multiple_of(t * tm, tm)
            pltpu.make_async_copy(ztile, xs_ref.at[pl.ds(row0, tm)], zsem).start()
            return c
        lax.fori_loop(pad_ref[2, 0], pad_ref[2, 1], spare, 0)

        def spare_wait(t, c):
            pltpu.make_async_copy(ztile, xs_ref.at[pl.ds(0, tm)], zsem).wait()
            return c
        lax.fori_loop(pad_ref[2, 0], pad_ref[2, 1], spare_wait, 0)

    def scatter(h2_ref):
        for t in range(h2_ref.shape[0] * ROW_UNROLL):
            for j in range(TOP_K):
                dst = dst_ref[j, t]
                pltpu.make_async_copy(h2_ref.at[t // ROW_UNROLL, pl.ds(t % ROW_UNROLL, 1)],
                                      xs_ref.at[pl.ds(dst, 1)], sem).start(priority=j % 2)
        for _ in range(TOP_K):
            pltpu.make_async_copy(h2_ref, h2_ref, sem).wait()

    pl.when(i < steps_a)(functools.partial(scatter, h2a_ref))
    pl.when(i == steps_a)(functools.partial(scatter, h2b_ref))


def _dispatch(dest, pads, h2a, h2b, n_rows, tk):
    na, d = h2a.shape
    nb = h2b.shape[0]
    steps_a = na // tk
    dest = jnp.pad(dest, ((0, 0), (0, (steps_a + 1) * tk - na - nb)))
    h2a = h2a.reshape(na // ROW_UNROLL, ROW_UNROLL, d)
    h2b = h2b.reshape(nb // ROW_UNROLL, ROW_UNROLL, d)
    kern = functools.partial(_dispatch_kernel, tk=tk, steps_a=steps_a)
    return pl.pallas_call(
        kern,
        grid=(steps_a + 1,),
        in_specs=[pl.BlockSpec((TOP_K, tk), lambda i: (0, i), memory_space=pltpu.SMEM),
                  pl.BlockSpec(memory_space=pltpu.SMEM),
                  pl.BlockSpec((tk // ROW_UNROLL, ROW_UNROLL, d), lambda i: (jnp.minimum(i, steps_a - 1), 0, 0)),
                  pl.BlockSpec((nb // ROW_UNROLL, ROW_UNROLL, d), lambda i: (0, 0, 0))],
        out_specs=pl.BlockSpec(memory_space=pl.ANY),
        out_shape=jax.ShapeDtypeStruct((n_rows, d), F32),
        scratch_shapes=[pltpu.VMEM((TM_FFN, d), F32), pltpu.SemaphoreType.DMA(()), pltpu.SemaphoreType.DMA(())],
        compiler_params=_cparams("arbitrary"),
        name="dispatch",
    )(dest, pads, h2a, h2b)


def _ffn_kernel(te_ref, nu_ref, nx_ref, xs_ref, wu_hbm, bu_ref, wd_hbm, bd_ref, ys_ref,
                wu32, wd32, wu16, wd16, sem):
    i = pl.program_id(0)
    n_up = wu32.shape[1]
    active = i < nu_ref[0]
    fresh = jnp.logical_or(i == 0, te_ref[i] != te_ref[jnp.maximum(i - 1, 0)])

    def weight_copies(e):
        return (pltpu.make_async_copy(wu_hbm.at[e], wu32, sem.at[0]),
                pltpu.make_async_copy(wd_hbm.at[e], wd32, sem.at[1]))

    @pl.when(i == 0)
    def _():
        for cp in weight_copies(te_ref[0]):
            cp.start()

    @pl.when(jnp.logical_and(active, fresh))
    def _():
        for cp in weight_copies(te_ref[i]):
            cp.wait()
        src = _iota((GROUP, GROUP), 0)
        dst = _iota((GROUP, GROUP), 1)
        want = jnp.where(dst < LANES, 2 * dst, 2 * (dst - LANES) + 1)
        perm = (src == want).astype(BF16)
        for b in range(n_up // GROUP):
            cs = slice(b * GROUP, (b + 1) * GROUP)
            wu16[:, cs] = _dot(wu32[:, cs].astype(BF16), perm).astype(BF16)
        wd16[...] = wd32[...].astype(BF16)

        @pl.when(nx_ref[i] >= 0)
        def _():
            for cp in weight_copies(nx_ref[i]):
                cp.start()

    @pl.when(active)
    def _():
        u = _dot(xs_ref[...].astype(BF16), wu16[...]) + bu_ref[...]
        acts = []
        for b in range(n_up // GROUP):
            glu = jnp.minimum(u[:, b * GROUP:b * GROUP + LANES], SWIGLU_LIMIT)
            lin = jnp.clip(u[:, b * GROUP + LANES:(b + 1) * GROUP], -SWIGLU_LIMIT, SWIGLU_LIMIT)
            acts.append((glu * jax.nn.sigmoid(SWIGLU_ALPHA * glu) * (lin + 1.0)).astype(BF16))
        act = jnp.concatenate(acts, axis=1)
        ys_ref[...] = _dot(act, wd16[...]) + bd_ref[...]

    @pl.when(jnp.logical_not(active))
    def _():
        ys_ref[...] = jnp.zeros_like(ys_ref)


def _ffn(tile_e, n_used, next_e, xs, wu, bu, wd, bd_, tm):
    rows, d = xs.shape
    n_exp, _, n_up = wu.shape
    dff = wd.shape[1]
    row_blk = lambda i, te, nu, nx: (jnp.minimum(i, nu[0] - 1), 0)
    exp_blk = lambda i, te, nu, nx: (te[i], 0, 0)
    grid_spec = pltpu.PrefetchScalarGridSpec(
        num_scalar_prefetch=3,
        grid=(rows // tm,),
        in_specs=[pl.BlockSpec((tm, d), row_blk),
                  pl.BlockSpec(memory_space=pl.ANY),
                  pl.BlockSpec((None, 1, n_up), exp_blk),
                  pl.BlockSpec(memory_space=pl.ANY),
                  pl.BlockSpec((None, 1, d), exp_blk)],
        out_specs=pl.BlockSpec((tm, d), lambda i, te, nu, nx: (i, 0)),
        scratch_shapes=[pltpu.VMEM((d, n_up), F32), pltpu.VMEM((dff, d), F32),
                        pltpu.VMEM((d, n_up), BF16), pltpu.VMEM((dff, d), BF16),
                        pltpu.SemaphoreType.DMA((2,))],
    )
    return pl.pallas_call(
        _ffn_kernel,
        grid_spec=grid_spec,
        out_shape=jax.ShapeDtypeStruct((rows, d), F32),
        compiler_params=_cparams("arbitrary"),
        name="expert_ffn",
    )(tile_e, n_used, next_e, xs, wu, bu.reshape(n_exp, 1, n_up), wd, bd_.reshape(n_exp, 1, d))


def _combine_kernel(src_ref, srcn_ref, gate_ref, x1_ref, gt2_ref, ys_ref, o_ref, buf, sem, *, tk):
    i = pl.program_id(1)
    n = pl.num_programs(1)
    flat = pl.program_id(0) * n + i
    total = pl.num_programs(0) * n
    slot = lax.rem(flat, 2)

    def issue(rows_ref, s, t0, t1):
        for t in range(t0, t1):
            for j in range(TOP_K):
                src = rows_ref[j, t]
                pltpu.make_async_copy(ys_ref.at[pl.ds(src, 1)],
                                      buf.at[s, j, t // ROW_UNROLL, pl.ds(t % ROW_UNROLL, 1)],
                                      sem.at[s]).start(priority=j % 2)

    @pl.when(flat == 0)
    def _():
        issue(src_ref, 0, 0, tk)

    def step(cur):
        @pl.when(flat + 1 < total)
        def _():
            issue(srcn_ref, 1 - cur, 0, tk)

        for j in range(TOP_K):
            pltpu.make_async_copy(buf.at[cur, j], buf.at[cur, j], sem.at[cur]).wait()
        eye = (_iota((tk, tk), 0) == _iota((tk, tk), 1)).astype(BF16)
        g_hi, g_lo = _split(gate_ref[...])
        gate = _dot(eye, g_hi, nt=True) + _dot(eye, g_lo, nt=True)
        rows = lambda j: buf[cur, j].reshape(tk, buf.shape[-1])
        acc = gate[:, 0:1] * rows(0)
        for j in range(1, TOP_K):
            acc = acc + gate[:, j:j + 1] * rows(j)
        o_ref[...] = x1_ref[...] + gt2_ref[...] * acc

    for cur in range(2):
        pl.when(slot == cur)(functools.partial(step, cur))


def _combine(dest, gates, x1, gt2, ys, tk):
    g, r, d = x1.shape
    per = r // tk
    last = g * per - 1
    mr = gt2.shape[1]
    mrow = (lambda b, i: (b, 0, 0)) if mr == 1 else (lambda b, i: (b, i, 0))
    mblk = 1 if mr == 1 else tk
    cur = pl.BlockSpec((TOP_K, tk), lambda b, i: (0, b * per + i), memory_space=pltpu.SMEM)
    nxt = pl.BlockSpec((TOP_K, tk), lambda b, i: (0, jnp.minimum(b * per + i + 1, last)),
                       memory_space=pltpu.SMEM)
    kern = functools.partial(_combine_kernel, tk=tk)
    return pl.pallas_call(
        kern,
        grid=(g, per),
        in_specs=[cur, nxt,
                  pl.BlockSpec((TOP_K, tk), lambda b, i: (0, b * per + i)),
                  pl.BlockSpec((None, tk, d), lambda b, i: (b, i, 0)),
                  pl.BlockSpec((None, mblk, d), mrow),
                  pl.BlockSpec(memory_space=pl.ANY)],
        out_specs=pl.BlockSpec((None, tk, d), lambda b, i: (b, i, 0)),
        out_shape=jax.ShapeDtypeStruct((g, r, d), F32),
        scratch_shapes=[pltpu.VMEM((2, TOP_K, tk // ROW_UNROLL, ROW_UNROLL, d), F32),
                        pltpu.SemaphoreType.DMA((2,))],
        compiler_params=_cparams("arbitrary", "arbitrary"),
        name="combine",
    )(dest, dest, gates, x1, gt2, ys)


def _tile_plan(counts, tm, n_tiles):
    cnt = counts.reshape(-1).astype(I32)
    tiles = (cnt + tm - 1) // tm
    tile_end = jnp.cumsum(tiles)
    pstart = ((tile_end - tiles) * tm).astype(I32)
    n_used = tile_end[-1:].astype(I32)
    idx = jnp.minimum(jnp.arange(n_tiles, dtype=I32), n_used[0] - 1)
    tile_e = jnp.sum((tile_end[None, :] <= idx[:, None]).astype(I32), axis=1)
    spare = jnp.zeros((N_EXPERTS,), I32).at[0].set(n_used[0]).at[1].set(n_tiles)
    pads = jnp.stack([pstart + cnt, tiles * tm - cnt, spare]).astype(I32)
    tile_e = jnp.minimum(tile_e, N_EXPERTS - 1)
    end_of = tile_end[tile_e]
    next_e = jnp.where(end_of < n_used[0], tile_e[jnp.minimum(end_of, n_tiles - 1)], -1).astype(I32)
    return pstart, tile_e, n_used, next_e, pads


def kernel(x_prompt, x_sample, state_shift, state_wkv, cache_win_k, cache_win_v, c_prompt, c_sample, rel_bias, norm1_g, norm2_g, w_ada, b_ada, w_in, w_out, rwkv_mu, rwkv_w0, rwkv_w_up, rwkv_a0, rwkv_a_up, rwkv_g_up, rwkv_k_k, rwkv_k_a, rwkv_r_k, rwkv_ln_g, rwkv_ln_b, q_norm, k_norm, sinks, router_w, router_b, exp_w_up, exp_b_up, exp_w_down, exp_b_down):
    bp, tp, d = x_prompt.shape
    bs, ts, _ = x_sample.shape
    depth = w_in.shape[0]
    assert depth == 1, "single-layer trunk"
    layer = 0
    lp = dict(rwkv_mu=rwkv_mu[layer], rwkv_w0=rwkv_w0[layer], rwkv_w_up=rwkv_w_up[layer],
              rwkv_a0=rwkv_a0[layer], rwkv_a_up=rwkv_a_up[layer], rwkv_g_up=rwkv_g_up[layer],
              rwkv_k_k=rwkv_k_k[layer], rwkv_k_a=rwkv_k_a[layer], rwkv_r_k=rwkv_r_k[layer].reshape(-1),
              rwkv_ln_g=rwkv_ln_g[layer], rwkv_ln_b=rwkv_ln_b[layer])

    rows = bp + bs
    pad = (-rows) % 8
    c_all = jnp.concatenate([c_prompt, c_sample, jnp.zeros((pad, d), F32)], axis=0)
    mod = _modulation(c_all, w_ada[layer], b_ada[layer])
    sh1, sc1, gt1, sh2, sc2, gt2 = [mod[:, i * d:(i + 1) * d] for i in range(6)]
    pm = lambda m: m[:bp].reshape(bp, 1, d)
    sm = lambda m: jnp.repeat(m[bp:bp + bs], ts, axis=0).reshape(1, bs * ts, d)

    w_in16 = w_in[layer].astype(BF16)
    wo16 = w_out[layer].astype(BF16)
    wo_r16, wo_a16 = wo16[:RWKV_WIDTH], wo16[RWKV_WIDTH:]
    wr_t = router_w[layer].T
    rb = router_b[layer].reshape(N_EXPERTS, 1)
    wu = exp_w_up.reshape(exp_w_up.shape[1:])
    wd = exp_w_down.reshape(exp_w_down.shape[1:])
    n_up = wu.shape[-1]
    bu = exp_b_up[layer].reshape(N_EXPERTS, n_up // GROUP, LANES, 2).transpose(0, 1, 3, 2).reshape(N_EXPERTS, n_up)

    pr_p, pa_p = _inproj(x_prompt, norm1_g[layer], pm(sh1), pm(sc1), w_in16, TM_PROJ)
    zero_prev = jnp.zeros((bp, 1, RWKV_PROJ), F32)
    zero_state = jnp.zeros((bp, RWKV_WIDTH // GROUP, GROUP, GROUP), F32)
    yr_p, s_p = _rwkv(pr_p, zero_prev, zero_state, lp, RWKV_SUB, None)
    zero_kv = jnp.zeros((bp, WINDOW, KV_WIDTH), F32)
    ya_p, kn_p = _swa(pa_p, zero_kv, zero_kv, rel_bias, sinks[layer], q_norm[layer], k_norm[layer],
                      CHUNK, SWA_QC, False)

    ns = bs * ts
    xs_flat = x_sample.reshape(1, ns, d)
    pr_s, pa_s = _inproj(xs_flat, norm1_g[layer], sm(sh1), sm(sc1), w_in16, ns)
    pr_s = pr_s.reshape(bs, ts, RWKV_PROJ)
    pa_s = pa_s.reshape(bs, ts, ATTN_PROJ)
    t_pad = -(-ts // CHUNK) * CHUNK
    pr_s_pad = jnp.pad(pr_s, ((0, 0), (0, t_pad - ts), (0, 0)))
    yr_s, s_s = _rwkv(pr_s_pad, state_shift[layer].reshape(bs, 1, RWKV_PROJ), _state_to_bd(state_wkv[layer]),
                      lp, t_pad // CHUNK, ts)
    yr_s = yr_s[:, :ts]
    ya_s, kn_s = _swa(pa_s, cache_win_k[layer].reshape(bs, WINDOW, KV_WIDTH),
                      cache_win_v[layer].reshape(bs, WINDOW, KV_WIDTH), rel_bias, sinks[layer],
                      q_norm[layer], k_norm[layer], ts, 1, True)

    base0 = jnp.zeros((N_EXPERTS, 1), F32)
    x1_p, h2_p, te_p, gate_p, pos_p, cnt_p = _mid(yr_p, ya_p, x_prompt, pm(gt1), pm(sh2), pm(sc2), norm2_g[layer],
                                                  wo_r16, wo_a16, wr_t, rb, base0, TM_PROJ)
    x1_s, h2_s, te_s, gate_s, pos_s, cnt = _mid(yr_s.reshape(1, ns, RWKV_WIDTH), ya_s.reshape(1, ns, ATTN_WIDTH),
                                                xs_flat, sm(gt1), sm(sh2), sm(sc2), norm2_g[layer],
                                                wo_r16, wo_a16, wr_t, rb, cnt_p, ns)

    n_p = bp * tp
    n_rows = (n_p + ns) * TOP_K + N_EXPERTS * (TM_FFN - 1)
    n_tiles = -(-n_rows // TM_FFN)
    pstart, tile_e, n_used, next_e, pads = _tile_plan(cnt, TM_FFN, n_tiles)
    dest = _slot_rows(jnp.concatenate([te_p, te_s], axis=1), jnp.concatenate([pos_p, pos_s], axis=1), pstart)
    dest_p, dest_s = dest[:, :n_p], dest[:, n_p:]
    xs_buf = _dispatch(dest, pads, h2_p.reshape(n_p, d), h2_s.reshape(ns, d), n_tiles * TM_FFN, TK_DISPATCH)
    ys_buf = _ffn(tile_e, n_used, next_e, xs_buf, wu, bu, wd, exp_b_down[layer], TM_FFN)
    y_p = _combine(dest_p, gate_p, x1_p, pm(gt2), ys_buf, TK_COMBINE)
    y_s = _combine(dest_s, gate_s, x1_s, sm(gt2), ys_buf, TK_COMBINE)

    kv4 = lambda z, rows_: z.reshape(z.shape[0], rows_, KV_HEADS, HEAD)[None]
    v_p = pa_p[:, tp - WINDOW:, ATTN_WIDTH + KV_WIDTH:]
    v_s = pa_s[:, :, ATTN_WIDTH + KV_WIDTH:]
    return (y_p, y_s.reshape(bs, ts, d),
            pr_p[:, tp - 1][None], _state_from_bd(s_p)[None], kv4(kn_p[:, tp - WINDOW:], WINDOW), kv4(v_p, WINDOW),
            pr_s[:, ts - 1][None], _state_from_bd(s_s)[None], kv4(kn_s, ts), kv4(v_s, ts))
```

```python
import functools
import math

import jax
import jax.numpy as jnp
from jax import lax
from jax.experimental import pallas as pl
from jax.experimental.pallas import tpu as pltpu

F32 = jnp.float32
BF16 = jnp.bfloat16
I32 = jnp.int32

HEAD = 64
RWKV_HEADS = 8
RWKV_WIDTH = RWKV_HEADS * HEAD
DECAY_RANK = 64
ICLR_RANK = 64
GATE_RANK = 128
RWKV_PROJ = 3 * RWKV_WIDTH + DECAY_RANK + ICLR_RANK + GATE_RANK
ATTN_HEADS = 8
KV_HEADS = 2
ATTN_WIDTH = ATTN_HEADS * HEAD
KV_WIDTH = KV_HEADS * HEAD
ATTN_PROJ = ATTN_WIDTH + 2 * KV_WIDTH
WINDOW = 128
CHUNK = 64
REL_BUCKETS = 32
REL_MAX_DIST = 128
N_EXPERTS = 32
TOP_K = 4
GN_EPS = 64e-5
RMS_EPS = 1e-6
NEG_INF = -1e30
ATTN_SCALE = HEAD ** -0.5
SWIGLU_ALPHA = 1.702
SWIGLU_LIMIT = 7.0

LANES = 128
GROUP = 256
VMEM_LIMIT = 56 * 1024 * 1024

TM_PROJ = 512
RWKV_SUB = 4
SWA_QC = 8
TM_FFN = 512
TK_DISPATCH = 1024
TK_COMBINE = 256
ROW_UNROLL = 8


def _cparams(*sem):
    return pltpu.CompilerParams(dimension_semantics=sem, vmem_limit_bytes=VMEM_LIMIT)


def _dot(a, b, nt=False):
    dims = (((1,), (1,)), ((), ())) if nt else (((1,), (0,)), ((), ()))
    return lax.dot_general(a, b, dims, preferred_element_type=F32)


def _split(x):
    hi = x.astype(BF16)
    lo = (x - hi.astype(F32)).astype(BF16)
    return hi, lo


def _mm(a, b, passes=1, nt=False):
    if passes == 1:
        return _dot(a.astype(BF16), b.astype(BF16), nt)
    ah, al = _split(a)
    bh, bl = _split(b)
    return _dot(ah, bh, nt) + _dot(al, bh, nt) + _dot(ah, bl, nt)


def _iota(shape, dim):
    return lax.broadcasted_iota(I32, shape, dim)


def _head_ones(n):
    r = lax.shift_right_logical(_iota((n, n), 0), 6)
    c = lax.shift_right_logical(_iota((n, n), 1), 6)
    return (r == c).astype(BF16)


def _head_sum(x):
    width = x.shape[1]
    x16 = x.astype(BF16)
    if width <= GROUP:
        return _dot(x16, _head_ones(width))
    ones = _head_ones(GROUP)
    parts = [_dot(x16[:, g:g + GROUP], ones) for g in range(0, width, GROUP)]
    return jnp.concatenate(parts, axis=1)


def _rms(x, eps):
    return x * lax.rsqrt(jnp.mean(x * x, axis=-1, keepdims=True) + eps)


def _mod_kernel(c_ref, w_ref, b_ref, o_ref):
    c = c_ref[...]
    s = c * jax.nn.sigmoid(c)
    o_ref[...] = _mm(s, w_ref[...], passes=3) + b_ref[...]


def _modulation(c_all, w_ada, b_ada):
    rows, d = c_all.shape
    n = w_ada.shape[1]
    tn = 1536
    return pl.pallas_call(
        _mod_kernel,
        grid=(n // tn,),
        in_specs=[pl.BlockSpec((rows, d), lambda j: (0, 0)),
                  pl.BlockSpec((d, tn), lambda j: (0, j)),
                  pl.BlockSpec((1, tn), lambda j: (0, j))],
        out_specs=pl.BlockSpec((rows, tn), lambda j: (0, j)),
        out_shape=jax.ShapeDtypeStruct((rows, n), F32),
        compiler_params=_cparams("arbitrary"),
        name="modulation",
    )(c_all, w_ada, b_ada.reshape(1, n))


def _inproj_kernel(x_ref, g_ref, sh_ref, sc_ref, w_ref, pr_ref, pa_ref):
    x = x_ref[...]
    h = _rms(x, RMS_EPS) * g_ref[...] * (1.0 + sc_ref[...]) + sh_ref[...]
    p = _dot(h.astype(BF16), w_ref[...])
    pr_ref[...] = p[:, :RWKV_PROJ]
    pa_ref[...] = p[:, RWKV_PROJ:]


def _inproj(x3, gain, sh3, sc3, w_in16, tm):
    g, r, d = x3.shape
    mr = sh3.shape[1]
    mrow = (lambda b, i: (b, 0, 0)) if mr == 1 else (lambda b, i: (b, i, 0))
    mblk = 1 if mr == 1 else tm
    return pl.pallas_call(
        _inproj_kernel,
        grid=(g, r // tm),
        in_specs=[pl.BlockSpec((None, tm, d), lambda b, i: (b, i, 0)),
                  pl.BlockSpec((1, d), lambda b, i: (0, 0)),
                  pl.BlockSpec((None, mblk, d), mrow),
                  pl.BlockSpec((None, mblk, d), mrow),
                  pl.BlockSpec(w_in16.shape, lambda b, i: (0, 0))],
        out_specs=[pl.BlockSpec((None, tm, RWKV_PROJ), lambda b, i: (b, i, 0)),
                   pl.BlockSpec((None, tm, ATTN_PROJ), lambda b, i: (b, i, 0))],
        out_shape=[jax.ShapeDtypeStruct((g, r, RWKV_PROJ), F32),
                   jax.ShapeDtypeStruct((g, r, ATTN_PROJ), F32)],
        compiler_params=_cparams("arbitrary", "arbitrary"),
        name="inproj",
    )(x3, gain.reshape(1, d), sh3, sc3, w_in16)


def _softplus(x):
    return jnp.maximum(x, 0.0) + jnp.log(1.0 + jnp.exp(-jnp.abs(x)))


def _rwkv_kernel(p_ref, prev0_ref, s0_ref, mu_ref, w0_ref, wwa_ref, a0_ref, gup_ref, kk_ref, ka_ref,
                 rk_ref, lng_ref, lnb_ref, y_ref, sout_ref, prev_sc, s_sc, *, nsub, t_valid):
    tc = nsub * CHUNK
    c = pl.program_id(1)

    @pl.when(c == 0)
    def _():
        prev_sc[...] = prev0_ref[...]
        s_sc[...] = s0_ref[...]

    p = p_ref[...]
    row = _iota((tc, 1), 0)
    p_prev = jnp.where(row == 0, prev_sc[...], pltpu.roll(p, 1, axis=0))
    prev_sc[...] = p_ref[tc - 1:tc, :]
    xp = p + (p_prev - p) * mu_ref[...]

    w3 = 3 * RWKV_WIDTH
    r = xp[:, 0:RWKV_WIDTH]
    k = xp[:, RWKV_WIDTH:2 * RWKV_WIDTH]
    v = xp[:, 2 * RWKV_WIDTH:w3]
    z = xp[:, w3:w3 + DECAY_RANK + ICLR_RANK]
    gd = xp[:, w3 + DECAY_RANK + ICLR_RANK:]
    zl = _iota((1, DECAY_RANK + ICLR_RANK), 1)
    zz = jnp.where(zl < DECAY_RANK, jnp.tanh(z), z)
    wa = _mm(zz, wwa_ref[...], passes=3)
    w_log = -_softplus(-(w0_ref[...] + wa[:, :RWKV_WIDTH])) - 0.5
    lw = -jnp.exp(w_log)
    a = jax.nn.sigmoid(a0_ref[...] + wa[:, RWKV_WIDTH:])
    g = _mm(jax.nn.sigmoid(gd), gup_ref[...])
    kk = k * kk_ref[...]
    kk = kk * lax.rsqrt(jnp.maximum(_head_sum(kk * kk), 1e-24))
    k2 = k * (1.0 + (a - 1.0) * ka_ref[...])
    bvec = kk * a
    if t_valid is not None:
        ok = (c * tc + row) < t_valid
        lw = jnp.where(ok, lw, 0.0)
        kk = jnp.where(ok, kk, 0.0)
        bvec = jnp.where(ok, bvec, 0.0)
        k2u = jnp.where(ok, k2, 0.0)
        vu = jnp.where(ok, v, 0.0)
    else:
        k2u, vu = k2, v

    tril16 = (_iota((CHUNK, CHUNK), 0) >= _iota((CHUNK, CHUNK), 1)).astype(BF16)
    bd_mask = (lax.shift_right_logical(_iota((GROUP, GROUP), 0), 6)
               == lax.shift_right_logical(_iota((GROUP, GROUP), 1), 6))
    t_idx = _iota((CHUNK, GROUP), 0)
    s_idx = jnp.bitwise_and(_iota((CHUNK, GROUP), 1), CHUNK - 1)
    strict = t_idx > s_idx
    incl = t_idx >= s_idx
    eye = (t_idx == s_idx).astype(F32)

    bd16 = bd_mask.astype(BF16)

    def bd(x):
        x16 = x.astype(BF16)
        return jnp.concatenate([x16, x16, x16, x16], axis=0) * bd16

    def b16(x):
        return x.astype(BF16)

    ng = RWKV_WIDTH // GROUP
    pairs = [(j, gi) for j in range(nsub) for gi in range(ng)]
    gsl = lambda gi: slice(gi * GROUP, (gi + 1) * GROUP)

    at, bt, kt, rt, bw, kw, vj, w_last = [], [], [], [], [], [], [], []
    for j in range(nsub):
        sl = slice(j * CHUNK, (j + 1) * CHUNK)
        lwj = lw[sl]
        h1 = lwj.astype(BF16)
        r1 = lwj - h1.astype(F32)
        h2 = r1.astype(BF16)
        h3 = (r1 - h2.astype(F32)).astype(BF16)
        cum = _dot(tril16, h1) + _dot(tril16, h2) + _dot(tril16, h3)
        e_cum = jnp.exp(cum)
        e_inv = jnp.exp(-cum)
        wl = e_cum[CHUNK - 1:CHUNK, :]
        at.append(-kk[sl] * jnp.exp(cum - lwj))
        bt.append(bvec[sl] * e_inv)
        kt.append(k2u[sl] * e_inv)
        rt.append(r[sl] * e_cum)
        bw.append(bt[j] * wl)
        kw.append(kt[j] * wl)
        vj.append(vu[sl])
        w_last.append(wl)

    a_ab, a_ak, a_rb, a_rk, vbd = {}, {}, {}, {}, {}
    for (j, gi) in pairs:
        gs = gsl(gi)
        q2 = b16(jnp.concatenate([at[j][:, gs], rt[j][:, gs]], axis=0))
        sc = _dot(q2, jnp.concatenate([bd(bt[j][:, gs]), bd(kt[j][:, gs])], axis=0), nt=True)
        a_ab[j, gi] = jnp.where(strict, sc[:CHUNK, :GROUP], 0.0)
        a_ak[j, gi] = jnp.where(strict, sc[:CHUNK, GROUP:], 0.0)
        a_rb[j, gi] = jnp.where(incl, sc[CHUNK:, :GROUP], 0.0)
        a_rk[j, gi] = jnp.where(incl, sc[CHUNK:, GROUP:], 0.0)
        vbd[j, gi] = bd(vj[j][:, gs])

    tinv = {p: eye + a_ab[p] for p in pairs}
    xpow = {p: _dot(b16(a_ab[p]), bd(a_ab[p])) for p in pairs}
    for step in range(1, 6):
        for p in pairs:
            if step < 5:
                both = _dot(b16(xpow[p]), jnp.concatenate([bd(tinv[p]), bd(xpow[p])], axis=1))
                tinv[p] = tinv[p] + both[:, :GROUP]
                xpow[p] = both[:, GROUP:]
            else:
                tinv[p] = tinv[p] + _dot(b16(xpow[p]), bd(tinv[p]))

    akv = {p: _dot(b16(a_ak[p]), vbd[p]) for p in pairs}
    pq = {(j, gi): _dot(b16(tinv[j, gi]), jnp.concatenate([bd(at[j][:, gsl(gi)]), bd(akv[j, gi])], axis=1))
          for (j, gi) in pairs}
    r2, y0, m_lr, n_add = {}, {}, {}, {}
    for (j, gi) in pairs:
        gs = gsl(gi)
        pm, qm = pq[j, gi][:, :GROUP], pq[j, gi][:, GROUP:]
        rr = _dot(b16(a_rb[j, gi]), jnp.concatenate([bd(pm), bd(qm)], axis=1))
        r2[j, gi] = b16(rt[j][:, gs] + rr[:, :GROUP])
        y0[j, gi] = rr[:, GROUP:] + _dot(b16(a_rk[j, gi]), vbd[j, gi])
        m_lr[j, gi] = b16(jnp.where(bd_mask, _dot(b16(pm.T), b16(bw[j][:, gs])), 0.0))
        qv = jnp.concatenate([qm, vj[j][:, gs]], axis=0)
        bk = jnp.concatenate([bw[j][:, gs], kw[j][:, gs]], axis=0)
        n_add[j, gi] = jnp.where(bd_mask, _dot(b16(qv.T), b16(bk)), 0.0)

    s_in = {}
    for gi in range(ng):
        s = s_sc[gi]
        for j in range(nsub):
            s16 = b16(s)
            s_in[j, gi] = s16
            s = s * w_last[j][:, gsl(gi)] + _dot(s16, m_lr[j, gi]) + n_add[j, gi]
        s_sc[gi] = s

    ys = []
    for j in range(nsub):
        yg = [_dot(r2[j, gi], s_in[j, gi], nt=True) + y0[j, gi] for gi in range(ng)]
        ys.append(jnp.concatenate(yg, axis=1))
    y = jnp.concatenate(ys, axis=0) if nsub > 1 else ys[0]

    inv_n = 1.0 / HEAD
    yc = y - _head_sum(y) * inv_n
    yn = yc * lax.rsqrt(_head_sum(yc * yc) * inv_n + GN_EPS)
    yn = yn * lng_ref[...] + lnb_ref[...]
    bonus = _head_sum(r * k2 * rk_ref[...]) * v
    y_ref[...] = (yn + bonus) * g
    sout_ref[...] = s_sc[...]


def _rwkv(p_r, prev0, s0_bd, lp, nsub, t_valid):
    b, t, _ = p_r.shape
    tc = nsub * CHUNK
    ng = RWKV_WIDTH // GROUP
    row = lambda a: a.reshape(1, -1)
    zeros = jnp.zeros((DECAY_RANK, RWKV_WIDTH), F32)
    wwa = jnp.concatenate([jnp.concatenate([lp["rwkv_w_up"], zeros], axis=1),
                           jnp.concatenate([zeros, lp["rwkv_a_up"]], axis=1)], axis=0)
    const = lambda shape: pl.BlockSpec(shape, lambda bi, ci: (0,) * len(shape))
    kern = functools.partial(_rwkv_kernel, nsub=nsub, t_valid=t_valid)
    return pl.pallas_call(
        kern,
        grid=(b, t // tc),
        in_specs=[pl.BlockSpec((None, tc, RWKV_PROJ), lambda bi, ci: (bi, ci, 0)),
                  pl.BlockSpec((None, 1, RWKV_PROJ), lambda bi, ci: (bi, 0, 0)),
                  pl.BlockSpec((None, ng, GROUP, GROUP), lambda bi, ci: (bi, 0, 0, 0)),
                  const((1, RWKV_PROJ)), const((1, RWKV_WIDTH)),
                  const((DECAY_RANK + ICLR_RANK, 2 * RWKV_WIDTH)), const((1, RWKV_WIDTH)),
                  const((GATE_RANK, RWKV_WIDTH)), const((1, RWKV_WIDTH)), const((1, RWKV_WIDTH)),
                  const((1, RWKV_WIDTH)), const((1, RWKV_WIDTH)), const((1, RWKV_WIDTH))],
        out_specs=[pl.BlockSpec((None, tc, RWKV_WIDTH), lambda bi, ci: (bi, ci, 0)),
                   pl.BlockSpec((None, ng, GROUP, GROUP), lambda bi, ci: (bi, 0, 0, 0))],
        out_shape=[jax.ShapeDtypeStruct((b, t, RWKV_WIDTH), F32),
                   jax.ShapeDtypeStruct((b, ng, GROUP, GROUP), F32)],
        scratch_shapes=[pltpu.VMEM((1, RWKV_PROJ), F32), pltpu.VMEM((ng, GROUP, GROUP), F32)],
        compiler_params=_cparams("arbitrary", "arbitrary"),
        name="rwkv7",
    )(p_r, prev0, s0_bd, row(lp["rwkv_mu"]), row(lp["rwkv_w0"]), wwa, row(lp["rwkv_a0"]),
      lp["rwkv_g_up"], row(lp["rwkv_k_k"]), row(lp["rwkv_k_a"]), row(lp["rwkv_r_k"]),
      row(lp["rwkv_ln_g"]), row(lp["rwkv_ln_b"]))


def _state_to_bd(s):
    b = s.shape[0]
    hg = GROUP // HEAD
    s5 = s.reshape(b, RWKV_HEADS // hg, hg, HEAD, HEAD)
    out = jnp.einsum("bghvk,hj->bghvjk", s5, jnp.eye(hg, dtype=s.dtype))
    return out.reshape(b, RWKV_HEADS // hg, GROUP, GROUP)


def _state_from_bd(s_bd):
    b = s_bd.shape[0]
    hg = GROUP // HEAD
    s6 = s_bd.reshape(b, RWKV_HEADS // hg, hg, HEAD, hg, HEAD)
    out = jnp.einsum("bghvjk,hj->bghvk", s6, jnp.eye(hg, dtype=s_bd.dtype))
    return out.reshape(b, RWKV_HEADS, HEAD, HEAD)


def _swa_kernel(tab_ref, sink_ref, pa_ref, ck_ref, cv_ref, bkt_ref, qg_ref, kg_ref,
                ya_ref, kn_ref, kall, vall, bias_sc, *, cq, nqc, has_cache):
    tq = cq * nqc
    band = WINDOW + cq
    bi = pl.program_id(0)
    i = pl.program_id(1)
    lane = _iota((1, LANES), 1)
    low = lane < HEAD

    @pl.when(jnp.logical_and(bi == 0, i == 0))
    def _():
        bkt = bkt_ref[...]
        for h in range(ATTN_HEADS):
            def body(j, acc, h=h):
                return acc + jnp.where(bkt == j, tab_ref[j, h], 0.0)
            bh = lax.fori_loop(0, REL_BUCKETS, body, jnp.zeros((cq, band), F32))
            bias_sc[h // 2, (h % 2) * cq:(h % 2 + 1) * cq, :] = bh

    @pl.when(i == 0)
    def _():
        kall[0:WINDOW, :] = ck_ref[...]
        vall[0:WINDOW, :] = cv_ref[...]

    pa = pa_ref[...]
    q = pa[:, :ATTN_WIDTH]
    k = pa[:, ATTN_WIDTH:ATTN_WIDTH + KV_WIDTH]
    v = pa[:, ATTN_WIDTH + KV_WIDTH:]
    inv_n = 1.0 / HEAD
    qn = q * lax.rsqrt(_head_sum(q * q) * inv_n + RMS_EPS) * qg_ref[...]
    kn = k * lax.rsqrt(_head_sum(k * k) * inv_n + RMS_EPS) * kg_ref[...]
    kn_ref[...] = kn
    kall[WINDOW:WINDOW + tq, :] = kn
    vall[WINDOW:WINDOW + tq, :] = v

    row2 = _iota((2 * cq, 1), 0)
    col = _iota((1, band), 1)
    thr = jnp.where(i == 0, WINDOW, 0)
    k_all = kall[...]
    v_all = vall[...]
    k_rot = pltpu.roll(k_all, HEAD, axis=1)
    v_rot = pltpu.roll(v_all, HEAD, axis=1)
    kd = [jnp.where(low, k_all, k_rot).astype(BF16), jnp.where(low, k_rot, k_all).astype(BF16)]
    vd = [jnp.where(low, v_all, v_rot).astype(BF16), jnp.where(low, v_rot, v_all).astype(BF16)]

    work = [(qc, pi) for qc in range(nqc) for pi in range(ATTN_HEADS // 2)]
    kv_of = lambda pi: (2 * pi) // (ATTN_HEADS // KV_HEADS)
    scores = {}
    for (qc, pi) in work:
        qp = qn[qc * cq:(qc + 1) * cq, pi * LANES:(pi + 1) * LANES]
        qs = jnp.concatenate([jnp.where(low, qp, 0.0), jnp.where(low, 0.0, qp)], axis=0)
        s = _dot(qs.astype(BF16), kd[kv_of(pi)][qc * cq:qc * cq + band], nt=True) * ATTN_SCALE + bias_sc[pi]
        if (not has_cache) and qc * cq < WINDOW:
            s = jnp.where((col + qc * cq) < thr, NEG_INF, s)
        scores[qc, pi] = s
    probs = {}
    for (qc, pi) in work:
        s = scores[qc, pi]
        sink = jnp.where(row2 < cq, sink_ref[0, 2 * pi], sink_ref[0, 2 * pi + 1])
        m = jnp.maximum(jnp.max(s, axis=-1, keepdims=True), sink)
        e = jnp.exp(s - m)
        den = jnp.sum(e, axis=-1, keepdims=True) + jnp.exp(sink - m)
        probs[qc, pi] = (e / den).astype(BF16)
    for (qc, pi) in work:
        o = _dot(probs[qc, pi], vd[kv_of(pi)][qc * cq:qc * cq + band])
        ya_ref[qc * cq:(qc + 1) * cq, pi * LANES:(pi + 1) * LANES] = jnp.where(low, o[:cq], o[cq:])

    if tq >= WINDOW:
        knext = kall[tq:tq + WINDOW, :]
        vnext = vall[tq:tq + WINDOW, :]
        kall[0:WINDOW, :] = knext
        vall[0:WINDOW, :] = vnext


def _t5_bucket(rel):
    half = REL_BUCKETS // 2
    max_exact = half // 2
    n = jnp.abs(rel)
    log_ratio = jnp.log(jnp.maximum(n, 1).astype(F32) / max_exact) / math.log(REL_MAX_DIST / max_exact)
    large = jnp.minimum(max_exact + (log_ratio * (half - max_exact)).astype(I32), half - 1)
    return jnp.where(rel > 0, half, 0) + jnp.where(n < max_exact, n, large)


def _swa(p_a, cache_k, cache_v, rel_table, sinks, q_norm, k_norm, cq, nqc, has_cache):
    b, t, _ = p_a.shape
    tq = cq * nqc
    band = WINDOW + cq
    rel = (jnp.arange(band) - WINDOW)[None, :] - jnp.arange(cq)[:, None]
    bkt = _t5_bucket(rel).astype(I32)
    qg = jnp.tile(q_norm, ATTN_HEADS).reshape(1, ATTN_WIDTH)
    kg = jnp.tile(k_norm, KV_HEADS).reshape(1, KV_WIDTH)
    kern = functools.partial(_swa_kernel, cq=cq, nqc=nqc, has_cache=has_cache)
    smem = pl.BlockSpec(memory_space=pltpu.SMEM)
    return pl.pallas_call(
        kern,
        grid=(b, t // tq),
        in_specs=[smem, smem,
                  pl.BlockSpec((None, tq, ATTN_PROJ), lambda bi, i: (bi, i, 0)),
                  pl.BlockSpec((None, WINDOW, KV_WIDTH), lambda bi, i: (bi, 0, 0)),
                  pl.BlockSpec((None, WINDOW, KV_WIDTH), lambda bi, i: (bi, 0, 0)),
                  pl.BlockSpec((cq, band), lambda bi, i: (0, 0)),
                  pl.BlockSpec((1, ATTN_WIDTH), lambda bi, i: (0, 0)),
                  pl.BlockSpec((1, KV_WIDTH), lambda bi, i: (0, 0))],
        out_specs=[pl.BlockSpec((None, tq, ATTN_WIDTH), lambda bi, i: (bi, i, 0)),
                   pl.BlockSpec((None, tq, KV_WIDTH), lambda bi, i: (bi, i, 0))],
        out_shape=[jax.ShapeDtypeStruct((b, t, ATTN_WIDTH), F32),
                   jax.ShapeDtypeStruct((b, t, KV_WIDTH), F32)],
        scratch_shapes=[pltpu.VMEM((WINDOW + tq, KV_WIDTH), F32),
                        pltpu.VMEM((WINDOW + tq, KV_WIDTH), F32),
                        pltpu.VMEM((ATTN_HEADS // 2, 2 * cq, band), F32)],
        compiler_params=_cparams("arbitrary", "arbitrary"),
        name="swa",
    )(rel_table, sinks.reshape(1, ATTN_HEADS), p_a, cache_k, cache_v, bkt, qg, kg)


def _mid_kernel(yr_ref, ya_ref, x_ref, gt1_ref, sh2_ref, sc2_ref, g2_ref, wor_ref, woa_ref, wrt_ref,
                rb_ref, base0_ref, x1_ref, h2_ref, te_ref, gate_ref, pos_ref, cnt_ref, base_sc, *, tm):
    @pl.when(jnp.logical_and(pl.program_id(0) == 0, pl.program_id(1) == 0))
    def _():
        base_sc[...] = base0_ref[...]

    mix = _dot(yr_ref[...].astype(BF16), wor_ref[...]) + _dot(ya_ref[...].astype(BF16), woa_ref[...])
    x1 = x_ref[...] + gt1_ref[...] * mix
    x1_ref[...] = x1
    h2 = _rms(x1, RMS_EPS) * g2_ref[...] * (1.0 + sc2_ref[...]) + sh2_ref[...]
    h2_ref[...] = h2

    logits = _mm(wrt_ref[...], h2, passes=3, nt=True) + rb_ref[...]
    eidx = _iota((N_EXPERTS, tm), 0)
    work = logits
    tops, hots = [], []
    for j in range(TOP_K):
        m = jnp.max(work, axis=0, keepdims=True)
        idx = jnp.min(jnp.where(work == m, eidx, N_EXPERTS), axis=0, keepdims=True)
        hot = eidx == idx
        work = jnp.where(hot, -jnp.inf, work)
        tops.append(m)
        hots.append(hot)
        te_ref[j:j + 1, :] = idx
    exps = [jnp.exp(tj - tops[0]) for tj in tops]
    den = exps[0] + exps[1] + exps[2] + exps[3]
    for j in range(TOP_K):
        gate_ref[j:j + 1, :] = exps[j] / den

    member = (hots[0] | hots[1] | hots[2] | hots[3]).astype(BF16)
    upper = (_iota((tm, tm), 0) < _iota((tm, tm), 1)).astype(BF16)
    before = _dot(member, upper) + base_sc[...]
    for j in range(TOP_K):
        pos_ref[j:j + 1, :] = jnp.sum(jnp.where(hots[j], before, 0.0), axis=0, keepdims=True).astype(I32)
    base_sc[...] = base_sc[...] + jnp.sum(member.astype(F32), axis=1, keepdims=True)
    cnt_ref[...] = base_sc[...]


def _mid(y_r, y_a, x3, gt1, sh2, sc2, gain2, wo_r16, wo_a16, wr_t, rb, base0, tm):
    g, r, d = x3.shape
    n = g * r
    per = r // tm
    mr = gt1.shape[1]
    mrow = (lambda b, i: (b, 0, 0)) if mr == 1 else (lambda b, i: (b, i, 0))
    mblk = 1 if mr == 1 else tm
    tok = lambda w: pl.BlockSpec((None, tm, w), lambda b, i: (b, i, 0))
    mod = pl.BlockSpec((None, mblk, d), mrow)
    const = lambda shape: pl.BlockSpec(shape, lambda b, i: (0,) * len(shape))
    lane_out = pl.BlockSpec((TOP_K, tm), lambda b, i: (0, b * per + i))
    kern = functools.partial(_mid_kernel, tm=tm)
    return pl.pallas_call(
        kern,
        grid=(g, per),
        in_specs=[tok(RWKV_WIDTH), tok(ATTN_WIDTH), tok(d), mod, mod, mod, const((1, d)),
                  const((RWKV_WIDTH, d)), const((ATTN_WIDTH, d)), const((N_EXPERTS, d)),
                  const((N_EXPERTS, 1)), const((N_EXPERTS, 1))],
        out_specs=[tok(d), tok(d), lane_out, lane_out, lane_out, const((N_EXPERTS, 1))],
        out_shape=[jax.ShapeDtypeStruct((g, r, d), F32), jax.ShapeDtypeStruct((g, r, d), F32),
                   jax.ShapeDtypeStruct((TOP_K, n), I32), jax.ShapeDtypeStruct((TOP_K, n), F32),
                   jax.ShapeDtypeStruct((TOP_K, n), I32), jax.ShapeDtypeStruct((N_EXPERTS, 1), F32)],
        scratch_shapes=[pltpu.VMEM((N_EXPERTS, 1), F32)],
        compiler_params=_cparams("arbitrary", "arbitrary"),
        name="mid",
    )(y_r, y_a, x3, gt1, sh2, sc2, gain2.reshape(1, d), wo_r16, wo_a16, wr_t, rb, base0)


def _slot_kernel(pstart_ref, te_ref, pos_ref, o_ref):
    te = te_ref[...]
    acc = pos_ref[...]
    for e in range(N_EXPERTS):
        acc = acc + jnp.where(te == e, pstart_ref[e], 0)
    o_ref[...] = acc


def _slot_rows(te, pos, pstart):
    k, n = te.shape
    return pl.pallas_call(
        _slot_kernel,
        grid=(1,),
        in_specs=[pl.BlockSpec(memory_space=pltpu.SMEM),
                  pl.BlockSpec((k, n), lambda i: (0, 0)), pl.BlockSpec((k, n), lambda i: (0, 0))],
        out_specs=pl.BlockSpec((k, n), lambda i: (0, 0)),
        out_shape=jax.ShapeDtypeStruct((k, n), I32),
        compiler_params=_cparams("arbitrary"),
        name="slot_rows",
    )(pstart, te, pos)


def _dispatch_kernel(dst_ref, pad_ref, h2a_ref, h2b_ref, xs_ref, ztile, sem, zsem, *, tk, steps_a):
    i = pl.program_id(0)

    @pl.when(i == 0)
    def _():
        ztile[...] = jnp.zeros_like(ztile)
        zrow = ztile.at[pl.ds(0, 1)]
        tm = ztile.shape[0]

        def zero_row(row, priority):
            pltpu.make_async_copy(zrow, xs_ref.at[pl.ds(row, 1)], zsem).start(priority=priority)

        def per_expert(e, total):
            first = pad_ref[0, e]
            count = pad_ref[1, e]

            def per_pair(r, c):
                zero_row(first + 2 * r, 0)
                zero_row(first + 2 * r + 1, 1)
                return c
            lax.fori_loop(0, count // 2, per_pair, 0)

            @pl.when(count % 2 == 1)
            def _():
                zero_row(first + count - 1, 0)
            return total + count
        n_pad = lax.fori_loop(0, N_EXPERTS, per_expert, 0)

        blk = 64
        def drain_blk(r, c):
            pltpu.make_async_copy(ztile.at[pl.ds(0, blk)], xs_ref.at[pl.ds(0, blk)], zsem).wait()
            return c
        lax.fori_loop(0, n_pad // blk, drain_blk, 0)

        def drain_row(r, c):
            pltpu.make_async_copy(zrow, xs_ref.at[pl.ds(0, 1)], zsem).wait()
            return c
        lax.fori_loop(0, n_pad % blk, drain_row, 0)

        def spare(t, c):
            row0 = pl.multiple_of(t * tm, tm)
            pltpu.make_async_copy(ztile, xs_ref.at[pl.ds(row0, tm)], zsem).start()
            return c
        lax.fori_loop(pad_ref[2, 0], pad_ref[2, 1], spare, 0)

        def spare_wait(t, c):
            pltpu.make_async_copy(ztile, xs_ref.at[pl.ds(0, tm)], zsem).wait()
            return c
        lax.fori_loop(pad_ref[2, 0], pad_ref[2, 1], spare_wait, 0)

    def scatter(h2_ref):
        for t in range(h2_ref.shape[0] * ROW_UNROLL):
            for j in range(TOP_K):
                dst = dst_ref[j, t]
                pltpu.make_async_copy(h2_ref.at[t // ROW_UNROLL, pl.ds(t % ROW_UNROLL, 1)],
                                      xs_ref.at[pl.ds(dst, 1)], sem).start(priority=j % 2)
        for _ in range(TOP_K):
            pltpu.make_async_copy(h2_ref, h2_ref, sem).wait()

    pl.when(i < steps_a)(functools.partial(scatter, h2a_ref))
    pl.when(i == steps_a)(functools.partial(scatter, h2b_ref))


def _dispatch(dest, pads, h2a, h2b, n_rows, tk):
    na, d = h2a.shape
    nb = h2b.shape[0]
    steps_a = na // tk
    dest = jnp.pad(dest, ((0, 0), (0, (steps_a + 1) * tk - na - nb)))
    h2a = h2a.reshape(na // ROW_UNROLL, ROW_UNROLL, d)
    h2b = h2b.reshape(nb // ROW_UNROLL, ROW_UNROLL, d)
    kern = functools.partial(_dispatch_kernel, tk=tk, steps_a=steps_a)
    return pl.pallas_call(
        kern,
        grid=(steps_a + 1,),
        in_specs=[pl.BlockSpec((TOP_K, tk), lambda i: (0, i), memory_space=pltpu.SMEM),
                  pl.BlockSpec(memory_space=pltpu.SMEM),
                  pl.BlockSpec((tk // ROW_UNROLL, ROW_UNROLL, d), lambda i: (jnp.minimum(i, steps_a - 1), 0, 0)),
                  pl.BlockSpec((nb // ROW_UNROLL, ROW_UNROLL, d), lambda i: (0, 0, 0))],
        out_specs=pl.BlockSpec(memory_space=pl.ANY),
        out_shape=jax.ShapeDtypeStruct((n_rows, d), F32),
        scratch_shapes=[pltpu.VMEM((TM_FFN, d), F32), pltpu.SemaphoreType.DMA(()), pltpu.SemaphoreType.DMA(())],
        compiler_params=_cparams("arbitrary"),
        name="dispatch",
    )(dest, pads, h2a, h2b)


def _ffn_kernel(te_ref, nu_ref, nx_ref, xs_ref, wu_hbm, bu_ref, wd_hbm, bd_ref, ys_ref,
                wu32, wd32, wu16, wd16, sem):
    i = pl.program_id(0)
    n_up = wu32.shape[1]
    active = i < nu_ref[0]
    fresh = jnp.logical_or(i == 0, te_ref[i] != te_ref[jnp.maximum(i - 1, 0)])

    def weight_copies(e):
        return (pltpu.make_async_copy(wu_hbm.at[e], wu32, sem.at[0]),
                pltpu.make_async_copy(wd_hbm.at[e], wd32, sem.at[1]))

    @pl.when(i == 0)
    def _():
        for cp in weight_copies(te_ref[0]):
            cp.start()

    @pl.when(jnp.logical_and(active, fresh))
    def _():
        for cp in weight_copies(te_ref[i]):
            cp.wait()
        src = _iota((GROUP, GROUP), 0)
        dst = _iota((GROUP, GROUP), 1)
        want = jnp.where(dst < LANES, 2 * dst, 2 * (dst - LANES) + 1)
        perm = (src == want).astype(BF16)
        for b in range(n_up // GROUP):
            cs = slice(b * GROUP, (b + 1) * GROUP)
            wu16[:, cs] = _dot(wu32[:, cs].astype(BF16), perm).astype(BF16)
        wd16[...] = wd32[...].astype(BF16)

        @pl.when(nx_ref[i] >= 0)
        def _():
            for cp in weight_copies(nx_ref[i]):
                cp.start()

    @pl.when(active)
    def _():
        u = _dot(xs_ref[...].astype(BF16), wu16[...]) + bu_ref[...]
        acts = []
        for b in range(n_up // GROUP):
            glu = jnp.minimum(u[:, b * GROUP:b * GROUP + LANES], SWIGLU_LIMIT)
            lin = jnp.clip(u[:, b * GROUP + LANES:(b + 1) * GROUP], -SWIGLU_LIMIT, SWIGLU_LIMIT)
            acts.append((glu * jax.nn.sigmoid(SWIGLU_ALPHA * glu) * (lin + 1.0)).astype(BF16))
        act = jnp.concatenate(acts, axis=1)
        ys_ref[...] = _dot(act, wd16[...]) + bd_ref[...]

    @pl.when(jnp.logical_not(active))
    def _():
        ys_ref[...] = jnp.zeros_like(ys_ref)


def _ffn(tile_e, n_used, next_e, xs, wu, bu, wd, bd_, tm):
    rows, d = xs.shape
    n_exp, _, n_up = wu.shape
    dff = wd.shape[1]
    row_blk = lambda i, te, nu, nx: (jnp.minimum(i, nu[0] - 1), 0)
    exp_blk = lambda i, te, nu, nx: (te[i], 0, 0)
    grid_spec = pltpu.PrefetchScalarGridSpec(
        num_scalar_prefetch=3,
        grid=(rows // tm,),
        in_specs=[pl.BlockSpec((tm, d), row_blk),
                  pl.BlockSpec(memory_space=pl.ANY),
                  pl.BlockSpec((None, 1, n_up), exp_blk),
                  pl.BlockSpec(memory_space=pl.ANY),
                  pl.BlockSpec((None, 1, d), exp_blk)],
        out_specs=pl.BlockSpec((tm, d), lambda i, te, nu, nx: (i, 0)),
        scratch_shapes=[pltpu.VMEM((d, n_up), F32), pltpu.VMEM((dff, d), F32),
                        pltpu.VMEM((d, n_up), BF16), pltpu.VMEM((dff, d), BF16),
                        pltpu.SemaphoreType.DMA((2,))],
    )
    return pl.pallas_call(
        _ffn_kernel,
        grid_spec=grid_spec,
        out_shape=jax.ShapeDtypeStruct((rows, d), F32),
        compiler_params=_cparams("arbitrary"),
        name="expert_ffn",
    )(tile_e, n_used, next_e, xs, wu, bu.reshape(n_exp, 1, n_up), wd, bd_.reshape(n_exp, 1, d))


def _combine_kernel(src_ref, srcn_ref, gate_ref, x1_ref, gt2_ref, ys_ref, o_ref, buf, sem, *, tk):
    i = pl.program_id(1)
    n = pl.num_programs(1)
    flat = pl.program_id(0) * n + i
    total = pl.num_programs(0) * n
    slot = lax.rem(flat, 2)

    def issue(rows_ref, s, t0, t1):
        for t in range(t0, t1):
            for j in range(TOP_K):
                src = rows_ref[j, t]
                pltpu.make_async_copy(ys_ref.at[pl.ds(src, 1)],
                                      buf.at[s, j, t // ROW_UNROLL, pl.ds(t % ROW_UNROLL, 1)],
                                      sem.at[s]).start(priority=j % 2)

    @pl.when(flat == 0)
    def _():
        issue(src_ref, 0, 0, tk)

    def step(cur):
        @pl.when(flat + 1 < total)
        def _():
            issue(srcn_ref, 1 - cur, 0, tk)

        for j in range(TOP_K):
            pltpu.make_async_copy(buf.at[cur, j], buf.at[cur, j], sem.at[cur]).wait()
        eye = (_iota((tk, tk), 0) == _iota((tk, tk), 1)).astype(BF16)
        g_hi, g_lo = _split(gate_ref[...])
        gate = _dot(eye, g_hi, nt=True) + _dot(eye, g_lo, nt=True)
        rows = lambda j: buf[cur, j].reshape(tk, buf.shape[-1])
        acc = gate[:, 0:1] * rows(0)
        for j in range(1, TOP_K):
            acc = acc + gate[:, j:j + 1] * rows(j)
        o_ref[...] = x1_ref[...] + gt2_ref[...] * acc

    for cur in range(2):
        pl.when(slot == cur)(functools.partial(step, cur))


def _combine(dest, gates, x1, gt2, ys, tk):
    g, r, d = x1.shape
    per = r // tk
    last = g * per - 1
    mr = gt2.shape[1]
    mrow = (lambda b, i: (b, 0, 0)) if mr == 1 else (lambda b, i: (b, i, 0))
    mblk = 1 if mr == 1 else tk
    cur = pl.BlockSpec((TOP_K, tk), lambda b, i: (0, b * per + i), memory_space=pltpu.SMEM)
    nxt = pl.BlockSpec((TOP_K, tk), lambda b, i: (0, jnp.minimum(b * per + i + 1, last)),
                       memory_space=pltpu.SMEM)
    kern = functools.partial(_combine_kernel, tk=tk)
    return pl.pallas_call(
        kern,
        grid=(g, per),
        in_specs=[cur, nxt,
                  pl.BlockSpec((TOP_K, tk), lambda b, i: (0, b * per + i)),
                  pl.BlockSpec((None, tk, d), lambda b, i: (b, i, 0)),
                  pl.BlockSpec((None, mblk, d), mrow),
                  pl.BlockSpec(memory_space=pl.ANY)],
        out_specs=pl.BlockSpec((None, tk, d), lambda b, i: (b, i, 0)),
        out_shape=jax.ShapeDtypeStruct((g, r, d), F32),
        scratch_shapes=[pltpu.VMEM((2, TOP_K, tk // ROW_UNROLL, ROW_UNROLL, d), F32),
                        pltpu.SemaphoreType.DMA((2,))],
        compiler_params=_cparams("arbitrary", "arbitrary"),
        name="combine",
    )(dest, dest, gates, x1, gt2, ys)


def _tile_plan(counts, tm, n_tiles):
    cnt = counts.reshape(-1).astype(I32)
    tiles = (cnt + tm - 1) // tm
    tile_end = jnp.cumsum(tiles)
    pstart = ((tile_end - tiles) * tm).astype(I32)
    n_used = tile_end[-1:].astype(I32)
    idx = jnp.minimum(jnp.arange(n_tiles, dtype=I32), n_used[0] - 1)
    tile_e = jnp.sum((tile_end[None, :] <= idx[:, None]).astype(I32), axis=1)
    spare = jnp.zeros((N_EXPERTS,), I32).at[0].set(n_used[0]).at[1].set(n_tiles)
    pads = jnp.stack([pstart + cnt, tiles * tm - cnt, spare]).astype(I32)
    tile_e = jnp.minimum(tile_e, N_EXPERTS - 1)
    end_of = tile_end[tile_e]
    next_e = jnp.where(end_of < n_used[0], tile_e[jnp.minimum(end_of, n_tiles - 1)], -1).astype(I32)
    return pstart, tile_e, n_used, next_e, pads


def kernel(x_prompt, x_sample, state_shift, state_wkv, cache_win_k, cache_win_v, c_prompt, c_sample, rel_bias, norm1_g, norm2_g, w_ada, b_ada, w_in, w_out, rwkv_mu, rwkv_w0, rwkv_w_up, rwkv_a0, rwkv_a_up, rwkv_g_up, rwkv_k_k, rwkv_k_a, rwkv_r_k, rwkv_ln_g, rwkv_ln_b, q_norm, k_norm, sinks, router_w, router_b, exp_w_up, exp_b_up, exp_w_down, exp_b_down):
    bp, tp, d = x_prompt.shape
    bs, ts, _ = x_sample.shape
    depth = w_in.shape[0]
    assert depth == 1, "single-layer trunk"
    layer = 0
    lp = dict(rwkv_mu=rwkv_mu[layer], rwkv_w0=rwkv_w0[layer], rwkv_w_up=rwkv_w_up[layer],
              rwkv_a0=rwkv_a0[layer], rwkv_a_up=rwkv_a_up[layer], rwkv_g_up=rwkv_g_up[layer],
              rwkv_k_k=rwkv_k_k[layer], rwkv_k_a=rwkv_k_a[layer], rwkv_r_k=rwkv_r_k[layer].reshape(-1),
              rwkv_ln_g=rwkv_ln_g[layer], rwkv_ln_b=rwkv_ln_b[layer])

    rows = bp + bs
    pad = (-rows) % 8
    c_all = jnp.concatenate([c_prompt, c_sample, jnp.zeros((pad, d), F32)], axis=0)
    mod = _modulation(c_all, w_ada[layer], b_ada[layer])
    sh1, sc1, gt1, sh2, sc2, gt2 = [mod[:, i * d:(i + 1) * d] for i in range(6)]
    pm = lambda m: m[:bp].reshape(bp, 1, d)
    sm = lambda m: jnp.repeat(m[bp:bp + bs], ts, axis=0).reshape(1, bs * ts, d)

    w_in16 = w_in[layer].astype(BF16)
    wo16 = w_out[layer].astype(BF16)
    wo_r16, wo_a16 = wo16[:RWKV_WIDTH], wo16[RWKV_WIDTH:]
    wr_t = router_w[layer].T
    rb = router_b[layer].reshape(N_EXPERTS, 1)
    wu = exp_w_up.reshape(exp_w_up.shape[1:])
    wd = exp_w_down.reshape(exp_w_down.shape[1:])
    n_up = wu.shape[-1]
    bu = exp_b_up[layer].reshape(N_EXPERTS, n_up // GROUP, LANES, 2).transpose(0, 1, 3, 2).reshape(N_EXPERTS, n_up)

    pr_p, pa_p = _inproj(x_prompt, norm1_g[layer], pm(sh1), pm(sc1), w_in16, TM_PROJ)
    zero_prev = jnp.zeros((bp, 1, RWKV_PROJ), F32)
    zero_state = jnp.zeros((bp, RWKV_WIDTH // GROUP, GROUP, GROUP), F32)
    yr_p, s_p = _rwkv(pr_p, zero_prev, zero_state, lp, RWKV_SUB, None)
    zero_kv = jnp.zeros((bp, WINDOW, KV_WIDTH), F32)
    ya_p, kn_p = _swa(pa_p, zero_kv, zero_kv, rel_bias, sinks[layer], q_norm[layer], k_norm[layer],
                      CHUNK, SWA_QC, False)

    ns = bs * ts
    xs_flat = x_sample.reshape(1, ns, d)
    pr_s, pa_s = _inproj(xs_flat, norm1_g[layer], sm(sh1), sm(sc1), w_in16, ns)
    pr_s = pr_s.reshape(bs, ts, RWKV_PROJ)
    pa_s = pa_s.reshape(bs, ts, ATTN_PROJ)
    t_pad = -(-ts // CHUNK) * CHUNK
    pr_s_pad = jnp.pad(pr_s, ((0, 0), (0, t_pad - ts), (0, 0)))
    yr_s, s_s = _rwkv(pr_s_pad, state_shift[layer].reshape(bs, 1, RWKV_PROJ), _state_to_bd(state_wkv[layer]),
                      lp, t_pad // CHUNK, ts)
    yr_s = yr_s[:, :ts]
    ya_s, kn_s = _swa(pa_s, cache_win_k[layer].reshape(bs, WINDOW, KV_WIDTH),
                      cache_win_v[layer].reshape(bs, WINDOW, KV_WIDTH), rel_bias, sinks[layer],
                      q_norm[layer], k_norm[layer], ts, 1, True)

    base0 = jnp.zeros((N_EXPERTS, 1), F32)
    x1_p, h2_p, te_p, gate_p, pos_p, cnt_p = _mid(yr_p, ya_p, x_prompt, pm(gt1), pm(sh2), pm(sc2), norm2_g[layer],
                                                  wo_r16, wo_a16, wr_t, rb, base0, TM_PROJ)
    x1_s, h2_s, te_s, gate_s, pos_s, cnt = _mid(yr_s.reshape(1, ns, RWKV_WIDTH), ya_s.reshape(1, ns, ATTN_WIDTH),
                                                xs_flat, sm(gt1), sm(sh2), sm(sc2), norm2_g[layer],
                                                wo_r16, wo_a16, wr_t, rb, cnt_p, ns)

    n_p = bp * tp
    n_rows = (n_p + ns) * TOP_K + N_EXPERTS * (TM_FFN - 1)
    n_tiles = -(-n_rows // TM_FFN)
    pstart, tile_e, n_used, next_e, pads = _tile_plan(cnt, TM_FFN, n_tiles)
    dest = _slot_rows(jnp.concatenate([te_p, te_s], axis=1), jnp.concatenate([pos_p, pos_s], axis=1), pstart)
    dest_p, dest_s = dest[:, :n_p], dest[:, n_p:]
    xs_buf = _dispatch(dest, pads, h2_p.reshape(n_p, d), h2_s.reshape(ns, d), n_tiles * TM_FFN, TK_DISPATCH)
    ys_buf = _ffn(tile_e, n_used, next_e, xs_buf, wu, bu, wd, exp_b_down[layer], TM_FFN)
    y_p = _combine(dest_p, gate_p, x1_p, pm(gt2), ys_buf, TK_COMBINE)
    y_s = _combine(dest_s, gate_s, x1_s, sm(gt2), ys_buf, TK_COMBINE)

    kv4 = lambda z, rows_: z.reshape(z.shape[0], rows_, KV_HEADS, HEAD)[None]
    v_p = pa_p[:, tp - WINDOW:, ATTN_WIDTH + KV_WIDTH:]
    v_s = pa_s[:, :, ATTN_WIDTH + KV_WIDTH:]
    return (y_p, y_s.reshape(bs, ts, d),
            pr_p[:, tp - 1][None], _state_from_bd(s_p)[None], kv4(kn_p[:, tp - WINDOW:], WINDOW), kv4(v_p, WINDOW),
            pr_s[:, ts - 1][None], _state_from_bd(s_s)[None], kv4(kn_s, ts), kv4(v_s, ts))
```

```python
import functools
import math

import jax
import jax.numpy as jnp
from jax import lax
from jax.experimental import pallas as pl
from jax.experimental.pallas import tpu as pltpu

F32 = jnp.float32
BF16 = jnp.bfloat16
I32 = jnp.int32

HEAD = 64
RWKV_HEADS = 8
RWKV_WIDTH = RWKV_HEADS * HEAD
DECAY_RANK = 64
ICLR_RANK = 64
GATE_RANK = 128
RWKV_PROJ = 3 * RWKV_WIDTH + DECAY_RANK + ICLR_RANK + GATE_RANK
ATTN_HEADS = 8
KV_HEADS = 2
ATTN_WIDTH = ATTN_HEADS * HEAD
KV_WIDTH = KV_HEADS * HEAD
ATTN_PROJ = ATTN_WIDTH + 2 * KV_WIDTH
WINDOW = 128
CHUNK = 64
REL_BUCKETS = 32
REL_MAX_DIST = 128
N_EXPERTS = 32
TOP_K = 4
GN_EPS = 64e-5
RMS_EPS = 1e-6
NEG_INF = -1e30
ATTN_SCALE = HEAD ** -0.5
SWIGLU_ALPHA = 1.702
SWIGLU_LIMIT = 7.0

LANES = 128
GROUP = 256
VMEM_LIMIT = 56 * 1024 * 1024

TM_PROJ = 512
RWKV_SUB = 4
SWA_QC = 8
TM_FFN = 512
TK_DISPATCH = 1024
TK_COMBINE = 512
ROW_UNROLL = 8


def _cparams(*sem):
    return pltpu.CompilerParams(dimension_semantics=sem, vmem_limit_bytes=VMEM_LIMIT)


def _dot(a, b, nt=False):
    dims = (((1,), (1,)), ((), ())) if nt else (((1,), (0,)), ((), ()))
    return lax.dot_general(a, b, dims, preferred_element_type=F32)


def _split(x):
    hi = x.astype(BF16)
    lo = (x - hi.astype(F32)).astype(BF16)
    return hi, lo


def _mm(a, b, passes=1, nt=False):
    if passes == 1:
        return _dot(a.astype(BF16), b.astype(BF16), nt)
    ah, al = _split(a)
    bh, bl = _split(b)
    return _dot(ah, bh, nt) + _dot(al, bh, nt) + _dot(ah, bl, nt)


def _iota(shape, dim):
    return lax.broadcasted_iota(I32, shape, dim)


def _head_ones(n):
    r = lax.shift_right_logical(_iota((n, n), 0), 6)
    c = lax.shift_right_logical(_iota((n, n), 1), 6)
    return (r == c).astype(BF16)


def _head_sum(x):
    width = x.shape[1]
    x16 = x.astype(BF16)
    if width <= GROUP:
        return _dot(x16, _head_ones(width))
    ones = _head_ones(GROUP)
    parts = [_dot(x16[:, g:g + GROUP], ones) for g in range(0, width, GROUP)]
    return jnp.concatenate(parts, axis=1)


def _rms(x, eps):
    return x * lax.rsqrt(jnp.mean(x * x, axis=-1, keepdims=True) + eps)


def _mod_kernel(c_ref, w_ref, b_ref, o_ref):
    c = c_ref[...]
    s = c * jax.nn.sigmoid(c)
    o_ref[...] = _mm(s, w_ref[...], passes=3) + b_ref[...]


def _modulation(c_all, w_ada, b_ada):
    rows, d = c_all.shape
    n = w_ada.shape[1]
    tn = 1536
    return pl.pallas_call(
        _mod_kernel,
        grid=(n // tn,),
        in_specs=[pl.BlockSpec((rows, d), lambda j: (0, 0)),
                  pl.BlockSpec((d, tn), lambda j: (0, j)),
                  pl.BlockSpec((1, tn), lambda j: (0, j))],
        out_specs=pl.BlockSpec((rows, tn), lambda j: (0, j)),
        out_shape=jax.ShapeDtypeStruct((rows, n), F32),
        compiler_params=_cparams("arbitrary"),
        name="modulation",
    )(c_all, w_ada, b_ada.reshape(1, n))


def _inproj_kernel(x_ref, g_ref, sh_ref, sc_ref, w_ref, pr_ref, pa_ref):
    x = x_ref[...]
    h = _rms(x, RMS_EPS) * g_ref[...] * (1.0 + sc_ref[...]) + sh_ref[...]
    p = _dot(h.astype(BF16), w_ref[...])
    pr_ref[...] = p[:, :RWKV_PROJ]
    pa_ref[...] = p[:, RWKV_PROJ:]


def _inproj(x3, gain, sh3, sc3, w_in16, tm):
    g, r, d = x3.shape
    mr = sh3.shape[1]
    mrow = (lambda b, i: (b, 0, 0)) if mr == 1 else (lambda b, i: (b, i, 0))
    mblk = 1 if mr == 1 else tm
    return pl.pallas_call(
        _inproj_kernel,
        grid=(g, r // tm),
        in_specs=[pl.BlockSpec((None, tm, d), lambda b, i: (b, i, 0)),
                  pl.BlockSpec((1, d), lambda b, i: (0, 0)),
                  pl.BlockSpec((None, mblk, d), mrow),
                  pl.BlockSpec((None, mblk, d), mrow),
                  pl.BlockSpec(w_in16.shape, lambda b, i: (0, 0))],
        out_specs=[pl.BlockSpec((None, tm, RWKV_PROJ), lambda b, i: (b, i, 0)),
                   pl.BlockSpec((None, tm, ATTN_PROJ), lambda b, i: (b, i, 0))],
        out_shape=[jax.ShapeDtypeStruct((g, r, RWKV_PROJ), F32),
                   jax.ShapeDtypeStruct((g, r, ATTN_PROJ), F32)],
        compiler_params=_cparams("arbitrary", "arbitrary"),
        name="inproj",
    )(x3, gain.reshape(1, d), sh3, sc3, w_in16)


def _softplus(x):
    return jnp.maximum(x, 0.0) + jnp.log(1.0 + jnp.exp(-jnp.abs(x)))


def _rwkv_kernel(p_ref, prev0_ref, s0_ref, mu_ref, w0_ref, wwa_ref, a0_ref, gup_ref, kk_ref, ka_ref,
                 rk_ref, lng_ref, lnb_ref, y_ref, sout_ref, prev_sc, s_sc, *, nsub, t_valid):
    tc = nsub * CHUNK
    c = pl.program_id(1)

    @pl.when(c == 0)
    def _():
        prev_sc[...] = prev0_ref[...]
        s_sc[...] = s0_ref[...]

    p = p_ref[...]
    row = _iota((tc, 1), 0)
    p_prev = jnp.where(row == 0, prev_sc[...], pltpu.roll(p, 1, axis=0))
    prev_sc[...] = p_ref[tc - 1:tc, :]
    xp = p + (p_prev - p) * mu_ref[...]

    w3 = 3 * RWKV_WIDTH
    r = xp[:, 0:RWKV_WIDTH]
    k = xp[:, RWKV_WIDTH:2 * RWKV_WIDTH]
    v = xp[:, 2 * RWKV_WIDTH:w3]
    z = xp[:, w3:w3 + DECAY_RANK + ICLR_RANK]
    gd = xp[:, w3 + DECAY_RANK + ICLR_RANK:]
    zl = _iota((1, DECAY_RANK + ICLR_RANK), 1)
    zz = jnp.where(zl < DECAY_RANK, jnp.tanh(z), z)
    wa = _mm(zz, wwa_ref[...], passes=3)
    w_log = -_softplus(-(w0_ref[...] + wa[:, :RWKV_WIDTH])) - 0.5
    lw = -jnp.exp(w_log)
    a = jax.nn.sigmoid(a0_ref[...] + wa[:, RWKV_WIDTH:])
    g = _mm(jax.nn.sigmoid(gd), gup_ref[...])
    kk = k * kk_ref[...]
    kk = kk * lax.rsqrt(jnp.maximum(_head_sum(kk * kk), 1e-24))
    k2 = k * (1.0 + (a - 1.0) * ka_ref[...])
    bvec = kk * a
    if t_valid is not None:
        ok = (c * tc + row) < t_valid
        lw = jnp.where(ok, lw, 0.0)
        kk = jnp.where(ok, kk, 0.0)
        bvec = jnp.where(ok, bvec, 0.0)
        k2u = jnp.where(ok, k2, 0.0)
        vu = jnp.where(ok, v, 0.0)
    else:
        k2u, vu = k2, v

    tril16 = (_iota((CHUNK, CHUNK), 0) >= _iota((CHUNK, CHUNK), 1)).astype(BF16)
    bd_mask = (lax.shift_right_logical(_iota((GROUP, GROUP), 0), 6)
               == lax.shift_right_logical(_iota((GROUP, GROUP), 1), 6))
    t_idx = _iota((CHUNK, GROUP), 0)
    s_idx = jnp.bitwise_and(_iota((CHUNK, GROUP), 1), CHUNK - 1)
    strict = t_idx > s_idx
    incl = t_idx >= s_idx
    eye = (t_idx == s_idx).astype(F32)

    bd16 = bd_mask.astype(BF16)

    def bd(x):
        x16 = x.astype(BF16)
        return jnp.concatenate([x16, x16, x16, x16], axis=0) * bd16

    def b16(x):
        return x.astype(BF16)

    ng = RWKV_WIDTH // GROUP
    pairs = [(j, gi) for j in range(nsub) for gi in range(ng)]
    gsl = lambda gi: slice(gi * GROUP, (gi + 1) * GROUP)

    at, bt, kt, rt, bw, kw, vj, w_last = [], [], [], [], [], [], [], []
    for j in range(nsub):
        sl = slice(j * CHUNK, (j + 1) * CHUNK)
        lwj = lw[sl]
        h1 = lwj.astype(BF16)
        r1 = lwj - h1.astype(F32)
        h2 = r1.astype(BF16)
        h3 = (r1 - h2.astype(F32)).astype(BF16)
        cum = _dot(tril16, h1) + _dot(tril16, h2) + _dot(tril16, h3)
        e_cum = jnp.exp(cum)
        e_inv = jnp.exp(-cum)
        wl = e_cum[CHUNK - 1:CHUNK, :]
        at.append(-kk[sl] * jnp.exp(cum - lwj))
        bt.append(bvec[sl] * e_inv)
        kt.append(k2u[sl] * e_inv)
        rt.append(r[sl] * e_cum)
        bw.append(bt[j] * wl)
        kw.append(kt[j] * wl)
        vj.append(vu[sl])
        w_last.append(wl)

    a_ab, a_ak, a_rb, a_rk, vbd = {}, {}, {}, {}, {}
    for (j, gi) in pairs:
        gs = gsl(gi)
        q2 = b16(jnp.concatenate([at[j][:, gs], rt[j][:, gs]], axis=0))
        sc = _dot(q2, jnp.concatenate([bd(bt[j][:, gs]), bd(kt[j][:, gs])], axis=0), nt=True)
        a_ab[j, gi] = jnp.where(strict, sc[:CHUNK, :GROUP], 0.0)
        a_ak[j, gi] = jnp.where(strict, sc[:CHUNK, GROUP:], 0.0)
        a_rb[j, gi] = jnp.where(incl, sc[CHUNK:, :GROUP], 0.0)
        a_rk[j, gi] = jnp.where(incl, sc[CHUNK:, GROUP:], 0.0)
        vbd[j, gi] = bd(vj[j][:, gs])

    tinv = {p: eye + a_ab[p] for p in pairs}
    xpow = {p: _dot(b16(a_ab[p]), bd(a_ab[p])) for p in pairs}
    for step in range(1, 6):
        for p in pairs:
            if step < 5:
                both = _dot(b16(xpow[p]), jnp.concatenate([bd(tinv[p]), bd(xpow[p])], axis=1))
                tinv[p] = tinv[p] + both[:, :GROUP]
                xpow[p] = both[:, GROUP:]
            else:
                tinv[p] = tinv[p] + _dot(b16(xpow[p]), bd(tinv[p]))

    akv = {p: _dot(b16(a_ak[p]), vbd[p]) for p in pairs}
    pq = {(j, gi): _dot(b16(tinv[j, gi]), jnp.concatenate([bd(at[j][:, gsl(gi)]), bd(akv[j, gi])], axis=1))
          for (j, gi) in pairs}
    r2, y0, m_lr, n_add = {}, {}, {}, {}
    for (j, gi) in pairs:
        gs = gsl(gi)
        pm, qm = pq[j, gi][:, :GROUP], pq[j, gi][:, GROUP:]
        rr = _dot(b16(a_rb[j, gi]), jnp.concatenate([bd(pm), bd(qm)], axis=1))
        r2[j, gi] = b16(rt[j][:, gs] + rr[:, :GROUP])
        y0[j, gi] = rr[:, GROUP:] + _dot(b16(a_rk[j, gi]), vbd[j, gi])
        m_lr[j, gi] = b16(jnp.where(bd_mask, _dot(b16(pm.T), b16(bw[j][:, gs])), 0.0))
        qv = jnp.concatenate([qm, vj[j][:, gs]], axis=0)
        bk = jnp.concatenate([bw[j][:, gs], kw[j][:, gs]], axis=0)
        n_add[j, gi] = jnp.where(bd_mask, _dot(b16(qv.T), b16(bk)), 0.0)

    s_in = {}
    for gi in range(ng):
        s = s_sc[gi]
        for j in range(nsub):
            s16 = b16(s)
            s_in[j, gi] = s16
            s = s * w_last[j][:, gsl(gi)] + _dot(s16, m_lr[j, gi]) + n_add[j, gi]
        s_sc[gi] = s

    ys = []
    for j in range(nsub):
        yg = [_dot(r2[j, gi], s_in[j, gi], nt=True) + y0[j, gi] for gi in range(ng)]
        ys.append(jnp.concatenate(yg, axis=1))
    y = jnp.concatenate(ys, axis=0) if nsub > 1 else ys[0]

    inv_n = 1.0 / HEAD
    yc = y - _head_sum(y) * inv_n
    yn = yc * lax.rsqrt(_head_sum(yc * yc) * inv_n + GN_EPS)
    yn = yn * lng_ref[...] + lnb_ref[...]
    bonus = _head_sum(r * k2 * rk_ref[...]) * v
    y_ref[...] = (yn + bonus) * g
    sout_ref[...] = s_sc[...]


def _rwkv(p_r, prev0, s0_bd, lp, nsub, t_valid):
    b, t, _ = p_r.shape
    tc = nsub * CHUNK
    ng = RWKV_WIDTH // GROUP
    row = lambda a: a.reshape(1, -1)
    zeros = jnp.zeros((DECAY_RANK, RWKV_WIDTH), F32)
    wwa = jnp.concatenate([jnp.concatenate([lp["rwkv_w_up"], zeros], axis=1),
                           jnp.concatenate([zeros, lp["rwkv_a_up"]], axis=1)], axis=0)
    const = lambda shape: pl.BlockSpec(shape, lambda bi, ci: (0,) * len(shape))
    kern = functools.partial(_rwkv_kernel, nsub=nsub, t_valid=t_valid)
    return pl.pallas_call(
        kern,
        grid=(b, t // tc),
        in_specs=[pl.BlockSpec((None, tc, RWKV_PROJ), lambda bi, ci: (bi, ci, 0)),
                  pl.BlockSpec((None, 1, RWKV_PROJ), lambda bi, ci: (bi, 0, 0)),
                  pl.BlockSpec((None, ng, GROUP, GROUP), lambda bi, ci: (bi, 0, 0, 0)),
                  const((1, RWKV_PROJ)), const((1, RWKV_WIDTH)),
                  const((DECAY_RANK + ICLR_RANK, 2 * RWKV_WIDTH)), const((1, RWKV_WIDTH)),
                  const((GATE_RANK, RWKV_WIDTH)), const((1, RWKV_WIDTH)), const((1, RWKV_WIDTH)),
                  const((1, RWKV_WIDTH)), const((1, RWKV_WIDTH)), const((1, RWKV_WIDTH))],
        out_specs=[pl.BlockSpec((None, tc, RWKV_WIDTH), lambda bi, ci: (bi, ci, 0)),
                   pl.BlockSpec((None, ng, GROUP, GROUP), lambda bi, ci: (bi, 0, 0, 0))],
        out_shape=[jax.ShapeDtypeStruct((b, t, RWKV_WIDTH), F32),
                   jax.ShapeDtypeStruct((b, ng, GROUP, GROUP), F32)],
        scratch_shapes=[pltpu.VMEM((1, RWKV_PROJ), F32), pltpu.VMEM((ng, GROUP, GROUP), F32)],
        compiler_params=_cparams("arbitrary", "arbitrary"),
        name="rwkv7",
    )(p_r, prev0, s0_bd, row(lp["rwkv_mu"]), row(lp["rwkv_w0"]), wwa, row(lp["rwkv_a0"]),
      lp["rwkv_g_up"], row(lp["rwkv_k_k"]), row(lp["rwkv_k_a"]), row(lp["rwkv_r_k"]),
      row(lp["rwkv_ln_g"]), row(lp["rwkv_ln_b"]))


def _state_to_bd(s):
    b = s.shape[0]
    hg = GROUP // HEAD
    s5 = s.reshape(b, RWKV_HEADS // hg, hg, HEAD, HEAD)
    out = jnp.einsum("bghvk,hj->bghvjk", s5, jnp.eye(hg, dtype=s.dtype))
    return out.reshape(b, RWKV_HEADS // hg, GROUP, GROUP)


def _state_from_bd(s_bd):
    b = s_bd.shape[0]
    hg = GROUP // HEAD
    s6 = s_bd.reshape(b, RWKV_HEADS // hg, hg, HEAD, hg, HEAD)
    out = jnp.einsum("bghvjk,hj->bghvk", s6, jnp.eye(hg, dtype=s_bd.dtype))
    return out.reshape(b, RWKV_HEADS, HEAD, HEAD)


def _swa_kernel(tab_ref, sink_ref, pa_ref, ck_ref, cv_ref, bkt_ref, qg_ref, kg_ref,
                ya_ref, kn_ref, kall, vall, bias_sc, *, cq, nqc, has_cache):
    tq = cq * nqc
    band = WINDOW + cq
    bi = pl.program_id(0)
    i = pl.program_id(1)
    lane = _iota((1, LANES), 1)
    low = lane < HEAD

    @pl.when(jnp.logical_and(bi == 0, i == 0))
    def _():
        bkt = bkt_ref[...]
        for h in range(ATTN_HEADS):
            def body(j, acc, h=h):
                return acc + jnp.where(bkt == j, tab_ref[j, h], 0.0)
            bh = lax.fori_loop(0, REL_BUCKETS, body, jnp.zeros((cq, band), F32))
            bias_sc[h // 2, (h % 2) * cq:(h % 2 + 1) * cq, :] = bh

    @pl.when(i == 0)
    def _():
        kall[0:WINDOW, :] = ck_ref[...]
        vall[0:WINDOW, :] = cv_ref[...]

    pa = pa_ref[...]
    q = pa[:, :ATTN_WIDTH]
    k = pa[:, ATTN_WIDTH:ATTN_WIDTH + KV_WIDTH]
    v = pa[:, ATTN_WIDTH + KV_WIDTH:]
    inv_n = 1.0 / HEAD
    qn = q * lax.rsqrt(_head_sum(q * q) * inv_n + RMS_EPS) * qg_ref[...]
    kn = k * lax.rsqrt(_head_sum(k * k) * inv_n + RMS_EPS) * kg_ref[...]
    kn_ref[...] = kn
    kall[WINDOW:WINDOW + tq, :] = kn
    vall[WINDOW:WINDOW + tq, :] = v

    row2 = _iota((2 * cq, 1), 0)
    col = _iota((1, band), 1)
    thr = jnp.where(i == 0, WINDOW, 0)
    k_all = kall[...]
    v_all = vall[...]
    k_rot = pltpu.roll(k_all, HEAD, axis=1)
    v_rot = pltpu.roll(v_all, HEAD, axis=1)
    kd = [jnp.where(low, k_all, k_rot).astype(BF16), jnp.where(low, k_rot, k_all).astype(BF16)]
    vd = [jnp.where(low, v_all, v_rot).astype(BF16), jnp.where(low, v_rot, v_all).astype(BF16)]

    work = [(qc, pi) for qc in range(nqc) for pi in range(ATTN_HEADS // 2)]
    kv_of = lambda pi: (2 * pi) // (ATTN_HEADS // KV_HEADS)
    scores = {}
    for (qc, pi) in work:
        qp = qn[qc * cq:(qc + 1) * cq, pi * LANES:(pi + 1) * LANES]
        qs = jnp.concatenate([jnp.where(low, qp, 0.0), jnp.where(low, 0.0, qp)], axis=0)
        s = _dot(qs.astype(BF16), kd[kv_of(pi)][qc * cq:qc * cq + band], nt=True) * ATTN_SCALE + bias_sc[pi]
        if (not has_cache) and qc * cq < WINDOW:
            s = jnp.where((col + qc * cq) < thr, NEG_INF, s)
        scores[qc, pi] = s
    probs = {}
    for (qc, pi) in work:
        s = scores[qc, pi]
        sink = jnp.where(row2 < cq, sink_ref[0, 2 * pi], sink_ref[0, 2 * pi + 1])
        m = jnp.maximum(jnp.max(s, axis=-1, keepdims=True), sink)
        e = jnp.exp(s - m)
        den = jnp.sum(e, axis=-1, keepdims=True) + jnp.exp(sink - m)
        probs[qc, pi] = (e / den).astype(BF16)
    for (qc, pi) in work:
        o = _dot(probs[qc, pi], vd[kv_of(pi)][qc * cq:qc * cq + band])
        ya_ref[qc * cq:(qc + 1) * cq, pi * LANES:(pi + 1) * LANES] = jnp.where(low, o[:cq], o[cq:])

    if tq >= WINDOW:
        knext = kall[tq:tq + WINDOW, :]
        vnext = vall[tq:tq + WINDOW, :]
        kall[0:WINDOW, :] = knext
        vall[0:WINDOW, :] = vnext


def _t5_bucket(rel):
    half = REL_BUCKETS // 2
    max_exact = half // 2
    n = jnp.abs(rel)
    log_ratio = jnp.log(jnp.maximum(n, 1).astype(F32) / max_exact) / math.log(REL_MAX_DIST / max_exact)
    large = jnp.minimum(max_exact + (log_ratio * (half - max_exact)).astype(I32), half - 1)
    return jnp.where(rel > 0, half, 0) + jnp.where(n < max_exact, n, large)


def _swa(p_a, cache_k, cache_v, rel_table, sinks, q_norm, k_norm, cq, nqc, has_cache):
    b, t, _ = p_a.shape
    tq = cq * nqc
    band = WINDOW + cq
    rel = (jnp.arange(band) - WINDOW)[None, :] - jnp.arange(cq)[:, None]
    bkt = _t5_bucket(rel).astype(I32)
    qg = jnp.tile(q_norm, ATTN_HEADS).reshape(1, ATTN_WIDTH)
    kg = jnp.tile(k_norm, KV_HEADS).reshape(1, KV_WIDTH)
    kern = functools.partial(_swa_kernel, cq=cq, nqc=nqc, has_cache=has_cache)
    smem = pl.BlockSpec(memory_space=pltpu.SMEM)
    return pl.pallas_call(
        kern,
        grid=(b, t // tq),
        in_specs=[smem, smem,
                  pl.BlockSpec((None, tq, ATTN_PROJ), lambda bi, i: (bi, i, 0)),
                  pl.BlockSpec((None, WINDOW, KV_WIDTH), lambda bi, i: (bi, 0, 0)),
                  pl.BlockSpec((None, WINDOW, KV_WIDTH), lambda bi, i: (bi, 0, 0)),
                  pl.BlockSpec((cq, band), lambda bi, i: (0, 0)),
                  pl.BlockSpec((1, ATTN_WIDTH), lambda bi, i: (0, 0)),
                  pl.BlockSpec((1, KV_WIDTH), lambda bi, i: (0, 0))],
        out_specs=[pl.BlockSpec((None, tq, ATTN_WIDTH), lambda bi, i: (bi, i, 0)),
                   pl.BlockSpec((None, tq, KV_WIDTH), lambda bi, i: (bi, i, 0))],
        out_shape=[jax.ShapeDtypeStruct((b, t, ATTN_WIDTH), F32),
                   jax.ShapeDtypeStruct((b, t, KV_WIDTH), F32)],
        scratch_shapes=[pltpu.VMEM((WINDOW + tq, KV_WIDTH), F32),
                        pltpu.VMEM((WINDOW + tq, KV_WIDTH), F32),
                        pltpu.VMEM((ATTN_HEADS // 2, 2 * cq, band), F32)],
        compiler_params=_cparams("arbitrary", "arbitrary"),
        name="swa",
    )(rel_table, sinks.reshape(1, ATTN_HEADS), p_a, cache_k, cache_v, bkt, qg, kg)


def _mid_kernel(yr_ref, ya_ref, x_ref, gt1_ref, sh2_ref, sc2_ref, g2_ref, wor_ref, woa_ref, wrt_ref,
                rb_ref, base0_ref, x1_ref, h2_ref, te_ref, gate_ref, pos_ref, cnt_ref, base_sc, *, tm):
    @pl.when(jnp.logical_and(pl.program_id(0) == 0, pl.program_id(1) == 0))
    def _():
        base_sc[...] = base0_ref[...]

    mix = _dot(yr_ref[...].astype(BF16), wor_ref[...]) + _dot(ya_ref[...].astype(BF16), woa_ref[...])
    x1 = x_ref[...] + gt1_ref[...] * mix
    x1_ref[...] = x1
    h2 = _rms(x1, RMS_EPS) * g2_ref[...] * (1.0 + sc2_ref[...]) + sh2_ref[...]
    h2_ref[...] = h2

    logits = _mm(wrt_ref[...], h2, passes=3, nt=True) + rb_ref[...]
    eidx = _iota((N_EXPERTS, tm), 0)
    work = logits
    tops, hots = [], []
    for j in range(TOP_K):
        m = jnp.max(work, axis=0, keepdims=True)
        idx = jnp.min(jnp.where(work == m, eidx, N_EXPERTS), axis=0, keepdims=True)
        hot = eidx == idx
        work = jnp.where(hot, -jnp.inf, work)
        tops.append(m)
        hots.append(hot)
        te_ref[j:j + 1, :] = idx
    exps = [jnp.exp(tj - tops[0]) for tj in tops]
    den = exps[0] + exps[1] + exps[2] + exps[3]
    for j in range(TOP_K):
        gate_ref[j:j + 1, :] = exps[j] / den

    member = (hots[0] | hots[1] | hots[2] | hots[3]).astype(BF16)
    upper = (_iota((tm, tm), 0) < _iota((tm, tm), 1)).astype(BF16)
    before = _dot(member, upper) + base_sc[...]
    for j in range(TOP_K):
        pos_ref[j:j + 1, :] = jnp.sum(jnp.where(hots[j], before, 0.0), axis=0, keepdims=True).astype(I32)
    base_sc[...] = base_sc[...] + jnp.sum(member.astype(F32), axis=1, keepdims=True)
    cnt_ref[...] = base_sc[...]


def _mid(y_r, y_a, x3, gt1, sh2, sc2, gain2, wo_r16, wo_a16, wr_t, rb, base0, tm):
    g, r, d = x3.shape
    n = g * r
    per = r // tm
    mr = gt1.shape[1]
    mrow = (lambda b, i: (b, 0, 0)) if mr == 1 else (lambda b, i: (b, i, 0))
    mblk = 1 if mr == 1 else tm
    tok = lambda w: pl.BlockSpec((None, tm, w), lambda b, i: (b, i, 0))
    mod = pl.BlockSpec((None, mblk, d), mrow)
    const = lambda shape: pl.BlockSpec(shape, lambda b, i: (0,) * len(shape))
    lane_out = pl.BlockSpec((TOP_K, tm), lambda b, i: (0, b * per + i))
    kern = functools.partial(_mid_kernel, tm=tm)
    return pl.pallas_call(
        kern,
        grid=(g, per),
        in_specs=[tok(RWKV_WIDTH), tok(ATTN_WIDTH), tok(d), mod, mod, mod, const((1, d)),
                  const((RWKV_WIDTH, d)), const((ATTN_WIDTH, d)), const((N_EXPERTS, d)),
                  const((N_EXPERTS, 1)), const((N_EXPERTS, 1))],
        out_specs=[tok(d), tok(d), lane_out, lane_out, lane_out, const((N_EXPERTS, 1))],
        out_shape=[jax.ShapeDtypeStruct((g, r, d), F32), jax.ShapeDtypeStruct((g, r, d), F32),
                   jax.ShapeDtypeStruct((TOP_K, n), I32), jax.ShapeDtypeStruct((TOP_K, n), F32),
                   jax.ShapeDtypeStruct((TOP_K, n), I32), jax.ShapeDtypeStruct((N_EXPERTS, 1), F32)],
        scratch_shapes=[pltpu.VMEM((N_EXPERTS, 1), F32)],
        compiler_params=_cparams("arbitrary", "arbitrary"),
        name="mid",
    )(y_r, y_a, x3, gt1, sh2, sc2, gain2.reshape(1, d), wo_r16, wo_a16, wr_t, rb, base0)


def _slot_kernel(pstart_ref, te_ref, pos_ref, o_ref):
    te = te_ref[...]
    acc = pos_ref[...]
    for e in range(N_EXPERTS):
        acc = acc + jnp.where(te == e, pstart_ref[e], 0)
    o_ref[...] = acc


def _slot_rows(te, pos, pstart):
    k, n = te.shape
    return pl.pallas_call(
        _slot_kernel,
        grid=(1,),
        in_specs=[pl.BlockSpec(memory_space=pltpu.SMEM),
                  pl.BlockSpec((k, n), lambda i: (0, 0)), pl.BlockSpec((k, n), lambda i: (0, 0))],
        out_specs=pl.BlockSpec((k, n), lambda i: (0, 0)),
        out_shape=jax.ShapeDtypeStruct((k, n), I32),
        compiler_params=_cparams("arbitrary"),
        name="slot_rows",
    )(pstart, te, pos)


def _dispatch_kernel(dst_ref, pad_ref, h2a_ref, h2b_ref, xs_ref, ztile, sem, zsem, *, tk, steps_a):
    i = pl.program_id(0)

    @pl.when(i == 0)
    def _():
        ztile[...] = jnp.zeros_like(ztile)
        zrow = ztile.at[pl.ds(0, 1)]
        tm = ztile.shape[0]

        def zero_row(row, priority):
            pltpu.make_async_copy(zrow, xs_ref.at[pl.ds(row, 1)], zsem).start(priority=priority)

        def per_expert(e, total):
            first = pad_ref[0, e]
            count = pad_ref[1, e]

            def per_pair(r, c):
                zero_row(first + 2 * r, 0)
                zero_row(first + 2 * r + 1, 1)
                return c
            lax.fori_loop(0, count // 2, per_pair, 0)

            @pl.when(count % 2 == 1)
            def _():
                zero_row(first + count - 1, 0)
            return total + count
        n_pad = lax.fori_loop(0, N_EXPERTS, per_expert, 0)

        blk = 64
        def drain_blk(r, c):
            pltpu.make_async_copy(ztile.at[pl.ds(0, blk)], xs_ref.at[pl.ds(0, blk)], zsem).wait()
            return c
        lax.fori_loop(0, n_pad // blk, drain_blk, 0)

        def drain_row(r, c):
            pltpu.make_async_copy(zrow, xs_ref.at[pl.ds(0, 1)], zsem).wait()
            return c
        lax.fori_loop(0, n_pad % blk, drain_row, 0)

        def spare(t, c):
            row0 = pl.multiple_of(t * tm, tm)
            pltpu.make_async_copy(ztile, xs_ref.at[pl.ds(row0, tm)], zsem).start()
            return c
        lax.fori_loop(pad_ref[2, 0], pad_ref[2, 1], spare, 0)

        def spare_wait(t, c):
            pltpu.make_async_copy(ztile, xs_ref.at[pl.ds(0, tm)], zsem).wait()
            return c
        lax.fori_loop(pad_ref[2, 0], pad_ref[2, 1], spare_wait, 0)

    def scatter(h2_ref):
        for t in range(h2_ref.shape[0] * ROW_UNROLL):
            for j in range(TOP_K):
                dst = dst_ref[j, t]
                pltpu.make_async_copy(h2_ref.at[t // ROW_UNROLL, pl.ds(t % ROW_UNROLL, 1)],
                                      xs_ref.at[pl.ds(dst, 1)], sem).start(priority=j % 2)
        for _ in range(TOP_K):
            pltpu.make_async_copy(h2_ref, h2_ref, sem).wait()

    pl.when(i < steps_a)(functools.partial(scatter, h2a_ref))
    pl.when(i == steps_a)(functools.partial(scatter, h2b_ref))


def _dispatch(dest, pads, h2a, h2b, n_rows, tk):
    na, d = h2a.shape
    nb = h2b.shape[0]
    steps_a = na // tk
    dest = jnp.pad(dest, ((0, 0), (0, (steps_a + 1) * tk - na - nb)))
    h2a = h2a.reshape(na // ROW_UNROLL, ROW_UNROLL, d)
    h2b = h2b.reshape(nb // ROW_UNROLL, ROW_UNROLL, d)
    kern = functools.partial(_dispatch_kernel, tk=tk, steps_a=steps_a)
    return pl.pallas_call(
        kern,
        grid=(steps_a + 1,),
        in_specs=[pl.BlockSpec((TOP_K, tk), lambda i: (0, i), memory_space=pltpu.SMEM),
                  pl.BlockSpec(memory_space=pltpu.SMEM),
                  pl.BlockSpec((tk // ROW_UNROLL, ROW_UNROLL, d), lambda i: (jnp.minimum(i, steps_a - 1), 0, 0)),
                  pl.BlockSpec((nb // ROW_UNROLL, ROW_UNROLL, d), lambda i: (0, 0, 0))],
        out_specs=pl.BlockSpec(memory_space=pl.ANY),
        out_shape=jax.ShapeDtypeStruct((n_rows, d), F32),
        scratch_shapes=[pltpu.VMEM((TM_FFN, d), F32), pltpu.SemaphoreType.DMA(()), pltpu.SemaphoreType.DMA(())],
        compiler_params=_cparams("arbitrary"),
        name="dispatch",
    )(dest, pads, h2a, h2b)


def _ffn_kernel(te_ref, nu_ref, nx_ref, xs_ref, wu_hbm, bu_ref, wd_hbm, bd_ref, ys_ref,
                wu32, wd32, wu16, wd16, sem):
    i = pl.program_id(0)
    n_up = wu32.shape[1]
    active = i < nu_ref[0]
    fresh = jnp.logical_or(i == 0, te_ref[i] != te_ref[jnp.maximum(i - 1, 0)])

    def weight_copies(e):
        return (pltpu.make_async_copy(wu_hbm.at[e], wu32, sem.at[0]),
                pltpu.make_async_copy(wd_hbm.at[e], wd32, sem.at[1]))

    @pl.when(i == 0)
    def _():
        for cp in weight_copies(te_ref[0]):
            cp.start()

    @pl.when(jnp.logical_and(active, fresh))
    def _():
        for cp in weight_copies(te_ref[i]):
            cp.wait()
        src = _iota((GROUP, GROUP), 0)
        dst = _iota((GROUP, GROUP), 1)
        want = jnp.where(dst < LANES, 2 * dst, 2 * (dst - LANES) + 1)
        perm = (src == want).astype(BF16)
        for b in range(n_up // GROUP):
            cs = slice(b * GROUP, (b + 1) * GROUP)
            wu16[:, cs] = _dot(wu32[:, cs].astype(BF16), perm).astype(BF16)
        wd16[...] = wd32[...].astype(BF16)

        @pl.when(nx_ref[i] >= 0)
        def _():
            for cp in weight_copies(nx_ref[i]):
                cp.start()

    @pl.when(active)
    def _():
        u = _dot(xs_ref[...].astype(BF16), wu16[...]) + bu_ref[...]
        acts = []
        for b in range(n_up // GROUP):
            glu = jnp.minimum(u[:, b * GROUP:b * GROUP + LANES], SWIGLU_LIMIT)
            lin = jnp.clip(u[:, b * GROUP + LANES:(b + 1) * GROUP], -SWIGLU_LIMIT, SWIGLU_LIMIT)
            acts.append((glu * jax.nn.sigmoid(SWIGLU_ALPHA * glu) * (lin + 1.0)).astype(BF16))
        act = jnp.concatenate(acts, axis=1)
        ys_ref[...] = _dot(act, wd16[...]) + bd_ref[...]

    @pl.when(jnp.logical_not(active))
    def _():
        ys_ref[...] = jnp.zeros_like(ys_ref)


def _ffn(tile_e, n_used, next_e, xs, wu, bu, wd, bd_, tm):
    rows, d = xs.shape
    n_exp, _, n_up = wu.shape
    dff = wd.shape[1]
    row_blk = lambda i, te, nu, nx: (jnp.minimum(i, nu[0] - 1), 0)
    exp_blk = lambda i, te, nu, nx: (te[i], 0, 0)
    grid_spec = pltpu.PrefetchScalarGridSpec(
        num_scalar_prefetch=3,
        grid=(rows // tm,),
        in_specs=[pl.BlockSpec((tm, d), row_blk),
                  pl.BlockSpec(memory_space=pl.ANY),
                  pl.BlockSpec((None, 1, n_up), exp_blk),
                  pl.BlockSpec(memory_space=pl.ANY),
                  pl.BlockSpec((None, 1, d), exp_blk)],
        out_specs=pl.BlockSpec((tm, d), lambda i, te, nu, nx: (i, 0)),
        scratch_shapes=[pltpu.VMEM((d, n_up), F32), pltpu.VMEM((dff, d), F32),
                        pltpu.VMEM((d, n_up), BF16), pltpu.VMEM((dff, d), BF16),
                        pltpu.SemaphoreType.DMA((2,))],
    )
    return pl.pallas_call(
        _ffn_kernel,
        grid_spec=grid_spec,
        out_shape=jax.ShapeDtypeStruct((rows, d), F32),
        compiler_params=_cparams("arbitrary"),
        name="expert_ffn",
    )(tile_e, n_used, next_e, xs, wu, bu.reshape(n_exp, 1, n_up), wd, bd_.reshape(n_exp, 1, d))


def _combine_kernel(src_ref, srcn_ref, gate_ref, x1_ref, gt2_ref, ys_ref, o_ref, buf, sem, *, tk):
    i = pl.program_id(1)
    n = pl.num_programs(1)
    flat = pl.program_id(0) * n + i
    total = pl.num_programs(0) * n
    slot = lax.rem(flat, 2)

    def issue(rows_ref, s, t0, t1):
        for t in range(t0, t1):
            for j in range(TOP_K):
                src = rows_ref[j, t]
                pltpu.make_async_copy(ys_ref.at[pl.ds(src, 1)],
                                      buf.at[s, j, t // ROW_UNROLL, pl.ds(t % ROW_UNROLL, 1)],
                                      sem.at[s]).start(priority=j % 2)

    @pl.when(flat == 0)
    def _():
        issue(src_ref, 0, 0, tk)

    def step(cur):
        @pl.when(flat + 1 < total)
        def _():
            issue(srcn_ref, 1 - cur, 0, tk)

        for j in range(TOP_K):
            pltpu.make_async_copy(buf.at[cur, j], buf.at[cur, j], sem.at[cur]).wait()
        eye = (_iota((tk, tk), 0) == _iota((tk, tk), 1)).astype(BF16)
        g_hi, g_lo = _split(gate_ref[...])
        gate = _dot(eye, g_hi, nt=True) + _dot(eye, g_lo, nt=True)
        rows = lambda j: buf[cur, j].reshape(tk, buf.shape[-1])
        acc = gate[:, 0:1] * rows(0)
        for j in range(1, TOP_K):
            acc = acc + gate[:, j:j + 1] * rows(j)
        o_ref[...] = x1_ref[...] + gt2_ref[...] * acc

    for cur in range(2):
        pl.when(slot == cur)(functools.partial(step, cur))


def _combine(dest, gates, x1, gt2, ys, tk):
    g, r, d = x1.shape
    per = r // tk
    last = g * per - 1
    mr = gt2.shape[1]
    mrow = (lambda b, i: (b, 0, 0)) if mr == 1 else (lambda b, i: (b, i, 0))
    mblk = 1 if mr == 1 else tk
    cur = pl.BlockSpec((TOP_K, tk), lambda b, i: (0, b * per + i), memory_space=pltpu.SMEM)
    nxt = pl.BlockSpec((TOP_K, tk), lambda b, i: (0, jnp.minimum(b * per + i + 1, last)),
                       memory_space=pltpu.SMEM)
    kern = functools.partial(_combine_kernel, tk=tk)
    return pl.pallas_call(
        kern,
        grid=(g, per),
        in_specs=[cur, nxt,
                  pl.BlockSpec((TOP_K, tk), lambda b, i: (0, b * per + i)),
                  pl.BlockSpec((None, tk, d), lambda b, i: (b, i, 0)),
                  pl.BlockSpec((None, mblk, d), mrow),
                  pl.BlockSpec(memory_space=pl.ANY)],
        out_specs=pl.BlockSpec((None, tk, d), lambda b, i: (b, i, 0)),
        out_shape=jax.ShapeDtypeStruct((g, r, d), F32),
        scratch_shapes=[pltpu.VMEM((2, TOP_K, tk // ROW_UNROLL, ROW_UNROLL, d), F32),
                        pltpu.SemaphoreType.DMA((2,))],
        compiler_params=_cparams("arbitrary", "arbitrary"),
        name="combine",
    )(dest, dest, gates, x1, gt2, ys)


def _tile_plan(counts, tm, n_tiles):
    cnt = counts.reshape(-1).astype(I32)
    tiles = (cnt + tm - 1) // tm
    tile_end = jnp.cumsum(tiles)
    pstart = ((tile_end - tiles) * tm).astype(I32)
    n_used = tile_end[-1:].astype(I32)
    idx = jnp.minimum(jnp.arange(n_tiles, dtype=I32), n_used[0] - 1)
    tile_e = jnp.sum((tile_end[None, :] <= idx[:, None]).astype(I32), axis=1)
    spare = jnp.zeros((N_EXPERTS,), I32).at[0].set(n_used[0]).at[1].set(n_tiles)
    pads = jnp.stack([pstart + cnt, tiles * tm - cnt, spare]).astype(I32)
    tile_e = jnp.minimum(tile_e, N_EXPERTS - 1)
    end_of = tile_end[tile_e]
    next_e = jnp.where(end_of < n_used[0], tile_e[jnp.minimum(end_of, n_tiles - 1)], -1).astype(I32)
    return pstart, tile_e, n_used, next_e, pads


def kernel(x_prompt, x_sample, state_shift, state_wkv, cache_win_k, cache_win_v, c_prompt, c_sample, rel_bias, norm1_g, norm2_g, w_ada, b_ada, w_in, w_out, rwkv_mu, rwkv_w0, rwkv_w_up, rwkv_a0, rwkv_a_up, rwkv_g_up, rwkv_k_k, rwkv_k_a, rwkv_r_k, rwkv_ln_g, rwkv_ln_b, q_norm, k_norm, sinks, router_w, router_b, exp_w_up, exp_b_up, exp_w_down, exp_b_down):
    bp, tp, d = x_prompt.shape
    bs, ts, _ = x_sample.shape
    depth = w_in.shape[0]
    assert depth == 1, "single-layer trunk"
    layer = 0
    lp = dict(rwkv_mu=rwkv_mu[layer], rwkv_w0=rwkv_w0[layer], rwkv_w_up=rwkv_w_up[layer],
              rwkv_a0=rwkv_a0[layer], rwkv_a_up=rwkv_a_up[layer], rwkv_g_up=rwkv_g_up[layer],
              rwkv_k_k=rwkv_k_k[layer], rwkv_k_a=rwkv_k_a[layer], rwkv_r_k=rwkv_r_k[layer].reshape(-1),
              rwkv_ln_g=rwkv_ln_g[layer], rwkv_ln_b=rwkv_ln_b[layer])

    rows = bp + bs
    pad = (-rows) % 8
    c_all = jnp.concatenate([c_prompt, c_sample, jnp.zeros((pad, d), F32)], axis=0)
    mod = _modulation(c_all, w_ada[layer], b_ada[layer])
    sh1, sc1, gt1, sh2, sc2, gt2 = [mod[:, i * d:(i + 1) * d] for i in range(6)]
    pm = lambda m: m[:bp].reshape(bp, 1, d)
    sm = lambda m: jnp.repeat(m[bp:bp + bs], ts, axis=0).reshape(1, bs * ts, d)

    w_in16 = w_in[layer].astype(BF16)
    wo16 = w_out[layer].astype(BF16)
    wo_r16, wo_a16 = wo16[:RWKV_WIDTH], wo16[RWKV_WIDTH:]
    wr_t = router_w[layer].T
    rb = router_b[layer].reshape(N_EXPERTS, 1)
    wu = exp_w_up.reshape(exp_w_up.shape[1:])
    wd = exp_w_down.reshape(exp_w_down.shape[1:])
    n_up = wu.shape[-1]
    bu = exp_b_up[layer].reshape(N_EXPERTS, n_up // GROUP, LANES, 2).transpose(0, 1, 3, 2).reshape(N_EXPERTS, n_up)

    pr_p, pa_p = _inproj(x_prompt, norm1_g[layer], pm(sh1), pm(sc1), w_in16, TM_PROJ)
    zero_prev = jnp.zeros((bp, 1, RWKV_PROJ), F32)
    zero_state = jnp.zeros((bp, RWKV_WIDTH // GROUP, GROUP, GROUP), F32)
    yr_p, s_p = _rwkv(pr_p, zero_prev, zero_state, lp, RWKV_SUB, None)
    zero_kv = jnp.zeros((bp, WINDOW, KV_WIDTH), F32)
    ya_p, kn_p = _swa(pa_p, zero_kv, zero_kv, rel_bias, sinks[layer], q_norm[layer], k_norm[layer],
                      CHUNK, SWA_QC, False)

    ns = bs * ts
    xs_flat = x_sample.reshape(1, ns, d)
    pr_s, pa_s = _inproj(xs_flat, norm1_g[layer], sm(sh1), sm(sc1), w_in16, ns)
    pr_s = pr_s.reshape(bs, ts, RWKV_PROJ)
    pa_s = pa_s.reshape(bs, ts, ATTN_PROJ)
    t_pad = -(-ts // CHUNK) * CHUNK
    pr_s_pad = jnp.pad(pr_s, ((0, 0), (0, t_pad - ts), (0, 0)))
    yr_s, s_s = _rwkv(pr_s_pad, state_shift[layer].reshape(bs, 1, RWKV_PROJ), _state_to_bd(state_wkv[layer]),
                      lp, t_pad // CHUNK, ts)
    yr_s = yr_s[:, :ts]
    ya_s, kn_s = _swa(pa_s, cache_win_k[layer].reshape(bs, WINDOW, KV_WIDTH),
                      cache_win_v[layer].reshape(bs, WINDOW, KV_WIDTH), rel_bias, sinks[layer],
                      q_norm[layer], k_norm[layer], ts, 1, True)

    base0 = jnp.zeros((N_EXPERTS, 1), F32)
    x1_p, h2_p, te_p, gate_p, pos_p, cnt_p = _mid(yr_p, ya_p, x_prompt, pm(gt1), pm(sh2), pm(sc2), norm2_g[layer],
                                                  wo_r16, wo_a16, wr_t, rb, base0, TM_PROJ)
    x1_s, h2_s, te_s, gate_s, pos_s, cnt = _mid(yr_s.reshape(1, ns, RWKV_WIDTH), ya_s.reshape(1, ns, ATTN_WIDTH),
                                                xs_flat, sm(gt1), sm(sh2), sm(sc2), norm2_g[layer],
                                                wo_r16, wo_a16, wr_t, rb, cnt_p, ns)

    n_p = bp * tp
    n_rows = (n_p + ns) * TOP_K + N_EXPERTS * (TM_FFN - 1)
    n_tiles = -(-n_rows // TM_FFN)
    pstart, tile_e, n_used, next_e, pads = _tile_plan(cnt, TM_FFN, n_tiles)
    dest = _slot_rows(jnp.concatenate([te_p, te_s], axis=1), jnp.concatenate([pos_p, pos_s], axis=1), pstart)
    dest_p, dest_s = dest[:, :n_p], dest[:, n_p:]
    xs_buf = _dispatch(dest, pads, h2_p.reshape(n_p, d), h2_s.reshape(ns, d), n_tiles * TM_FFN, TK_DISPATCH)
    ys_buf = _ffn(tile_e, n_used, next_e, xs_buf, wu, bu, wd, exp_b_down[layer], TM_FFN)
    y_p = _combine(dest_p, gate_p, x1_p, pm(gt2), ys_buf, TK_COMBINE)
    y_s = _combine(dest_s, gate_s, x1_s, sm(gt2), ys_buf, min(TK_COMBINE, ns))

    kv4 = lambda z, rows_: z.reshape(z.shape[0], rows_, KV_HEADS, HEAD)[None]
    v_p = pa_p[:, tp - WINDOW:, ATTN_WIDTH + KV_WIDTH:]
    v_s = pa_s[:, :, ATTN_WIDTH + KV_WIDTH:]
    return (y_p, y_s.reshape(bs, ts, d),
            pr_p[:, tp - 1][None], _state_from_bd(s_p)[None], kv4(kn_p[:, tp - WINDOW:], WINDOW), kv4(v_p, WINDOW),
            pr_s[:, ts - 1][None], _state_from_bd(s_s)[None], kv4(kn_s, ts), kv4(v_s, ts))
```
